```python
import math
import jax, jax.numpy as jnp
from jax import lax
import numpy as np


D_MODEL = 1024
BATCH = 2
SEQ = 8192
DEPTH = 1

HG_HEADS = 8
HG_DK = 128
HG_DV = D_MODEL // HG_HEADS
HG_KWIDTH = HG_HEADS * HG_DK
HG_VWIDTH = HG_HEADS * HG_DV
HG_CHUNK = 64

AT_Q_HEADS = 16
AT_KV_HEADS = 4
AT_HEAD_DIM = D_MODEL // AT_Q_HEADS
AT_GROUP = AT_Q_HEADS // AT_KV_HEADS
WINDOW = 128
AT_BLOCK = WINDOW
AT_QWIDTH = AT_Q_HEADS * AT_HEAD_DIM
AT_KVWIDTH = AT_KV_HEADS * AT_HEAD_DIM

N_EXPERTS = 32
TOP_K = 4
D_FF_EXPERT = D_MODEL
SWIGLU_LIMIT = 7.0
SWIGLU_ALPHA = 1.702
MOE_BLOCK = 128

NORM_EPS = 1e-5

IN_SIZES = (HG_KWIDTH, HG_KWIDTH, HG_KWIDTH, HG_VWIDTH, HG_VWIDTH, AT_QWIDTH, AT_KVWIDTH, AT_KVWIDTH, D_MODEL, D_MODEL)
IN_WIDTH = sum(IN_SIZES)
IN_SPLITS = tuple(int(c) for c in np.cumsum(IN_SIZES)[:-1])

kernel_name = 'hybrid_hgrn2_swa_moe_encoder_block'


def rms_norm(x, gain):
    xf = x.astype(jnp.float32)
    y = xf * lax.rsqrt(jnp.mean(xf * xf, axis=-1, keepdims=True) + NORM_EPS)
    return (y * gain.astype(jnp.float32)).astype(x.dtype)


def gla_chunkwise(q, k, v, log_f):
    bsz, seqlen, nh, dk = q.shape
    dv = v.shape[-1]
    n_chunks = seqlen // HG_CHUNK

    def chunks(t):
        return t.reshape(bsz, n_chunks, HG_CHUNK, nh, t.shape[-1]).transpose(0, 3, 1, 2, 4)

    qc = chunks(q).astype(jnp.float32)
    kc = chunks(k).astype(jnp.float32)
    vc = chunks(v).astype(jnp.float32)
    cum = jnp.cumsum(chunks(log_f).astype(jnp.float32), axis=3)
    last = cum[:, :, :, -1:, :]
    q_dec = qc * jnp.exp(cum)
    k_inv = kc * jnp.exp(-cum)
    k_end = kc * jnp.exp(last - cum)
    decay = jnp.exp(last[:, :, :, 0, :])
    tri = jnp.tril(jnp.ones((HG_CHUNK, HG_CHUNK), dtype=bool))
    scores = jnp.where(tri, jnp.einsum('bhncd,bhnsd->bhncs', q_dec, k_inv), 0.0)
    o_intra = jnp.einsum('bhncs,bhnsv->bhncv', scores, vc)

    def step(state, inp):
        qd, ke, vv, dec = inp
        o_inter = jnp.einsum('bhcd,bhdv->bhcv', qd, state)
        state = dec[..., None] * state + jnp.einsum('bhcd,bhcv->bhdv', ke, vv)
        return state, o_inter

    s0 = jnp.zeros((bsz, nh, dk, dv), jnp.float32)
    xs = (jnp.moveaxis(q_dec, 2, 0), jnp.moveaxis(k_end, 2, 0), jnp.moveaxis(vc, 2, 0), jnp.moveaxis(decay, 2, 0))
    _, o_inter = lax.scan(step, s0, xs)
    o = o_intra + jnp.moveaxis(o_inter, 0, 2)
    return o.transpose(0, 2, 3, 1, 4).reshape(bsz, seqlen, nh, dv)


def hgrn2_mixer(q_pre, f_fwd_pre, f_bwd_pre, i_val, g_pre, lb_fwd, lb_bwd, gnorm_gain):
    bsz, seqlen, _ = q_pre.shape

    def heads(t, d):
        return t.reshape(bsz, seqlen, HG_HEADS, d)

    q = heads(jax.nn.silu(q_pre), HG_DK)
    v = heads(i_val, HG_DV)

    def gate(f_pre, lb):
        f = lb + (1.0 - lb) * jax.nn.sigmoid(f_pre.astype(jnp.float32))
        return heads(1.0 - f, HG_DK), heads(jnp.log(f), HG_DK)

    k_fwd, logf_fwd = gate(f_fwd_pre, lb_fwd)
    k_bwd, logf_bwd = gate(f_bwd_pre, lb_bwd)
    flip = lambda t: jnp.flip(t, axis=1)
    o = gla_chunkwise(q, k_fwd, v, logf_fwd) + flip(gla_chunkwise(flip(q), flip(k_bwd), flip(v), flip(logf_bwd)))
    o = o * lax.rsqrt(jnp.mean(o * o, axis=-1, keepdims=True) + NORM_EPS) * gnorm_gain.astype(jnp.float32)
    o = o.reshape(bsz, seqlen, HG_VWIDTH) * jax.nn.silu(g_pre.astype(jnp.float32))
    return o.astype(q_pre.dtype)


def window_attention(q_pre, k_pre, v_pre, sink):
    bsz, seqlen, _ = q_pre.shape
    nb = seqlen // AT_BLOCK
    span = 3 * AT_BLOCK
    q = q_pre.reshape(bsz, nb, AT_BLOCK, AT_KV_HEADS, AT_GROUP, AT_HEAD_DIM)

    def band(t):
        t = t.reshape(bsz, seqlen, AT_KV_HEADS, AT_HEAD_DIM)
        t = jnp.pad(t, ((0, 0), (AT_BLOCK, AT_BLOCK), (0, 0), (0, 0)))
        t = t.reshape(bsz, nb + 2, AT_BLOCK, AT_KV_HEADS, AT_HEAD_DIM)
        return jnp.concatenate([t[:, :-2], t[:, 1:-1], t[:, 2:]], axis=2)

    kb, vb = band(k_pre), band(v_pre)
    s = jnp.einsum('bnqkgd,bnskd->bnkgqs', q, kb).astype(jnp.float32) * (AT_HEAD_DIM ** -0.5)
    rel = (jnp.arange(AT_BLOCK)[:, None] + AT_BLOCK) - jnp.arange(span)[None, :]
    key_pos = jnp.arange(nb)[:, None] * AT_BLOCK - AT_BLOCK + jnp.arange(span)[None, :]
    valid = (jnp.abs(rel) <= WINDOW)[None] & ((key_pos >= 0) & (key_pos < seqlen))[:, None, :]
    slopes = jnp.exp2(-8.0 * jnp.arange(1, AT_Q_HEADS + 1, dtype=jnp.float32) / AT_Q_HEADS)
    alibi = (-slopes[:, None, None] * jnp.abs(rel).astype(jnp.float32)).reshape(AT_KV_HEADS, AT_GROUP, AT_BLOCK, span)
    s = jnp.where(valid[None, :, None, None], s + alibi, -jnp.inf)
    sink_l = sink.astype(jnp.float32).reshape(1, 1, AT_KV_HEADS, AT_GROUP, 1, 1)
    m = jnp.maximum(jnp.max(s, axis=-1, keepdims=True), sink_l)
    p = jnp.exp(s - m)
    p = p / (jnp.sum(p, axis=-1, keepdims=True) + jnp.exp(sink_l - m))
    o = jnp.einsum('bnkgqs,bnskd->bnqkgd', p.astype(vb.dtype), vb)
    return o.reshape(bsz, seqlen, AT_QWIDTH)


def moe_ffn(xn, w_router, b_router, w_up_gate, b_up_gate, w_down, b_down):
    bsz, seqlen, d = xn.shape
    n_tok = bsz * seqlen
    n_assign = n_tok * TOP_K
    n_pad = n_assign + N_EXPERTS * MOE_BLOCK
    n_blocks = n_pad // MOE_BLOCK
    xt = xn.reshape(n_tok, d)
    logits = xt.astype(jnp.float32) @ w_router.astype(jnp.float32) + b_router.astype(jnp.float32)
    top_val, top_idx = lax.top_k(logits, TOP_K)
    gates = jax.nn.softmax(top_val, axis=-1)
    flat_e = top_idx.reshape(-1).astype(jnp.int32)
    flat_tok = jnp.repeat(jnp.arange(n_tok, dtype=jnp.int32), TOP_K)
    order = jnp.argsort(flat_e)
    sorted_e = flat_e[order]
    counts = jnp.bincount(flat_e, length=N_EXPERTS).astype(jnp.int32)
    padded = (counts + MOE_BLOCK - 1) // MOE_BLOCK * MOE_BLOCK
    grp_start = jnp.cumsum(counts) - counts
    pad_end = jnp.cumsum(padded)
    pad_start = pad_end - padded
    dest = (pad_start[sorted_e] + jnp.arange(n_assign, dtype=jnp.int32) - grp_start[sorted_e]).astype(jnp.int32)
    rows_tok = jnp.zeros((n_pad,), jnp.int32).at[dest].set(flat_tok[order])
    block_expert = jnp.minimum(jnp.searchsorted(pad_end, jnp.arange(n_blocks, dtype=jnp.int32) * MOE_BLOCK, side='right'), N_EXPERTS - 1)
    xb = xt[rows_tok].reshape(n_blocks, MOE_BLOCK, d)

    def expert_block(args):
        xblk, e = args
        hu = xblk @ w_up_gate[e] + b_up_gate[e]
        g, u = hu[:, :D_FF_EXPERT], hu[:, D_FF_EXPERT:]
        g = jnp.minimum(g, SWIGLU_LIMIT)
        u = jnp.clip(u, -SWIGLU_LIMIT, SWIGLU_LIMIT)
        act = g * jax.nn.sigmoid(SWIGLU_ALPHA * g) * (u + 1.0)
        return act @ w_down[e] + b_down[e]

    yb = lax.map(expert_block, (xb, block_expert)).reshape(n_pad, d)
    dest_of_assign = jnp.zeros((n_assign,), jnp.int32).at[order].set(dest)
    y = yb[dest_of_assign].reshape(n_tok, TOP_K, d)
    out = jnp.einsum('tk,tkd->td', gates.astype(y.dtype), y)
    return out.reshape(bsz, seqlen, d)


def setup_inputs(seed: int = 0) -> dict:
    key = jax.random.key(seed)
    ks = jax.random.split(key, 16)
    f32 = jnp.float32

    def nrm(k, shape, scale):
        return scale * jax.random.normal(k, shape, f32)

    x = jax.random.normal(ks[0], (BATCH, SEQ, D_MODEL), f32)
    norm_mix_gain = 1.0 + nrm(ks[1], (DEPTH, D_MODEL), 0.01)
    w_in = nrm(ks[2], (DEPTH, D_MODEL, IN_WIDTH), D_MODEL ** -0.5)
    hg_lb_fwd = nrm(ks[3], (DEPTH + 1, HG_KWIDTH), 0.1)
    hg_lb_bwd = nrm(ks[4], (DEPTH + 1, HG_KWIDTH), 0.1)
    hg_norm_gain = 1.0 + nrm(ks[5], (DEPTH, HG_DV), 0.01)
    attn_sink = nrm(ks[6], (DEPTH, AT_Q_HEADS), 0.5)
    w_out = nrm(ks[7], (DEPTH, D_MODEL, D_MODEL), D_MODEL ** -0.5)
    norm_ffn_gain = 1.0 + nrm(ks[8], (DEPTH, D_MODEL), 0.01)
    w_router = nrm(ks[9], (DEPTH, D_MODEL, N_EXPERTS), D_MODEL ** -0.5)
    b_router = nrm(ks[10], (DEPTH, N_EXPERTS), 0.01)
    w_up_gate = nrm(ks[11], (DEPTH, N_EXPERTS, D_MODEL, 2 * D_FF_EXPERT), D_MODEL ** -0.5)
    b_up_gate = nrm(ks[12], (DEPTH, N_EXPERTS, 2 * D_FF_EXPERT), 0.01)
    w_down = nrm(ks[13], (DEPTH, N_EXPERTS, D_FF_EXPERT, D_MODEL), D_FF_EXPERT ** -0.5)
    b_down = nrm(ks[14], (DEPTH, N_EXPERTS, D_MODEL), 0.01)
    final_norm_gain = 1.0 + nrm(ks[15], (D_MODEL,), 0.01)
    return {'x': x, 'norm_mix_gain': norm_mix_gain, 'w_in': w_in, 'hg_lb_fwd': hg_lb_fwd,
            'hg_lb_bwd': hg_lb_bwd, 'hg_norm_gain': hg_norm_gain, 'attn_sink': attn_sink,
            'w_out': w_out, 'norm_ffn_gain': norm_ffn_gain, 'w_router': w_router,
            'b_router': b_router, 'w_up_gate': w_up_gate, 'b_up_gate': b_up_gate,
            'w_down': w_down, 'b_down': b_down, 'final_norm_gain': final_norm_gain}


def reference(x, norm_mix_gain, w_in, hg_lb_fwd, hg_lb_bwd, hg_norm_gain, attn_sink, w_out,
              norm_ffn_gain, w_router, b_router, w_up_gate, b_up_gate, w_down, b_down,
              final_norm_gain):
    lb_fwd_all = jnp.cumsum(jax.nn.softmax(hg_lb_fwd.astype(jnp.float32), axis=0), axis=0)
    lb_bwd_all = jnp.cumsum(jax.nn.softmax(hg_lb_bwd.astype(jnp.float32), axis=0), axis=0)
    h = x
    for layer in range(DEPTH):
        xn = rms_norm(h, norm_mix_gain[layer])
        proj = xn @ w_in[layer]
        hq, hf_fwd, hf_bwd, hi, hg, aq, ak, av, gate_a, gate_b = jnp.split(proj, IN_SPLITS, axis=-1)
        y_rec = hgrn2_mixer(hq, hf_fwd, hf_bwd, hi, hg, lb_fwd_all[layer], lb_bwd_all[layer], hg_norm_gain[layer])
        y_att = window_attention(aq, ak, av, attn_sink[layer])
        merged = jax.nn.sigmoid(gate_a) * y_rec + jax.nn.sigmoid(gate_b) * y_att
        h = h + merged @ w_out[layer]
        h = h + moe_ffn(rms_norm(h, norm_ffn_gain[layer]), w_router[layer], b_router[layer],
                        w_up_gate[layer], b_up_gate[layer], w_down[layer], b_down[layer])
    return rms_norm(h, final_norm_gain)
```

```python
import functools

import jax
import jax.numpy as jnp
from jax import lax
from jax.experimental import pallas as pl
from jax.experimental.pallas import tpu as pltpu

F32 = jnp.float32
BF16 = jnp.bfloat16

D_MODEL = 1024
HG_HEADS = 8
HG_DK = 128
HG_CHUNK = 128
AT_Q_HEADS = 16
AT_KV_HEADS = 4
AT_GROUP = AT_Q_HEADS // AT_KV_HEADS
AT_HEAD_DIM = 64
WINDOW = 128
AT_BLOCK = WINDOW
AT_KVWIDTH = AT_KV_HEADS * AT_HEAD_DIM
N_EXPERTS = 32
TOP_K = 4
D_FF = 1024
SWIGLU_LIMIT = 7.0
SWIGLU_ALPHA = 1.702
NORM_EPS = 1e-5
IN_WIDTH = 8704

COL_HQ, COL_FF, COL_FB, COL_HI, COL_HG = 0, 1024, 2048, 3072, 4096
COL_AQ, COL_AK, COL_AV, COL_GA, COL_GB = 5120, 6144, 6400, 6656, 7680

MOE_ROWS = 256
VMEM_LIMIT = 48 * 1024 * 1024

NT_DIMS = (((1,), (1,)), ((), ()))


def _sigmoid(x):
    return 1.0 / (1.0 + jnp.exp(-x))


def _inproj_body(x_ref, g_ref, w_ref, o_ref, xn_ref):
    @pl.when(pl.program_id(1) == 0)
    def _():
        x = x_ref[...]
        ms = jnp.mean(x * x, axis=-1, keepdims=True)
        xn_ref[...] = (x * lax.rsqrt(ms + NORM_EPS) * g_ref[...]).astype(BF16)

    o_ref[...] = jnp.dot(xn_ref[...], w_ref[...], preferred_element_type=F32)


def _inproj(x2d, gain, w_bf16, tm=1024, tn=512):
    n_tok = x2d.shape[0]
    n_out = w_bf16.shape[1]
    return pl.pallas_call(
        _inproj_body,
        grid=(n_tok // tm, n_out // tn),
        in_specs=[
            pl.BlockSpec((tm, D_MODEL), lambda i, j: (i, 0)),
            pl.BlockSpec((1, D_MODEL), lambda i, j: (0, 0)),
            pl.BlockSpec((D_MODEL, tn), lambda i, j: (0, j)),
        ],
        out_specs=pl.BlockSpec((tm, tn), lambda i, j: (i, j)),
        out_shape=jax.ShapeDtypeStruct((n_tok, n_out), F32),
        scratch_shapes=[pltpu.VMEM((tm, D_MODEL), BF16)],
        compiler_params=pltpu.CompilerParams(
            dimension_semantics=("parallel", "arbitrary"), vmem_limit_bytes=VMEM_LIMIT),
        name="inproj",
    )(x2d, gain.reshape(1, D_MODEL), w_bf16)


def _split_dot(tri_bf16, x):
    hi = x.astype(BF16)
    lo = (x - hi.astype(F32)).astype(BF16)
    return (jnp.dot(tri_bf16, hi, preferred_element_type=F32)
            + jnp.dot(tri_bf16, lo, preferred_element_type=F32))


def _gla_chunk(q, f_pre, v, lb, st_ref, tri_bf16, mask, mid_row, last_row):
    f = lb + (1.0 - lb) * _sigmoid(f_pre)
    k = 1.0 - f
    cum = _split_dot(tri_bf16, jnp.log(f))
    mid = cum[mid_row:mid_row + 1, :]
    last = cum[last_row:last_row + 1, :]
    a = cum - mid
    qa = q * jnp.exp(a)
    ka = k * jnp.exp(-a)
    qd = (qa * jnp.exp(mid)).astype(BF16)
    kend = (ka * jnp.exp(last - mid)).astype(BF16)
    dec = jnp.exp(last)
    v_bf = v.astype(BF16)
    scores = lax.dot_general(qa.astype(BF16), ka.astype(BF16), NT_DIMS,
                             preferred_element_type=F32)
    scores = jnp.where(mask, scores, 0.0).astype(BF16)
    st = st_ref[...]
    o = (jnp.dot(scores, v_bf, preferred_element_type=F32)
         + lax.dot_general(qd, st.astype(BF16), NT_DIMS, preferred_element_type=F32))
    vt = jnp.transpose(v).astype(BF16)
    st_ref[...] = st * dec + jnp.dot(vt, kend, preferred_element_type=F32)
    return o


def _hgrn2_body(qf_ref, ff_ref, vf_ref, qb_ref, fb_ref, vb_ref, lbf_ref, lbb_ref,
                of_ref, ob_ref, sf_ref, sb_ref):
    @pl.when(pl.program_id(2) == 0)
    def _():
        sf_ref[...] = jnp.zeros_like(sf_ref)
        sb_ref[...] = jnp.zeros_like(sb_ref)

    c = HG_CHUNK
    n_chunks = qf_ref.shape[0] // c
    row = lax.broadcasted_iota(jnp.int32, (c, c), 0)
    col = lax.broadcasted_iota(jnp.int32, (c, c), 1)
    lower = col <= row
    upper = col >= row
    tri_lower = jnp.where(lower, 1.0, 0.0).astype(BF16)
    tri_upper = jnp.where(upper, 1.0, 0.0).astype(BF16)
    lbf = lbf_ref[...]
    lbb = lbb_ref[...]

    def step(n, carry):
        rf = pl.multiple_of(n * c, c)
        rb = pl.multiple_of((n_chunks - 1 - n) * c, c)
        qf = qf_ref[pl.ds(rf, c), :]
        qf = qf * _sigmoid(qf)
        of_ref[pl.ds(rf, c), :] = _gla_chunk(
            qf, ff_ref[pl.ds(rf, c), :], vf_ref[pl.ds(rf, c), :], lbf, sf_ref,
            tri_lower, lower, c // 2 - 1, c - 1)
        qb = qb_ref[pl.ds(rb, c), :]
        qb = qb * _sigmoid(qb)
        ob_ref[pl.ds(rb, c), :] = _gla_chunk(
            qb, fb_ref[pl.ds(rb, c), :], vb_ref[pl.ds(rb, c), :], lbb, sb_ref,
            tri_upper, upper, c // 2, 0)
        return carry

    lax.fori_loop(0, n_chunks, step, 0)


def _hgrn2(proj, lb_fwd, lb_bwd, bsz, seqlen, rows=1024):
    n_tok = bsz * seqlen
    rows = min(rows, seqlen)
    nb = seqlen // rows
    dk = HG_DK

    def fwd_map(col0):
        return lambda b, h, j: (b * nb + j, col0 // dk + h)

    def bwd_map(col0):
        return lambda b, h, j: (b * nb + (nb - 1 - j), col0 // dk + h)

    blk = (rows, dk)
    lb_spec = pl.BlockSpec((1, dk), lambda b, h, j: (0, h))
    return pl.pallas_call(
        _hgrn2_body,
        grid=(bsz, HG_HEADS, nb),
        in_specs=[
            pl.BlockSpec(blk, fwd_map(COL_HQ)), pl.BlockSpec(blk, fwd_map(COL_FF)),
            pl.BlockSpec(blk, fwd_map(COL_HI)),
            pl.BlockSpec(blk, bwd_map(COL_HQ)), pl.BlockSpec(blk, bwd_map(COL_FB)),
            pl.BlockSpec(blk, bwd_map(COL_HI)),
            lb_spec, lb_spec,
        ],
        out_specs=[pl.BlockSpec(blk, fwd_map(0)), pl.BlockSpec(blk, bwd_map(0))],
        out_shape=[jax.ShapeDtypeStruct((n_tok, HG_HEADS * dk), F32)] * 2,
        scratch_shapes=[pltpu.VMEM((dk, dk), F32), pltpu.VMEM((dk, dk), F32)],
        compiler_params=pltpu.CompilerParams(
            dimension_semantics=("parallel", "parallel", "arbitrary"),
            vmem_limit_bytes=VMEM_LIMIT),
        name="hgrn2",
    )(proj, proj, proj, proj, proj, proj, lb_fwd.reshape(1, -1), lb_bwd.reshape(1, -1))


def _attn_body(seqlen, sink_ref, q_ref, kp_ref, ko_ref, kn_ref, vp_ref, vo_ref, vn_ref, o_ref):
    j = pl.program_id(1)
    blk = AT_BLOCK
    span = 3 * blk
    row = lax.broadcasted_iota(jnp.int32, (blk, span), 0)
    col = lax.broadcasted_iota(jnp.int32, (blk, span), 1)
    absrel = jnp.abs(row + blk - col)
    key_pos = j * blk - blk + col
    valid = (absrel <= WINDOW) & (key_pos >= 0) & (key_pos < seqlen)
    negabs = -absrel.astype(F32)
    dh = AT_HEAD_DIM
    outs = []
    for g in range(AT_KV_HEADS):
        ks = slice(g * dh, (g + 1) * dh)
        kb = jnp.concatenate([kp_ref[:, ks], ko_ref[:, ks], kn_ref[:, ks]], axis=0).astype(BF16)
        vb = jnp.concatenate([vp_ref[:, ks], vo_ref[:, ks], vn_ref[:, ks]], axis=0).astype(BF16)
        for u in range(AT_GROUP):
            h = g * AT_GROUP + u
            slope = 2.0 ** (-8.0 * (h + 1) / AT_Q_HEADS)
            sink = sink_ref[h]
            q = (q_ref[:, h * dh:(h + 1) * dh] * (dh ** -0.5)).astype(BF16)
            s = lax.dot_general(q, kb, NT_DIMS, preferred_element_type=F32)
            s = jnp.where(valid, s + slope * negabs, -jnp.inf)
            m = jnp.maximum(jnp.max(s, axis=-1, keepdims=True), sink)
            p = jnp.exp(s - m)
            denom = jnp.sum(p, axis=-1, keepdims=True) + jnp.exp(sink - m)
            o = jnp.dot(p.astype(BF16), vb, preferred_element_type=F32)
            outs.append(o / denom)
    o_ref[...] = jnp.concatenate(outs, axis=-1)


def _attention(proj, sink, bsz, seqlen):
    n_tok = bsz * seqlen
    nb = seqlen // AT_BLOCK
    kcol = COL_AK // AT_KVWIDTH
    vcol = COL_AV // AT_KVWIDTH

    def prev(c):
        return lambda b, j: (b * nb + jnp.maximum(j - 1, 0), c)

    def own(c):
        return lambda b, j: (b * nb + j, c)

    def nxt(c):
        return lambda b, j: (b * nb + jnp.minimum(j + 1, nb - 1), c)

    kv_blk = (AT_BLOCK, AT_KVWIDTH)
    return pl.pallas_call(
        functools.partial(_attn_body, seqlen),
        grid=(bsz, nb),
        in_specs=[
            pl.BlockSpec(memory_space=pltpu.SMEM),
            pl.BlockSpec((AT_BLOCK, D_MODEL), own(COL_AQ // D_MODEL)),
            pl.BlockSpec(kv_blk, prev(kcol)), pl.BlockSpec(kv_blk, own(kcol)),
            pl.BlockSpec(kv_blk, nxt(kcol)),
            pl.BlockSpec(kv_blk, prev(vcol)), pl.BlockSpec(kv_blk, own(vcol)),
            pl.BlockSpec(kv_blk, nxt(vcol)),
        ],
        out_specs=pl.BlockSpec((AT_BLOCK, D_MODEL), lambda b, j: (b * nb + j, 0)),
        out_shape=jax.ShapeDtypeStruct((n_tok, D_MODEL), F32),
        compiler_params=pltpu.CompilerParams(
            dimension_semantics=("parallel", "arbitrary"), vmem_limit_bytes=VMEM_LIMIT),
        name="window_attn",
    )(sink.astype(F32), proj, proj, proj, proj, proj, proj, proj)


def _merge_body(of_ref, ob_ref, hg_ref, ga0_ref, ga1_ref, gb0_ref, gb1_ref, ya_ref, x_ref,
                gn_ref, wo_ref, g2_ref, wr_ref, br_ref,
                h_ref, xn_ref, idx_ref, gate_ref, rank_ref, cnt_ref, carry_ref):
    i = pl.program_id(0)

    @pl.when(i == 0)
    def _():
        carry_ref[...] = jnp.zeros_like(carry_ref)

    tm = x_ref.shape[0]
    o = of_ref[...] + ob_ref[...]
    gn = gn_ref[...]
    parts = []
    for h in range(HG_HEADS):
        oh = o[:, h * HG_DK:(h + 1) * HG_DK]
        ms = jnp.mean(oh * oh, axis=-1, keepdims=True)
        parts.append(oh * lax.rsqrt(ms + NORM_EPS) * gn)
    hg = hg_ref[...]
    y_rec = jnp.concatenate(parts, axis=-1) * (hg * _sigmoid(hg))
    gate_a = jnp.concatenate([ga0_ref[...], ga1_ref[...]], axis=-1)
    gate_b = jnp.concatenate([gb0_ref[...], gb1_ref[...]], axis=-1)
    merged = _sigmoid(gate_a) * y_rec + _sigmoid(gate_b) * ya_ref[...]
    h_res = x_ref[...] + jnp.dot(merged.astype(BF16), wo_ref[...], preferred_element_type=F32)
    h_ref[...] = h_res
    ms = jnp.mean(h_res * h_res, axis=-1, keepdims=True)
    xn = h_res * lax.rsqrt(ms + NORM_EPS) * g2_ref[...]
    xn_ref[...] = xn

    logits = lax.dot_general(wr_ref[...], xn, NT_DIMS, preferred_element_type=F32,
                             precision=lax.Precision.HIGHEST) + br_ref[...]
    e_iota = lax.broadcasted_iota(jnp.int32, (N_EXPERTS, tm), 0).astype(F32)
    vals = logits
    tops, idxs = [], []
    for _ in range(TOP_K):
        m = jnp.max(vals, axis=0, keepdims=True)
        sel = jnp.min(jnp.where(vals == m, e_iota, float(N_EXPERTS)), axis=0, keepdims=True)
        tops.append(m)
        idxs.append(sel)
        vals = jnp.where(e_iota == sel, -jnp.inf, vals)
    exps = [jnp.exp(t - tops[0]) for t in tops]
    den = exps[0] + exps[1] + exps[2] + exps[3]
    gate_ref[...] = jnp.concatenate([e / den for e in exps], axis=0)
    idx_ref[...] = jnp.concatenate(idxs, axis=0).astype(jnp.int32)

    chosen = jnp.where(vals == -jnp.inf, 1.0, 0.0)
    s_iota = lax.broadcasted_iota(jnp.int32, (tm, tm), 0)
    t_iota = lax.broadcasted_iota(jnp.int32, (tm, tm), 1)
    strict = jnp.where(s_iota < t_iota, 1.0, 0.0).astype(BF16)
    carry = carry_ref[...]
    prefix = jnp.dot(chosen.astype(BF16), strict, preferred_element_type=F32) + carry
    ranks = [jnp.sum(jnp.where(e_iota == sel, prefix, 0.0), axis=0, keepdims=True) for sel in idxs]
    rank_ref[...] = jnp.concatenate(ranks, axis=0).astype(jnp.int32)
    carry = carry + jnp.sum(chosen, axis=1, keepdims=True)
    carry_ref[...] = carry
    cnt_ref[...] = jnp.broadcast_to(carry, cnt_ref.shape).astype(jnp.int32)


def _merge_route(o_f, o_b, proj, y_att, x2d, gn_gain, w_out_bf16, g2, w_router_t, b_router, tm=256):
    n_tok = x2d.shape[0]
    nblk = n_tok // tm
    row_blk = (tm, D_MODEL)
    half = D_MODEL // 2
    half_blk = (tm, half)

    def rows(c):
        return lambda i: (i, c)

    full = lambda i: (0, 0)
    tok_out = pl.BlockSpec((TOP_K, tm), lambda i: (0, i))
    return pl.pallas_call(
        _merge_body,
        grid=(nblk,),
        in_specs=[
            pl.BlockSpec(row_blk, rows(0)), pl.BlockSpec(row_blk, rows(0)),
            pl.BlockSpec(row_blk, rows(COL_HG // D_MODEL)),
            pl.BlockSpec(half_blk, rows(COL_GA // half)),
            pl.BlockSpec(half_blk, rows(COL_GA // half + 1)),
            pl.BlockSpec(half_blk, rows(COL_GB // half)),
            pl.BlockSpec(half_blk, rows(COL_GB // half + 1)),
            pl.BlockSpec(row_blk, rows(0)), pl.BlockSpec(row_blk, rows(0)),
            pl.BlockSpec((1, HG_DK), full),
            pl.BlockSpec((D_MODEL, D_MODEL), full),
            pl.BlockSpec((1, D_MODEL), full),
            pl.BlockSpec((N_EXPERTS, D_MODEL), full),
            pl.BlockSpec((N_EXPERTS, 1), full),
        ],
        out_specs=[
            pl.BlockSpec(row_blk, rows(0)), pl.BlockSpec(row_blk, rows(0)),
            tok_out, tok_out, tok_out,
            pl.BlockSpec((N_EXPERTS, 128), full),
        ],
        out_shape=[
            jax.ShapeDtypeStruct((n_tok, D_MODEL), F32),
            jax.ShapeDtypeStruct((n_tok, D_MODEL), F32),
            jax.ShapeDtypeStruct((TOP_K, n_tok), jnp.int32),
            jax.ShapeDtypeStruct((TOP_K, n_tok), F32),
            jax.ShapeDtypeStruct((TOP_K, n_tok), jnp.int32),
            jax.ShapeDtypeStruct((N_EXPERTS, 128), jnp.int32),
        ],
        scratch_shapes=[pltpu.VMEM((N_EXPERTS, 1), F32)],
        compiler_params=pltpu.CompilerParams(
            dimension_semantics=("arbitrary",), vmem_limit_bytes=VMEM_LIMIT),
        name="merge_route",
    )(o_f, o_b, proj, proj, proj, proj, proj, y_att, x2d, gn_gain.reshape(1, HG_DK), w_out_bf16,
      g2.reshape(1, D_MODEL), w_router_t, b_router.reshape(N_EXPERTS, 1))


DISPATCH_TOKENS = 64


def _dispatch_body(dest_ref, pend_ref, x_hbm, xb_hbm, zero_ref, sem):
    i = pl.program_id(0)
    n_steps = pl.num_programs(0)
    rows = MOE_ROWS

    def row_copy(t, k):
        return pltpu.make_async_copy(
            x_hbm.at[pl.ds(t, 1)], xb_hbm.at[pl.ds(dest_ref[t * TOP_K + k], 1)], sem.at[0])

    def wait_rows(n):
        def body(_, c):
            pltpu.make_async_copy(x_hbm.at[pl.ds(0, 1)], xb_hbm.at[pl.ds(0, 1)], sem.at[0]).wait()
            return c
        lax.fori_loop(0, n, body, 0)

    @pl.when(i == 0)
    def _():
        zero_ref[...] = jnp.zeros_like(zero_ref)
        for e in range(N_EXPERTS):
            end = pend_ref[e]
            start = pl.multiple_of(end - rows, rows)
            prev_end = pend_ref[e - 1] if e > 0 else 0

            @pl.when(end > prev_end)
            def _():
                cp = pltpu.make_async_copy(zero_ref, xb_hbm.at[pl.ds(start, rows)], sem.at[1])
                cp.start()
                cp.wait()

        def zero_block(b, c):
            cp = pltpu.make_async_copy(
                zero_ref, xb_hbm.at[pl.ds(pl.multiple_of(b * rows, rows), rows)], sem.at[1])
            cp.start()
            cp.wait()
            return c
        lax.fori_loop(pend_ref[N_EXPERTS - 1] // rows, xb_hbm.shape[0] // rows, zero_block, 0)

    @pl.when(i > 0)
    def _():
        wait_rows(DISPATCH_TOKENS * TOP_K)

    base = i * DISPATCH_TOKENS

    def issue(r, c):
        for k in range(TOP_K):
            row_copy(base + r, k).start()
        return c

    lax.fori_loop(0, DISPATCH_TOKENS, issue, 0)

    @pl.when(i == n_steps - 1)
    def _():
        wait_rows(DISPATCH_TOKENS * TOP_K)


def _dispatch(xn, dest_flat, pad_end, n_pad):
    n_tok = xn.shape[0]
    grid_spec = pltpu.PrefetchScalarGridSpec(
        num_scalar_prefetch=2,
        grid=(n_tok // DISPATCH_TOKENS,),
        in_specs=[pl.BlockSpec(memory_space=pl.ANY)],
        out_specs=pl.BlockSpec(memory_space=pl.ANY),
        scratch_shapes=[pltpu.VMEM((MOE_ROWS, D_MODEL), F32), pltpu.SemaphoreType.DMA((2,))],
    )
    return pl.pallas_call(
        _dispatch_body,
        grid_spec=grid_spec,
        out_shape=jax.ShapeDtypeStruct((n_pad, D_MODEL), F32),
        compiler_params=pltpu.CompilerParams(dimension_semantics=("arbitrary",)),
        name="moe_dispatch",
    )(dest_flat, pad_end, xn)


def _expert_body(be_ref, nu_ref, x_ref, wug_ref, bug_ref, wd_ref, bd_ref, y_ref, wug_bf, wd_bf):
    i = pl.program_id(0)
    prev = be_ref[jnp.maximum(i - 1, 0)]

    @pl.when((i == 0) | (be_ref[i] != prev))
    def _():
        wug_bf[...] = wug_ref[0].astype(BF16)
        wd_bf[...] = wd_ref[0].astype(BF16)

    @pl.when(i < nu_ref[0])
    def _():
        x = x_ref[...].astype(BF16)
        hu = jnp.dot(x, wug_bf[...], preferred_element_type=F32) + bug_ref[0]
        g = jnp.minimum(hu[:, :D_FF], SWIGLU_LIMIT)
        u = jnp.clip(hu[:, D_FF:], -SWIGLU_LIMIT, SWIGLU_LIMIT)
        act = g * _sigmoid(SWIGLU_ALPHA * g) * (u + 1.0)
        y_ref[...] = jnp.dot(act.astype(BF16), wd_bf[...], preferred_element_type=F32) + bd_ref[0]

    @pl.when(i >= nu_ref[0])
    def _():
        y_ref[...] = jnp.zeros_like(y_ref)


def _experts(xb, block_expert, n_used, w_up_gate, b_up_gate, w_down, b_down):
    n_pad = xb.shape[0]
    n_blocks = n_pad // MOE_ROWS

    def x_map(i, be, nu):
        return (jnp.where(i < nu[0], i, 0), 0)

    def w_map(i, be, nu):
        return (be[i], 0, 0)

    grid_spec = pltpu.PrefetchScalarGridSpec(
        num_scalar_prefetch=2,
        grid=(n_blocks,),
        in_specs=[
            pl.BlockSpec((MOE_ROWS, D_MODEL), x_map),
            pl.BlockSpec((1, D_MODEL, 2 * D_FF), w_map),
            pl.BlockSpec((1, 1, 2 * D_FF), w_map),
            pl.BlockSpec((1, D_FF, D_MODEL), w_map),
            pl.BlockSpec((1, 1, D_MODEL), w_map),
        ],
        out_specs=pl.BlockSpec((MOE_ROWS, D_MODEL), lambda i, be, nu: (i, 0)),
        scratch_shapes=[pltpu.VMEM((D_MODEL, 2 * D_FF), BF16), pltpu.VMEM((D_FF, D_MODEL), BF16)],
    )
    return pl.pallas_call(
        _expert_body,
        grid_spec=grid_spec,
        out_shape=jax.ShapeDtypeStruct((n_pad, D_MODEL), F32),
        compiler_params=pltpu.CompilerParams(
            dimension_semantics=("arbitrary",), vmem_limit_bytes=VMEM_LIMIT),
        name="moe_experts",
    )(block_expert, n_used, xb, w_up_gate, b_up_gate.reshape(N_EXPERTS, 1, -1),
      w_down, b_down.reshape(N_EXPERTS, 1, -1))


COMBINE_TOKENS = 64


def _combine_body(dest_ref, yb_hbm, h_ref, gate_ref, gain_ref, o_ref, buf_ref, sem):
    i = pl.program_id(0)
    n_steps = pl.num_programs(0)
    nt = COMBINE_TOKENS

    def issue(step, slot):
        base = step * nt

        def body(r, c):
            for k in range(TOP_K):
                pltpu.make_async_copy(
                    yb_hbm.at[pl.ds(dest_ref[(base + r) * TOP_K + k], 1)],
                    buf_ref.at[slot, k, pl.ds(r, 1)], sem.at[slot]).start()
            return c
        lax.fori_loop(0, nt, body, 0)

    slot = lax.rem(i, 2)

    @pl.when(i == 0)
    def _():
        issue(0, 0)

    @pl.when(i + 1 < n_steps)
    def _():
        issue(i + 1, 1 - slot)

    def wait_body(_, c):
        pltpu.make_async_copy(yb_hbm.at[pl.ds(0, 1)], buf_ref.at[slot, 0, pl.ds(0, 1)],
                              sem.at[slot]).wait()
        return c
    lax.fori_loop(0, nt * TOP_K, wait_body, 0)

    gates = gate_ref[...]
    acc = h_ref[...]
    for k in range(TOP_K):
        acc = acc + gates[:, k:k + 1] * buf_ref[slot, k]
    ms = jnp.mean(acc * acc, axis=-1, keepdims=True)
    o_ref[...] = acc * lax.rsqrt(ms + NORM_EPS) * gain_ref[...]


def _combine(yb, dest_flat, h_res, gates_tk, final_gain):
    n_tok = h_res.shape[0]
    nt = COMBINE_TOKENS
    grid_spec = pltpu.PrefetchScalarGridSpec(
        num_scalar_prefetch=1,
        grid=(n_tok // nt,),
        in_specs=[
            pl.BlockSpec(memory_space=pl.ANY),
            pl.BlockSpec((nt, D_MODEL), lambda i, d: (i, 0)),
            pl.BlockSpec((nt, TOP_K), lambda i, d: (i, 0)),
            pl.BlockSpec((1, D_MODEL), lambda i, d: (0, 0)),
        ],
        out_specs=pl.BlockSpec((nt, D_MODEL), lambda i, d: (i, 0)),
        scratch_shapes=[pltpu.VMEM((2, TOP_K, nt, D_MODEL), F32), pltpu.SemaphoreType.DMA((2,))],
    )
    return pl.pallas_call(
        _combine_body,
        grid_spec=grid_spec,
        out_shape=jax.ShapeDtypeStruct((n_tok, D_MODEL), F32),
        compiler_params=pltpu.CompilerParams(
            dimension_semantics=("arbitrary",), vmem_limit_bytes=VMEM_LIMIT),
        name="moe_combine",
    )(dest_flat, yb, h_res, gates_tk, final_gain.reshape(1, D_MODEL))


def kernel(x, norm_mix_gain, w_in, hg_lb_fwd, hg_lb_bwd, hg_norm_gain, attn_sink, w_out,
           norm_ffn_gain, w_router, b_router, w_up_gate, b_up_gate, w_down, b_down,
           final_norm_gain):
    bsz, seqlen, d = x.shape
    n_tok = bsz * seqlen
    x2d = x.reshape(n_tok, d)

    lb_fwd = jax.nn.softmax(hg_lb_fwd.astype(F32), axis=0)[0]
    lb_bwd = jax.nn.softmax(hg_lb_bwd.astype(F32), axis=0)[0]

    proj = _inproj(x2d, norm_mix_gain[0], w_in[0].astype(BF16))
    o_f, o_b = _hgrn2(proj, lb_fwd, lb_bwd, bsz, seqlen)
    y_att = _attention(proj, attn_sink[0], bsz, seqlen)
    h_res, xn, idx_t, gate_t, rank_t, counts = _merge_route(
        o_f, o_b, proj, y_att, x2d, hg_norm_gain[0], w_out[0].astype(BF16),
        norm_ffn_gain[0], w_router[0].T.astype(F32), b_router[0].astype(F32))

    counts = counts[:, 0]
    padded = (counts + MOE_ROWS - 1) // MOE_ROWS * MOE_ROWS
    pad_end = jnp.cumsum(padded).astype(jnp.int32)
    pad_start = pad_end - padded
    n_pad = n_tok * TOP_K + N_EXPERTS * MOE_ROWS
    n_blocks = n_pad // MOE_ROWS
    dest_flat = (pad_start[idx_t] + rank_t).T.reshape(-1).astype(jnp.int32)
    block_expert = jnp.minimum(
        jnp.searchsorted(pad_end, jnp.arange(n_blocks, dtype=jnp.int32) * MOE_ROWS, side='right'),
        N_EXPERTS - 1).astype(jnp.int32)
    n_used = (pad_end[-1:] // MOE_ROWS).astype(jnp.int32)

    xb = _dispatch(xn, dest_flat, pad_end, n_pad)
    yb = _experts(xb, block_expert, n_used, w_up_gate[0], b_up_gate[0], w_down[0], b_down[0])
    out = _combine(yb, dest_flat, h_res, gate_t.T, final_norm_gain)
    return out.reshape(bsz, seqlen, d)
```

```python
import functools

import jax
import jax.numpy as jnp
from jax import lax
from jax.experimental import pallas as pl
from jax.experimental.pallas import tpu as pltpu

F32 = jnp.float32
BF16 = jnp.bfloat16

D_MODEL = 1024
HG_HEADS = 8
HG_DK = 128
HG_CHUNK = 128
AT_Q_HEADS = 16
AT_KV_HEADS = 4
AT_GROUP = AT_Q_HEADS // AT_KV_HEADS
AT_HEAD_DIM = 64
WINDOW = 128
AT_BLOCK = WINDOW
AT_KVWIDTH = AT_KV_HEADS * AT_HEAD_DIM
N_EXPERTS = 32
TOP_K = 4
D_FF = 1024
SWIGLU_LIMIT = 7.0
SWIGLU_ALPHA = 1.702
NORM_EPS = 1e-5
IN_WIDTH = 8704

COL_HQ, COL_FF, COL_FB, COL_HI, COL_HG = 0, 1024, 2048, 3072, 4096
COL_AQ, COL_AK, COL_AV, COL_GA, COL_GB = 5120, 6144, 6400, 6656, 7680

MOE_ROWS = 256
VMEM_LIMIT = 48 * 1024 * 1024

NT_DIMS = (((1,), (1,)), ((), ()))


def _sigmoid(x):
    return 1.0 / (1.0 + jnp.exp(-x))


def _inproj_body(x_ref, g_ref, w_ref, o_ref, xn_ref):
    @pl.when(pl.program_id(1) == 0)
    def _():
        x = x_ref[...]
        ms = jnp.mean(x * x, axis=-1, keepdims=True)
        xn_ref[...] = (x * lax.rsqrt(ms + NORM_EPS) * g_ref[...]).astype(BF16)

    o_ref[...] = jnp.dot(xn_ref[...], w_ref[...], preferred_element_type=F32)


def _inproj(x2d, gain, w_bf16, tm=1024, tn=512):
    n_tok = x2d.shape[0]
    n_out = w_bf16.shape[1]
    return pl.pallas_call(
        _inproj_body,
        grid=(n_tok // tm, n_out // tn),
        in_specs=[
            pl.BlockSpec((tm, D_MODEL), lambda i, j: (i, 0)),
            pl.BlockSpec((1, D_MODEL), lambda i, j: (0, 0)),
            pl.BlockSpec((D_MODEL, tn), lambda i, j: (0, j)),
        ],
        out_specs=pl.BlockSpec((tm, tn), lambda i, j: (i, j)),
        out_shape=jax.ShapeDtypeStruct((n_tok, n_out), F32),
        scratch_shapes=[pltpu.VMEM((tm, D_MODEL), BF16)],
        compiler_params=pltpu.CompilerParams(
            dimension_semantics=("parallel", "arbitrary"), vmem_limit_bytes=VMEM_LIMIT),
        name="inproj",
    )(x2d, gain.reshape(1, D_MODEL), w_bf16)


def _split_dot(tri_bf16, x):
    hi = x.astype(BF16)
    lo = (x - hi.astype(F32)).astype(BF16)
    return (jnp.dot(tri_bf16, hi, preferred_element_type=F32)
            + jnp.dot(tri_bf16, lo, preferred_element_type=F32))


def _gla_chunk(q, f_pre, v, lb, st_ref, tri_bf16, mask, mid_row, last_row):
    f = lb + (1.0 - lb) * _sigmoid(f_pre)
    k = 1.0 - f
    cum = _split_dot(tri_bf16, jnp.log(f))
    mid = cum[mid_row:mid_row + 1, :]
    last = cum[last_row:last_row + 1, :]
    a = cum - mid
    qa = q * jnp.exp(a)
    ka = k * jnp.exp(-a)
    qd = (qa * jnp.exp(mid)).astype(BF16)
    kend = (ka * jnp.exp(last - mid)).astype(BF16)
    dec = jnp.exp(last)
    v_bf = v.astype(BF16)
    scores = lax.dot_general(qa.astype(BF16), ka.astype(BF16), NT_DIMS,
                             preferred_element_type=F32)
    scores = jnp.where(mask, scores, 0.0).astype(BF16)
    st = st_ref[...]
    o = (jnp.dot(scores, v_bf, preferred_element_type=F32)
         + lax.dot_general(qd, st.astype(BF16), NT_DIMS, preferred_element_type=F32))
    vt = jnp.transpose(v).astype(BF16)
    st_ref[...] = st * dec + jnp.dot(vt, kend, preferred_element_type=F32)
    return o


def _hgrn2_body(qf_ref, ff_ref, vf_ref, qb_ref, fb_ref, vb_ref, lbf_ref, lbb_ref,
                of_ref, ob_ref, sf_ref, sb_ref):
    @pl.when(pl.program_id(2) == 0)
    def _():
        sf_ref[...] = jnp.zeros_like(sf_ref)
        sb_ref[...] = jnp.zeros_like(sb_ref)

    c = HG_CHUNK
    n_chunks = qf_ref.shape[0] // c
    row = lax.broadcasted_iota(jnp.int32, (c, c), 0)
    col = lax.broadcasted_iota(jnp.int32, (c, c), 1)
    lower = col <= row
    upper = col >= row
    tri_lower = jnp.where(lower, 1.0, 0.0).astype(BF16)
    tri_upper = jnp.where(upper, 1.0, 0.0).astype(BF16)
    lbf = lbf_ref[...]
    lbb = lbb_ref[...]

    def step(n, carry):
        rf = pl.multiple_of(n * c, c)
        rb = pl.multiple_of((n_chunks - 1 - n) * c, c)
        qf = qf_ref[pl.ds(rf, c), :]
        qf = qf * _sigmoid(qf)
        of_ref[pl.ds(rf, c), :] = _gla_chunk(
            qf, ff_ref[pl.ds(rf, c), :], vf_ref[pl.ds(rf, c), :], lbf, sf_ref,
            tri_lower, lower, c // 2 - 1, c - 1)
        qb = qb_ref[pl.ds(rb, c), :]
        qb = qb * _sigmoid(qb)
        ob_ref[pl.ds(rb, c), :] = _gla_chunk(
            qb, fb_ref[pl.ds(rb, c), :], vb_ref[pl.ds(rb, c), :], lbb, sb_ref,
            tri_upper, upper, c // 2, 0)
        return carry

    lax.fori_loop(0, n_chunks, step, 0)


def _hgrn2(proj, lb_fwd, lb_bwd, bsz, seqlen, rows=1024):
    n_tok = bsz * seqlen
    rows = min(rows, seqlen)
    nb = seqlen // rows
    dk = HG_DK

    def fwd_map(col0):
        return lambda b, h, j: (b * nb + j, col0 // dk + h)

    def bwd_map(col0):
        return lambda b, h, j: (b * nb + (nb - 1 - j), col0 // dk + h)

    blk = (rows, dk)
    lb_spec = pl.BlockSpec((1, dk), lambda b, h, j: (0, h))
    return pl.pallas_call(
        _hgrn2_body,
        grid=(bsz, HG_HEADS, nb),
        in_specs=[
            pl.BlockSpec(blk, fwd_map(COL_HQ)), pl.BlockSpec(blk, fwd_map(COL_FF)),
            pl.BlockSpec(blk, fwd_map(COL_HI)),
            pl.BlockSpec(blk, bwd_map(COL_HQ)), pl.BlockSpec(blk, bwd_map(COL_FB)),
            pl.BlockSpec(blk, bwd_map(COL_HI)),
            lb_spec, lb_spec,
        ],
        out_specs=[pl.BlockSpec(blk, fwd_map(0)), pl.BlockSpec(blk, bwd_map(0))],
        out_shape=[jax.ShapeDtypeStruct((n_tok, HG_HEADS * dk), F32)] * 2,
        scratch_shapes=[pltpu.VMEM((dk, dk), F32), pltpu.VMEM((dk, dk), F32)],
        compiler_params=pltpu.CompilerParams(
            dimension_semantics=("parallel", "parallel", "arbitrary"),
            vmem_limit_bytes=VMEM_LIMIT),
        name="hgrn2",
    )(proj, proj, proj, proj, proj, proj, lb_fwd.reshape(1, -1), lb_bwd.reshape(1, -1))


def _attn_body(seqlen, sink_ref, q_ref, kp_ref, ko_ref, kn_ref, vp_ref, vo_ref, vn_ref, o_ref):
    j = pl.program_id(1)
    blk = AT_BLOCK
    span = 3 * blk
    row = lax.broadcasted_iota(jnp.int32, (blk, span), 0)
    col = lax.broadcasted_iota(jnp.int32, (blk, span), 1)
    absrel = jnp.abs(row + blk - col)
    key_pos = j * blk - blk + col
    valid = (absrel <= WINDOW) & (key_pos >= 0) & (key_pos < seqlen)
    negabs = -absrel.astype(F32)
    dh = AT_HEAD_DIM
    outs = []
    for g in range(AT_KV_HEADS):
        ks = slice(g * dh, (g + 1) * dh)
        kb = jnp.concatenate([kp_ref[:, ks], ko_ref[:, ks], kn_ref[:, ks]], axis=0).astype(BF16)
        vb = jnp.concatenate([vp_ref[:, ks], vo_ref[:, ks], vn_ref[:, ks]], axis=0).astype(BF16)
        for u in range(AT_GROUP):
            h = g * AT_GROUP + u
            slope = 2.0 ** (-8.0 * (h + 1) / AT_Q_HEADS)
            sink = sink_ref[h]
            q = (q_ref[:, h * dh:(h + 1) * dh] * (dh ** -0.5)).astype(BF16)
            s = lax.dot_general(q, kb, NT_DIMS, preferred_element_type=F32)
            s = jnp.where(valid, s + slope * negabs, -jnp.inf)
            m = jnp.maximum(jnp.max(s, axis=-1, keepdims=True), sink)
            p = jnp.exp(s - m)
            denom = jnp.sum(p, axis=-1, keepdims=True) + jnp.exp(sink - m)
            o = jnp.dot(p.astype(BF16), vb, preferred_element_type=F32)
            outs.append(o / denom)
    o_ref[...] = jnp.concatenate(outs, axis=-1)


def _attention(proj, sink, bsz, seqlen):
    n_tok = bsz * seqlen
    nb = seqlen // AT_BLOCK
    kcol = COL_AK // AT_KVWIDTH
    vcol = COL_AV // AT_KVWIDTH

    def prev(c):
        return lambda b, j: (b * nb + jnp.maximum(j - 1, 0), c)

    def own(c):
        return lambda b, j: (b * nb + j, c)

    def nxt(c):
        return lambda b, j: (b * nb + jnp.minimum(j + 1, nb - 1), c)

    kv_blk = (AT_BLOCK, AT_KVWIDTH)
    return pl.pallas_call(
        functools.partial(_attn_body, seqlen),
        grid=(bsz, nb),
        in_specs=[
            pl.BlockSpec(memory_space=pltpu.SMEM),
            pl.BlockSpec((AT_BLOCK, D_MODEL), own(COL_AQ // D_MODEL)),
            pl.BlockSpec(kv_blk, prev(kcol)), pl.BlockSpec(kv_blk, own(kcol)),
            pl.BlockSpec(kv_blk, nxt(kcol)),
            pl.BlockSpec(kv_blk, prev(vcol)), pl.BlockSpec(kv_blk, own(vcol)),
            pl.BlockSpec(kv_blk, nxt(vcol)),
        ],
        out_specs=pl.BlockSpec((AT_BLOCK, D_MODEL), lambda b, j: (b * nb + j, 0)),
        out_shape=jax.ShapeDtypeStruct((n_tok, D_MODEL), F32),
        compiler_params=pltpu.CompilerParams(
            dimension_semantics=("parallel", "arbitrary"), vmem_limit_bytes=VMEM_LIMIT),
        name="window_attn",
    )(sink.astype(F32), proj, proj, proj, proj, proj, proj, proj)


def _merge_body(of_ref, ob_ref, hg_ref, ga0_ref, ga1_ref, gb0_ref, gb1_ref, ya_ref, x_ref,
                gn_ref, wo_ref, g2_ref, wr_ref, br_ref,
                h_ref, xn_ref, idx_ref, gate_ref, rank_ref, cnt_ref, carry_ref):
    i = pl.program_id(0)

    @pl.when(i == 0)
    def _():
        carry_ref[...] = jnp.zeros_like(carry_ref)

    tm = x_ref.shape[0]
    o = of_ref[...] + ob_ref[...]
    gn = gn_ref[...]
    parts = []
    for h in range(HG_HEADS):
        oh = o[:, h * HG_DK:(h + 1) * HG_DK]
        ms = jnp.mean(oh * oh, axis=-1, keepdims=True)
        parts.append(oh * lax.rsqrt(ms + NORM_EPS) * gn)
    hg = hg_ref[...]
    y_rec = jnp.concatenate(parts, axis=-1) * (hg * _sigmoid(hg))
    gate_a = jnp.concatenate([ga0_ref[...], ga1_ref[...]], axis=-1)
    gate_b = jnp.concatenate([gb0_ref[...], gb1_ref[...]], axis=-1)
    merged = _sigmoid(gate_a) * y_rec + _sigmoid(gate_b) * ya_ref[...]
    h_res = x_ref[...] + jnp.dot(merged.astype(BF16), wo_ref[...], preferred_element_type=F32)
    h_ref[...] = h_res
    ms = jnp.mean(h_res * h_res, axis=-1, keepdims=True)
    xn = h_res * lax.rsqrt(ms + NORM_EPS) * g2_ref[...]
    xn_ref[...] = xn

    logits = lax.dot_general(wr_ref[...], xn, NT_DIMS, preferred_element_type=F32,
                             precision=lax.Precision.HIGHEST) + br_ref[...]
    e_iota = lax.broadcasted_iota(jnp.int32, (N_EXPERTS, tm), 0).astype(F32)
    vals = logits
    tops, idxs = [], []
    for _ in range(TOP_K):
        m = jnp.max(vals, axis=0, keepdims=True)
        sel = jnp.min(jnp.where(vals == m, e_iota, float(N_EXPERTS)), axis=0, keepdims=True)
        tops.append(m)
        idxs.append(sel)
        vals = jnp.where(e_iota == sel, -jnp.inf, vals)
    exps = [jnp.exp(t - tops[0]) for t in tops]
    den = exps[0] + exps[1] + exps[2] + exps[3]
    gate_ref[...] = jnp.concatenate([e / den for e in exps], axis=0)
    idx_ref[...] = jnp.concatenate(idxs, axis=0).astype(jnp.int32)

    chosen = jnp.where(vals == -jnp.inf, 1.0, 0.0)
    s_iota = lax.broadcasted_iota(jnp.int32, (tm, tm), 0)
    t_iota = lax.broadcasted_iota(jnp.int32, (tm, tm), 1)
    strict = jnp.where(s_iota < t_iota, 1.0, 0.0).astype(BF16)
    carry = carry_ref[...]
    prefix = jnp.dot(chosen.astype(BF16), strict, preferred_element_type=F32) + carry
    ranks = [jnp.sum(jnp.where(e_iota == sel, prefix, 0.0), axis=0, keepdims=True) for sel in idxs]
    rank_ref[...] = jnp.concatenate(ranks, axis=0).astype(jnp.int32)
    carry = carry + jnp.sum(chosen, axis=1, keepdims=True)
    carry_ref[...] = carry
    cnt_ref[...] = jnp.broadcast_to(carry, cnt_ref.shape).astype(jnp.int32)


def _merge_route(o_f, o_b, proj, y_att, x2d, gn_gain, w_out_bf16, g2, w_router_t, b_router, tm=256):
    n_tok = x2d.shape[0]
    nblk = n_tok // tm
    row_blk = (tm, D_MODEL)
    half = D_MODEL // 2
    half_blk = (tm, half)

    def rows(c):
        return lambda i: (i, c)

    full = lambda i: (0, 0)
    tok_out = pl.BlockSpec((TOP_K, tm), lambda i: (0, i))
    return pl.pallas_call(
        _merge_body,
        grid=(nblk,),
        in_specs=[
            pl.BlockSpec(row_blk, rows(0)), pl.BlockSpec(row_blk, rows(0)),
            pl.BlockSpec(row_blk, rows(COL_HG // D_MODEL)),
            pl.BlockSpec(half_blk, rows(COL_GA // half)),
            pl.BlockSpec(half_blk, rows(COL_GA // half + 1)),
            pl.BlockSpec(half_blk, rows(COL_GB // half)),
            pl.BlockSpec(half_blk, rows(COL_GB // half + 1)),
            pl.BlockSpec(row_blk, rows(0)), pl.BlockSpec(row_blk, rows(0)),
            pl.BlockSpec((1, HG_DK), full),
            pl.BlockSpec((D_MODEL, D_MODEL), full),
            pl.BlockSpec((1, D_MODEL), full),
            pl.BlockSpec((N_EXPERTS, D_MODEL), full),
            pl.BlockSpec((N_EXPERTS, 1), full),
        ],
        out_specs=[
            pl.BlockSpec(row_blk, rows(0)), pl.BlockSpec(row_blk, rows(0)),
            tok_out, tok_out, tok_out,
            pl.BlockSpec((N_EXPERTS, 128), full),
        ],
        out_shape=[
            jax.ShapeDtypeStruct((n_tok, D_MODEL), F32),
            jax.ShapeDtypeStruct((n_tok, D_MODEL), F32),
            jax.ShapeDtypeStruct((TOP_K, n_tok), jnp.int32),
            jax.ShapeDtypeStruct((TOP_K, n_tok), F32),
            jax.ShapeDtypeStruct((TOP_K, n_tok), jnp.int32),
            jax.ShapeDtypeStruct((N_EXPERTS, 128), jnp.int32),
        ],
        scratch_shapes=[pltpu.VMEM((N_EXPERTS, 1), F32)],
        compiler_params=pltpu.CompilerParams(
            dimension_semantics=("arbitrary",), vmem_limit_bytes=VMEM_LIMIT),
        name="merge_route",
    )(o_f, o_b, proj, proj, proj, proj, proj, y_att, x2d, gn_gain.reshape(1, HG_DK), w_out_bf16,
      g2.reshape(1, D_MODEL), w_router_t, b_router.reshape(N_EXPERTS, 1))


DISPATCH_TOKENS = 128
ISSUE_UNROLL = 8


def _dispatch_body(dest_ref, pend_ref, x_ref, xb_hbm, zero_ref, sem):
    i = pl.program_id(0)
    rows = MOE_ROWS
    nt = DISPATCH_TOKENS

    @pl.when(i == 0)
    def _():
        zero_ref[...] = jnp.zeros_like(zero_ref)
        for e in range(N_EXPERTS):
            end = pend_ref[e]
            start = pl.multiple_of(end - rows, rows)
            prev_end = pend_ref[e - 1] if e > 0 else 0

            @pl.when(end > prev_end)
            def _():
                cp = pltpu.make_async_copy(zero_ref, xb_hbm.at[pl.ds(start, rows)], sem.at[1])
                cp.start()
                cp.wait()

        def zero_block(b, c):
            cp = pltpu.make_async_copy(
                zero_ref, xb_hbm.at[pl.ds(pl.multiple_of(b * rows, rows), rows)], sem.at[1])
            cp.start()
            cp.wait()
            return c
        lax.fori_loop(pend_ref[N_EXPERTS - 1] // rows, xb_hbm.shape[0] // rows, zero_block, 0)

    base = i * (nt * TOP_K)

    def issue(r, c):
        for k in range(TOP_K):
            pltpu.make_async_copy(
                x_ref.at[pl.ds(r, 1)],
                xb_hbm.at[pl.ds(dest_ref[base + r * TOP_K + k], 1)],
                sem.at[0]).start(priority=k % 2)
        return c

    lax.fori_loop(0, nt, issue, 0, unroll=ISSUE_UNROLL)
    pltpu.make_async_copy(xb_hbm.at[pl.ds(0, nt * TOP_K)], xb_hbm.at[pl.ds(0, nt * TOP_K)],
                          sem.at[0]).wait()


def _dispatch(xn, dest_flat, pad_end, n_pad):
    n_tok = xn.shape[0]
    grid_spec = pltpu.PrefetchScalarGridSpec(
        num_scalar_prefetch=2,
        grid=(n_tok // DISPATCH_TOKENS,),
        in_specs=[pl.BlockSpec((DISPATCH_TOKENS, D_MODEL), lambda i, d, p: (i, 0))],
        out_specs=pl.BlockSpec(memory_space=pl.ANY),
        scratch_shapes=[pltpu.VMEM((MOE_ROWS, D_MODEL), F32), pltpu.SemaphoreType.DMA((2,))],
    )
    return pl.pallas_call(
        _dispatch_body,
        grid_spec=grid_spec,
        out_shape=jax.ShapeDtypeStruct((n_pad, D_MODEL), F32),
        compiler_params=pltpu.CompilerParams(dimension_semantics=("arbitrary",)),
        name="moe_dispatch",
    )(dest_flat, pad_end, xn)


def _expert_body(be_ref, nu_ref, x_ref, wug_ref, bug_ref, wd_ref, bd_ref, y_ref, wug_bf, wd_bf):
    i = pl.program_id(0)
    prev = be_ref[jnp.maximum(i - 1, 0)]

    @pl.when((i == 0) | (be_ref[i] != prev))
    def _():
        wug_bf[...] = wug_ref[0].astype(BF16)
        wd_bf[...] = wd_ref[0].astype(BF16)

    @pl.when(i < nu_ref[0])
    def _():
        x = x_ref[...].astype(BF16)
        hu = jnp.dot(x, wug_bf[...], preferred_element_type=F32) + bug_ref[0]
        g = jnp.minimum(hu[:, :D_FF], SWIGLU_LIMIT)
        u = jnp.clip(hu[:, D_FF:], -SWIGLU_LIMIT, SWIGLU_LIMIT)
        act = g * _sigmoid(SWIGLU_ALPHA * g) * (u + 1.0)
        y_ref[...] = jnp.dot(act.astype(BF16), wd_bf[...], preferred_element_type=F32) + bd_ref[0]

    @pl.when(i >= nu_ref[0])
    def _():
        y_ref[...] = jnp.zeros_like(y_ref)


def _experts(xb, block_expert, n_used, w_up_gate, b_up_gate, w_down, b_down):
    n_pad = xb.shape[0]
    n_blocks = n_pad // MOE_ROWS

    def x_map(i, be, nu):
        return (jnp.where(i < nu[0], i, 0), 0)

    def w_map(i, be, nu):
        return (be[i], 0, 0)

    grid_spec = pltpu.PrefetchScalarGridSpec(
        num_scalar_prefetch=2,
        grid=(n_blocks,),
        in_specs=[
            pl.BlockSpec((MOE_ROWS, D_MODEL), x_map),
            pl.BlockSpec((1, D_MODEL, 2 * D_FF), w_map),
            pl.BlockSpec((1, 1, 2 * D_FF), w_map),
            pl.BlockSpec((1, D_FF, D_MODEL), w_map),
            pl.BlockSpec((1, 1, D_MODEL), w_map),
        ],
        out_specs=pl.BlockSpec((MOE_ROWS, D_MODEL), lambda i, be, nu: (i, 0)),
        scratch_shapes=[pltpu.VMEM((D_MODEL, 2 * D_FF), BF16), pltpu.VMEM((D_FF, D_MODEL), BF16)],
    )
    return pl.pallas_call(
        _expert_body,
        grid_spec=grid_spec,
        out_shape=jax.ShapeDtypeStruct((n_pad, D_MODEL), F32),
        compiler_params=pltpu.CompilerParams(
            dimension_semantics=("arbitrary",), vmem_limit_bytes=VMEM_LIMIT),
        name="moe_experts",
    )(block_expert, n_used, xb, w_up_gate, b_up_gate.reshape(N_EXPERTS, 1, -1),
      w_down, b_down.reshape(N_EXPERTS, 1, -1))


COMBINE_TOKENS = 128


def _combine_body(dest_ref, yb_hbm, h_ref, gate_ref, gain_ref, o_ref, buf_ref, sem):
    i = pl.program_id(0)
    n_steps = pl.num_programs(0)
    nt = COMBINE_TOKENS

    def issue(step, slot):
        base = step * nt

        def body(r, c):
            for k in range(TOP_K):
                pltpu.make_async_copy(
                    yb_hbm.at[pl.ds(dest_ref[(base + r) * TOP_K + k], 1)],
                    buf_ref.at[slot, k, pl.ds(r, 1)], sem.at[slot]).start(priority=k % 2)
            return c
        lax.fori_loop(0, nt, body, 0, unroll=ISSUE_UNROLL)

    slot = lax.rem(i, 2)

    @pl.when(i == 0)
    def _():
        issue(0, 0)

    @pl.when(i + 1 < n_steps)
    def _():
        issue(i + 1, 1 - slot)

    pltpu.make_async_copy(buf_ref.at[slot], buf_ref.at[slot], sem.at[slot]).wait()

    gates = gate_ref[...]
    acc = h_ref[...]
    for k in range(TOP_K):
        acc = acc + gates[:, k:k + 1] * buf_ref[slot, k]
    ms = jnp.mean(acc * acc, axis=-1, keepdims=True)
    o_ref[...] = acc * lax.rsqrt(ms + NORM_EPS) * gain_ref[...]


def _combine(yb, dest_flat, h_res, gates_tk, final_gain):
    n_tok = h_res.shape[0]
    nt = COMBINE_TOKENS
    grid_spec = pltpu.PrefetchScalarGridSpec(
        num_scalar_prefetch=1,
        grid=(n_tok // nt,),
        in_specs=[
            pl.BlockSpec(memory_space=pl.ANY),
            pl.BlockSpec((nt, D_MODEL), lambda i, d: (i, 0)),
            pl.BlockSpec((nt, TOP_K), lambda i, d: (i, 0)),
            pl.BlockSpec((1, D_MODEL), lambda i, d: (0, 0)),
        ],
        out_specs=pl.BlockSpec((nt, D_MODEL), lambda i, d: (i, 0)),
        scratch_shapes=[pltpu.VMEM((2, TOP_K, nt, D_MODEL), F32), pltpu.SemaphoreType.DMA((2,))],
    )
    return pl.pallas_call(
        _combine_body,
        grid_spec=grid_spec,
        out_shape=jax.ShapeDtypeStruct((n_tok, D_MODEL), F32),
        compiler_params=pltpu.CompilerParams(
            dimension_semantics=("arbitrary",), vmem_limit_bytes=VMEM_LIMIT),
        name="moe_combine",
    )(dest_flat, yb, h_res, gates_tk, final_gain.reshape(1, D_MODEL))


def kernel(x, norm_mix_gain, w_in, hg_lb_fwd, hg_lb_bwd, hg_norm_gain, attn_sink, w_out,
           norm_ffn_gain, w_router, b_router, w_up_gate, b_up_gate, w_down, b_down,
           final_norm_gain):
    bsz, seqlen, d = x.shape
    n_tok = bsz * seqlen
    x2d = x.reshape(n_tok, d)

    lb_fwd = jax.nn.softmax(hg_lb_fwd.astype(F32), axis=0)[0]
    lb_bwd = jax.nn.softmax(hg_lb_bwd.astype(F32), axis=0)[0]

    proj = _inproj(x2d, norm_mix_gain[0], w_in[0].astype(BF16))
    o_f, o_b = _hgrn2(proj, lb_fwd, lb_bwd, bsz, seqlen)
    y_att = _attention(proj, attn_sink[0], bsz, seqlen)
    h_res, xn, idx_t, gate_t, rank_t, counts = _merge_route(
        o_f, o_b, proj, y_att, x2d, hg_norm_gain[0], w_out[0].astype(BF16),
        norm_ffn_gain[0], w_router[0].T.astype(F32), b_router[0].astype(F32))

    counts = counts[:, 0]
    padded = (counts + MOE_ROWS - 1) // MOE_ROWS * MOE_ROWS
    pad_end = jnp.cumsum(padded).astype(jnp.int32)
    pad_start = pad_end - padded
    n_pad = n_tok * TOP_K + N_EXPERTS * MOE_ROWS
    n_blocks = n_pad // MOE_ROWS
    experts = jnp.arange(N_EXPERTS, dtype=jnp.int32)
    start_of = jnp.sum(jnp.where(idx_t[..., None] == experts, pad_start, 0), axis=-1)
    dest_flat = (start_of + rank_t).T.reshape(-1).astype(jnp.int32)
    block_row0 = jnp.arange(n_blocks, dtype=jnp.int32) * MOE_ROWS
    block_expert = jnp.minimum(
        jnp.sum((pad_end[None, :] <= block_row0[:, None]).astype(jnp.int32), axis=-1),
        N_EXPERTS - 1).astype(jnp.int32)
    n_used = (pad_end[-1:] // MOE_ROWS).astype(jnp.int32)

    xb = _dispatch(xn, dest_flat, pad_end, n_pad)
    yb = _experts(xb, block_expert, n_used, w_up_gate[0], b_up_gate[0], w_down[0], b_down[0])
    out = _combine(yb, dest_flat, h_res, gate_t.T, final_norm_gain)
    return out.reshape(bsz, seqlen, d)
```

```python
import functools

import jax
import jax.numpy as jnp
from jax import lax
from jax.experimental import pallas as pl
from jax.experimental.pallas import tpu as pltpu

F32 = jnp.float32
BF16 = jnp.bfloat16

D_MODEL = 1024
HG_HEADS = 8
HG_DK = 128
HG_CHUNK = 128
HG_GROUP = 4
AT_Q_HEADS = 16
AT_KV_HEADS = 4
AT_GROUP = AT_Q_HEADS // AT_KV_HEADS
AT_HEAD_DIM = 64
WINDOW = 128
AT_BLOCK = WINDOW
AT_KVWIDTH = AT_KV_HEADS * AT_HEAD_DIM
N_EXPERTS = 32
TOP_K = 4
D_FF = 1024
SWIGLU_LIMIT = 7.0
SWIGLU_ALPHA = 1.702
NORM_EPS = 1e-5
IN_WIDTH = 8704

COL_HQ, COL_FF, COL_FB, COL_HI, COL_HG = 0, 1024, 2048, 3072, 4096
COL_AQ, COL_AK, COL_AV, COL_GA, COL_GB = 5120, 6144, 6400, 6656, 7680

MOE_ROWS = 256
VMEM_LIMIT = 48 * 1024 * 1024

NT_DIMS = (((1,), (1,)), ((), ()))


def _sigmoid(x):
    return 1.0 / (1.0 + jnp.exp(-x))


def _inproj_body(x_ref, g_ref, w_ref, o_ref, xn_ref):
    @pl.when(pl.program_id(1) == 0)
    def _():
        x = x_ref[...]
        ms = jnp.mean(x * x, axis=-1, keepdims=True)
        xn_ref[...] = (x * lax.rsqrt(ms + NORM_EPS) * g_ref[...]).astype(BF16)

    o_ref[...] = jnp.dot(xn_ref[...], w_ref[...], preferred_element_type=F32)


def _inproj(x2d, gain, w_bf16, tm=1024, tn=2176):
    n_tok = x2d.shape[0]
    n_out = w_bf16.shape[1]
    return pl.pallas_call(
        _inproj_body,
        grid=(n_tok // tm, n_out // tn),
        in_specs=[
            pl.BlockSpec((tm, D_MODEL), lambda i, j: (i, 0)),
            pl.BlockSpec((1, D_MODEL), lambda i, j: (0, 0)),
            pl.BlockSpec((D_MODEL, tn), lambda i, j: (0, j)),
        ],
        out_specs=pl.BlockSpec((tm, tn), lambda i, j: (i, j)),
        out_shape=jax.ShapeDtypeStruct((n_tok, n_out), F32),
        scratch_shapes=[pltpu.VMEM((tm, D_MODEL), BF16)],
        compiler_params=pltpu.CompilerParams(
            dimension_semantics=("parallel", "arbitrary"), vmem_limit_bytes=VMEM_LIMIT),
        name="inproj",
    )(x2d, gain.reshape(1, D_MODEL), w_bf16)


class _Dir:
    def __init__(self, q_ref, f_ref, v_ref, o_ref, st_ref, lb, tri, mask, mid_row, last_row):
        self.q_ref, self.f_ref, self.v_ref, self.o_ref, self.st_ref = q_ref, f_ref, v_ref, o_ref, st_ref
        self.lb, self.tri, self.mask, self.mid_row, self.last_row = lb, tri, mask, mid_row, last_row


def _gla_group(chains):
    c = HG_CHUNK
    dk = HG_DK
    pre = []
    for d, r0 in chains:
        q = d.q_ref[pl.ds(r0, c), :]
        q = q * _sigmoid(q)
        f = d.lb + (1.0 - d.lb) * _sigmoid(d.f_ref[pl.ds(r0, c), :])
        logf = jnp.log(f)
        hi = logf.astype(BF16)
        lo = (logf - hi.astype(F32)).astype(BF16)
        pre.append((q, 1.0 - f, jnp.concatenate([hi, lo], axis=1)))
    cums = []
    for (d, _), (_, _, hl) in zip(chains, pre):
        r = jnp.dot(d.tri, hl, preferred_element_type=F32)
        cums.append(r[:, :dk] + r[:, dk:])
    mids = []
    for (d, r0), (q, k, _), cum in zip(chains, pre, cums):
        mid = cum[d.mid_row:d.mid_row + 1, :]
        last = cum[d.last_row:d.last_row + 1, :]
        a = cum - mid
        qa = q * jnp.exp(a)
        ka = k * jnp.exp(-a)
        qd = (qa * jnp.exp(mid)).astype(BF16)
        kend = (ka * jnp.exp(last - mid)).astype(BF16)
        dec = jnp.exp(last)
        scores = lax.dot_general(qa.astype(BF16), ka.astype(BF16), NT_DIMS,
                                 preferred_element_type=F32)
        mids.append((qd, kend, dec, scores))
    outs = []
    for (d, r0), (qd, kend, dec, scores) in zip(chains, mids):
        v = d.v_ref[pl.ds(r0, c), :]
        sm = jnp.where(d.mask, scores, 0.0).astype(BF16)
        o_intra = jnp.dot(sm, v.astype(BF16), preferred_element_type=F32)
        upd = jnp.dot(jnp.transpose(v).astype(BF16), kend, preferred_element_type=F32)
        outs.append((o_intra, upd))
    for (d, r0), (qd, _, dec, _), (o_intra, upd) in zip(chains, mids, outs):
        st = d.st_ref[...]
        d.o_ref[pl.ds(r0, c), :] = o_intra + lax.dot_general(
            qd, st.astype(BF16), NT_DIMS, preferred_element_type=F32)
        d.st_ref[...] = st * dec + upd


def _hgrn2_body(qf_ref, ff_ref, vf_ref, qb_ref, fb_ref, vb_ref, lbf_ref, lbb_ref,
                of_ref, ob_ref, sf_ref, sb_ref):
    @pl.when(pl.program_id(2) == 0)
    def _():
        sf_ref[...] = jnp.zeros_like(sf_ref)
        sb_ref[...] = jnp.zeros_like(sb_ref)

    c = HG_CHUNK
    g = HG_GROUP
    n_chunks = qf_ref.shape[0] // c
    row = lax.broadcasted_iota(jnp.int32, (c, c), 0)
    col = lax.broadcasted_iota(jnp.int32, (c, c), 1)
    lower = col <= row
    upper = col >= row
    fwd = _Dir(qf_ref, ff_ref, vf_ref, of_ref, sf_ref, lbf_ref[...],
               jnp.where(lower, 1.0, 0.0).astype(BF16), lower, c // 2 - 1, c - 1)
    bwd = _Dir(qb_ref, fb_ref, vb_ref, ob_ref, sb_ref, lbb_ref[...],
               jnp.where(upper, 1.0, 0.0).astype(BF16), upper, c // 2, 0)

    def step(n, carry):
        chains = []
        for u in range(g):
            chains.append((fwd, pl.multiple_of((n * g + u) * c, c)))
        for u in range(g):
            chains.append((bwd, pl.multiple_of((n_chunks - 1 - n * g - u) * c, c)))
        _gla_group(chains)
        return carry

    lax.fori_loop(0, n_chunks // g, step, 0)


def _hgrn2(proj, lb_fwd, lb_bwd, bsz, seqlen, rows=1024):
    n_tok = bsz * seqlen
    rows = min(rows, seqlen)
    nb = seqlen // rows
    dk = HG_DK

    def fwd_map(col0):
        return lambda b, h, j: (b * nb + j, col0 // dk + h)

    def bwd_map(col0):
        return lambda b, h, j: (b * nb + (nb - 1 - j), col0 // dk + h)

    blk = (rows, dk)
    lb_spec = pl.BlockSpec((1, dk), lambda b, h, j: (0, h))
    return pl.pallas_call(
        _hgrn2_body,
        grid=(bsz, HG_HEADS, nb),
        in_specs=[
            pl.BlockSpec(blk, fwd_map(COL_HQ)), pl.BlockSpec(blk, fwd_map(COL_FF)),
            pl.BlockSpec(blk, fwd_map(COL_HI)),
            pl.BlockSpec(blk, bwd_map(COL_HQ)), pl.BlockSpec(blk, bwd_map(COL_FB)),
            pl.BlockSpec(blk, bwd_map(COL_HI)),
            lb_spec, lb_spec,
        ],
        out_specs=[pl.BlockSpec(blk, fwd_map(0)), pl.BlockSpec(blk, bwd_map(0))],
        out_shape=[jax.ShapeDtypeStruct((n_tok, HG_HEADS * dk), F32)] * 2,
        scratch_shapes=[pltpu.VMEM((dk, dk), F32), pltpu.VMEM((dk, dk), F32)],
        compiler_params=pltpu.CompilerParams(
            dimension_semantics=("parallel", "parallel", "arbitrary"),
            vmem_limit_bytes=VMEM_LIMIT),
        name="hgrn2",
    )(proj, proj, proj, proj, proj, proj, lb_fwd.reshape(1, -1), lb_bwd.reshape(1, -1))


def _attn_body(seqlen, sink_ref, q_ref, kp_ref, ko_ref, kn_ref, vp_ref, vo_ref, vn_ref, o_ref):
    j = pl.program_id(1)
    blk = AT_BLOCK
    span = 3 * blk
    row = lax.broadcasted_iota(jnp.int32, (blk, span), 0)
    col = lax.broadcasted_iota(jnp.int32, (blk, span), 1)
    absrel = jnp.abs(row + blk - col)
    key_pos = j * blk - blk + col
    valid = (absrel <= WINDOW) & (key_pos >= 0) & (key_pos < seqlen)
    base = jnp.where(valid, -absrel.astype(F32), -jnp.inf)
    dh = AT_HEAD_DIM
    lanes = 2 * dh
    lo_half = lax.broadcasted_iota(jnp.int32, (blk, lanes), 1) < dh

    def padded(ref, g):
        t = ref[:, (g // 2) * lanes:(g // 2 + 1) * lanes]
        if g % 2 == 0:
            lo = jnp.where(lo_half, t, 0.0)
            hi = pltpu.roll(lo, dh, axis=1)
        else:
            hi = jnp.where(lo_half, 0.0, t)
            lo = pltpu.roll(hi, dh, axis=1)
        return lo, hi

    def banded(refs, g):
        pads = [padded(r, g) for r in refs]
        return jnp.concatenate([p[0] for p in pads] + [p[1] for p in pads], axis=0).astype(BF16)

    scores, values = [], []
    for g in range(AT_KV_HEADS):
        q2 = q_ref[:, g * 2 * lanes:(g + 1) * 2 * lanes] * (dh ** -0.5)
        lhs = jnp.concatenate([q2[:, :lanes], q2[:, lanes:]], axis=0).astype(BF16)
        scores.append(lax.dot_general(lhs, banded((kp_ref, ko_ref, kn_ref), g), NT_DIMS,
                                      preferred_element_type=F32))
        values.append(banded((vp_ref, vo_ref, vn_ref), g))
    probs, denoms = [], []
    for g in range(AT_KV_HEADS):
        p_rows, d_rows = [], []
        for r in range(2):
            p_cols, d_cols = [], []
            for c in range(2):
                h = g * AT_GROUP + 2 * r + c
                slope = 2.0 ** (-8.0 * (h + 1) / AT_Q_HEADS)
                sink = sink_ref[h]
                s = scores[g][r * blk:(r + 1) * blk, c * span:(c + 1) * span] + slope * base
                m = jnp.maximum(jnp.max(s, axis=-1, keepdims=True), sink)
                p = jnp.exp(s - m)
                d_cols.append(jnp.sum(p, axis=-1, keepdims=True) + jnp.exp(sink - m))
                p_cols.append(p.astype(BF16))
            p_rows.append(jnp.concatenate(p_cols, axis=1))
            d_rows.append(jnp.where(lo_half, d_cols[0], d_cols[1]))
        probs.append(jnp.concatenate(p_rows, axis=0))
        denoms.append(d_rows)
    outs = []
    for g in range(AT_KV_HEADS):
        o = jnp.dot(probs[g], values[g], preferred_element_type=F32)
        for r in range(2):
            outs.append(o[r * blk:(r + 1) * blk, :] / denoms[g][r])
    o_ref[...] = jnp.concatenate(outs, axis=-1)


def _attention(proj, sink, bsz, seqlen):
    n_tok = bsz * seqlen
    nb = seqlen // AT_BLOCK
    kcol = COL_AK // AT_KVWIDTH
    vcol = COL_AV // AT_KVWIDTH

    def prev(c):
        return lambda b, j: (b * nb + jnp.maximum(j - 1, 0), c)

    def own(c):
        return lambda b, j: (b * nb + j, c)

    def nxt(c):
        return lambda b, j: (b * nb + jnp.minimum(j + 1, nb - 1), c)

    kv_blk = (AT_BLOCK, AT_KVWIDTH)
    return pl.pallas_call(
        functools.partial(_attn_body, seqlen),
        grid=(bsz, nb),
        in_specs=[
            pl.BlockSpec(memory_space=pltpu.SMEM),
            pl.BlockSpec((AT_BLOCK, D_MODEL), own(COL_AQ // D_MODEL)),
            pl.BlockSpec(kv_blk, prev(kcol)), pl.BlockSpec(kv_blk, own(kcol)),
            pl.BlockSpec(kv_blk, nxt(kcol)),
            pl.BlockSpec(kv_blk, prev(vcol)), pl.BlockSpec(kv_blk, own(vcol)),
            pl.BlockSpec(kv_blk, nxt(vcol)),
        ],
        out_specs=pl.BlockSpec((AT_BLOCK, D_MODEL), lambda b, j: (b * nb + j, 0)),
        out_shape=jax.ShapeDtypeStruct((n_tok, D_MODEL), F32),
        compiler_params=pltpu.CompilerParams(
            dimension_semantics=("parallel", "arbitrary"), vmem_limit_bytes=VMEM_LIMIT),
        name="window_attn",
    )(sink.astype(F32), proj, proj, proj, proj, proj, proj, proj)


def _merge_body(of_ref, ob_ref, hg_ref, ga0_ref, ga1_ref, gb0_ref, gb1_ref, ya_ref, x_ref,
                gn_ref, wo_ref, g2_ref, wr_ref, br_ref,
                h_ref, xn_ref, idx_ref, gate_ref, rank_ref, cnt_ref, carry_ref):
    i = pl.program_id(0)

    @pl.when(i == 0)
    def _():
        carry_ref[...] = jnp.zeros_like(carry_ref)

    tm = x_ref.shape[0]
    o = of_ref[...] + ob_ref[...]
    gn = gn_ref[...]
    parts = []
    for h in range(HG_HEADS):
        oh = o[:, h * HG_DK:(h + 1) * HG_DK]
        ms = jnp.mean(oh * oh, axis=-1, keepdims=True)
        parts.append(oh * lax.rsqrt(ms + NORM_EPS) * gn)
    hg = hg_ref[...]
    y_rec = jnp.concatenate(parts, axis=-1) * (hg * _sigmoid(hg))
    gate_a = jnp.concatenate([ga0_ref[...], ga1_ref[...]], axis=-1)
    gate_b = jnp.concatenate([gb0_ref[...], gb1_ref[...]], axis=-1)
    merged = _sigmoid(gate_a) * y_rec + _sigmoid(gate_b) * ya_ref[...]
    h_res = x_ref[...] + jnp.dot(merged.astype(BF16), wo_ref[...], preferred_element_type=F32)
    h_ref[...] = h_res
    ms = jnp.mean(h_res * h_res, axis=-1, keepdims=True)
    xn = h_res * lax.rsqrt(ms + NORM_EPS) * g2_ref[...]
    xn_ref[...] = xn

    logits = lax.dot_general(wr_ref[...], xn, NT_DIMS, preferred_element_type=F32,
                             precision=lax.Precision.HIGHEST) + br_ref[...]
    e_iota = lax.broadcasted_iota(jnp.int32, (N_EXPERTS, tm), 0).astype(F32)
    vals = logits
    tops, idxs = [], []
    for _ in range(TOP_K):
        m = jnp.max(vals, axis=0, keepdims=True)
        sel = jnp.min(jnp.where(vals == m, e_iota, float(N_EXPERTS)), axis=0, keepdims=True)
        tops.append(m)
        idxs.append(sel)
        vals = jnp.where(e_iota == sel, -jnp.inf, vals)
    exps = [jnp.exp(t - tops[0]) for t in tops]
    den = exps[0] + exps[1] + exps[2] + exps[3]
    gate_ref[...] = jnp.concatenate([e / den for e in exps], axis=0)
    idx_ref[...] = jnp.concatenate(idxs, axis=0).astype(jnp.int32)

    chosen = jnp.where(vals == -jnp.inf, 1.0, 0.0)
    s_iota = lax.broadcasted_iota(jnp.int32, (tm, tm), 0)
    t_iota = lax.broadcasted_iota(jnp.int32, (tm, tm), 1)
    strict = jnp.where(s_iota < t_iota, 1.0, 0.0).astype(BF16)
    carry = carry_ref[...]
    prefix = jnp.dot(chosen.astype(BF16), strict, preferred_element_type=F32) + carry
    ranks = [jnp.sum(jnp.where(e_iota == sel, prefix, 0.0), axis=0, keepdims=True) for sel in idxs]
    rank_ref[...] = jnp.concatenate(ranks, axis=0).astype(jnp.int32)
    carry = carry + jnp.sum(chosen, axis=1, keepdims=True)
    carry_ref[...] = carry
    cnt_ref[...] = jnp.broadcast_to(carry, cnt_ref.shape).astype(jnp.int32)


def _merge_route(o_f, o_b, proj, y_att, x2d, gn_gain, w_out_bf16, g2, w_router_t, b_router, tm=256):
    n_tok = x2d.shape[0]
    nblk = n_tok // tm
    row_blk = (tm, D_MODEL)
    half = D_MODEL // 2
    half_blk = (tm, half)

    def rows(c):
        return lambda i: (i, c)

    full = lambda i: (0, 0)
    tok_out = pl.BlockSpec((TOP_K, tm), lambda i: (0, i))
    return pl.pallas_call(
        _merge_body,
        grid=(nblk,),
        in_specs=[
            pl.BlockSpec(row_blk, rows(0)), pl.BlockSpec(row_blk, rows(0)),
            pl.BlockSpec(row_blk, rows(COL_HG // D_MODEL)),
            pl.BlockSpec(half_blk, rows(COL_GA // half)),
            pl.BlockSpec(half_blk, rows(COL_GA // half + 1)),
            pl.BlockSpec(half_blk, rows(COL_GB // half)),
            pl.BlockSpec(half_blk, rows(COL_GB // half + 1)),
            pl.BlockSpec(row_blk, rows(0)), pl.BlockSpec(row_blk, rows(0)),
            pl.BlockSpec((1, HG_DK), full),
            pl.BlockSpec((D_MODEL, D_MODEL), full),
            pl.BlockSpec((1, D_MODEL), full),
            pl.BlockSpec((N_EXPERTS, D_MODEL), full),
            pl.BlockSpec((N_EXPERTS, 1), full),
        ],
        out_specs=[
            pl.BlockSpec(row_blk, rows(0)), pl.BlockSpec(row_blk, rows(0)),
            tok_out, tok_out, tok_out,
            pl.BlockSpec((N_EXPERTS, 128), full),
        ],
        out_shape=[
            jax.ShapeDtypeStruct((n_tok, D_MODEL), F32),
            jax.ShapeDtypeStruct((n_tok, D_MODEL), F32),
            jax.ShapeDtypeStruct((TOP_K, n_tok), jnp.int32),
            jax.ShapeDtypeStruct((TOP_K, n_tok), F32),
            jax.ShapeDtypeStruct((TOP_K, n_tok), jnp.int32),
            jax.ShapeDtypeStruct((N_EXPERTS, 128), jnp.int32),
        ],
        scratch_shapes=[pltpu.VMEM((N_EXPERTS, 1), F32)],
        compiler_params=pltpu.CompilerParams(
            dimension_semantics=("arbitrary",), vmem_limit_bytes=VMEM_LIMIT),
        name="merge_route",
    )(o_f, o_b, proj, proj, proj, proj, proj, y_att, x2d, gn_gain.reshape(1, HG_DK), w_out_bf16,
      g2.reshape(1, D_MODEL), w_router_t, b_router.reshape(N_EXPERTS, 1))


DISPATCH_TOKENS = 128
ISSUE_UNROLL = 8


def _dispatch_body(dest_ref, pend_ref, x_ref, xb_hbm, zero_ref, sem):
    i = pl.program_id(0)
    rows = MOE_ROWS
    nt = DISPATCH_TOKENS

    @pl.when(i == 0)
    def _():
        zero_ref[...] = jnp.zeros_like(zero_ref)
        for e in range(N_EXPERTS):
            end = pend_ref[e]
            start = pl.multiple_of(end - rows, rows)
            prev_end = pend_ref[e - 1] if e > 0 else 0

            @pl.when(end > prev_end)
            def _():
                cp = pltpu.make_async_copy(zero_ref, xb_hbm.at[pl.ds(start, rows)], sem.at[1])
                cp.start()
                cp.wait()

        def zero_block(b, c):
            cp = pltpu.make_async_copy(
                zero_ref, xb_hbm.at[pl.ds(pl.multiple_of(b * rows, rows), rows)], sem.at[1])
            cp.start()
            cp.wait()
            return c
        lax.fori_loop(pend_ref[N_EXPERTS - 1] // rows, xb_hbm.shape[0] // rows, zero_block, 0)

    base = i * (nt * TOP_K)

    def issue(r, c):
        for k in range(TOP_K):
            pltpu.make_async_copy(
                x_ref.at[pl.ds(r, 1)],
                xb_hbm.at[pl.ds(dest_ref[base + r * TOP_K + k], 1)],
                sem.at[0]).start(priority=k % 2)
        return c

    lax.fori_loop(0, nt, issue, 0, unroll=ISSUE_UNROLL)
    pltpu.make_async_copy(xb_hbm.at[pl.ds(0, nt * TOP_K)], xb_hbm.at[pl.ds(0, nt * TOP_K)],
                          sem.at[0]).wait()


def _dispatch(xn, dest_flat, pad_end, n_pad):
    n_tok = xn.shape[0]
    grid_spec = pltpu.PrefetchScalarGridSpec(
        num_scalar_prefetch=2,
        grid=(n_tok // DISPATCH_TOKENS,),
        in_specs=[pl.BlockSpec((DISPATCH_TOKENS, D_MODEL), lambda i, d, p: (i, 0))],
        out_specs=pl.BlockSpec(memory_space=pl.ANY),
        scratch_shapes=[pltpu.VMEM((MOE_ROWS, D_MODEL), F32), pltpu.SemaphoreType.DMA((2,))],
    )
    return pl.pallas_call(
        _dispatch_body,
        grid_spec=grid_spec,
        out_shape=jax.ShapeDtypeStruct((n_pad, D_MODEL), F32),
        compiler_params=pltpu.CompilerParams(dimension_semantics=("arbitrary",)),
        name="moe_dispatch",
    )(dest_flat, pad_end, xn)


def _expert_body(bend_ref, x_hbm, wug_ref, bug_ref, wd_ref, bd_ref, y_hbm,
                 wug_bf, wd_bf, xbuf, ybuf, sem_in, sem_out):
    e = pl.program_id(0)
    rows = MOE_ROWS
    n_total = y_hbm.shape[0] // rows
    n_used = bend_ref[N_EXPERTS - 1]
    b_first = jnp.where(e == 0, 0, bend_ref[jnp.maximum(e - 1, 0)])
    b_last = bend_ref[e]

    def x_copy(b, slot):
        return pltpu.make_async_copy(
            x_hbm.at[pl.ds(pl.multiple_of(b * rows, rows), rows)], xbuf.at[slot], sem_in.at[slot])

    def y_copy(b, slot):
        return pltpu.make_async_copy(
            ybuf.at[slot], y_hbm.at[pl.ds(pl.multiple_of(b * rows, rows), rows)], sem_out.at[slot])

    @pl.when((e == 0) & (n_used > 0))
    def _():
        x_copy(0, 0).start()

    wug_bf[...] = wug_ref[0].astype(BF16)
    wd_bf[...] = wd_ref[0].astype(BF16)
    b_ug = bug_ref[0]
    b_d = bd_ref[0]

    def block(b, carry):
        slot = lax.rem(b, 2)
        x_copy(b, slot).wait()

        @pl.when(b + 1 < n_used)
        def _():
            x_copy(b + 1, 1 - slot).start()

        @pl.when(b >= 2)
        def _():
            y_copy(b - 2, slot).wait()

        x = xbuf[slot].astype(BF16)
        hu = jnp.dot(x, wug_bf[...], preferred_element_type=F32) + b_ug
        g = jnp.minimum(hu[:, :D_FF], SWIGLU_LIMIT)
        u = jnp.clip(hu[:, D_FF:], -SWIGLU_LIMIT, SWIGLU_LIMIT)
        act = g * _sigmoid(SWIGLU_ALPHA * g) * (u + 1.0)
        ybuf[slot] = jnp.dot(act.astype(BF16), wd_bf[...], preferred_element_type=F32) + b_d
        y_copy(b, slot).start()
        return carry

    lax.fori_loop(b_first, b_last, block, 0)

    @pl.when(e == N_EXPERTS - 1)
    def _():
        for back in (2, 1):
            @pl.when(n_used >= back)
            def _():
                y_copy(n_used - back, lax.rem(n_used - back, 2)).wait()

        ybuf[0] = jnp.zeros((rows, D_MODEL), F32)

        def zero_block(b, carry):
            cp = y_copy(b, 0)
            cp.start()
            cp.wait()
            return carry
        lax.fori_loop(n_used, n_total, zero_block, 0)


def _experts(xb, block_end, w_up_gate, b_up_gate, w_down, b_down):
    n_pad = xb.shape[0]

    def w_map(e, bend):
        return (e, 0, 0)

    grid_spec = pltpu.PrefetchScalarGridSpec(
        num_scalar_prefetch=1,
        grid=(N_EXPERTS,),
        in_specs=[
            pl.BlockSpec(memory_space=pl.ANY),
            pl.BlockSpec((1, D_MODEL, 2 * D_FF), w_map),
            pl.BlockSpec((1, 1, 2 * D_FF), w_map),
            pl.BlockSpec((1, D_FF, D_MODEL), w_map),
            pl.BlockSpec((1, 1, D_MODEL), w_map),
        ],
        out_specs=pl.BlockSpec(memory_space=pl.ANY),
        scratch_shapes=[
            pltpu.VMEM((D_MODEL, 2 * D_FF), BF16), pltpu.VMEM((D_FF, D_MODEL), BF16),
            pltpu.VMEM((2, MOE_ROWS, D_MODEL), F32), pltpu.VMEM((2, MOE_ROWS, D_MODEL), F32),
            pltpu.SemaphoreType.DMA((2,)), pltpu.SemaphoreType.DMA((2,)),
        ],
    )
    return pl.pallas_call(
        _expert_body,
        grid_spec=grid_spec,
        out_shape=jax.ShapeDtypeStruct((n_pad, D_MODEL), F32),
        compiler_params=pltpu.CompilerParams(
            dimension_semantics=("arbitrary",), vmem_limit_bytes=VMEM_LIMIT),
        name="moe_experts",
    )(block_end, xb, w_up_gate, b_up_gate.reshape(N_EXPERTS, 1, -1),
      w_down, b_down.reshape(N_EXPERTS, 1, -1))


COMBINE_TOKENS = 128


def _combine_body(dest_ref, yb_hbm, h_ref, gate_ref, gain_ref, o_ref, buf_ref, sem):
    i = pl.program_id(0)
    n_steps = pl.num_programs(0)
    nt = COMBINE_TOKENS

    def issue(step, slot):
        base = step * nt

        def body(r, c):
            for k in range(TOP_K):
                pltpu.make_async_copy(
                    yb_hbm.at[pl.ds(dest_ref[(base + r) * TOP_K + k], 1)],
                    buf_ref.at[slot, k, pl.ds(r, 1)], sem.at[slot]).start(priority=k % 2)
            return c
        lax.fori_loop(0, nt, body, 0, unroll=ISSUE_UNROLL)

    slot = lax.rem(i, 2)

    @pl.when(i == 0)
    def _():
        issue(0, 0)

    @pl.when(i + 1 < n_steps)
    def _():
        issue(i + 1, 1 - slot)

    pltpu.make_async_copy(buf_ref.at[slot], buf_ref.at[slot], sem.at[slot]).wait()

    gates = gate_ref[...]
    acc = h_ref[...]
    for k in range(TOP_K):
        acc = acc + gates[:, k:k + 1] * buf_ref[slot, k]
    ms = jnp.mean(acc * acc, axis=-1, keepdims=True)
    o_ref[...] = acc * lax.rsqrt(ms + NORM_EPS) * gain_ref[...]


def _combine(yb, dest_flat, h_res, gates_tk, final_gain):
    n_tok = h_res.shape[0]
    nt = COMBINE_TOKENS
    grid_spec = pltpu.PrefetchScalarGridSpec(
        num_scalar_prefetch=1,
        grid=(n_tok // nt,),
        in_specs=[
            pl.BlockSpec(memory_space=pl.ANY),
            pl.BlockSpec((nt, D_MODEL), lambda i, d: (i, 0)),
            pl.BlockSpec((nt, TOP_K), lambda i, d: (i, 0)),
            pl.BlockSpec((1, D_MODEL), lambda i, d: (0, 0)),
        ],
        out_specs=pl.BlockSpec((nt, D_MODEL), lambda i, d: (i, 0)),
        scratch_shapes=[pltpu.VMEM((2, TOP_K, nt, D_MODEL), F32), pltpu.SemaphoreType.DMA((2,))],
    )
    return pl.pallas_call(
        _combine_body,
        grid_spec=grid_spec,
        out_shape=jax.ShapeDtypeStruct((n_tok, D_MODEL), F32),
        compiler_params=pltpu.CompilerParams(
            dimension_semantics=("arbitrary",), vmem_limit_bytes=VMEM_LIMIT),
        name="moe_combine",
    )(dest_flat, yb, h_res, gates_tk, final_gain.reshape(1, D_MODEL))


def kernel(x, norm_mix_gain, w_in, hg_lb_fwd, hg_lb_bwd, hg_norm_gain, attn_sink, w_out,
           norm_ffn_gain, w_router, b_router, w_up_gate, b_up_gate, w_down, b_down,
           final_norm_gain):
    bsz, seqlen, d = x.shape
    n_tok = bsz * seqlen
    x2d = x.reshape(n_tok, d)

    lb_fwd = jax.nn.softmax(hg_lb_fwd.astype(F32), axis=0)[0]
    lb_bwd = jax.nn.softmax(hg_lb_bwd.astype(F32), axis=0)[0]

    proj = _inproj(x2d, norm_mix_gain[0], w_in[0].astype(BF16))
    o_f, o_b = _hgrn2(proj, lb_fwd, lb_bwd, bsz, seqlen)
    y_att = _attention(proj, attn_sink[0], bsz, seqlen)
    h_res, xn, idx_t, gate_t, rank_t, counts = _merge_route(
        o_f, o_b, proj, y_att, x2d, hg_norm_gain[0], w_out[0].astype(BF16),
        norm_ffn_gain[0], w_router[0].T.astype(F32), b_router[0].astype(F32))

    counts = counts[:, 0]
    padded = (counts + MOE_ROWS - 1) // MOE_ROWS * MOE_ROWS
    pad_end = jnp.cumsum(padded).astype(jnp.int32)
    pad_start = pad_end - padded
    n_pad = n_tok * TOP_K + N_EXPERTS * MOE_ROWS
    experts = jnp.arange(N_EXPERTS, dtype=jnp.int32)
    start_of = jnp.sum(jnp.where(idx_t[..., None] == experts, pad_start, 0), axis=-1)
    dest_flat = (start_of + rank_t).T.reshape(-1).astype(jnp.int32)

    xb = _dispatch(xn, dest_flat, pad_end, n_pad)
    yb = _experts(xb, pad_end // MOE_ROWS, w_up_gate[0], b_up_gate[0], w_down[0], b_down[0])
    out = _combine(yb, dest_flat, h_res, gate_t.T, final_norm_gain)
    return out.reshape(bsz, seqlen, d)
```

```python
import functools

import jax
import jax.numpy as jnp
from jax import lax
from jax.experimental import pallas as pl
from jax.experimental.pallas import tpu as pltpu

F32 = jnp.float32
BF16 = jnp.bfloat16

D_MODEL = 1024
HG_HEADS = 8
HG_DK = 128
HG_CHUNK = 128
HG_GROUP = 4
AT_Q_HEADS = 16
AT_KV_HEADS = 4
AT_GROUP = AT_Q_HEADS // AT_KV_HEADS
AT_HEAD_DIM = 64
WINDOW = 128
AT_BLOCK = WINDOW
AT_KVWIDTH = AT_KV_HEADS * AT_HEAD_DIM
N_EXPERTS = 32
TOP_K = 4
D_FF = 1024
SWIGLU_LIMIT = 7.0
SWIGLU_ALPHA = 1.702
NORM_EPS = 1e-5
IN_WIDTH = 8704

COL_HQ, COL_FF, COL_FB, COL_HI, COL_HG = 0, 1024, 2048, 3072, 4096
COL_AQ, COL_AK, COL_AV, COL_GA, COL_GB = 5120, 6144, 6400, 6656, 7680

MOE_ROWS = 256
FF_SLAB = 256
VMEM_LIMIT = 48 * 1024 * 1024

NT_DIMS = (((1,), (1,)), ((), ()))


def _sigmoid(x):
    return 0.5 * jnp.tanh(0.5 * x) + 0.5


def _inproj_body(x_ref, g_ref, w_ref, o_ref, xn_ref):
    @pl.when(pl.program_id(1) == 0)
    def _():
        x = x_ref[...]
        ms = jnp.mean(x * x, axis=-1, keepdims=True)
        xn_ref[...] = (x * lax.rsqrt(ms + NORM_EPS) * g_ref[...]).astype(BF16)

    o_ref[...] = jnp.dot(xn_ref[...], w_ref[...], preferred_element_type=F32).astype(o_ref.dtype)


def _inproj(x2d, gain, w_bf16, tm=1024, tn=2176):
    n_tok = x2d.shape[0]
    n_out = w_bf16.shape[1]
    return pl.pallas_call(
        _inproj_body,
        grid=(n_tok // tm, n_out // tn),
        in_specs=[
            pl.BlockSpec((tm, D_MODEL), lambda i, j: (i, 0)),
            pl.BlockSpec((1, D_MODEL), lambda i, j: (0, 0)),
            pl.BlockSpec((D_MODEL, tn), lambda i, j: (0, j)),
        ],
        out_specs=pl.BlockSpec((tm, tn), lambda i, j: (i, j)),
        out_shape=jax.ShapeDtypeStruct((n_tok, n_out), BF16),
        scratch_shapes=[pltpu.VMEM((tm, D_MODEL), BF16)],
        compiler_params=pltpu.CompilerParams(
            dimension_semantics=("parallel", "arbitrary"), vmem_limit_bytes=VMEM_LIMIT),
        name="inproj",
    )(x2d, gain.reshape(1, D_MODEL), w_bf16)


class _Dir:
    def __init__(self, q_ref, f_ref, v_ref, o_ref, st_ref, lb, tri, mask, mid_row, last_row):
        self.q_ref, self.f_ref, self.v_ref, self.o_ref, self.st_ref = q_ref, f_ref, v_ref, o_ref, st_ref
        self.lb, self.tri, self.mask, self.mid_row, self.last_row = lb, tri, mask, mid_row, last_row


def _gla_group(chains):
    c = HG_CHUNK
    dk = HG_DK
    pre = []
    for d, r0 in chains:
        q = d.q_ref[pl.ds(r0, c), :].astype(F32)
        q = q * _sigmoid(q)
        f = d.lb + (1.0 - d.lb) * _sigmoid(d.f_ref[pl.ds(r0, c), :].astype(F32))
        logf = jnp.log(f)
        hi = logf.astype(BF16)
        lo = (logf - hi.astype(F32)).astype(BF16)
        pre.append((q, 1.0 - f, jnp.concatenate([hi, lo], axis=1)))
    cums = []
    for (d, _), (_, _, hl) in zip(chains, pre):
        r = jnp.dot(d.tri, hl, preferred_element_type=F32)
        cums.append(r[:, :dk] + r[:, dk:])
    mids = []
    for (d, r0), (q, k, _), cum in zip(chains, pre, cums):
        mid = cum[d.mid_row:d.mid_row + 1, :]
        last = cum[d.last_row:d.last_row + 1, :]
        a = cum - mid
        qa = q * jnp.exp(a)
        ka = k * jnp.exp(-a)
        qd = (qa * jnp.exp(mid)).astype(BF16)
        kend = (ka * jnp.exp(last - mid)).astype(BF16)
        dec = jnp.exp(last)
        scores = lax.dot_general(qa.astype(BF16), ka.astype(BF16), NT_DIMS,
                                 preferred_element_type=F32)
        mids.append((qd, kend, dec, scores))
    outs = []
    for (d, r0), (qd, kend, dec, scores) in zip(chains, mids):
        v = d.v_ref[pl.ds(r0, c), :]
        sm = jnp.where(d.mask, scores, 0.0).astype(BF16)
        o_intra = jnp.dot(sm, v, preferred_element_type=F32)
        upd = jnp.dot(jnp.transpose(v.astype(F32)).astype(BF16), kend, preferred_element_type=F32)
        outs.append((o_intra, upd))
    for (d, r0), (qd, _, dec, _), (o_intra, upd) in zip(chains, mids, outs):
        st = d.st_ref[...]
        d.o_ref[pl.ds(r0, c), :] = (o_intra + lax.dot_general(
            qd, st.astype(BF16), NT_DIMS, preferred_element_type=F32)).astype(d.o_ref.dtype)
        d.st_ref[...] = st * dec + upd


def _hgrn2_body(qf_ref, ff_ref, vf_ref, qb_ref, fb_ref, vb_ref, lbf_ref, lbb_ref,
                of_ref, ob_ref, sf_ref, sb_ref):
    @pl.when(pl.program_id(2) == 0)
    def _():
        sf_ref[...] = jnp.zeros_like(sf_ref)
        sb_ref[...] = jnp.zeros_like(sb_ref)

    c = HG_CHUNK
    g = HG_GROUP
    n_chunks = qf_ref.shape[0] // c
    row = lax.broadcasted_iota(jnp.int32, (c, c), 0)
    col = lax.broadcasted_iota(jnp.int32, (c, c), 1)
    lower = col <= row
    upper = col >= row
    fwd = _Dir(qf_ref, ff_ref, vf_ref, of_ref, sf_ref, lbf_ref[...],
               jnp.where(lower, 1.0, 0.0).astype(BF16), lower, c // 2 - 1, c - 1)
    bwd = _Dir(qb_ref, fb_ref, vb_ref, ob_ref, sb_ref, lbb_ref[...],
               jnp.where(upper, 1.0, 0.0).astype(BF16), upper, c // 2, 0)

    def step(n, carry):
        chains = []
        for u in range(g):
            chains.append((fwd, pl.multiple_of((n * g + u) * c, c)))
        for u in range(g):
            chains.append((bwd, pl.multiple_of((n_chunks - 1 - n * g - u) * c, c)))
        _gla_group(chains)
        return carry

    lax.fori_loop(0, n_chunks // g, step, 0)


def _hgrn2(proj, lb_fwd, lb_bwd, bsz, seqlen, rows=1024):
    n_tok = bsz * seqlen
    rows = min(rows, seqlen)
    nb = seqlen // rows
    dk = HG_DK

    def fwd_map(col0):
        return lambda b, h, j: (b * nb + j, col0 // dk + h)

    def bwd_map(col0):
        return lambda b, h, j: (b * nb + (nb - 1 - j), col0 // dk + h)

    blk = (rows, dk)
    lb_spec = pl.BlockSpec((1, dk), lambda b, h, j: (0, h))
    return pl.pallas_call(
        _hgrn2_body,
        grid=(bsz, HG_HEADS, nb),
        in_specs=[
            pl.BlockSpec(blk, fwd_map(COL_HQ)), pl.BlockSpec(blk, fwd_map(COL_FF)),
            pl.BlockSpec(blk, fwd_map(COL_HI)),
            pl.BlockSpec(blk, bwd_map(COL_HQ)), pl.BlockSpec(blk, bwd_map(COL_FB)),
            pl.BlockSpec(blk, bwd_map(COL_HI)),
            lb_spec, lb_spec,
        ],
        out_specs=[pl.BlockSpec(blk, fwd_map(0)), pl.BlockSpec(blk, bwd_map(0))],
        out_shape=[jax.ShapeDtypeStruct((n_tok, HG_HEADS * dk), BF16)] * 2,
        scratch_shapes=[pltpu.VMEM((dk, dk), F32), pltpu.VMEM((dk, dk), F32)],
        compiler_params=pltpu.CompilerParams(
            dimension_semantics=("parallel", "parallel", "arbitrary"),
            vmem_limit_bytes=VMEM_LIMIT),
        name="hgrn2",
    )(proj, proj, proj, proj, proj, proj, lb_fwd.reshape(1, -1), lb_bwd.reshape(1, -1))


def _attn_body(seqlen, sink_ref, q_ref, kp_ref, ko_ref, kn_ref, vp_ref, vo_ref, vn_ref, o_ref):
    j = pl.program_id(1)
    blk = AT_BLOCK
    span = 3 * blk
    row = lax.broadcasted_iota(jnp.int32, (blk, span), 0)
    col = lax.broadcasted_iota(jnp.int32, (blk, span), 1)
    absrel = jnp.abs(row + blk - col)
    key_pos = j * blk - blk + col
    valid = (absrel <= WINDOW) & (key_pos >= 0) & (key_pos < seqlen)
    base = jnp.where(valid, -absrel.astype(F32), -jnp.inf)
    dh = AT_HEAD_DIM
    lanes = 2 * dh
    lo_half = lax.broadcasted_iota(jnp.int32, (blk, lanes), 1) < dh

    def padded(ref, g):
        t = ref[:, (g // 2) * lanes:(g // 2 + 1) * lanes].astype(F32)
        if g % 2 == 0:
            lo = jnp.where(lo_half, t, 0.0)
            hi = pltpu.roll(lo, dh, axis=1)
        else:
            hi = jnp.where(lo_half, 0.0, t)
            lo = pltpu.roll(hi, dh, axis=1)
        return lo, hi

    def banded(refs, g):
        pads = [padded(r, g) for r in refs]
        return jnp.concatenate([p[0] for p in pads] + [p[1] for p in pads], axis=0).astype(BF16)

    scores, values = [], []
    for g in range(AT_KV_HEADS):
        q2 = q_ref[:, g * 2 * lanes:(g + 1) * 2 * lanes] * (dh ** -0.5)
        lhs = jnp.concatenate([q2[:, :lanes], q2[:, lanes:]], axis=0).astype(BF16)
        scores.append(lax.dot_general(lhs, banded((kp_ref, ko_ref, kn_ref), g), NT_DIMS,
                                      preferred_element_type=F32))
        values.append(banded((vp_ref, vo_ref, vn_ref), g))
    probs, denoms = [], []
    for g in range(AT_KV_HEADS):
        p_rows, d_rows = [], []
        for r in range(2):
            p_cols, d_cols = [], []
            for c in range(2):
                h = g * AT_GROUP + 2 * r + c
                slope = 2.0 ** (-8.0 * (h + 1) / AT_Q_HEADS)
                sink = sink_ref[h]
                s = scores[g][r * blk:(r + 1) * blk, c * span:(c + 1) * span] + slope * base
                m = jnp.maximum(jnp.max(s, axis=-1, keepdims=True), sink)
                p = jnp.exp(s - m)
                d_cols.append(jnp.sum(p, axis=-1, keepdims=True) + jnp.exp(sink - m))
                p_cols.append(p.astype(BF16))
            p_rows.append(jnp.concatenate(p_cols, axis=1))
            d_rows.append(jnp.where(lo_half, d_cols[0], d_cols[1]))
        probs.append(jnp.concatenate(p_rows, axis=0))
        denoms.append(d_rows)
    outs = []
    for g in range(AT_KV_HEADS):
        o = jnp.dot(probs[g], values[g], preferred_element_type=F32)
        for r in range(2):
            outs.append(o[r * blk:(r + 1) * blk, :] / denoms[g][r])
    o_ref[...] = jnp.concatenate(outs, axis=-1).astype(o_ref.dtype)


def _attention(proj, sink, bsz, seqlen):
    n_tok = bsz * seqlen
    nb = seqlen // AT_BLOCK
    kcol = COL_AK // AT_KVWIDTH
    vcol = COL_AV // AT_KVWIDTH

    def prev(c):
        return lambda b, j: (b * nb + jnp.maximum(j - 1, 0), c)

    def own(c):
        return lambda b, j: (b * nb + j, c)

    def nxt(c):
        return lambda b, j: (b * nb + jnp.minimum(j + 1, nb - 1), c)

    kv_blk = (AT_BLOCK, AT_KVWIDTH)
    return pl.pallas_call(
        functools.partial(_attn_body, seqlen),
        grid=(bsz, nb),
        in_specs=[
            pl.BlockSpec(memory_space=pltpu.SMEM),
            pl.BlockSpec((AT_BLOCK, D_MODEL), own(COL_AQ // D_MODEL)),
            pl.BlockSpec(kv_blk, prev(kcol)), pl.BlockSpec(kv_blk, own(kcol)),
            pl.BlockSpec(kv_blk, nxt(kcol)),
            pl.BlockSpec(kv_blk, prev(vcol)), pl.BlockSpec(kv_blk, own(vcol)),
            pl.BlockSpec(kv_blk, nxt(vcol)),
        ],
        out_specs=pl.BlockSpec((AT_BLOCK, D_MODEL), lambda b, j: (b * nb + j, 0)),
        out_shape=jax.ShapeDtypeStruct((n_tok, D_MODEL), BF16),
        compiler_params=pltpu.CompilerParams(
            dimension_semantics=("parallel", "arbitrary"), vmem_limit_bytes=VMEM_LIMIT),
        name="window_attn",
    )(sink.astype(F32), proj, proj, proj, proj, proj, proj, proj)


def _merge_body(of_ref, ob_ref, hg_ref, ga0_ref, ga1_ref, gb0_ref, gb1_ref, ya_ref, x_ref,
                gn_ref, wo_ref, g2_ref, wr_ref, br_ref,
                h_ref, xn_ref, idx_ref, gate_ref, rank_ref, cnt_ref, carry_ref):
    i = pl.program_id(0)

    @pl.when(i == 0)
    def _():
        carry_ref[...] = jnp.zeros_like(carry_ref)

    tm = x_ref.shape[0]
    o = of_ref[...].astype(F32) + ob_ref[...].astype(F32)
    gn = gn_ref[...]
    parts = []
    for h in range(HG_HEADS):
        oh = o[:, h * HG_DK:(h + 1) * HG_DK]
        ms = jnp.mean(oh * oh, axis=-1, keepdims=True)
        parts.append(oh * lax.rsqrt(ms + NORM_EPS) * gn)
    hg = hg_ref[...].astype(F32)
    y_rec = jnp.concatenate(parts, axis=-1) * (hg * _sigmoid(hg))
    gate_a = jnp.concatenate([ga0_ref[...], ga1_ref[...]], axis=-1).astype(F32)
    gate_b = jnp.concatenate([gb0_ref[...], gb1_ref[...]], axis=-1).astype(F32)
    merged = _sigmoid(gate_a) * y_rec + _sigmoid(gate_b) * ya_ref[...].astype(F32)
    h_res = x_ref[...] + jnp.dot(merged.astype(BF16), wo_ref[...], preferred_element_type=F32)
    h_ref[...] = h_res
    ms = jnp.mean(h_res * h_res, axis=-1, keepdims=True)
    xn = h_res * lax.rsqrt(ms + NORM_EPS) * g2_ref[...]
    xn_ref[...] = xn

    logits = lax.dot_general(wr_ref[...], xn, NT_DIMS, preferred_element_type=F32,
                             precision=lax.Precision.HIGHEST) + br_ref[...]
    e_iota = lax.broadcasted_iota(jnp.int32, (N_EXPERTS, tm), 0).astype(F32)
    vals = logits
    tops, idxs = [], []
    for _ in range(TOP_K):
        m = jnp.max(vals, axis=0, keepdims=True)
        sel = jnp.min(jnp.where(vals == m, e_iota, float(N_EXPERTS)), axis=0, keepdims=True)
        tops.append(m)
        idxs.append(sel)
        vals = jnp.where(e_iota == sel, -jnp.inf, vals)
    exps = [jnp.exp(t - tops[0]) for t in tops]
    den = exps[0] + exps[1] + exps[2] + exps[3]
    gate_ref[...] = jnp.concatenate([e / den for e in exps], axis=0)
    idx_ref[...] = jnp.concatenate(idxs, axis=0).astype(jnp.int32)

    chosen = jnp.where(vals == -jnp.inf, 1.0, 0.0)
    s_iota = lax.broadcasted_iota(jnp.int32, (tm, tm), 0)
    t_iota = lax.broadcasted_iota(jnp.int32, (tm, tm), 1)
    strict = jnp.where(s_iota < t_iota, 1.0, 0.0).astype(BF16)
    carry = carry_ref[...]
    prefix = jnp.dot(chosen.astype(BF16), strict, preferred_element_type=F32) + carry
    ranks = [jnp.sum(jnp.where(e_iota == sel, prefix, 0.0), axis=0, keepdims=True) for sel in idxs]
    rank_ref[...] = jnp.concatenate(ranks, axis=0).astype(jnp.int32)
    carry = carry + jnp.sum(chosen, axis=1, keepdims=True)
    carry_ref[...] = carry
    cnt_ref[...] = jnp.broadcast_to(carry, cnt_ref.shape).astype(jnp.int32)


def _merge_route(o_f, o_b, proj, y_att, x2d, gn_gain, w_out_bf16, g2, w_router_t, b_router, tm=256):
    n_tok = x2d.shape[0]
    nblk = n_tok // tm
    row_blk = (tm, D_MODEL)
    half = D_MODEL // 2
    half_blk = (tm, half)

    def rows(c):
        return lambda i: (i, c)

    full = lambda i: (0, 0)
    tok_out = pl.BlockSpec((TOP_K, tm), lambda i: (0, i))
    return pl.pallas_call(
        _merge_body,
        grid=(nblk,),
        in_specs=[
            pl.BlockSpec(row_blk, rows(0)), pl.BlockSpec(row_blk, rows(0)),
            pl.BlockSpec(row_blk, rows(COL_HG // D_MODEL)),
            pl.BlockSpec(half_blk, rows(COL_GA // half)),
            pl.BlockSpec(half_blk, rows(COL_GA // half + 1)),
            pl.BlockSpec(half_blk, rows(COL_GB // half)),
            pl.BlockSpec(half_blk, rows(COL_GB // half + 1)),
            pl.BlockSpec(row_blk, rows(0)), pl.BlockSpec(row_blk, rows(0)),
            pl.BlockSpec((1, HG_DK), full),
            pl.BlockSpec((D_MODEL, D_MODEL), full),
            pl.BlockSpec((1, D_MODEL), full),
            pl.BlockSpec((N_EXPERTS, D_MODEL), full),
            pl.BlockSpec((N_EXPERTS, 1), full),
        ],
        out_specs=[
            pl.BlockSpec(row_blk, rows(0)), pl.BlockSpec(row_blk, rows(0)),
            tok_out, tok_out, tok_out,
            pl.BlockSpec((N_EXPERTS, 128), full),
        ],
        out_shape=[
            jax.ShapeDtypeStruct((n_tok, D_MODEL), F32),
            jax.ShapeDtypeStruct((n_tok, D_MODEL), F32),
            jax.ShapeDtypeStruct((TOP_K, n_tok), jnp.int32),
            jax.ShapeDtypeStruct((TOP_K, n_tok), F32),
            jax.ShapeDtypeStruct((TOP_K, n_tok), jnp.int32),
            jax.ShapeDtypeStruct((N_EXPERTS, 128), jnp.int32),
        ],
        scratch_shapes=[pltpu.VMEM((N_EXPERTS, 1), F32)],
        compiler_params=pltpu.CompilerParams(
            dimension_semantics=("arbitrary",), vmem_limit_bytes=VMEM_LIMIT),
        name="merge_route",
    )(o_f, o_b, proj, proj, proj, proj, proj, y_att, x2d, gn_gain.reshape(1, HG_DK), w_out_bf16,
      g2.reshape(1, D_MODEL), w_router_t, b_router.reshape(N_EXPERTS, 1))


DISPATCH_TOKENS = 128
ISSUE_UNROLL = 8


def _dispatch_body(dest_ref, pend_ref, x_ref, xb_hbm, zero_ref, sem):
    i = pl.program_id(0)
    rows = MOE_ROWS
    nt = DISPATCH_TOKENS

    @pl.when(i == 0)
    def _():
        zero_ref[...] = jnp.zeros_like(zero_ref)
        for e in range(N_EXPERTS):
            end = pend_ref[e]
            start = pl.multiple_of(end - rows, rows)
            prev_end = pend_ref[e - 1] if e > 0 else 0

            @pl.when(end > prev_end)
            def _():
                cp = pltpu.make_async_copy(zero_ref, xb_hbm.at[pl.ds(start, rows)], sem.at[1])
                cp.start()
                cp.wait()

        def zero_block(b, c):
            cp = pltpu.make_async_copy(
                zero_ref, xb_hbm.at[pl.ds(pl.multiple_of(b * rows, rows), rows)], sem.at[1])
            cp.start()
            cp.wait()
            return c
        lax.fori_loop(pend_ref[N_EXPERTS - 1] // rows, xb_hbm.shape[0] // rows, zero_block, 0)

    base = i * (nt * TOP_K)

    def issue(r, c):
        for k in range(TOP_K):
            pltpu.make_async_copy(
                x_ref.at[pl.ds(r, 1)],
                xb_hbm.at[pl.ds(dest_ref[base + r * TOP_K + k], 1)],
                sem.at[0]).start(priority=k % 2)
        return c

    lax.fori_loop(0, nt, issue, 0, unroll=ISSUE_UNROLL)
    pltpu.make_async_copy(xb_hbm.at[pl.ds(0, nt * TOP_K)], xb_hbm.at[pl.ds(0, nt * TOP_K)],
                          sem.at[0]).wait()


def _dispatch(xn, dest_flat, pad_end, n_pad):
    n_tok = xn.shape[0]
    grid_spec = pltpu.PrefetchScalarGridSpec(
        num_scalar_prefetch=2,
        grid=(n_tok // DISPATCH_TOKENS,),
        in_specs=[pl.BlockSpec((DISPATCH_TOKENS, D_MODEL), lambda i, d, p: (i, 0))],
        out_specs=pl.BlockSpec(memory_space=pl.ANY),
        scratch_shapes=[pltpu.VMEM((MOE_ROWS, D_MODEL), F32), pltpu.SemaphoreType.DMA((2,))],
    )
    return pl.pallas_call(
        _dispatch_body,
        grid_spec=grid_spec,
        out_shape=jax.ShapeDtypeStruct((n_pad, D_MODEL), F32),
        compiler_params=pltpu.CompilerParams(dimension_semantics=("arbitrary",)),
        name="moe_dispatch",
    )(dest_flat, pad_end, xn)


def _expert_body(bend_ref, x_hbm, wug_ref, bug_ref, wd_ref, bd_ref, y_hbm,
                 wug_bf, wd_bf, act_ref, xbuf, ybuf, sem_in, sem_out):
    e = pl.program_id(0)
    rows = MOE_ROWS
    n_total = y_hbm.shape[0] // rows
    n_used = bend_ref[N_EXPERTS - 1]
    b_first = jnp.where(e == 0, 0, bend_ref[jnp.maximum(e - 1, 0)])
    b_last = bend_ref[e]

    def x_copy(b, slot):
        return pltpu.make_async_copy(
            x_hbm.at[pl.ds(pl.multiple_of(b * rows, rows), rows)], xbuf.at[slot], sem_in.at[slot])

    def y_copy(b, slot):
        return pltpu.make_async_copy(
            ybuf.at[slot], y_hbm.at[pl.ds(pl.multiple_of(b * rows, rows), rows)], sem_out.at[slot])

    @pl.when((e == 0) & (n_used > 0))
    def _():
        x_copy(0, 0).start()

    ns = D_FF // FF_SLAB
    for s in range(ns):
        lo, hi = s * FF_SLAB, (s + 1) * FF_SLAB
        wug_bf[:, 2 * lo:2 * lo + FF_SLAB] = wug_ref[0, :, lo:hi].astype(BF16)
        wug_bf[:, 2 * lo + FF_SLAB:2 * hi] = wug_ref[0, :, D_FF + lo:D_FF + hi].astype(BF16)
    wd_bf[...] = wd_ref[0].astype(BF16)
    b_ug = bug_ref[0]
    b_d = bd_ref[0]

    def block(b, carry):
        slot = lax.rem(b, 2)
        x_copy(b, slot).wait()

        @pl.when(b + 1 < n_used)
        def _():
            x_copy(b + 1, 1 - slot).start()

        @pl.when(b >= 2)
        def _():
            y_copy(b - 2, slot).wait()

        x = xbuf[slot].astype(BF16)
        for s in range(ns):
            lo, hi = s * FF_SLAB, (s + 1) * FF_SLAB
            hu = jnp.dot(x, wug_bf[:, 2 * lo:2 * hi], preferred_element_type=F32)
            g = jnp.minimum(hu[:, :FF_SLAB] + b_ug[:, lo:hi], SWIGLU_LIMIT)
            u = jnp.clip(hu[:, FF_SLAB:] + b_ug[:, D_FF + lo:D_FF + hi], -SWIGLU_LIMIT, SWIGLU_LIMIT)
            act_ref[:, lo:hi] = (g * _sigmoid(SWIGLU_ALPHA * g) * (u + 1.0)).astype(BF16)
        ybuf[slot] = jnp.dot(act_ref[...], wd_bf[...], preferred_element_type=F32) + b_d
        y_copy(b, slot).start()
        return carry

    lax.fori_loop(b_first, b_last, block, 0)

    @pl.when(e == N_EXPERTS - 1)
    def _():
        for back in (2, 1):
            @pl.when(n_used >= back)
            def _():
                y_copy(n_used - back, lax.rem(n_used - back, 2)).wait()

        ybuf[0] = jnp.zeros((rows, D_MODEL), F32)

        def zero_block(b, carry):
            cp = y_copy(b, 0)
            cp.start()
            cp.wait()
            return carry
        lax.fori_loop(n_used, n_total, zero_block, 0)


def _experts(xb, block_end, w_up_gate, b_up_gate, w_down, b_down):
    n_pad = xb.shape[0]

    def w_map(e, bend):
        return (e, 0, 0)

    grid_spec = pltpu.PrefetchScalarGridSpec(
        num_scalar_prefetch=1,
        grid=(N_EXPERTS,),
        in_specs=[
            pl.BlockSpec(memory_space=pl.ANY),
            pl.BlockSpec((1, D_MODEL, 2 * D_FF), w_map),
            pl.BlockSpec((1, 1, 2 * D_FF), w_map),
            pl.BlockSpec((1, D_FF, D_MODEL), w_map),
            pl.BlockSpec((1, 1, D_MODEL), w_map),
        ],
        out_specs=pl.BlockSpec(memory_space=pl.ANY),
        scratch_shapes=[
            pltpu.VMEM((D_MODEL, 2 * D_FF), BF16), pltpu.VMEM((D_FF, D_MODEL), BF16),
            pltpu.VMEM((MOE_ROWS, D_FF), BF16),
            pltpu.VMEM((2, MOE_ROWS, D_MODEL), F32), pltpu.VMEM((2, MOE_ROWS, D_MODEL), F32),
            pltpu.SemaphoreType.DMA((2,)), pltpu.SemaphoreType.DMA((2,)),
        ],
    )
    return pl.pallas_call(
        _expert_body,
        grid_spec=grid_spec,
        out_shape=jax.ShapeDtypeStruct((n_pad, D_MODEL), F32),
        compiler_params=pltpu.CompilerParams(
            dimension_semantics=("arbitrary",), vmem_limit_bytes=VMEM_LIMIT),
        name="moe_experts",
    )(block_end, xb, w_up_gate, b_up_gate.reshape(N_EXPERTS, 1, -1),
      w_down, b_down.reshape(N_EXPERTS, 1, -1))


COMBINE_TOKENS = 128


def _combine_body(dest_ref, yb_hbm, h_ref, gate_ref, gain_ref, o_ref, buf_ref, sem):
    i = pl.program_id(0)
    n_steps = pl.num_programs(0)
    nt = COMBINE_TOKENS

    def issue(step, slot):
        base = step * nt

        def body(r, c):
            for k in range(TOP_K):
                pltpu.make_async_copy(
                    yb_hbm.at[pl.ds(dest_ref[(base + r) * TOP_K + k], 1)],
                    buf_ref.at[slot, k, pl.ds(r, 1)], sem.at[slot]).start(priority=k % 2)
            return c
        lax.fori_loop(0, nt, body, 0, unroll=ISSUE_UNROLL)

    slot = lax.rem(i, 2)

    @pl.when(i == 0)
    def _():
        issue(0, 0)

    @pl.when(i + 1 < n_steps)
    def _():
        issue(i + 1, 1 - slot)

    pltpu.make_async_copy(buf_ref.at[slot], buf_ref.at[slot], sem.at[slot]).wait()

    gates = gate_ref[...]
    acc = h_ref[...]
    for k in range(TOP_K):
        acc = acc + gates[:, k:k + 1] * buf_ref[slot, k]
    ms = jnp.mean(acc * acc, axis=-1, keepdims=True)
    o_ref[...] = acc * lax.rsqrt(ms + NORM_EPS) * gain_ref[...]


def _combine(yb, dest_flat, h_res, gates_tk, final_gain):
    n_tok = h_res.shape[0]
    nt = COMBINE_TOKENS
    grid_spec = pltpu.PrefetchScalarGridSpec(
        num_scalar_prefetch=1,
        grid=(n_tok // nt,),
        in_specs=[
            pl.BlockSpec(memory_space=pl.ANY),
            pl.BlockSpec((nt, D_MODEL), lambda i, d: (i, 0)),
            pl.BlockSpec((nt, TOP_K), lambda i, d: (i, 0)),
            pl.BlockSpec((1, D_MODEL), lambda i, d: (0, 0)),
        ],
        out_specs=pl.BlockSpec((nt, D_MODEL), lambda i, d: (i, 0)),
        scratch_shapes=[pltpu.VMEM((2, TOP_K, nt, D_MODEL), F32), pltpu.SemaphoreType.DMA((2,))],
    )
    return pl.pallas_call(
        _combine_body,
        grid_spec=grid_spec,
        out_shape=jax.ShapeDtypeStruct((n_tok, D_MODEL), F32),
        compiler_params=pltpu.CompilerParams(
            dimension_semantics=("arbitrary",), vmem_limit_bytes=VMEM_LIMIT),
        name="moe_combine",
    )(dest_flat, yb, h_res, gates_tk, final_gain.reshape(1, D_MODEL))


def kernel(x, norm_mix_gain, w_in, hg_lb_fwd, hg_lb_bwd, hg_norm_gain, attn_sink, w_out,
           norm_ffn_gain, w_router, b_router, w_up_gate, b_up_gate, w_down, b_down,
           final_norm_gain):
    bsz, seqlen, d = x.shape
    n_tok = bsz * seqlen
    x2d = x.reshape(n_tok, d)

    lb_fwd = jax.nn.softmax(hg_lb_fwd.astype(F32), axis=0)[0]
    lb_bwd = jax.nn.softmax(hg_lb_bwd.astype(F32), axis=0)[0]

    proj = _inproj(x2d, norm_mix_gain[0], w_in[0].astype(BF16))
    o_f, o_b = _hgrn2(proj, lb_fwd, lb_bwd, bsz, seqlen)
    y_att = _attention(proj, attn_sink[0], bsz, seqlen)
    h_res, xn, idx_t, gate_t, rank_t, counts = _merge_route(
        o_f, o_b, proj, y_att, x2d, hg_norm_gain[0], w_out[0].astype(BF16),
        norm_ffn_gain[0], w_router[0].T.astype(F32), b_router[0].astype(F32))

    counts = counts[:, 0]
    padded = (counts + MOE_ROWS - 1) // MOE_ROWS * MOE_ROWS
    pad_end = jnp.cumsum(padded).astype(jnp.int32)
    pad_start = pad_end - padded
    n_pad = n_tok * TOP_K + N_EXPERTS * MOE_ROWS
    experts = jnp.arange(N_EXPERTS, dtype=jnp.int32)
    start_of = jnp.sum(jnp.where(idx_t[..., None] == experts, pad_start, 0), axis=-1)
    dest_flat = (start_of + rank_t).T.reshape(-1).astype(jnp.int32)

    xb = _dispatch(xn, dest_flat, pad_end, n_pad)
    yb = _experts(xb, pad_end // MOE_ROWS, w_up_gate[0], b_up_gate[0], w_down[0], b_down[0])
    out = _combine(yb, dest_flat, h_res, gate_t.T, final_norm_gain)
    return out.reshape(bsz, seqlen, d)
```

```python
import functools

import jax
import jax.numpy as jnp
from jax import lax
from jax.experimental import pallas as pl
from jax.experimental.pallas import tpu as pltpu

F32 = jnp.float32
BF16 = jnp.bfloat16

D_MODEL = 1024
HG_HEADS = 8
HG_DK = 128
HG_CHUNK = 128
HG_GROUP = 4
AT_Q_HEADS = 16
AT_KV_HEADS = 4
AT_GROUP = AT_Q_HEADS // AT_KV_HEADS
AT_HEAD_DIM = 64
WINDOW = 128
AT_BLOCK = WINDOW
AT_KVWIDTH = AT_KV_HEADS * AT_HEAD_DIM
N_EXPERTS = 32
TOP_K = 4
D_FF = 1024
SWIGLU_LIMIT = 7.0
SWIGLU_ALPHA = 1.702
NORM_EPS = 1e-5
IN_WIDTH = 8704

COL_HQ, COL_FF, COL_FB, COL_HI, COL_HG = 0, 1024, 2048, 3072, 4096
COL_AQ, COL_AK, COL_AV, COL_GA, COL_GB = 5120, 6144, 6400, 6656, 7680

MOE_ROWS = 256
FF_SLAB = 256
MERGE_SUB = 256
BLOCK_DMA_PRIORITY = 1
VMEM_LIMIT = 48 * 1024 * 1024

NT_DIMS = (((1,), (1,)), ((), ()))


def _sigmoid(x):
    return 0.5 * jnp.tanh(0.5 * x) + 0.5


def _inproj_body(x_ref, g_ref, w_ref, o_ref, xn_ref):
    @pl.when(pl.program_id(1) == 0)
    def _():
        x = x_ref[...]
        ms = jnp.mean(x * x, axis=-1, keepdims=True)
        xn_ref[...] = (x * lax.rsqrt(ms + NORM_EPS) * g_ref[...]).astype(BF16)

    o_ref[...] = jnp.dot(xn_ref[...], w_ref[...], preferred_element_type=F32).astype(o_ref.dtype)


def _inproj(x2d, gain, w_bf16, tm=1024, tn=2176):
    n_tok = x2d.shape[0]
    n_out = w_bf16.shape[1]
    return pl.pallas_call(
        _inproj_body,
        grid=(n_tok // tm, n_out // tn),
        in_specs=[
            pl.BlockSpec((tm, D_MODEL), lambda i, j: (i, 0)),
            pl.BlockSpec((1, D_MODEL), lambda i, j: (0, 0)),
            pl.BlockSpec((D_MODEL, tn), lambda i, j: (0, j)),
        ],
        out_specs=pl.BlockSpec((tm, tn), lambda i, j: (i, j)),
        out_shape=jax.ShapeDtypeStruct((n_tok, n_out), BF16),
        scratch_shapes=[pltpu.VMEM((tm, D_MODEL), BF16)],
        compiler_params=pltpu.CompilerParams(
            dimension_semantics=("parallel", "arbitrary"), vmem_limit_bytes=VMEM_LIMIT),
        name="inproj",
    )(x2d, gain.reshape(1, D_MODEL), w_bf16)


class _Dir:
    def __init__(self, q_ref, f_ref, v_ref, o_ref, st_ref, lb, tri, mask, mid_row, last_row):
        self.q_ref, self.f_ref, self.v_ref, self.o_ref, self.st_ref = q_ref, f_ref, v_ref, o_ref, st_ref
        self.lb, self.tri, self.mask, self.mid_row, self.last_row = lb, tri, mask, mid_row, last_row


def _gla_group(chains):
    c = HG_CHUNK
    dk = HG_DK
    pre = []
    for d, r0 in chains:
        q = d.q_ref[pl.ds(r0, c), :].astype(F32)
        q = q * _sigmoid(q)
        f = d.lb + (1.0 - d.lb) * _sigmoid(d.f_ref[pl.ds(r0, c), :].astype(F32))
        logf = jnp.log(f)
        hi = logf.astype(BF16)
        lo = (logf - hi.astype(F32)).astype(BF16)
        pre.append((q, 1.0 - f, jnp.concatenate([hi, lo], axis=1)))
    cums = []
    for (d, _), (_, _, hl) in zip(chains, pre):
        r = jnp.dot(d.tri, hl, preferred_element_type=F32)
        cums.append(r[:, :dk] + r[:, dk:])
    mids = []
    for (d, r0), (q, k, _), cum in zip(chains, pre, cums):
        mid = cum[d.mid_row:d.mid_row + 1, :]
        last = cum[d.last_row:d.last_row + 1, :]
        a = cum - mid
        qa = q * jnp.exp(a)
        ka = k * jnp.exp(-a)
        qd = (qa * jnp.exp(mid)).astype(BF16)
        kend = (ka * jnp.exp(last - mid)).astype(BF16)
        dec = jnp.exp(last)
        scores = lax.dot_general(qa.astype(BF16), ka.astype(BF16), NT_DIMS,
                                 preferred_element_type=F32)
        mids.append((qd, kend, dec, scores))
    outs = []
    for (d, r0), (qd, kend, dec, scores) in zip(chains, mids):
        v = d.v_ref[pl.ds(r0, c), :]
        sm = jnp.where(d.mask, scores, 0.0).astype(BF16)
        o_intra = jnp.dot(sm, v, preferred_element_type=F32)
        upd = jnp.dot(jnp.transpose(v.astype(F32)).astype(BF16), kend, preferred_element_type=F32)
        outs.append((o_intra, upd))
    for (d, r0), (qd, _, dec, _), (o_intra, upd) in zip(chains, mids, outs):
        st = d.st_ref[...]
        d.o_ref[pl.ds(r0, c), :] = (o_intra + lax.dot_general(
            qd, st.astype(BF16), NT_DIMS, preferred_element_type=F32)).astype(d.o_ref.dtype)
        d.st_ref[...] = st * dec + upd


def _hgrn2_body(qf_ref, ff_ref, vf_ref, qb_ref, fb_ref, vb_ref, lbf_ref, lbb_ref,
                of_ref, ob_ref, sf_ref, sb_ref):
    @pl.when(pl.program_id(2) == 0)
    def _():
        sf_ref[...] = jnp.zeros_like(sf_ref)
        sb_ref[...] = jnp.zeros_like(sb_ref)

    c = HG_CHUNK
    g = HG_GROUP
    n_chunks = qf_ref.shape[0] // c
    row = lax.broadcasted_iota(jnp.int32, (c, c), 0)
    col = lax.broadcasted_iota(jnp.int32, (c, c), 1)
    lower = col <= row
    upper = col >= row
    fwd = _Dir(qf_ref, ff_ref, vf_ref, of_ref, sf_ref, lbf_ref[...],
               jnp.where(lower, 1.0, 0.0).astype(BF16), lower, c // 2 - 1, c - 1)
    bwd = _Dir(qb_ref, fb_ref, vb_ref, ob_ref, sb_ref, lbb_ref[...],
               jnp.where(upper, 1.0, 0.0).astype(BF16), upper, c // 2, 0)

    def step(n, carry):
        chains = []
        for u in range(g):
            chains.append((fwd, pl.multiple_of((n * g + u) * c, c)))
        for u in range(g):
            chains.append((bwd, pl.multiple_of((n_chunks - 1 - n * g - u) * c, c)))
        _gla_group(chains)
        return carry

    lax.fori_loop(0, n_chunks // g, step, 0)


def _hgrn2(proj, lb_fwd, lb_bwd, bsz, seqlen, rows=1024):
    n_tok = bsz * seqlen
    rows = min(rows, seqlen)
    nb = seqlen // rows
    dk = HG_DK

    def fwd_map(col0):
        return lambda b, h, j: (b * nb + j, col0 // dk + h)

    def bwd_map(col0):
        return lambda b, h, j: (b * nb + (nb - 1 - j), col0 // dk + h)

    blk = (rows, dk)
    lb_spec = pl.BlockSpec((1, dk), lambda b, h, j: (0, h))
    return pl.pallas_call(
        _hgrn2_body,
        grid=(bsz, HG_HEADS, nb),
        in_specs=[
            pl.BlockSpec(blk, fwd_map(COL_HQ)), pl.BlockSpec(blk, fwd_map(COL_FF)),
            pl.BlockSpec(blk, fwd_map(COL_HI)),
            pl.BlockSpec(blk, bwd_map(COL_HQ)), pl.BlockSpec(blk, bwd_map(COL_FB)),
            pl.BlockSpec(blk, bwd_map(COL_HI)),
            lb_spec, lb_spec,
        ],
        out_specs=[pl.BlockSpec(blk, fwd_map(0)), pl.BlockSpec(blk, bwd_map(0))],
        out_shape=[jax.ShapeDtypeStruct((n_tok, HG_HEADS * dk), BF16)] * 2,
        scratch_shapes=[pltpu.VMEM((dk, dk), F32), pltpu.VMEM((dk, dk), F32)],
        compiler_params=pltpu.CompilerParams(
            dimension_semantics=("parallel", "parallel", "arbitrary"),
            vmem_limit_bytes=VMEM_LIMIT),
        name="hgrn2",
    )(proj, proj, proj, proj, proj, proj, lb_fwd.reshape(1, -1), lb_bwd.reshape(1, -1))


def _attn_body(seqlen, sink_ref, q_ref, kp_ref, ko_ref, kn_ref, vp_ref, vo_ref, vn_ref, o_ref):
    j = pl.program_id(1)
    blk = AT_BLOCK
    span = 3 * blk
    row = lax.broadcasted_iota(jnp.int32, (blk, span), 0)
    col = lax.broadcasted_iota(jnp.int32, (blk, span), 1)
    absrel = jnp.abs(row + blk - col)
    key_pos = j * blk - blk + col
    valid = (absrel <= WINDOW) & (key_pos >= 0) & (key_pos < seqlen)
    base = jnp.where(valid, -absrel.astype(F32), -jnp.inf)
    dh = AT_HEAD_DIM
    lanes = 2 * dh
    lo_half = lax.broadcasted_iota(jnp.int32, (blk, lanes), 1) < dh

    def padded(ref, g):
        t = ref[:, (g // 2) * lanes:(g // 2 + 1) * lanes].astype(F32)
        if g % 2 == 0:
            lo = jnp.where(lo_half, t, 0.0)
            hi = pltpu.roll(lo, dh, axis=1)
        else:
            hi = jnp.where(lo_half, 0.0, t)
            lo = pltpu.roll(hi, dh, axis=1)
        return lo, hi

    def banded(refs, g):
        pads = [padded(r, g) for r in refs]
        return jnp.concatenate([p[0] for p in pads] + [p[1] for p in pads], axis=0).astype(BF16)

    scores, values = [], []
    for g in range(AT_KV_HEADS):
        q2 = q_ref[:, g * 2 * lanes:(g + 1) * 2 * lanes] * (dh ** -0.5)
        lhs = jnp.concatenate([q2[:, :lanes], q2[:, lanes:]], axis=0).astype(BF16)
        scores.append(lax.dot_general(lhs, banded((kp_ref, ko_ref, kn_ref), g), NT_DIMS,
                                      preferred_element_type=F32))
        values.append(banded((vp_ref, vo_ref, vn_ref), g))
    probs, denoms = [], []
    for g in range(AT_KV_HEADS):
        p_rows, d_rows = [], []
        for r in range(2):
            p_cols, d_cols = [], []
            for c in range(2):
                h = g * AT_GROUP + 2 * r + c
                slope = 2.0 ** (-8.0 * (h + 1) / AT_Q_HEADS)
                sink = sink_ref[h]
                s = scores[g][r * blk:(r + 1) * blk, c * span:(c + 1) * span] + slope * base
                m = jnp.maximum(jnp.max(s, axis=-1, keepdims=True), sink)
                p = jnp.exp(s - m)
                d_cols.append(jnp.sum(p, axis=-1, keepdims=True) + jnp.exp(sink - m))
                p_cols.append(p.astype(BF16))
            p_rows.append(jnp.concatenate(p_cols, axis=1))
            d_rows.append(jnp.where(lo_half, d_cols[0], d_cols[1]))
        probs.append(jnp.concatenate(p_rows, axis=0))
        denoms.append(d_rows)
    outs = []
    for g in range(AT_KV_HEADS):
        o = jnp.dot(probs[g], values[g], preferred_element_type=F32)
        for r in range(2):
            outs.append(o[r * blk:(r + 1) * blk, :] / denoms[g][r])
    o_ref[...] = jnp.concatenate(outs, axis=-1).astype(o_ref.dtype)


def _attention(proj, sink, bsz, seqlen):
    n_tok = bsz * seqlen
    nb = seqlen // AT_BLOCK
    kcol = COL_AK // AT_KVWIDTH
    vcol = COL_AV // AT_KVWIDTH

    def prev(c):
        return lambda b, j: (b * nb + jnp.maximum(j - 1, 0), c)

    def own(c):
        return lambda b, j: (b * nb + j, c)

    def nxt(c):
        return lambda b, j: (b * nb + jnp.minimum(j + 1, nb - 1), c)

    kv_blk = (AT_BLOCK, AT_KVWIDTH)
    return pl.pallas_call(
        functools.partial(_attn_body, seqlen),
        grid=(bsz, nb),
        in_specs=[
            pl.BlockSpec(memory_space=pltpu.SMEM),
            pl.BlockSpec((AT_BLOCK, D_MODEL), own(COL_AQ // D_MODEL)),
            pl.BlockSpec(kv_blk, prev(kcol)), pl.BlockSpec(kv_blk, own(kcol)),
            pl.BlockSpec(kv_blk, nxt(kcol)),
            pl.BlockSpec(kv_blk, prev(vcol)), pl.BlockSpec(kv_blk, own(vcol)),
            pl.BlockSpec(kv_blk, nxt(vcol)),
        ],
        out_specs=pl.BlockSpec((AT_BLOCK, D_MODEL), lambda b, j: (b * nb + j, 0)),
        out_shape=jax.ShapeDtypeStruct((n_tok, D_MODEL), BF16),
        compiler_params=pltpu.CompilerParams(
            dimension_semantics=("parallel", "arbitrary"), vmem_limit_bytes=VMEM_LIMIT),
        name="window_attn",
    )(sink.astype(F32), proj, proj, proj, proj, proj, proj, proj)


def _merge_body(of_ref, ob_ref, hg_ref, ga0_ref, ga1_ref, gb0_ref, gb1_ref, ya_ref, x_ref,
                gn_ref, wo_ref, g2_ref, wr_ref, br_ref,
                h_ref, xn_ref, idx_ref, gate_ref, rank_ref, cnt_ref, carry_ref):
    i = pl.program_id(0)

    @pl.when(i == 0)
    def _():
        carry_ref[...] = jnp.zeros_like(carry_ref)

    sub = MERGE_SUB
    subs = [slice(s0, s0 + sub) for s0 in range(0, x_ref.shape[0], sub)]
    gn = gn_ref[...]

    merged = []
    for rs in subs:
        o = of_ref[rs, :].astype(F32) + ob_ref[rs, :].astype(F32)
        parts = []
        for h in range(HG_HEADS):
            oh = o[:, h * HG_DK:(h + 1) * HG_DK]
            ms = jnp.mean(oh * oh, axis=-1, keepdims=True)
            parts.append(oh * lax.rsqrt(ms + NORM_EPS) * gn)
        hg = hg_ref[rs, :].astype(F32)
        y_rec = jnp.concatenate(parts, axis=-1) * (hg * _sigmoid(hg))
        gate_a = jnp.concatenate([ga0_ref[rs, :], ga1_ref[rs, :]], axis=-1).astype(F32)
        gate_b = jnp.concatenate([gb0_ref[rs, :], gb1_ref[rs, :]], axis=-1).astype(F32)
        merged.append((_sigmoid(gate_a) * y_rec
                       + _sigmoid(gate_b) * ya_ref[rs, :].astype(F32)).astype(BF16))

    xns = []
    for rs, mg in zip(subs, merged):
        h_res = x_ref[rs, :] + jnp.dot(mg, wo_ref[...], preferred_element_type=F32)
        h_ref[rs, :] = h_res
        ms = jnp.mean(h_res * h_res, axis=-1, keepdims=True)
        xn = h_res * lax.rsqrt(ms + NORM_EPS) * g2_ref[...]
        xn_ref[rs, :] = xn
        xns.append(xn)

    logits = [lax.dot_general(wr_ref[...], xn, NT_DIMS, preferred_element_type=F32,
                              precision=lax.Precision.HIGHEST) + br_ref[...] for xn in xns]
    e_iota = lax.broadcasted_iota(jnp.int32, (N_EXPERTS, sub), 0).astype(F32)
    picked = []
    for rs, vals in zip(subs, logits):
        tops, idxs = [], []
        for _ in range(TOP_K):
            m = jnp.max(vals, axis=0, keepdims=True)
            sel = jnp.min(jnp.where(vals == m, e_iota, float(N_EXPERTS)), axis=0, keepdims=True)
            tops.append(m)
            idxs.append(sel)
            vals = jnp.where(e_iota == sel, -jnp.inf, vals)
        exps = [jnp.exp(t - tops[0]) for t in tops]
        den = exps[0] + exps[1] + exps[2] + exps[3]
        gate_ref[:, rs] = jnp.concatenate([e / den for e in exps], axis=0)
        idx_ref[:, rs] = jnp.concatenate(idxs, axis=0).astype(jnp.int32)
        picked.append((idxs, jnp.where(vals == -jnp.inf, 1.0, 0.0)))

    s_iota = lax.broadcasted_iota(jnp.int32, (sub, sub), 0)
    t_iota = lax.broadcasted_iota(jnp.int32, (sub, sub), 1)
    strict = jnp.where(s_iota < t_iota, 1.0, 0.0).astype(BF16)
    prefixes = [jnp.dot(chosen.astype(BF16), strict, preferred_element_type=F32)
                for _, chosen in picked]
    carry = carry_ref[...]
    for rs, (idxs, chosen), prefix in zip(subs, picked, prefixes):
        prefix = prefix + carry
        ranks = [jnp.sum(jnp.where(e_iota == sel, prefix, 0.0), axis=0, keepdims=True)
                 for sel in idxs]
        rank_ref[:, rs] = jnp.concatenate(ranks, axis=0).astype(jnp.int32)
        carry = carry + jnp.sum(chosen, axis=1, keepdims=True)
    carry_ref[...] = carry
    cnt_ref[...] = jnp.broadcast_to(carry, cnt_ref.shape).astype(jnp.int32)


def _merge_route(o_f, o_b, proj, y_att, x2d, gn_gain, w_out_bf16, g2, w_router_t, b_router, tm=512):
    n_tok = x2d.shape[0]
    nblk = n_tok // tm
    row_blk = (tm, D_MODEL)
    half = D_MODEL // 2
    half_blk = (tm, half)

    def rows(c):
        return lambda i: (i, c)

    full = lambda i: (0, 0)
    tok_out = pl.BlockSpec((TOP_K, tm), lambda i: (0, i))
    return pl.pallas_call(
        _merge_body,
        grid=(nblk,),
        in_specs=[
            pl.BlockSpec(row_blk, rows(0)), pl.BlockSpec(row_blk, rows(0)),
            pl.BlockSpec(row_blk, rows(COL_HG // D_MODEL)),
            pl.BlockSpec(half_blk, rows(COL_GA // half)),
            pl.BlockSpec(half_blk, rows(COL_GA // half + 1)),
            pl.BlockSpec(half_blk, rows(COL_GB // half)),
            pl.BlockSpec(half_blk, rows(COL_GB // half + 1)),
            pl.BlockSpec(row_blk, rows(0)), pl.BlockSpec(row_blk, rows(0)),
            pl.BlockSpec((1, HG_DK), full),
            pl.BlockSpec((D_MODEL, D_MODEL), full),
            pl.BlockSpec((1, D_MODEL), full),
            pl.BlockSpec((N_EXPERTS, D_MODEL), full),
            pl.BlockSpec((N_EXPERTS, 1), full),
        ],
        out_specs=[
            pl.BlockSpec(row_blk, rows(0)), pl.BlockSpec(row_blk, rows(0)),
            tok_out, tok_out, tok_out,
            pl.BlockSpec((N_EXPERTS, 128), full),
        ],
        out_shape=[
            jax.ShapeDtypeStruct((n_tok, D_MODEL), F32),
            jax.ShapeDtypeStruct((n_tok, D_MODEL), F32),
            jax.ShapeDtypeStruct((TOP_K, n_tok), jnp.int32),
            jax.ShapeDtypeStruct((TOP_K, n_tok), F32),
            jax.ShapeDtypeStruct((TOP_K, n_tok), jnp.int32),
            jax.ShapeDtypeStruct((N_EXPERTS, 128), jnp.int32),
        ],
        scratch_shapes=[pltpu.VMEM((N_EXPERTS, 1), F32)],
        compiler_params=pltpu.CompilerParams(
            dimension_semantics=("arbitrary",), vmem_limit_bytes=VMEM_LIMIT),
        name="merge_route",
    )(o_f, o_b, proj, proj, proj, proj, proj, y_att, x2d, gn_gain.reshape(1, HG_DK), w_out_bf16,
      g2.reshape(1, D_MODEL), w_router_t, b_router.reshape(N_EXPERTS, 1))


DISPATCH_TOKENS = 128
ISSUE_UNROLL = 8


def _dispatch_body(dest_ref, pend_ref, x_ref, xb_hbm, zero_ref, sem):
    i = pl.program_id(0)
    rows = MOE_ROWS
    nt = DISPATCH_TOKENS

    @pl.when(i == 0)
    def _():
        zero_ref[...] = jnp.zeros_like(zero_ref)
        for e in range(N_EXPERTS):
            end = pend_ref[e]
            start = pl.multiple_of(end - rows, rows)
            prev_end = pend_ref[e - 1] if e > 0 else 0

            @pl.when(end > prev_end)
            def _():
                cp = pltpu.make_async_copy(zero_ref, xb_hbm.at[pl.ds(start, rows)], sem.at[1])
                cp.start()
                cp.wait()

        def zero_block(b, c):
            cp = pltpu.make_async_copy(
                zero_ref, xb_hbm.at[pl.ds(pl.multiple_of(b * rows, rows), rows)], sem.at[1])
            cp.start()
            cp.wait()
            return c
        lax.fori_loop(pend_ref[N_EXPERTS - 1] // rows, xb_hbm.shape[0] // rows, zero_block, 0)

    base = i * (nt * TOP_K)

    def issue(r, c):
        for k in range(TOP_K):
            pltpu.make_async_copy(
                x_ref.at[pl.ds(r, 1)],
                xb_hbm.at[pl.ds(dest_ref[base + r * TOP_K + k], 1)],
                sem.at[0]).start(priority=k % 2)
        return c

    lax.fori_loop(0, nt, issue, 0, unroll=ISSUE_UNROLL)
    pltpu.make_async_copy(xb_hbm.at[pl.ds(0, nt * TOP_K)], xb_hbm.at[pl.ds(0, nt * TOP_K)],
                          sem.at[0]).wait()


def _dispatch(xn, dest_flat, pad_end, n_pad):
    n_tok = xn.shape[0]
    grid_spec = pltpu.PrefetchScalarGridSpec(
        num_scalar_prefetch=2,
        grid=(n_tok // DISPATCH_TOKENS,),
        in_specs=[pl.BlockSpec((DISPATCH_TOKENS, D_MODEL), lambda i, d, p: (i, 0))],
        out_specs=pl.BlockSpec(memory_space=pl.ANY),
        scratch_shapes=[pltpu.VMEM((MOE_ROWS, D_MODEL), F32), pltpu.SemaphoreType.DMA((2,))],
    )
    return pl.pallas_call(
        _dispatch_body,
        grid_spec=grid_spec,
        out_shape=jax.ShapeDtypeStruct((n_pad, D_MODEL), F32),
        compiler_params=pltpu.CompilerParams(dimension_semantics=("arbitrary",)),
        name="moe_dispatch",
    )(dest_flat, pad_end, xn)


def _expert_body(bend_ref, x_hbm, wug_ref, bug_ref, wd_ref, bd_ref, y_hbm,
                 wug_bf, wd_bf, act_ref, xbuf, ybuf, sem_in, sem_out):
    e = pl.program_id(0)
    rows = MOE_ROWS
    n_total = y_hbm.shape[0] // rows
    n_used = bend_ref[N_EXPERTS - 1]
    b_first = jnp.where(e == 0, 0, bend_ref[jnp.maximum(e - 1, 0)])
    b_last = bend_ref[e]

    def x_copy(b, slot):
        return pltpu.make_async_copy(
            x_hbm.at[pl.ds(pl.multiple_of(b * rows, rows), rows)], xbuf.at[slot], sem_in.at[slot])

    def y_copy(b, slot):
        return pltpu.make_async_copy(
            ybuf.at[slot], y_hbm.at[pl.ds(pl.multiple_of(b * rows, rows), rows)], sem_out.at[slot])

    @pl.when((e == 0) & (n_used > 0))
    def _():
        x_copy(0, 0).start(priority=BLOCK_DMA_PRIORITY)

    ns = D_FF // FF_SLAB
    for s in range(ns):
        lo, hi = s * FF_SLAB, (s + 1) * FF_SLAB
        wug_bf[:, 2 * lo:2 * lo + FF_SLAB] = wug_ref[0, :, lo:hi].astype(BF16)
        wug_bf[:, 2 * lo + FF_SLAB:2 * hi] = wug_ref[0, :, D_FF + lo:D_FF + hi].astype(BF16)
    wd_bf[...] = wd_ref[0].astype(BF16)
    b_ug = bug_ref[0]
    b_d = bd_ref[0]

    def block(b, carry):
        slot = lax.rem(b, 2)
        x_copy(b, slot).wait()

        @pl.when(b + 1 < n_used)
        def _():
            x_copy(b + 1, 1 - slot).start(priority=BLOCK_DMA_PRIORITY)

        @pl.when(b >= 2)
        def _():
            y_copy(b - 2, slot).wait()

        x = xbuf[slot].astype(BF16)
        for s in range(ns):
            lo, hi = s * FF_SLAB, (s + 1) * FF_SLAB
            hu = jnp.dot(x, wug_bf[:, 2 * lo:2 * hi], preferred_element_type=F32)
            g = jnp.minimum(hu[:, :FF_SLAB] + b_ug[:, lo:hi], SWIGLU_LIMIT)
            u = jnp.clip(hu[:, FF_SLAB:] + b_ug[:, D_FF + lo:D_FF + hi], -SWIGLU_LIMIT, SWIGLU_LIMIT)
            act_ref[:, lo:hi] = (g * _sigmoid(SWIGLU_ALPHA * g) * (u + 1.0)).astype(BF16)
        ybuf[slot] = jnp.dot(act_ref[...], wd_bf[...], preferred_element_type=F32) + b_d
        y_copy(b, slot).start(priority=BLOCK_DMA_PRIORITY)
        return carry

    lax.fori_loop(b_first, b_last, block, 0)

    @pl.when(e == N_EXPERTS - 1)
    def _():
        for back in (2, 1):
            @pl.when(n_used >= back)
            def _():
                y_copy(n_used - back, lax.rem(n_used - back, 2)).wait()

        ybuf[0] = jnp.zeros((rows, D_MODEL), F32)

        def zero_block(b, carry):
            cp = y_copy(b, 0)
            cp.start()
            cp.wait()
            return carry
        lax.fori_loop(n_used, n_total, zero_block, 0)


def _experts(xb, block_end, w_up_gate, b_up_gate, w_down, b_down):
    n_pad = xb.shape[0]

    def w_map(e, bend):
        return (e, 0, 0)

    grid_spec = pltpu.PrefetchScalarGridSpec(
        num_scalar_prefetch=1,
        grid=(N_EXPERTS,),
        in_specs=[
            pl.BlockSpec(memory_space=pl.ANY),
            pl.BlockSpec((1, D_MODEL, 2 * D_FF), w_map),
            pl.BlockSpec((1, 1, 2 * D_FF), w_map),
            pl.BlockSpec((1, D_FF, D_MODEL), w_map),
            pl.BlockSpec((1, 1, D_MODEL), w_map),
        ],
        out_specs=pl.BlockSpec(memory_space=pl.ANY),
        scratch_shapes=[
            pltpu.VMEM((D_MODEL, 2 * D_FF), BF16), pltpu.VMEM((D_FF, D_MODEL), BF16),
            pltpu.VMEM((MOE_ROWS, D_FF), BF16),
            pltpu.VMEM((2, MOE_ROWS, D_MODEL), F32), pltpu.VMEM((2, MOE_ROWS, D_MODEL), F32),
            pltpu.SemaphoreType.DMA((2,)), pltpu.SemaphoreType.DMA((2,)),
        ],
    )
    return pl.pallas_call(
        _expert_body,
        grid_spec=grid_spec,
        out_shape=jax.ShapeDtypeStruct((n_pad, D_MODEL), F32),
        compiler_params=pltpu.CompilerParams(
            dimension_semantics=("arbitrary",), vmem_limit_bytes=VMEM_LIMIT),
        name="moe_experts",
    )(block_end, xb, w_up_gate, b_up_gate.reshape(N_EXPERTS, 1, -1),
      w_down, b_down.reshape(N_EXPERTS, 1, -1))


COMBINE_TOKENS = 128


def _combine_body(dest_ref, yb_hbm, h_ref, gate_ref, gain_ref, o_ref, buf_ref, sem):
    i = pl.program_id(0)
    n_steps = pl.num_programs(0)
    nt = COMBINE_TOKENS

    def issue(step, slot):
        base = step * nt

        def body(r, c):
            for k in range(TOP_K):
                pltpu.make_async_copy(
                    yb_hbm.at[pl.ds(dest_ref[(base + r) * TOP_K + k], 1)],
                    buf_ref.at[slot, k, pl.ds(r, 1)], sem.at[slot]).start(priority=k % 2)
            return c
        lax.fori_loop(0, nt, body, 0, unroll=ISSUE_UNROLL)

    slot = lax.rem(i, 2)

    @pl.when(i == 0)
    def _():
        issue(0, 0)

    @pl.when(i + 1 < n_steps)
    def _():
        issue(i + 1, 1 - slot)

    pltpu.make_async_copy(buf_ref.at[slot], buf_ref.at[slot], sem.at[slot]).wait()

    gates = gate_ref[...]
    acc = h_ref[...]
    for k in range(TOP_K):
        acc = acc + gates[:, k:k + 1] * buf_ref[slot, k]
    ms = jnp.mean(acc * acc, axis=-1, keepdims=True)
    o_ref[...] = acc * lax.rsqrt(ms + NORM_EPS) * gain_ref[...]


def _combine(yb, dest_flat, h_res, gates_tk, final_gain):
    n_tok = h_res.shape[0]
    nt = COMBINE_TOKENS
    grid_spec = pltpu.PrefetchScalarGridSpec(
        num_scalar_prefetch=1,
        grid=(n_tok // nt,),
        in_specs=[
            pl.BlockSpec(memory_space=pl.ANY),
            pl.BlockSpec((nt, D_MODEL), lambda i, d: (i, 0)),
            pl.BlockSpec((nt, TOP_K), lambda i, d: (i, 0)),
            pl.BlockSpec((1, D_MODEL), lambda i, d: (0, 0)),
        ],
        out_specs=pl.BlockSpec((nt, D_MODEL), lambda i, d: (i, 0)),
        scratch_shapes=[pltpu.VMEM((2, TOP_K, nt, D_MODEL), F32), pltpu.SemaphoreType.DMA((2,))],
    )
    return pl.pallas_call(
        _combine_body,
        grid_spec=grid_spec,
        out_shape=jax.ShapeDtypeStruct((n_tok, D_MODEL), F32),
        compiler_params=pltpu.CompilerParams(
            dimension_semantics=("arbitrary",), vmem_limit_bytes=VMEM_LIMIT),
        name="moe_combine",
    )(dest_flat, yb, h_res, gates_tk, final_gain.reshape(1, D_MODEL))


def kernel(x, norm_mix_gain, w_in, hg_lb_fwd, hg_lb_bwd, hg_norm_gain, attn_sink, w_out,
           norm_ffn_gain, w_router, b_router, w_up_gate, b_up_gate, w_down, b_down,
           final_norm_gain):
    bsz, seqlen, d = x.shape
    n_tok = bsz * seqlen
    x2d = x.reshape(n_tok, d)

    lb_fwd = jax.nn.softmax(hg_lb_fwd.astype(F32), axis=0)[0]
    lb_bwd = jax.nn.softmax(hg_lb_bwd.astype(F32), axis=0)[0]

    proj = _inproj(x2d, norm_mix_gain[0], w_in[0].astype(BF16))
    o_f, o_b = _hgrn2(proj, lb_fwd, lb_bwd, bsz, seqlen)
    y_att = _attention(proj, attn_sink[0], bsz, seqlen)
    h_res, xn, idx_t, gate_t, rank_t, counts = _merge_route(
        o_f, o_b, proj, y_att, x2d, hg_norm_gain[0], w_out[0].astype(BF16),
        norm_ffn_gain[0], w_router[0].T.astype(F32), b_router[0].astype(F32))

    counts = counts[:, 0]
    padded = (counts + MOE_ROWS - 1) // MOE_ROWS * MOE_ROWS
    pad_end = jnp.cumsum(padded).astype(jnp.int32)
    pad_start = pad_end - padded
    n_pad = n_tok * TOP_K + N_EXPERTS * MOE_ROWS
    experts = jnp.arange(N_EXPERTS, dtype=jnp.int32)
    start_of = jnp.sum(jnp.where(idx_t[..., None] == experts, pad_start, 0), axis=-1)
    dest_flat = (start_of + rank_t).T.reshape(-1).astype(jnp.int32)

    xb = _dispatch(xn, dest_flat, pad_end, n_pad)
    yb = _experts(xb, pad_end // MOE_ROWS, w_up_gate[0], b_up_gate[0], w_down[0], b_down[0])
    out = _combine(yb, dest_flat, h_res, gate_t.T, final_norm_gain)
    return out.reshape(bsz, seqlen, d)
```

```python
import functools

import jax
import jax.numpy as jnp
from jax import lax
from jax.experimental import pallas as pl
from jax.experimental.pallas import tpu as pltpu

F32 = jnp.float32
BF16 = jnp.bfloat16

D_MODEL = 1024
HG_HEADS = 8
HG_DK = 128
HG_CHUNK = 128
HG_GROUP = 4
AT_Q_HEADS = 16
AT_KV_HEADS = 4
AT_GROUP = AT_Q_HEADS // AT_KV_HEADS
AT_HEAD_DIM = 64
WINDOW = 128
AT_BLOCK = WINDOW
AT_KVWIDTH = AT_KV_HEADS * AT_HEAD_DIM
N_EXPERTS = 32
TOP_K = 4
D_FF = 1024
SWIGLU_LIMIT = 7.0
SWIGLU_ALPHA = 1.702
NORM_EPS = 1e-5
IN_WIDTH = 8704

COL_HQ, COL_FF, COL_FB, COL_HI, COL_HG = 0, 1024, 2048, 3072, 4096
COL_AQ, COL_AK, COL_AV, COL_GA, COL_GB = 5120, 6144, 6400, 6656, 7680

MOE_ROWS = 512
FF_SLAB = 256
MERGE_SUB = 256
BLOCK_DMA_PRIORITY = 1
VMEM_LIMIT = 48 * 1024 * 1024

NT_DIMS = (((1,), (1,)), ((), ()))


def _sigmoid(x):
    return 0.5 * jnp.tanh(0.5 * x) + 0.5


def _inproj_body(x_ref, g_ref, w_ref, o_ref, xn_ref):
    @pl.when(pl.program_id(1) == 0)
    def _():
        x = x_ref[...]
        ms = jnp.mean(x * x, axis=-1, keepdims=True)
        xn_ref[...] = (x * lax.rsqrt(ms + NORM_EPS) * g_ref[...]).astype(BF16)

    o_ref[...] = jnp.dot(xn_ref[...], w_ref[...], preferred_element_type=F32).astype(o_ref.dtype)


def _inproj(x2d, gain, w_bf16, tm=1024, tn=2176):
    n_tok = x2d.shape[0]
    n_out = w_bf16.shape[1]
    return pl.pallas_call(
        _inproj_body,
        grid=(n_tok // tm, n_out // tn),
        in_specs=[
            pl.BlockSpec((tm, D_MODEL), lambda i, j: (i, 0)),
            pl.BlockSpec((1, D_MODEL), lambda i, j: (0, 0)),
            pl.BlockSpec((D_MODEL, tn), lambda i, j: (0, j)),
        ],
        out_specs=pl.BlockSpec((tm, tn), lambda i, j: (i, j)),
        out_shape=jax.ShapeDtypeStruct((n_tok, n_out), BF16),
        scratch_shapes=[pltpu.VMEM((tm, D_MODEL), BF16)],
        compiler_params=pltpu.CompilerParams(
            dimension_semantics=("parallel", "arbitrary"), vmem_limit_bytes=VMEM_LIMIT),
        name="inproj",
    )(x2d, gain.reshape(1, D_MODEL), w_bf16)


class _Dir:
    def __init__(self, q_ref, f_ref, v_ref, o_ref, st_ref, lb, tri, mask, mid_row, last_row):
        self.q_ref, self.f_ref, self.v_ref, self.o_ref, self.st_ref = q_ref, f_ref, v_ref, o_ref, st_ref
        self.lb, self.tri, self.mask, self.mid_row, self.last_row = lb, tri, mask, mid_row, last_row


def _gla_group(chains):
    c = HG_CHUNK
    dk = HG_DK
    pre = []
    for d, r0 in chains:
        q = d.q_ref[pl.ds(r0, c), :].astype(F32)
        q = q * _sigmoid(q)
        f = d.lb + (1.0 - d.lb) * _sigmoid(d.f_ref[pl.ds(r0, c), :].astype(F32))
        logf = jnp.log(f)
        hi = logf.astype(BF16)
        lo = (logf - hi.astype(F32)).astype(BF16)
        pre.append((q, 1.0 - f, jnp.concatenate([hi, lo], axis=1)))
    cums = []
    for (d, _), (_, _, hl) in zip(chains, pre):
        r = jnp.dot(d.tri, hl, preferred_element_type=F32)
        cums.append(r[:, :dk] + r[:, dk:])
    mids = []
    for (d, r0), (q, k, _), cum in zip(chains, pre, cums):
        mid = cum[d.mid_row:d.mid_row + 1, :]
        last = cum[d.last_row:d.last_row + 1, :]
        a = cum - mid
        qa = q * jnp.exp(a)
        ka = k * jnp.exp(-a)
        qd = (qa * jnp.exp(mid)).astype(BF16)
        kend = (ka * jnp.exp(last - mid)).astype(BF16)
        dec = jnp.exp(last)
        scores = lax.dot_general(qa.astype(BF16), ka.astype(BF16), NT_DIMS,
                                 preferred_element_type=F32)
        mids.append((qd, kend, dec, scores))
    outs = []
    for (d, r0), (qd, kend, dec, scores) in zip(chains, mids):
        v = d.v_ref[pl.ds(r0, c), :]
        sm = jnp.where(d.mask, scores, 0.0).astype(BF16)
        o_intra = jnp.dot(sm, v, preferred_element_type=F32)
        upd = jnp.dot(jnp.transpose(v.astype(F32)).astype(BF16), kend, preferred_element_type=F32)
        outs.append((o_intra, upd))
    for (d, r0), (qd, _, dec, _), (o_intra, upd) in zip(chains, mids, outs):
        st = d.st_ref[...]
        d.o_ref[pl.ds(r0, c), :] = (o_intra + lax.dot_general(
            qd, st.astype(BF16), NT_DIMS, preferred_element_type=F32)).astype(d.o_ref.dtype)
        d.st_ref[...] = st * dec + upd


def _hgrn2_body(qf_ref, ff_ref, vf_ref, qb_ref, fb_ref, vb_ref, lbf_ref, lbb_ref,
                of_ref, ob_ref, sf_ref, sb_ref):
    @pl.when(pl.program_id(2) == 0)
    def _():
        sf_ref[...] = jnp.zeros_like(sf_ref)
        sb_ref[...] = jnp.zeros_like(sb_ref)

    c = HG_CHUNK
    g = HG_GROUP
    n_chunks = qf_ref.shape[0] // c
    row = lax.broadcasted_iota(jnp.int32, (c, c), 0)
    col = lax.broadcasted_iota(jnp.int32, (c, c), 1)
    lower = col <= row
    upper = col >= row
    fwd = _Dir(qf_ref, ff_ref, vf_ref, of_ref, sf_ref, lbf_ref[...],
               jnp.where(lower, 1.0, 0.0).astype(BF16), lower, c // 2 - 1, c - 1)
    bwd = _Dir(qb_ref, fb_ref, vb_ref, ob_ref, sb_ref, lbb_ref[...],
               jnp.where(upper, 1.0, 0.0).astype(BF16), upper, c // 2, 0)

    def step(n, carry):
        chains = []
        for u in range(g):
            chains.append((fwd, pl.multiple_of((n * g + u) * c, c)))
        for u in range(g):
            chains.append((bwd, pl.multiple_of((n_chunks - 1 - n * g - u) * c, c)))
        _gla_group(chains)
        return carry

    lax.fori_loop(0, n_chunks // g, step, 0)


def _hgrn2(proj, lb_fwd, lb_bwd, bsz, seqlen, rows=1024):
    n_tok = bsz * seqlen
    rows = min(rows, seqlen)
    nb = seqlen // rows
    dk = HG_DK

    def fwd_map(col0):
        return lambda b, h, j: (b * nb + j, col0 // dk + h)

    def bwd_map(col0):
        return lambda b, h, j: (b * nb + (nb - 1 - j), col0 // dk + h)

    blk = (rows, dk)
    lb_spec = pl.BlockSpec((1, dk), lambda b, h, j: (0, h))
    return pl.pallas_call(
        _hgrn2_body,
        grid=(bsz, HG_HEADS, nb),
        in_specs=[
            pl.BlockSpec(blk, fwd_map(COL_HQ)), pl.BlockSpec(blk, fwd_map(COL_FF)),
            pl.BlockSpec(blk, fwd_map(COL_HI)),
            pl.BlockSpec(blk, bwd_map(COL_HQ)), pl.BlockSpec(blk, bwd_map(COL_FB)),
            pl.BlockSpec(blk, bwd_map(COL_HI)),
            lb_spec, lb_spec,
        ],
        out_specs=[pl.BlockSpec(blk, fwd_map(0)), pl.BlockSpec(blk, bwd_map(0))],
        out_shape=[jax.ShapeDtypeStruct((n_tok, HG_HEADS * dk), BF16)] * 2,
        scratch_shapes=[pltpu.VMEM((dk, dk), F32), pltpu.VMEM((dk, dk), F32)],
        compiler_params=pltpu.CompilerParams(
            dimension_semantics=("parallel", "parallel", "arbitrary"),
            vmem_limit_bytes=VMEM_LIMIT),
        name="hgrn2",
    )(proj, proj, proj, proj, proj, proj, lb_fwd.reshape(1, -1), lb_bwd.reshape(1, -1))


def _attn_body(seqlen, sink_ref, q_ref, kp_ref, ko_ref, kn_ref, vp_ref, vo_ref, vn_ref, o_ref):
    j = pl.program_id(1)
    blk = AT_BLOCK
    span = 3 * blk
    row = lax.broadcasted_iota(jnp.int32, (blk, span), 0)
    col = lax.broadcasted_iota(jnp.int32, (blk, span), 1)
    absrel = jnp.abs(row + blk - col)
    key_pos = j * blk - blk + col
    valid = (absrel <= WINDOW) & (key_pos >= 0) & (key_pos < seqlen)
    base = jnp.where(valid, -absrel.astype(F32), -jnp.inf)
    dh = AT_HEAD_DIM
    lanes = 2 * dh
    lo_half = lax.broadcasted_iota(jnp.int32, (blk, lanes), 1) < dh

    def padded(ref, g):
        t = ref[:, (g // 2) * lanes:(g // 2 + 1) * lanes].astype(F32)
        if g % 2 == 0:
            lo = jnp.where(lo_half, t, 0.0)
            hi = pltpu.roll(lo, dh, axis=1)
        else:
            hi = jnp.where(lo_half, 0.0, t)
            lo = pltpu.roll(hi, dh, axis=1)
        return lo, hi

    def banded(refs, g):
        pads = [padded(r, g) for r in refs]
        return jnp.concatenate([p[0] for p in pads] + [p[1] for p in pads], axis=0).astype(BF16)

    scores, values = [], []
    for g in range(AT_KV_HEADS):
        q2 = q_ref[:, g * 2 * lanes:(g + 1) * 2 * lanes] * (dh ** -0.5)
        lhs = jnp.concatenate([q2[:, :lanes], q2[:, lanes:]], axis=0).astype(BF16)
        scores.append(lax.dot_general(lhs, banded((kp_ref, ko_ref, kn_ref), g), NT_DIMS,
                                      preferred_element_type=F32))
        values.append(banded((vp_ref, vo_ref, vn_ref), g))
    probs, denoms = [], []
    for g in range(AT_KV_HEADS):
        p_rows, d_rows = [], []
        for r in range(2):
            p_cols, d_cols = [], []
            for c in range(2):
                h = g * AT_GROUP + 2 * r + c
                slope = 2.0 ** (-8.0 * (h + 1) / AT_Q_HEADS)
                sink = sink_ref[h]
                s = scores[g][r * blk:(r + 1) * blk, c * span:(c + 1) * span] + slope * base
                m = jnp.maximum(jnp.max(s, axis=-1, keepdims=True), sink)
                p = jnp.exp(s - m)
                d_cols.append(jnp.sum(p, axis=-1, keepdims=True) + jnp.exp(sink - m))
                p_cols.append(p.astype(BF16))
            p_rows.append(jnp.concatenate(p_cols, axis=1))
            d_rows.append(jnp.where(lo_half, d_cols[0], d_cols[1]))
        probs.append(jnp.concatenate(p_rows, axis=0))
        denoms.append(d_rows)
    outs = []
    for g in range(AT_KV_HEADS):
        o = jnp.dot(probs[g], values[g], preferred_element_type=F32)
        for r in range(2):
            outs.append(o[r * blk:(r + 1) * blk, :] / denoms[g][r])
    o_ref[...] = jnp.concatenate(outs, axis=-1).astype(o_ref.dtype)


def _attention(proj, sink, bsz, seqlen):
    n_tok = bsz * seqlen
    nb = seqlen // AT_BLOCK
    kcol = COL_AK // AT_KVWIDTH
    vcol = COL_AV // AT_KVWIDTH

    def prev(c):
        return lambda b, j: (b * nb + jnp.maximum(j - 1, 0), c)

    def own(c):
        return lambda b, j: (b * nb + j, c)

    def nxt(c):
        return lambda b, j: (b * nb + jnp.minimum(j + 1, nb - 1), c)

    kv_blk = (AT_BLOCK, AT_KVWIDTH)
    return pl.pallas_call(
        functools.partial(_attn_body, seqlen),
        grid=(bsz, nb),
        in_specs=[
            pl.BlockSpec(memory_space=pltpu.SMEM),
            pl.BlockSpec((AT_BLOCK, D_MODEL), own(COL_AQ // D_MODEL)),
            pl.BlockSpec(kv_blk, prev(kcol)), pl.BlockSpec(kv_blk, own(kcol)),
            pl.BlockSpec(kv_blk, nxt(kcol)),
            pl.BlockSpec(kv_blk, prev(vcol)), pl.BlockSpec(kv_blk, own(vcol)),
            pl.BlockSpec(kv_blk, nxt(vcol)),
        ],
        out_specs=pl.BlockSpec((AT_BLOCK, D_MODEL), lambda b, j: (b * nb + j, 0)),
        out_shape=jax.ShapeDtypeStruct((n_tok, D_MODEL), BF16),
        compiler_params=pltpu.CompilerParams(
            dimension_semantics=("parallel", "arbitrary"), vmem_limit_bytes=VMEM_LIMIT),
        name="window_attn",
    )(sink.astype(F32), proj, proj, proj, proj, proj, proj, proj)


def _merge_body(of_ref, ob_ref, hg_ref, ga0_ref, ga1_ref, gb0_ref, gb1_ref, ya_ref, x_ref,
                gn_ref, wo_ref, g2_ref, wr_ref, br_ref,
                h_ref, xn_ref, idx_ref, gate_ref, rank_ref, cnt_ref, carry_ref):
    i = pl.program_id(0)

    @pl.when(i == 0)
    def _():
        carry_ref[...] = jnp.zeros_like(carry_ref)

    sub = MERGE_SUB
    subs = [slice(s0, s0 + sub) for s0 in range(0, x_ref.shape[0], sub)]
    gn = gn_ref[...]

    merged = []
    for rs in subs:
        o = of_ref[rs, :].astype(F32) + ob_ref[rs, :].astype(F32)
        parts = []
        for h in range(HG_HEADS):
            oh = o[:, h * HG_DK:(h + 1) * HG_DK]
            ms = jnp.mean(oh * oh, axis=-1, keepdims=True)
            parts.append(oh * lax.rsqrt(ms + NORM_EPS) * gn)
        hg = hg_ref[rs, :].astype(F32)
        y_rec = jnp.concatenate(parts, axis=-1) * (hg * _sigmoid(hg))
        gate_a = jnp.concatenate([ga0_ref[rs, :], ga1_ref[rs, :]], axis=-1).astype(F32)
        gate_b = jnp.concatenate([gb0_ref[rs, :], gb1_ref[rs, :]], axis=-1).astype(F32)
        merged.append((_sigmoid(gate_a) * y_rec
                       + _sigmoid(gate_b) * ya_ref[rs, :].astype(F32)).astype(BF16))

    xns = []
    for rs, mg in zip(subs, merged):
        h_res = x_ref[rs, :] + jnp.dot(mg, wo_ref[...], preferred_element_type=F32)
        h_ref[rs, :] = h_res
        ms = jnp.mean(h_res * h_res, axis=-1, keepdims=True)
        xn = h_res * lax.rsqrt(ms + NORM_EPS) * g2_ref[...]
        xn_ref[rs, :] = xn
        xns.append(xn)

    logits = [lax.dot_general(wr_ref[...], xn, NT_DIMS, preferred_element_type=F32,
                              precision=lax.Precision.HIGHEST) + br_ref[...] for xn in xns]
    e_iota = lax.broadcasted_iota(jnp.int32, (N_EXPERTS, sub), 0).astype(F32)
    picked = []
    for rs, vals in zip(subs, logits):
        tops, idxs = [], []
        for _ in range(TOP_K):
            m = jnp.max(vals, axis=0, keepdims=True)
            sel = jnp.min(jnp.where(vals == m, e_iota, float(N_EXPERTS)), axis=0, keepdims=True)
            tops.append(m)
            idxs.append(sel)
            vals = jnp.where(e_iota == sel, -jnp.inf, vals)
        exps = [jnp.exp(t - tops[0]) for t in tops]
        den = exps[0] + exps[1] + exps[2] + exps[3]
        gate_ref[:, rs] = jnp.concatenate([e / den for e in exps], axis=0)
        idx_ref[:, rs] = jnp.concatenate(idxs, axis=0).astype(jnp.int32)
        picked.append((idxs, jnp.where(vals == -jnp.inf, 1.0, 0.0)))

    s_iota = lax.broadcasted_iota(jnp.int32, (sub, sub), 0)
    t_iota = lax.broadcasted_iota(jnp.int32, (sub, sub), 1)
    strict = jnp.where(s_iota < t_iota, 1.0, 0.0).astype(BF16)
    prefixes = [jnp.dot(chosen.astype(BF16), strict, preferred_element_type=F32)
                for _, chosen in picked]
    carry = carry_ref[...]
    for rs, (idxs, chosen), prefix in zip(subs, picked, prefixes):
        prefix = prefix + carry
        ranks = [jnp.sum(jnp.where(e_iota == sel, prefix, 0.0), axis=0, keepdims=True)
                 for sel in idxs]
        rank_ref[:, rs] = jnp.concatenate(ranks, axis=0).astype(jnp.int32)
        carry = carry + jnp.sum(chosen, axis=1, keepdims=True)
    carry_ref[...] = carry
    cnt_ref[...] = jnp.broadcast_to(carry, cnt_ref.shape).astype(jnp.int32)


def _merge_route(o_f, o_b, proj, y_att, x2d, gn_gain, w_out_bf16, g2, w_router_t, b_router, tm=512):
    n_tok = x2d.shape[0]
    nblk = n_tok // tm
    row_blk = (tm, D_MODEL)
    half = D_MODEL // 2
    half_blk = (tm, half)

    def rows(c):
        return lambda i: (i, c)

    full = lambda i: (0, 0)
    tok_out = pl.BlockSpec((TOP_K, tm), lambda i: (0, i))
    return pl.pallas_call(
        _merge_body,
        grid=(nblk,),
        in_specs=[
            pl.BlockSpec(row_blk, rows(0)), pl.BlockSpec(row_blk, rows(0)),
            pl.BlockSpec(row_blk, rows(COL_HG // D_MODEL)),
            pl.BlockSpec(half_blk, rows(COL_GA // half)),
            pl.BlockSpec(half_blk, rows(COL_GA // half + 1)),
            pl.BlockSpec(half_blk, rows(COL_GB // half)),
            pl.BlockSpec(half_blk, rows(COL_GB // half + 1)),
            pl.BlockSpec(row_blk, rows(0)), pl.BlockSpec(row_blk, rows(0)),
            pl.BlockSpec((1, HG_DK), full),
            pl.BlockSpec((D_MODEL, D_MODEL), full),
            pl.BlockSpec((1, D_MODEL), full),
            pl.BlockSpec((N_EXPERTS, D_MODEL), full),
            pl.BlockSpec((N_EXPERTS, 1), full),
        ],
        out_specs=[
            pl.BlockSpec(row_blk, rows(0)), pl.BlockSpec(row_blk, rows(0)),
            tok_out, tok_out, tok_out,
            pl.BlockSpec((N_EXPERTS, 128), full),
        ],
        out_shape=[
            jax.ShapeDtypeStruct((n_tok, D_MODEL), F32),
            jax.ShapeDtypeStruct((n_tok, D_MODEL), F32),
            jax.ShapeDtypeStruct((TOP_K, n_tok), jnp.int32),
            jax.ShapeDtypeStruct((TOP_K, n_tok), F32),
            jax.ShapeDtypeStruct((TOP_K, n_tok), jnp.int32),
            jax.ShapeDtypeStruct((N_EXPERTS, 128), jnp.int32),
        ],
        scratch_shapes=[pltpu.VMEM((N_EXPERTS, 1), F32)],
        compiler_params=pltpu.CompilerParams(
            dimension_semantics=("arbitrary",), vmem_limit_bytes=VMEM_LIMIT),
        name="merge_route",
    )(o_f, o_b, proj, proj, proj, proj, proj, y_att, x2d, gn_gain.reshape(1, HG_DK), w_out_bf16,
      g2.reshape(1, D_MODEL), w_router_t, b_router.reshape(N_EXPERTS, 1))


DISPATCH_TOKENS = 128
ISSUE_UNROLL = 8


def _dispatch_body(dest_ref, pend_ref, x_ref, xb_hbm, zero_ref, sem):
    i = pl.program_id(0)
    rows = MOE_ROWS
    nt = DISPATCH_TOKENS

    @pl.when(i == 0)
    def _():
        zero_ref[...] = jnp.zeros_like(zero_ref)
        for e in range(N_EXPERTS):
            end = pend_ref[e]
            start = pl.multiple_of(end - rows, rows)
            prev_end = pend_ref[e - 1] if e > 0 else 0

            @pl.when(end > prev_end)
            def _():
                cp = pltpu.make_async_copy(zero_ref, xb_hbm.at[pl.ds(start, rows)], sem.at[1])
                cp.start()
                cp.wait()

        def zero_block(b, c):
            cp = pltpu.make_async_copy(
                zero_ref, xb_hbm.at[pl.ds(pl.multiple_of(b * rows, rows), rows)], sem.at[1])
            cp.start()
            cp.wait()
            return c
        lax.fori_loop(pend_ref[N_EXPERTS - 1] // rows, xb_hbm.shape[0] // rows, zero_block, 0)

    base = i * (nt * TOP_K)

    def issue(r, c):
        for k in range(TOP_K):
            pltpu.make_async_copy(
                x_ref.at[pl.ds(r, 1)],
                xb_hbm.at[pl.ds(dest_ref[base + r * TOP_K + k], 1)],
                sem.at[0]).start(priority=k % 2)
        return c

    lax.fori_loop(0, nt, issue, 0, unroll=ISSUE_UNROLL)
    pltpu.make_async_copy(xb_hbm.at[pl.ds(0, nt * TOP_K)], xb_hbm.at[pl.ds(0, nt * TOP_K)],
                          sem.at[0]).wait()


def _dispatch(xn, dest_flat, pad_end, n_pad):
    n_tok = xn.shape[0]
    grid_spec = pltpu.PrefetchScalarGridSpec(
        num_scalar_prefetch=2,
        grid=(n_tok // DISPATCH_TOKENS,),
        in_specs=[pl.BlockSpec((DISPATCH_TOKENS, D_MODEL), lambda i, d, p: (i, 0))],
        out_specs=pl.BlockSpec(memory_space=pl.ANY),
        scratch_shapes=[pltpu.VMEM((MOE_ROWS, D_MODEL), F32), pltpu.SemaphoreType.DMA((2,))],
    )
    return pl.pallas_call(
        _dispatch_body,
        grid_spec=grid_spec,
        out_shape=jax.ShapeDtypeStruct((n_pad, D_MODEL), F32),
        compiler_params=pltpu.CompilerParams(dimension_semantics=("arbitrary",)),
        name="moe_dispatch",
    )(dest_flat, pad_end, xn)


def _expert_body(bend_ref, x_hbm, wug_ref, bug_ref, wd_ref, bd_ref, y_hbm,
                 wug_bf, wd_bf, act_ref, xbuf, ybuf, sem_in, sem_out):
    e = pl.program_id(0)
    rows = MOE_ROWS
    n_total = y_hbm.shape[0] // rows
    n_used = bend_ref[N_EXPERTS - 1]
    b_first = jnp.where(e == 0, 0, bend_ref[jnp.maximum(e - 1, 0)])
    b_last = bend_ref[e]

    def x_copy(b, slot):
        return pltpu.make_async_copy(
            x_hbm.at[pl.ds(pl.multiple_of(b * rows, rows), rows)], xbuf.at[slot], sem_in.at[slot])

    def y_copy(b, slot):
        return pltpu.make_async_copy(
            ybuf.at[slot], y_hbm.at[pl.ds(pl.multiple_of(b * rows, rows), rows)], sem_out.at[slot])

    @pl.when((e == 0) & (n_used > 0))
    def _():
        x_copy(0, 0).start(priority=BLOCK_DMA_PRIORITY)

    ns = D_FF // FF_SLAB
    for s in range(ns):
        lo, hi = s * FF_SLAB, (s + 1) * FF_SLAB
        wug_bf[:, 2 * lo:2 * lo + FF_SLAB] = wug_ref[0, :, lo:hi].astype(BF16)
        wug_bf[:, 2 * lo + FF_SLAB:2 * hi] = wug_ref[0, :, D_FF + lo:D_FF + hi].astype(BF16)
    wd_bf[...] = wd_ref[0].astype(BF16)
    b_ug = bug_ref[0]
    b_d = bd_ref[0]

    def block(b, carry):
        slot = lax.rem(b, 2)
        x_copy(b, slot).wait()

        @pl.when(b + 1 < n_used)
        def _():
            x_copy(b + 1, 1 - slot).start(priority=BLOCK_DMA_PRIORITY)

        @pl.when(b >= 2)
        def _():
            y_copy(b - 2, slot).wait()

        x = xbuf[slot].astype(BF16)
        for s in range(ns):
            lo, hi = s * FF_SLAB, (s + 1) * FF_SLAB
            hu = jnp.dot(x, wug_bf[:, 2 * lo:2 * hi], preferred_element_type=F32)
            g = jnp.minimum(hu[:, :FF_SLAB] + b_ug[:, lo:hi], SWIGLU_LIMIT)
            u = jnp.clip(hu[:, FF_SLAB:] + b_ug[:, D_FF + lo:D_FF + hi], -SWIGLU_LIMIT, SWIGLU_LIMIT)
            act_ref[:, lo:hi] = (g * _sigmoid(SWIGLU_ALPHA * g) * (u + 1.0)).astype(BF16)
        ybuf[slot] = jnp.dot(act_ref[...], wd_bf[...], preferred_element_type=F32) + b_d
        y_copy(b, slot).start(priority=BLOCK_DMA_PRIORITY)
        return carry

    lax.fori_loop(b_first, b_last, block, 0)

    @pl.when(e == N_EXPERTS - 1)
    def _():
        for back in (2, 1):
            @pl.when(n_used >= back)
            def _():
                y_copy(n_used - back, lax.rem(n_used - back, 2)).wait()

        ybuf[0] = jnp.zeros((rows, D_MODEL), F32)

        def zero_block(b, carry):
            cp = y_copy(b, 0)
            cp.start()
            cp.wait()
            return carry
        lax.fori_loop(n_used, n_total, zero_block, 0)


def _experts(xb, block_end, w_up_gate, b_up_gate, w_down, b_down):
    n_pad = xb.shape[0]

    def w_map(e, bend):
        return (e, 0, 0)

    grid_spec = pltpu.PrefetchScalarGridSpec(
        num_scalar_prefetch=1,
        grid=(N_EXPERTS,),
        in_specs=[
            pl.BlockSpec(memory_space=pl.ANY),
            pl.BlockSpec((1, D_MODEL, 2 * D_FF), w_map),
            pl.BlockSpec((1, 1, 2 * D_FF), w_map),
            pl.BlockSpec((1, D_FF, D_MODEL), w_map),
            pl.BlockSpec((1, 1, D_MODEL), w_map),
        ],
        out_specs=pl.BlockSpec(memory_space=pl.ANY),
        scratch_shapes=[
            pltpu.VMEM((D_MODEL, 2 * D_FF), BF16), pltpu.VMEM((D_FF, D_MODEL), BF16),
            pltpu.VMEM((MOE_ROWS, D_FF), BF16),
            pltpu.VMEM((2, MOE_ROWS, D_MODEL), F32), pltpu.VMEM((2, MOE_ROWS, D_MODEL), F32),
            pltpu.SemaphoreType.DMA((2,)), pltpu.SemaphoreType.DMA((2,)),
        ],
    )
    return pl.pallas_call(
        _expert_body,
        grid_spec=grid_spec,
        out_shape=jax.ShapeDtypeStruct((n_pad, D_MODEL), F32),
        compiler_params=pltpu.CompilerParams(
            dimension_semantics=("arbitrary",), vmem_limit_bytes=VMEM_LIMIT),
        name="moe_experts",
    )(block_end, xb, w_up_gate, b_up_gate.reshape(N_EXPERTS, 1, -1),
      w_down, b_down.reshape(N_EXPERTS, 1, -1))


COMBINE_TOKENS = 128


def _combine_body(dest_ref, yb_hbm, h_ref, gate_ref, gain_ref, o_ref, buf_ref, sem):
    i = pl.program_id(0)
    n_steps = pl.num_programs(0)
    nt = COMBINE_TOKENS

    def issue(step, slot):
        base = step * nt

        def body(r, c):
            for k in range(TOP_K):
                pltpu.make_async_copy(
                    yb_hbm.at[pl.ds(dest_ref[(base + r) * TOP_K + k], 1)],
                    buf_ref.at[slot, k, pl.ds(r, 1)], sem.at[slot]).start(priority=k % 2)
            return c
        lax.fori_loop(0, nt, body, 0, unroll=ISSUE_UNROLL)

    slot = lax.rem(i, 2)

    @pl.when(i == 0)
    def _():
        issue(0, 0)

    @pl.when(i + 1 < n_steps)
    def _():
        issue(i + 1, 1 - slot)

    pltpu.make_async_copy(buf_ref.at[slot], buf_ref.at[slot], sem.at[slot]).wait()

    gates = gate_ref[...]
    acc = h_ref[...]
    for k in range(TOP_K):
        acc = acc + gates[:, k:k + 1] * buf_ref[slot, k]
    ms = jnp.mean(acc * acc, axis=-1, keepdims=True)
    o_ref[...] = acc * lax.rsqrt(ms + NORM_EPS) * gain_ref[...]


def _combine(yb, dest_flat, h_res, gates_tk, final_gain):
    n_tok = h_res.shape[0]
    nt = COMBINE_TOKENS
    grid_spec = pltpu.PrefetchScalarGridSpec(
        num_scalar_prefetch=1,
        grid=(n_tok // nt,),
        in_specs=[
            pl.BlockSpec(memory_space=pl.ANY),
            pl.BlockSpec((nt, D_MODEL), lambda i, d: (i, 0)),
            pl.BlockSpec((nt, TOP_K), lambda i, d: (i, 0)),
            pl.BlockSpec((1, D_MODEL), lambda i, d: (0, 0)),
        ],
        out_specs=pl.BlockSpec((nt, D_MODEL), lambda i, d: (i, 0)),
        scratch_shapes=[pltpu.VMEM((2, TOP_K, nt, D_MODEL), F32), pltpu.SemaphoreType.DMA((2,))],
    )
    return pl.pallas_call(
        _combine_body,
        grid_spec=grid_spec,
        out_shape=jax.ShapeDtypeStruct((n_tok, D_MODEL), F32),
        compiler_params=pltpu.CompilerParams(
            dimension_semantics=("arbitrary",), vmem_limit_bytes=VMEM_LIMIT),
        name="moe_combine",
    )(dest_flat, yb, h_res, gates_tk, final_gain.reshape(1, D_MODEL))


def kernel(x, norm_mix_gain, w_in, hg_lb_fwd, hg_lb_bwd, hg_norm_gain, attn_sink, w_out,
           norm_ffn_gain, w_router, b_router, w_up_gate, b_up_gate, w_down, b_down,
           final_norm_gain):
    bsz, seqlen, d = x.shape
    n_tok = bsz * seqlen
    x2d = x.reshape(n_tok, d)

    lb_fwd = jax.nn.softmax(hg_lb_fwd.astype(F32), axis=0)[0]
    lb_bwd = jax.nn.softmax(hg_lb_bwd.astype(F32), axis=0)[0]

    proj = _inproj(x2d, norm_mix_gain[0], w_in[0].astype(BF16))
    o_f, o_b = _hgrn2(proj, lb_fwd, lb_bwd, bsz, seqlen)
    y_att = _attention(proj, attn_sink[0], bsz, seqlen)
    h_res, xn, idx_t, gate_t, rank_t, counts = _merge_route(
        o_f, o_b, proj, y_att, x2d, hg_norm_gain[0], w_out[0].astype(BF16),
        norm_ffn_gain[0], w_router[0].T.astype(F32), b_router[0].astype(F32))

    counts = counts[:, 0]
    padded = (counts + MOE_ROWS - 1) // MOE_ROWS * MOE_ROWS
    pad_end = jnp.cumsum(padded).astype(jnp.int32)
    pad_start = pad_end - padded
    n_pad = n_tok * TOP_K + N_EXPERTS * MOE_ROWS
    experts = jnp.arange(N_EXPERTS, dtype=jnp.int32)
    start_of = jnp.sum(jnp.where(idx_t[..., None] == experts, pad_start, 0), axis=-1)
    dest_flat = (start_of + rank_t).T.reshape(-1).astype(jnp.int32)

    xb = _dispatch(xn, dest_flat, pad_end, n_pad)
    yb = _experts(xb, pad_end // MOE_ROWS, w_up_gate[0], b_up_gate[0], w_down[0], b_down[0])
    out = _combine(yb, dest_flat, h_res, gate_t.T, final_norm_gain)
    return out.reshape(bsz, seqlen, d)
```

```python
import functools

import jax
import jax.numpy as jnp
from jax import lax
from jax.experimental import pallas as pl
from jax.experimental.pallas import tpu as pltpu

F32 = jnp.float32
BF16 = jnp.bfloat16

D_MODEL = 1024
HG_HEADS = 8
HG_DK = 128
HG_CHUNK = 128
HG_GROUP = 4
AT_Q_HEADS = 16
AT_KV_HEADS = 4
AT_GROUP = AT_Q_HEADS // AT_KV_HEADS
AT_HEAD_DIM = 64
WINDOW = 128
AT_BLOCK = WINDOW
AT_KVWIDTH = AT_KV_HEADS * AT_HEAD_DIM
N_EXPERTS = 32
TOP_K = 4
D_FF = 1024
SWIGLU_LIMIT = 7.0
SWIGLU_ALPHA = 1.702
NORM_EPS = 1e-5
IN_WIDTH = 8704

COL_HQ, COL_FF, COL_FB, COL_HI, COL_HG = 0, 1024, 2048, 3072, 4096
COL_AQ, COL_AK, COL_AV, COL_GA, COL_GB = 5120, 6144, 6400, 6656, 7680

MOE_ROWS = 512
FF_SLAB = 256
MERGE_SUB = 256
BLOCK_DMA_PRIORITY = 1
VMEM_LIMIT = 48 * 1024 * 1024

NT_DIMS = (((1,), (1,)), ((), ()))

SUBLANES = 8
LANES = 128
LANE_TILES = D_MODEL // LANES
TILE_VIEW = (LANE_TILES, SUBLANES, LANES)
TOKEN_TILES = (SUBLANES, LANE_TILES, LANES)


def _store_tiles(ref, val):
    groups = val.shape[0] // SUBLANES
    for t in range(LANE_TILES):
        ref[:, t] = val[:, t * LANES:(t + 1) * LANES].reshape(groups, SUBLANES, LANES)


def _load_tiles(ref):
    groups = ref.shape[0]
    return jnp.concatenate(
        [ref[:, t].reshape(groups * SUBLANES, LANES) for t in range(LANE_TILES)], axis=1)


def _sigmoid(x):
    return 0.5 * jnp.tanh(0.5 * x) + 0.5


def _inproj_body(x_ref, g_ref, w_ref, o_ref, xn_ref):
    @pl.when(pl.program_id(1) == 0)
    def _():
        x = x_ref[...]
        ms = jnp.mean(x * x, axis=-1, keepdims=True)
        xn_ref[...] = (x * lax.rsqrt(ms + NORM_EPS) * g_ref[...]).astype(BF16)

    o_ref[...] = jnp.dot(xn_ref[...], w_ref[...], preferred_element_type=F32).astype(o_ref.dtype)


def _inproj(x2d, gain, w_bf16, tm=1024, tn=2176):
    n_tok = x2d.shape[0]
    n_out = w_bf16.shape[1]
    return pl.pallas_call(
        _inproj_body,
        grid=(n_tok // tm, n_out // tn),
        in_specs=[
            pl.BlockSpec((tm, D_MODEL), lambda i, j: (i, 0)),
            pl.BlockSpec((1, D_MODEL), lambda i, j: (0, 0)),
            pl.BlockSpec((D_MODEL, tn), lambda i, j: (0, j)),
        ],
        out_specs=pl.BlockSpec((tm, tn), lambda i, j: (i, j)),
        out_shape=jax.ShapeDtypeStruct((n_tok, n_out), BF16),
        scratch_shapes=[pltpu.VMEM((tm, D_MODEL), BF16)],
        compiler_params=pltpu.CompilerParams(
            dimension_semantics=("parallel", "arbitrary"), vmem_limit_bytes=VMEM_LIMIT),
        name="inproj",
    )(x2d, gain.reshape(1, D_MODEL), w_bf16)


class _Dir:
    def __init__(self, q_ref, f_ref, v_ref, o_ref, st_ref, lb, tri, mask, mid_row, last_row):
        self.q_ref, self.f_ref, self.v_ref, self.o_ref, self.st_ref = q_ref, f_ref, v_ref, o_ref, st_ref
        self.lb, self.tri, self.mask, self.mid_row, self.last_row = lb, tri, mask, mid_row, last_row


def _gla_group(chains):
    c = HG_CHUNK
    dk = HG_DK
    pre = []
    for d, r0 in chains:
        q = d.q_ref[pl.ds(r0, c), :].astype(F32)
        q = q * _sigmoid(q)
        f = d.lb + (1.0 - d.lb) * _sigmoid(d.f_ref[pl.ds(r0, c), :].astype(F32))
        logf = jnp.log(f)
        hi = logf.astype(BF16)
        lo = (logf - hi.astype(F32)).astype(BF16)
        pre.append((q, 1.0 - f, jnp.concatenate([hi, lo], axis=1)))
    cums = []
    for (d, _), (_, _, hl) in zip(chains, pre):
        r = jnp.dot(d.tri, hl, preferred_element_type=F32)
        cums.append(r[:, :dk] + r[:, dk:])
    mids = []
    for (d, r0), (q, k, _), cum in zip(chains, pre, cums):
        mid = cum[d.mid_row:d.mid_row + 1, :]
        last = cum[d.last_row:d.last_row + 1, :]
        a = cum - mid
        qa = q * jnp.exp(a)
        ka = k * jnp.exp(-a)
        qd = (qa * jnp.exp(mid)).astype(BF16)
        kend = (ka * jnp.exp(last - mid)).astype(BF16)
        dec = jnp.exp(last)
        scores = lax.dot_general(qa.astype(BF16), ka.astype(BF16), NT_DIMS,
                                 preferred_element_type=F32)
        mids.append((qd, kend, dec, scores))
    outs = []
    for (d, r0), (qd, kend, dec, scores) in zip(chains, mids):
        v = d.v_ref[pl.ds(r0, c), :]
        sm = jnp.where(d.mask, scores, 0.0).astype(BF16)
        o_intra = jnp.dot(sm, v, preferred_element_type=F32)
        upd = jnp.dot(jnp.transpose(v.astype(F32)).astype(BF16), kend, preferred_element_type=F32)
        outs.append((o_intra, upd))
    for (d, r0), (qd, _, dec, _), (o_intra, upd) in zip(chains, mids, outs):
        st = d.st_ref[...]
        d.o_ref[pl.ds(r0, c), :] = (o_intra + lax.dot_general(
            qd, st.astype(BF16), NT_DIMS, preferred_element_type=F32)).astype(d.o_ref.dtype)
        d.st_ref[...] = st * dec + upd


def _hgrn2_body(qf_ref, ff_ref, vf_ref, qb_ref, fb_ref, vb_ref, lbf_ref, lbb_ref,
                of_ref, ob_ref, sf_ref, sb_ref):
    @pl.when(pl.program_id(2) == 0)
    def _():
        sf_ref[...] = jnp.zeros_like(sf_ref)
        sb_ref[...] = jnp.zeros_like(sb_ref)

    c = HG_CHUNK
    g = HG_GROUP
    n_chunks = qf_ref.shape[0] // c
    row = lax.broadcasted_iota(jnp.int32, (c, c), 0)
    col = lax.broadcasted_iota(jnp.int32, (c, c), 1)
    lower = col <= row
    upper = col >= row
    fwd = _Dir(qf_ref, ff_ref, vf_ref, of_ref, sf_ref, lbf_ref[...],
               jnp.where(lower, 1.0, 0.0).astype(BF16), lower, c // 2 - 1, c - 1)
    bwd = _Dir(qb_ref, fb_ref, vb_ref, ob_ref, sb_ref, lbb_ref[...],
               jnp.where(upper, 1.0, 0.0).astype(BF16), upper, c // 2, 0)

    def step(n, carry):
        chains = []
        for u in range(g):
            chains.append((fwd, pl.multiple_of((n * g + u) * c, c)))
        for u in range(g):
            chains.append((bwd, pl.multiple_of((n_chunks - 1 - n * g - u) * c, c)))
        _gla_group(chains)
        return carry

    lax.fori_loop(0, n_chunks // g, step, 0)


def _hgrn2(proj, lb_fwd, lb_bwd, bsz, seqlen, rows=1024):
    n_tok = bsz * seqlen
    rows = min(rows, seqlen)
    nb = seqlen // rows
    dk = HG_DK

    def fwd_map(col0):
        return lambda b, h, j: (b * nb + j, col0 // dk + h)

    def bwd_map(col0):
        return lambda b, h, j: (b * nb + (nb - 1 - j), col0 // dk + h)

    blk = (rows, dk)
    lb_spec = pl.BlockSpec((1, dk), lambda b, h, j: (0, h))
    return pl.pallas_call(
        _hgrn2_body,
        grid=(bsz, HG_HEADS, nb),
        in_specs=[
            pl.BlockSpec(blk, fwd_map(COL_HQ)), pl.BlockSpec(blk, fwd_map(COL_FF)),
            pl.BlockSpec(blk, fwd_map(COL_HI)),
            pl.BlockSpec(blk, bwd_map(COL_HQ)), pl.BlockSpec(blk, bwd_map(COL_FB)),
            pl.BlockSpec(blk, bwd_map(COL_HI)),
            lb_spec, lb_spec,
        ],
        out_specs=[pl.BlockSpec(blk, fwd_map(0)), pl.BlockSpec(blk, bwd_map(0))],
        out_shape=[jax.ShapeDtypeStruct((n_tok, HG_HEADS * dk), BF16)] * 2,
        scratch_shapes=[pltpu.VMEM((dk, dk), F32), pltpu.VMEM((dk, dk), F32)],
        compiler_params=pltpu.CompilerParams(
            dimension_semantics=("parallel", "parallel", "arbitrary"),
            vmem_limit_bytes=VMEM_LIMIT),
        name="hgrn2",
    )(proj, proj, proj, proj, proj, proj, lb_fwd.reshape(1, -1), lb_bwd.reshape(1, -1))


def _attn_body(seqlen, sink_ref, q_ref, kp_ref, ko_ref, kn_ref, vp_ref, vo_ref, vn_ref, o_ref):
    j = pl.program_id(1)
    blk = AT_BLOCK
    span = 3 * blk
    row = lax.broadcasted_iota(jnp.int32, (blk, span), 0)
    col = lax.broadcasted_iota(jnp.int32, (blk, span), 1)
    absrel = jnp.abs(row + blk - col)
    key_pos = j * blk - blk + col
    valid = (absrel <= WINDOW) & (key_pos >= 0) & (key_pos < seqlen)
    base = jnp.where(valid, -absrel.astype(F32), -jnp.inf)
    dh = AT_HEAD_DIM
    lanes = 2 * dh
    lo_half = lax.broadcasted_iota(jnp.int32, (blk, lanes), 1) < dh

    def padded(ref, g):
        t = ref[:, (g // 2) * lanes:(g // 2 + 1) * lanes].astype(F32)
        if g % 2 == 0:
            lo = jnp.where(lo_half, t, 0.0)
            hi = pltpu.roll(lo, dh, axis=1)
        else:
            hi = jnp.where(lo_half, 0.0, t)
            lo = pltpu.roll(hi, dh, axis=1)
        return lo, hi

    def banded(refs, g):
        pads = [padded(r, g) for r in refs]
        return jnp.concatenate([p[0] for p in pads] + [p[1] for p in pads], axis=0).astype(BF16)

    scores, values = [], []
    for g in range(AT_KV_HEADS):
        q2 = q_ref[:, g * 2 * lanes:(g + 1) * 2 * lanes] * (dh ** -0.5)
        lhs = jnp.concatenate([q2[:, :lanes], q2[:, lanes:]], axis=0).astype(BF16)
        scores.append(lax.dot_general(lhs, banded((kp_ref, ko_ref, kn_ref), g), NT_DIMS,
                                      preferred_element_type=F32))
        values.append(banded((vp_ref, vo_ref, vn_ref), g))
    probs, denoms = [], []
    for g in range(AT_KV_HEADS):
        p_rows, d_rows = [], []
        for r in range(2):
            p_cols, d_cols = [], []
            for c in range(2):
                h = g * AT_GROUP + 2 * r + c
                slope = 2.0 ** (-8.0 * (h + 1) / AT_Q_HEADS)
                sink = sink_ref[h]
                s = scores[g][r * blk:(r + 1) * blk, c * span:(c + 1) * span] + slope * base
                m = jnp.maximum(jnp.max(s, axis=-1, keepdims=True), sink)
                p = jnp.exp(s - m)
                d_cols.append(jnp.sum(p, axis=-1, keepdims=True) + jnp.exp(sink - m))
                p_cols.append(p.astype(BF16))
            p_rows.append(jnp.concatenate(p_cols, axis=1))
            d_rows.append(jnp.where(lo_half, d_cols[0], d_cols[1]))
        probs.append(jnp.concatenate(p_rows, axis=0))
        denoms.append(d_rows)
    outs = []
    for g in range(AT_KV_HEADS):
        o = jnp.dot(probs[g], values[g], preferred_element_type=F32)
        for r in range(2):
            outs.append(o[r * blk:(r + 1) * blk, :] / denoms[g][r])
    o_ref[...] = jnp.concatenate(outs, axis=-1).astype(o_ref.dtype)


def _attention(proj, sink, bsz, seqlen):
    n_tok = bsz * seqlen
    nb = seqlen // AT_BLOCK
    kcol = COL_AK // AT_KVWIDTH
    vcol = COL_AV // AT_KVWIDTH

    def prev(c):
        return lambda b, j: (b * nb + jnp.maximum(j - 1, 0), c)

    def own(c):
        return lambda b, j: (b * nb + j, c)

    def nxt(c):
        return lambda b, j: (b * nb + jnp.minimum(j + 1, nb - 1), c)

    kv_blk = (AT_BLOCK, AT_KVWIDTH)
    return pl.pallas_call(
        functools.partial(_attn_body, seqlen),
        grid=(bsz, nb),
        in_specs=[
            pl.BlockSpec(memory_space=pltpu.SMEM),
            pl.BlockSpec((AT_BLOCK, D_MODEL), own(COL_AQ // D_MODEL)),
            pl.BlockSpec(kv_blk, prev(kcol)), pl.BlockSpec(kv_blk, own(kcol)),
            pl.BlockSpec(kv_blk, nxt(kcol)),
            pl.BlockSpec(kv_blk, prev(vcol)), pl.BlockSpec(kv_blk, own(vcol)),
            pl.BlockSpec(kv_blk, nxt(vcol)),
        ],
        out_specs=pl.BlockSpec((AT_BLOCK, D_MODEL), lambda b, j: (b * nb + j, 0)),
        out_shape=jax.ShapeDtypeStruct((n_tok, D_MODEL), BF16),
        compiler_params=pltpu.CompilerParams(
            dimension_semantics=("parallel", "arbitrary"), vmem_limit_bytes=VMEM_LIMIT),
        name="window_attn",
    )(sink.astype(F32), proj, proj, proj, proj, proj, proj, proj)


def _merge_body(of_ref, ob_ref, hg_ref, ga0_ref, ga1_ref, gb0_ref, gb1_ref, ya_ref, x_ref,
                gn_ref, wo_ref, g2_ref, wr_ref, br_ref,
                h_ref, xn_ref, idx_ref, gate_ref, rank_ref, cnt_ref, carry_ref):
    i = pl.program_id(0)

    @pl.when(i == 0)
    def _():
        carry_ref[...] = jnp.zeros_like(carry_ref)

    sub = MERGE_SUB
    subs = [slice(s0, s0 + sub) for s0 in range(0, x_ref.shape[0], sub)]
    gn = gn_ref[...]

    merged = []
    for rs in subs:
        o = of_ref[rs, :].astype(F32) + ob_ref[rs, :].astype(F32)
        parts = []
        for h in range(HG_HEADS):
            oh = o[:, h * HG_DK:(h + 1) * HG_DK]
            ms = jnp.mean(oh * oh, axis=-1, keepdims=True)
            parts.append(oh * lax.rsqrt(ms + NORM_EPS) * gn)
        hg = hg_ref[rs, :].astype(F32)
        y_rec = jnp.concatenate(parts, axis=-1) * (hg * _sigmoid(hg))
        gate_a = jnp.concatenate([ga0_ref[rs, :], ga1_ref[rs, :]], axis=-1).astype(F32)
        gate_b = jnp.concatenate([gb0_ref[rs, :], gb1_ref[rs, :]], axis=-1).astype(F32)
        merged.append((_sigmoid(gate_a) * y_rec
                       + _sigmoid(gate_b) * ya_ref[rs, :].astype(F32)).astype(BF16))

    xns = []
    for rs, mg in zip(subs, merged):
        h_res = x_ref[rs, :] + jnp.dot(mg, wo_ref[...], preferred_element_type=F32)
        h_ref[rs, :] = h_res
        ms = jnp.mean(h_res * h_res, axis=-1, keepdims=True)
        xn = h_res * lax.rsqrt(ms + NORM_EPS) * g2_ref[...]
        _store_tiles(xn_ref.at[rs.start // SUBLANES:rs.stop // SUBLANES], xn)
        xns.append(xn)

    logits = [lax.dot_general(wr_ref[...], xn, NT_DIMS, preferred_element_type=F32,
                              precision=lax.Precision.HIGHEST) + br_ref[...] for xn in xns]
    e_iota = lax.broadcasted_iota(jnp.int32, (N_EXPERTS, sub), 0).astype(F32)
    picked = []
    for rs, vals in zip(subs, logits):
        tops, idxs = [], []
        for _ in range(TOP_K):
            m = jnp.max(vals, axis=0, keepdims=True)
            sel = jnp.min(jnp.where(vals == m, e_iota, float(N_EXPERTS)), axis=0, keepdims=True)
            tops.append(m)
            idxs.append(sel)
            vals = jnp.where(e_iota == sel, -jnp.inf, vals)
        exps = [jnp.exp(t - tops[0]) for t in tops]
        den = exps[0] + exps[1] + exps[2] + exps[3]
        gate_ref[:, rs] = jnp.concatenate([e / den for e in exps], axis=0)
        idx_ref[:, rs] = jnp.concatenate(idxs, axis=0).astype(jnp.int32)
        picked.append((idxs, jnp.where(vals == -jnp.inf, 1.0, 0.0)))

    s_iota = lax.broadcasted_iota(jnp.int32, (sub, sub), 0)
    t_iota = lax.broadcasted_iota(jnp.int32, (sub, sub), 1)
    strict = jnp.where(s_iota < t_iota, 1.0, 0.0).astype(BF16)
    prefixes = [jnp.dot(chosen.astype(BF16), strict, preferred_element_type=F32)
                for _, chosen in picked]
    carry = carry_ref[...]
    for rs, (idxs, chosen), prefix in zip(subs, picked, prefixes):
        prefix = prefix + carry
        ranks = [jnp.sum(jnp.where(e_iota == sel, prefix, 0.0), axis=0, keepdims=True)
                 for sel in idxs]
        rank_ref[:, rs] = jnp.concatenate(ranks, axis=0).astype(jnp.int32)
        carry = carry + jnp.sum(chosen, axis=1, keepdims=True)
    carry_ref[...] = carry
    cnt_ref[...] = jnp.broadcast_to(carry, cnt_ref.shape).astype(jnp.int32)


def _merge_route(o_f, o_b, proj, y_att, x2d, gn_gain, w_out_bf16, g2, w_router_t, b_router, tm=512):
    n_tok = x2d.shape[0]
    nblk = n_tok // tm
    row_blk = (tm, D_MODEL)
    half = D_MODEL // 2
    half_blk = (tm, half)

    def rows(c):
        return lambda i: (i, c)

    full = lambda i: (0, 0)
    tok_out = pl.BlockSpec((TOP_K, tm), lambda i: (0, i))
    return pl.pallas_call(
        _merge_body,
        grid=(nblk,),
        in_specs=[
            pl.BlockSpec(row_blk, rows(0)), pl.BlockSpec(row_blk, rows(0)),
            pl.BlockSpec(row_blk, rows(COL_HG // D_MODEL)),
            pl.BlockSpec(half_blk, rows(COL_GA // half)),
            pl.BlockSpec(half_blk, rows(COL_GA // half + 1)),
            pl.BlockSpec(half_blk, rows(COL_GB // half)),
            pl.BlockSpec(half_blk, rows(COL_GB // half + 1)),
            pl.BlockSpec(row_blk, rows(0)), pl.BlockSpec(row_blk, rows(0)),
            pl.BlockSpec((1, HG_DK), full),
            pl.BlockSpec((D_MODEL, D_MODEL), full),
            pl.BlockSpec((1, D_MODEL), full),
            pl.BlockSpec((N_EXPERTS, D_MODEL), full),
            pl.BlockSpec((N_EXPERTS, 1), full),
        ],
        out_specs=[
            pl.BlockSpec(row_blk, rows(0)),
            pl.BlockSpec((tm // SUBLANES,) + TILE_VIEW, lambda i: (i, 0, 0, 0)),
            tok_out, tok_out, tok_out,
            pl.BlockSpec((N_EXPERTS, 128), full),
        ],
        out_shape=[
            jax.ShapeDtypeStruct((n_tok, D_MODEL), F32),
            jax.ShapeDtypeStruct((n_tok // SUBLANES,) + TILE_VIEW, F32),
            jax.ShapeDtypeStruct((TOP_K, n_tok), jnp.int32),
            jax.ShapeDtypeStruct((TOP_K, n_tok), F32),
            jax.ShapeDtypeStruct((TOP_K, n_tok), jnp.int32),
            jax.ShapeDtypeStruct((N_EXPERTS, 128), jnp.int32),
        ],
        scratch_shapes=[pltpu.VMEM((N_EXPERTS, 1), F32)],
        compiler_params=pltpu.CompilerParams(
            dimension_semantics=("arbitrary",), vmem_limit_bytes=VMEM_LIMIT),
        name="merge_route",
    )(o_f, o_b, proj, proj, proj, proj, proj, y_att, x2d, gn_gain.reshape(1, HG_DK), w_out_bf16,
      g2.reshape(1, D_MODEL), w_router_t, b_router.reshape(N_EXPERTS, 1))


DISPATCH_TOKENS = 256
ISSUE_UNROLL = 2


def _dispatch_body(dest_ref, pend_ref, x_ref, xb_hbm, zero_ref, sem):
    i = pl.program_id(0)
    rows = MOE_ROWS
    nt = DISPATCH_TOKENS

    @pl.when(i == 0)
    def _():
        zero_ref[...] = jnp.zeros_like(zero_ref)
        for e in range(N_EXPERTS):
            end = pend_ref[e]
            prev_end = pend_ref[e - 1] if e > 0 else 0

            @pl.when(end > prev_end)
            def _():
                cp = pltpu.make_async_copy(zero_ref, xb_hbm.at[pl.ds(end - rows, rows)], sem.at[1])
                cp.start()
                cp.wait()

        def zero_block(b, c):
            cp = pltpu.make_async_copy(zero_ref, xb_hbm.at[pl.ds(b * rows, rows)], sem.at[1])
            cp.start()
            cp.wait()
            return c
        lax.fori_loop(pend_ref[N_EXPERTS - 1] // rows, xb_hbm.shape[0] // rows, zero_block, 0)

    base = i * (nt * TOP_K)

    def issue(g, c):
        for s in range(SUBLANES):
            for k in range(TOP_K):
                pltpu.make_async_copy(
                    x_ref.at[g, :, s, :],
                    xb_hbm.at[dest_ref[base + (g * SUBLANES + s) * TOP_K + k]],
                    sem.at[0]).start(priority=k % 2)
        return c

    lax.fori_loop(0, nt // SUBLANES, issue, 0, unroll=ISSUE_UNROLL)
    pltpu.make_async_copy(xb_hbm.at[pl.ds(0, nt * TOP_K)], xb_hbm.at[pl.ds(0, nt * TOP_K)],
                          sem.at[0]).wait()


def _dispatch(xn_tiles, dest_flat, pad_end, n_pad):
    n_tok = xn_tiles.shape[0] * SUBLANES
    grid_spec = pltpu.PrefetchScalarGridSpec(
        num_scalar_prefetch=2,
        grid=(n_tok // DISPATCH_TOKENS,),
        in_specs=[pl.BlockSpec((DISPATCH_TOKENS // SUBLANES,) + TILE_VIEW,
                               lambda i, d, p: (i, 0, 0, 0))],
        out_specs=pl.BlockSpec(memory_space=pl.ANY),
        scratch_shapes=[pltpu.VMEM((MOE_ROWS, LANE_TILES, LANES), F32),
                        pltpu.SemaphoreType.DMA((2,))],
    )
    return pl.pallas_call(
        _dispatch_body,
        grid_spec=grid_spec,
        out_shape=jax.ShapeDtypeStruct((n_pad, LANE_TILES, LANES), F32),
        compiler_params=pltpu.CompilerParams(dimension_semantics=("arbitrary",)),
        name="moe_dispatch",
    )(dest_flat, pad_end, xn_tiles)


def _expert_body(bend_ref, x_hbm, wug_ref, bug_ref, wd_ref, bd_ref, y_hbm,
                 wug_bf, wd_bf, act_ref, xbuf, ybuf, sem_in, sem_out):
    e = pl.program_id(0)
    rows = MOE_ROWS
    groups = rows // SUBLANES
    n_total = y_hbm.shape[0] // groups
    n_used = bend_ref[N_EXPERTS - 1]
    b_first = jnp.where(e == 0, 0, bend_ref[jnp.maximum(e - 1, 0)])
    b_last = bend_ref[e]

    def x_copies(b, slot):
        return [pltpu.make_async_copy(x_hbm.at[pl.ds(b * groups, groups), s],
                                      xbuf.at[slot, :, :, s, :], sem_in.at[slot])
                for s in range(SUBLANES)]

    def y_copies(b, slot):
        return [pltpu.make_async_copy(ybuf.at[slot, :, :, s, :],
                                      y_hbm.at[pl.ds(b * groups, groups), s], sem_out.at[slot])
                for s in range(SUBLANES)]

    def start_all(copies):
        for cp in copies:
            cp.start(priority=BLOCK_DMA_PRIORITY)

    def wait_all(copies):
        for cp in copies:
            cp.wait()

    @pl.when((e == 0) & (n_used > 0))
    def _():
        start_all(x_copies(0, 0))

    ns = D_FF // FF_SLAB
    for s in range(ns):
        lo, hi = s * FF_SLAB, (s + 1) * FF_SLAB
        wug_bf[:, 2 * lo:2 * lo + FF_SLAB] = wug_ref[0, :, lo:hi].astype(BF16)
        wug_bf[:, 2 * lo + FF_SLAB:2 * hi] = wug_ref[0, :, D_FF + lo:D_FF + hi].astype(BF16)
    wd_bf[...] = wd_ref[0].astype(BF16)
    b_ug = bug_ref[0]
    b_d = bd_ref[0]

    def block(b, carry):
        slot = lax.rem(b, 2)
        wait_all(x_copies(b, slot))

        @pl.when(b + 1 < n_used)
        def _():
            start_all(x_copies(b + 1, 1 - slot))

        @pl.when(b >= 2)
        def _():
            wait_all(y_copies(b - 2, slot))

        x = _load_tiles(xbuf.at[slot]).astype(BF16)
        for s in range(ns):
            lo, hi = s * FF_SLAB, (s + 1) * FF_SLAB
            hu = jnp.dot(x, wug_bf[:, 2 * lo:2 * hi], preferred_element_type=F32)
            g = jnp.minimum(hu[:, :FF_SLAB] + b_ug[:, lo:hi], SWIGLU_LIMIT)
            u = jnp.clip(hu[:, FF_SLAB:] + b_ug[:, D_FF + lo:D_FF + hi], -SWIGLU_LIMIT, SWIGLU_LIMIT)
            act_ref[:, lo:hi] = (g * _sigmoid(SWIGLU_ALPHA * g) * (u + 1.0)).astype(BF16)
        _store_tiles(ybuf.at[slot],
                     jnp.dot(act_ref[...], wd_bf[...], preferred_element_type=F32) + b_d)
        start_all(y_copies(b, slot))
        return carry

    lax.fori_loop(b_first, b_last, block, 0)

    @pl.when(e == N_EXPERTS - 1)
    def _():
        for back in (2, 1):
            @pl.when(n_used >= back)
            def _():
                wait_all(y_copies(n_used - back, lax.rem(n_used - back, 2)))

        ybuf[0] = jnp.zeros(ybuf.shape[1:], F32)

        def zero_block(b, carry):
            start_all(y_copies(b, 0))
            wait_all(y_copies(b, 0))
            return carry
        lax.fori_loop(n_used, n_total, zero_block, 0)


def _experts(xb, block_end, w_up_gate, b_up_gate, w_down, b_down):
    ring = (2, MOE_ROWS // SUBLANES) + TILE_VIEW

    def w_map(e, bend):
        return (e, 0, 0)

    grid_spec = pltpu.PrefetchScalarGridSpec(
        num_scalar_prefetch=1,
        grid=(N_EXPERTS,),
        in_specs=[
            pl.BlockSpec(memory_space=pl.ANY),
            pl.BlockSpec((1, D_MODEL, 2 * D_FF), w_map),
            pl.BlockSpec((1, 1, 2 * D_FF), w_map),
            pl.BlockSpec((1, D_FF, D_MODEL), w_map),
            pl.BlockSpec((1, 1, D_MODEL), w_map),
        ],
        out_specs=pl.BlockSpec(memory_space=pl.ANY),
        scratch_shapes=[
            pltpu.VMEM((D_MODEL, 2 * D_FF), BF16), pltpu.VMEM((D_FF, D_MODEL), BF16),
            pltpu.VMEM((MOE_ROWS, D_FF), BF16),
            pltpu.VMEM(ring, F32), pltpu.VMEM(ring, F32),
            pltpu.SemaphoreType.DMA((2,)), pltpu.SemaphoreType.DMA((2,)),
        ],
    )
    return pl.pallas_call(
        _expert_body,
        grid_spec=grid_spec,
        out_shape=jax.ShapeDtypeStruct(xb.shape, F32),
        compiler_params=pltpu.CompilerParams(
            dimension_semantics=("arbitrary",), vmem_limit_bytes=VMEM_LIMIT),
        name="moe_experts",
    )(block_end, xb, w_up_gate, b_up_gate.reshape(N_EXPERTS, 1, -1),
      w_down, b_down.reshape(N_EXPERTS, 1, -1))


COMBINE_TOKENS = 128


def _combine_body(dest_ref, yb_hbm, h_ref, gate_ref, gain_ref, o_ref, buf_ref, sem):
    i = pl.program_id(0)
    n_steps = pl.num_programs(0)
    nt = COMBINE_TOKENS

    def issue(step, slot):
        base = step * nt

        def body(g, c):
            for s in range(SUBLANES):
                for k in range(TOP_K):
                    pltpu.make_async_copy(
                        yb_hbm.at[dest_ref[(base + g * SUBLANES + s) * TOP_K + k]],
                        buf_ref.at[slot, k, g, :, s, :], sem.at[slot]).start(priority=k % 2)
            return c
        lax.fori_loop(0, nt // SUBLANES, body, 0, unroll=ISSUE_UNROLL)

    slot = lax.rem(i, 2)

    @pl.when(i == 0)
    def _():
        issue(0, 0)

    @pl.when(i + 1 < n_steps)
    def _():
        issue(i + 1, 1 - slot)

    pltpu.make_async_copy(buf_ref.at[slot], buf_ref.at[slot], sem.at[slot]).wait()

    gates = gate_ref[...]
    acc = h_ref[...]
    for k in range(TOP_K):
        acc = acc + gates[:, k:k + 1] * _load_tiles(buf_ref.at[slot, k])
    ms = jnp.mean(acc * acc, axis=-1, keepdims=True)
    o_ref[...] = acc * lax.rsqrt(ms + NORM_EPS) * gain_ref[...]


def _combine(yb, dest_flat, h_res, gates_tk, final_gain):
    n_tok = h_res.shape[0]
    nt = COMBINE_TOKENS
    grid_spec = pltpu.PrefetchScalarGridSpec(
        num_scalar_prefetch=1,
        grid=(n_tok // nt,),
        in_specs=[
            pl.BlockSpec(memory_space=pl.ANY),
            pl.BlockSpec((nt, D_MODEL), lambda i, d: (i, 0)),
            pl.BlockSpec((nt, TOP_K), lambda i, d: (i, 0)),
            pl.BlockSpec((1, D_MODEL), lambda i, d: (0, 0)),
        ],
        out_specs=pl.BlockSpec((nt, D_MODEL), lambda i, d: (i, 0)),
        scratch_shapes=[pltpu.VMEM((2, TOP_K, nt // SUBLANES) + TILE_VIEW, F32),
                        pltpu.SemaphoreType.DMA((2,))],
    )
    return pl.pallas_call(
        _combine_body,
        grid_spec=grid_spec,
        out_shape=jax.ShapeDtypeStruct((n_tok, D_MODEL), F32),
        compiler_params=pltpu.CompilerParams(
            dimension_semantics=("arbitrary",), vmem_limit_bytes=VMEM_LIMIT),
        name="moe_combine",
    )(dest_flat, yb, h_res, gates_tk, final_gain.reshape(1, D_MODEL))


def kernel(x, norm_mix_gain, w_in, hg_lb_fwd, hg_lb_bwd, hg_norm_gain, attn_sink, w_out,
           norm_ffn_gain, w_router, b_router, w_up_gate, b_up_gate, w_down, b_down,
           final_norm_gain):
    bsz, seqlen, d = x.shape
    n_tok = bsz * seqlen
    x2d = x.reshape(n_tok, d)

    lb_fwd = jax.nn.softmax(hg_lb_fwd.astype(F32), axis=0)[0]
    lb_bwd = jax.nn.softmax(hg_lb_bwd.astype(F32), axis=0)[0]

    proj = _inproj(x2d, norm_mix_gain[0], w_in[0].astype(BF16))
    o_f, o_b = _hgrn2(proj, lb_fwd, lb_bwd, bsz, seqlen)
    y_att = _attention(proj, attn_sink[0], bsz, seqlen)
    h_res, xn, idx_t, gate_t, rank_t, counts = _merge_route(
        o_f, o_b, proj, y_att, x2d, hg_norm_gain[0], w_out[0].astype(BF16),
        norm_ffn_gain[0], w_router[0].T.astype(F32), b_router[0].astype(F32))

    counts = counts[:, 0]
    padded = (counts + MOE_ROWS - 1) // MOE_ROWS * MOE_ROWS
    pad_end = jnp.cumsum(padded).astype(jnp.int32)
    pad_start = pad_end - padded
    n_pad = n_tok * TOP_K + N_EXPERTS * MOE_ROWS
    experts = jnp.arange(N_EXPERTS, dtype=jnp.int32)
    start_of = jnp.sum(jnp.where(idx_t[..., None] == experts, pad_start, 0), axis=-1)
    dest_flat = (start_of + rank_t).T.reshape(-1).astype(jnp.int32)

    xb = _dispatch(xn, dest_flat, pad_end, n_pad)
    yb = _experts(xb.reshape((n_pad // SUBLANES,) + TOKEN_TILES), pad_end // MOE_ROWS,
                  w_up_gate[0], b_up_gate[0], w_down[0], b_down[0])
    out = _combine(yb.reshape(n_pad, LANE_TILES, LANES), dest_flat, h_res, gate_t.T, final_norm_gain)
    return out.reshape(bsz, seqlen, d)
```

```python
import functools

import jax
import jax.numpy as jnp
from jax import lax
from jax.experimental import pallas as pl
from jax.experimental.pallas import tpu as pltpu

F32 = jnp.float32
BF16 = jnp.bfloat16

D_MODEL = 1024
HG_HEADS = 8
HG_DK = 128
HG_CHUNK = 128
HG_GROUP = 4
AT_Q_HEADS = 16
AT_KV_HEADS = 4
AT_GROUP = AT_Q_HEADS // AT_KV_HEADS
AT_HEAD_DIM = 64
WINDOW = 128
AT_BLOCK = WINDOW
AT_KVWIDTH = AT_KV_HEADS * AT_HEAD_DIM
N_EXPERTS = 32
TOP_K = 4
D_FF = 1024
SWIGLU_LIMIT = 7.0
SWIGLU_ALPHA = 1.702
NORM_EPS = 1e-5
IN_WIDTH = 8704

COL_HQ, COL_FF, COL_FB, COL_HI, COL_HG = 0, 1024, 2048, 3072, 4096
COL_AQ, COL_AK, COL_AV, COL_GA, COL_GB = 5120, 6144, 6400, 6656, 7680

MOE_ROWS = 512
FF_SLAB = 256
MERGE_SUB = 256
BLOCK_DMA_PRIORITY = 1
VMEM_LIMIT = 48 * 1024 * 1024

NT_DIMS = (((1,), (1,)), ((), ()))

SUBLANES = 8
LANES = 128
LANE_TILES = D_MODEL // LANES
TILE_VIEW = (LANE_TILES, SUBLANES, LANES)
TOKEN_TILES = (SUBLANES, LANE_TILES, LANES)


def _store_tiles(ref, val):
    groups = val.shape[0] // SUBLANES
    for t in range(LANE_TILES):
        ref[:, t] = val[:, t * LANES:(t + 1) * LANES].reshape(groups, SUBLANES, LANES)


def _load_tiles(ref):
    groups = ref.shape[0]
    return jnp.concatenate(
        [ref[:, t].reshape(groups * SUBLANES, LANES) for t in range(LANE_TILES)], axis=1)


def _sigmoid(x):
    return 0.5 * jnp.tanh(0.5 * x) + 0.5


def _inproj_body(x_ref, g_ref, w_ref, o_ref, xn_ref):
    @pl.when(pl.program_id(1) == 0)
    def _():
        x = x_ref[...]
        ms = jnp.mean(x * x, axis=-1, keepdims=True)
        xn_ref[...] = (x * lax.rsqrt(ms + NORM_EPS) * g_ref[...]).astype(BF16)

    o_ref[...] = jnp.dot(xn_ref[...], w_ref[...], preferred_element_type=F32).astype(o_ref.dtype)


def _inproj(x2d, gain, w_bf16, tm=1024, tn=2176):
    n_tok = x2d.shape[0]
    n_out = w_bf16.shape[1]
    return pl.pallas_call(
        _inproj_body,
        grid=(n_tok // tm, n_out // tn),
        in_specs=[
            pl.BlockSpec((tm, D_MODEL), lambda i, j: (i, 0)),
            pl.BlockSpec((1, D_MODEL), lambda i, j: (0, 0)),
            pl.BlockSpec((D_MODEL, tn), lambda i, j: (0, j)),
        ],
        out_specs=pl.BlockSpec((tm, tn), lambda i, j: (i, j)),
        out_shape=jax.ShapeDtypeStruct((n_tok, n_out), BF16),
        scratch_shapes=[pltpu.VMEM((tm, D_MODEL), BF16)],
        compiler_params=pltpu.CompilerParams(
            dimension_semantics=("parallel", "arbitrary"), vmem_limit_bytes=VMEM_LIMIT),
        name="inproj",
    )(x2d, gain.reshape(1, D_MODEL), w_bf16)


class _Dir:
    def __init__(self, q_ref, f_ref, v_ref, o_ref, st_ref, lb, tri, mask, mid_row, last_row):
        self.q_ref, self.f_ref, self.v_ref, self.o_ref, self.st_ref = q_ref, f_ref, v_ref, o_ref, st_ref
        self.lb, self.tri, self.mask, self.mid_row, self.last_row = lb, tri, mask, mid_row, last_row


def _gla_group(chains):
    c = HG_CHUNK
    dk = HG_DK
    pre = []
    for d, r0 in chains:
        q = d.q_ref[pl.ds(r0, c), :].astype(F32)
        q = q * _sigmoid(q)
        f = d.lb + (1.0 - d.lb) * _sigmoid(d.f_ref[pl.ds(r0, c), :].astype(F32))
        logf = jnp.log(f)
        hi = logf.astype(BF16)
        lo = (logf - hi.astype(F32)).astype(BF16)
        pre.append((q, 1.0 - f, jnp.concatenate([hi, lo], axis=1)))
    cums = []
    for (d, _), (_, _, hl) in zip(chains, pre):
        r = jnp.dot(d.tri, hl, preferred_element_type=F32)
        cums.append(r[:, :dk] + r[:, dk:])
    mids = []
    for (d, r0), (q, k, _), cum in zip(chains, pre, cums):
        mid = cum[d.mid_row:d.mid_row + 1, :]
        last = cum[d.last_row:d.last_row + 1, :]
        a = cum - mid
        qa = q * jnp.exp(a)
        ka = k * jnp.exp(-a)
        qd = (qa * jnp.exp(mid)).astype(BF16)
        kend = (ka * jnp.exp(last - mid)).astype(BF16)
        dec = jnp.exp(last)
        scores = lax.dot_general(qa.astype(BF16), ka.astype(BF16), NT_DIMS,
                                 preferred_element_type=F32)
        mids.append((qd, kend, dec, scores))
    outs = []
    for (d, r0), (qd, kend, dec, scores) in zip(chains, mids):
        v = d.v_ref[pl.ds(r0, c), :]
        sm = jnp.where(d.mask, scores, 0.0).astype(BF16)
        o_intra = jnp.dot(sm, v, preferred_element_type=F32)
        upd = jnp.dot(jnp.transpose(v.astype(F32)).astype(BF16), kend, preferred_element_type=F32)
        outs.append((o_intra, upd))
    for (d, r0), (qd, _, dec, _), (o_intra, upd) in zip(chains, mids, outs):
        st = d.st_ref[...]
        d.o_ref[pl.ds(r0, c), :] = (o_intra + lax.dot_general(
            qd, st.astype(BF16), NT_DIMS, preferred_element_type=F32)).astype(d.o_ref.dtype)
        d.st_ref[...] = st * dec + upd


def _hgrn2_body(qf_ref, ff_ref, vf_ref, qb_ref, fb_ref, vb_ref, lbf_ref, lbb_ref,
                of_ref, ob_ref, sf_ref, sb_ref):
    @pl.when(pl.program_id(2) == 0)
    def _():
        sf_ref[...] = jnp.zeros_like(sf_ref)
        sb_ref[...] = jnp.zeros_like(sb_ref)

    c = HG_CHUNK
    g = HG_GROUP
    n_chunks = qf_ref.shape[0] // c
    row = lax.broadcasted_iota(jnp.int32, (c, c), 0)
    col = lax.broadcasted_iota(jnp.int32, (c, c), 1)
    lower = col <= row
    upper = col >= row
    fwd = _Dir(qf_ref, ff_ref, vf_ref, of_ref, sf_ref, lbf_ref[...],
               jnp.where(lower, 1.0, 0.0).astype(BF16), lower, c // 2 - 1, c - 1)
    bwd = _Dir(qb_ref, fb_ref, vb_ref, ob_ref, sb_ref, lbb_ref[...],
               jnp.where(upper, 1.0, 0.0).astype(BF16), upper, c // 2, 0)

    def step(n, carry):
        chains = []
        for u in range(g):
            chains.append((fwd, pl.multiple_of((n * g + u) * c, c)))
        for u in range(g):
            chains.append((bwd, pl.multiple_of((n_chunks - 1 - n * g - u) * c, c)))
        _gla_group(chains)
        return carry

    lax.fori_loop(0, n_chunks // g, step, 0)


def _hgrn2(proj, lb_fwd, lb_bwd, bsz, seqlen, rows=1024):
    n_tok = bsz * seqlen
    rows = min(rows, seqlen)
    nb = seqlen // rows
    dk = HG_DK

    def fwd_map(col0):
        return lambda b, h, j: (b * nb + j, col0 // dk + h)

    def bwd_map(col0):
        return lambda b, h, j: (b * nb + (nb - 1 - j), col0 // dk + h)

    blk = (rows, dk)
    lb_spec = pl.BlockSpec((1, dk), lambda b, h, j: (0, h))
    return pl.pallas_call(
        _hgrn2_body,
        grid=(bsz, HG_HEADS, nb),
        in_specs=[
            pl.BlockSpec(blk, fwd_map(COL_HQ)), pl.BlockSpec(blk, fwd_map(COL_FF)),
            pl.BlockSpec(blk, fwd_map(COL_HI)),
            pl.BlockSpec(blk, bwd_map(COL_HQ)), pl.BlockSpec(blk, bwd_map(COL_FB)),
            pl.BlockSpec(blk, bwd_map(COL_HI)),
            lb_spec, lb_spec,
        ],
        out_specs=[pl.BlockSpec(blk, fwd_map(0)), pl.BlockSpec(blk, bwd_map(0))],
        out_shape=[jax.ShapeDtypeStruct((n_tok, HG_HEADS * dk), BF16)] * 2,
        scratch_shapes=[pltpu.VMEM((dk, dk), F32), pltpu.VMEM((dk, dk), F32)],
        compiler_params=pltpu.CompilerParams(
            dimension_semantics=("parallel", "parallel", "arbitrary"),
            vmem_limit_bytes=VMEM_LIMIT),
        name="hgrn2",
    )(proj, proj, proj, proj, proj, proj, lb_fwd.reshape(1, -1), lb_bwd.reshape(1, -1))


def _attn_body(seqlen, sink_ref, q_ref, kp_ref, ko_ref, kn_ref, vp_ref, vo_ref, vn_ref, o_ref):
    j = pl.program_id(1)
    blk = AT_BLOCK
    span = 3 * blk
    row = lax.broadcasted_iota(jnp.int32, (blk, span), 0)
    col = lax.broadcasted_iota(jnp.int32, (blk, span), 1)
    absrel = jnp.abs(row + blk - col)
    key_pos = j * blk - blk + col
    valid = (absrel <= WINDOW) & (key_pos >= 0) & (key_pos < seqlen)
    base = jnp.where(valid, -absrel.astype(F32), -jnp.inf)
    dh = AT_HEAD_DIM
    lanes = 2 * dh
    lo_half = lax.broadcasted_iota(jnp.int32, (blk, lanes), 1) < dh

    def padded(ref, g):
        t = ref[:, (g // 2) * lanes:(g // 2 + 1) * lanes].astype(F32)
        if g % 2 == 0:
            lo = jnp.where(lo_half, t, 0.0)
            hi = pltpu.roll(lo, dh, axis=1)
        else:
            hi = jnp.where(lo_half, 0.0, t)
            lo = pltpu.roll(hi, dh, axis=1)
        return lo, hi

    def banded(refs, g):
        pads = [padded(r, g) for r in refs]
        return jnp.concatenate([p[0] for p in pads] + [p[1] for p in pads], axis=0).astype(BF16)

    scores, values = [], []
    for g in range(AT_KV_HEADS):
        q2 = q_ref[:, g * 2 * lanes:(g + 1) * 2 * lanes] * (dh ** -0.5)
        lhs = jnp.concatenate([q2[:, :lanes], q2[:, lanes:]], axis=0).astype(BF16)
        scores.append(lax.dot_general(lhs, banded((kp_ref, ko_ref, kn_ref), g), NT_DIMS,
                                      preferred_element_type=F32))
        values.append(banded((vp_ref, vo_ref, vn_ref), g))
    probs, denoms = [], []
    for g in range(AT_KV_HEADS):
        p_rows, d_rows = [], []
        for r in range(2):
            p_cols, d_cols = [], []
            for c in range(2):
                h = g * AT_GROUP + 2 * r + c
                slope = 2.0 ** (-8.0 * (h + 1) / AT_Q_HEADS)
                sink = sink_ref[h]
                s = scores[g][r * blk:(r + 1) * blk, c * span:(c + 1) * span] + slope * base
                m = jnp.maximum(jnp.max(s, axis=-1, keepdims=True), sink)
                p = jnp.exp(s - m)
                d_cols.append(jnp.sum(p, axis=-1, keepdims=True) + jnp.exp(sink - m))
                p_cols.append(p.astype(BF16))
            p_rows.append(jnp.concatenate(p_cols, axis=1))
            d_rows.append(jnp.where(lo_half, d_cols[0], d_cols[1]))
        probs.append(jnp.concatenate(p_rows, axis=0))
        denoms.append(d_rows)
    outs = []
    for g in range(AT_KV_HEADS):
        o = jnp.dot(probs[g], values[g], preferred_element_type=F32)
        for r in range(2):
            outs.append(o[r * blk:(r + 1) * blk, :] / denoms[g][r])
    o_ref[...] = jnp.concatenate(outs, axis=-1).astype(o_ref.dtype)


def _attention(proj, sink, bsz, seqlen):
    n_tok = bsz * seqlen
    nb = seqlen // AT_BLOCK
    kcol = COL_AK // AT_KVWIDTH
    vcol = COL_AV // AT_KVWIDTH

    def prev(c):
        return lambda b, j: (b * nb + jnp.maximum(j - 1, 0), c)

    def own(c):
        return lambda b, j: (b * nb + j, c)

    def nxt(c):
        return lambda b, j: (b * nb + jnp.minimum(j + 1, nb - 1), c)

    kv_blk = (AT_BLOCK, AT_KVWIDTH)
    return pl.pallas_call(
        functools.partial(_attn_body, seqlen),
        grid=(bsz, nb),
        in_specs=[
            pl.BlockSpec(memory_space=pltpu.SMEM),
            pl.BlockSpec((AT_BLOCK, D_MODEL), own(COL_AQ // D_MODEL)),
            pl.BlockSpec(kv_blk, prev(kcol)), pl.BlockSpec(kv_blk, own(kcol)),
            pl.BlockSpec(kv_blk, nxt(kcol)),
            pl.BlockSpec(kv_blk, prev(vcol)), pl.BlockSpec(kv_blk, own(vcol)),
            pl.BlockSpec(kv_blk, nxt(vcol)),
        ],
        out_specs=pl.BlockSpec((AT_BLOCK, D_MODEL), lambda b, j: (b * nb + j, 0)),
        out_shape=jax.ShapeDtypeStruct((n_tok, D_MODEL), BF16),
        compiler_params=pltpu.CompilerParams(
            dimension_semantics=("parallel", "arbitrary"), vmem_limit_bytes=VMEM_LIMIT),
        name="window_attn",
    )(sink.astype(F32), proj, proj, proj, proj, proj, proj, proj)


def _merge_body(n_steps, of_ref, ob_ref, hg_ref, ga0_ref, ga1_ref, gb0_ref, gb1_ref, ya_ref, x_ref,
                gn_ref, wo_ref, g2_ref, wr_ref, br_ref,
                h_ref, gate_ref, dest_ref, cnt_ref, xb_hbm,
                carry_ref, ring, dest_vmem, dest_smem, cnt_vmem, cnt_smem, zero_ref,
                sem_rows, sem_dest, sem_misc):
    i = pl.program_id(0)
    slot = lax.rem(i, 2)
    tm = x_ref.shape[0]
    cap = xb_hbm.shape[0] // N_EXPERTS

    def rows_wait(sl):
        pltpu.make_async_copy(xb_hbm.at[pl.ds(0, tm * TOP_K)], xb_hbm.at[pl.ds(0, tm * TOP_K)],
                              sem_rows.at[sl]).wait()

    def dest_copy(sl):
        return pltpu.make_async_copy(dest_vmem.at[sl], dest_smem.at[sl], sem_dest.at[sl])

    def issue_rows(sl):
        dest_copy(sl).wait()

        def body(g, c):
            for s in range(SUBLANES):
                for k in range(TOP_K):
                    pltpu.make_async_copy(
                        ring.at[sl, g, :, s, :], xb_hbm.at[dest_smem[sl, k, g * SUBLANES + s]],
                        sem_rows.at[sl]).start(priority=k % 2)
            return c
        lax.fori_loop(0, tm // SUBLANES, body, 0, unroll=ISSUE_UNROLL)

    @pl.when(i == 0)
    def _():
        carry_ref[...] = jnp.zeros_like(carry_ref)

    @pl.when(i >= 2)
    def _():
        rows_wait(slot)

    sub = MERGE_SUB
    subs = [slice(s0, s0 + sub) for s0 in range(0, x_ref.shape[0], sub)]
    gn = gn_ref[...]

    merged = []
    for rs in subs:
        o = of_ref[rs, :].astype(F32) + ob_ref[rs, :].astype(F32)
        parts = []
        for h in range(HG_HEADS):
            oh = o[:, h * HG_DK:(h + 1) * HG_DK]
            ms = jnp.mean(oh * oh, axis=-1, keepdims=True)
            parts.append(oh * lax.rsqrt(ms + NORM_EPS) * gn)
        hg = hg_ref[rs, :].astype(F32)
        y_rec = jnp.concatenate(parts, axis=-1) * (hg * _sigmoid(hg))
        gate_a = jnp.concatenate([ga0_ref[rs, :], ga1_ref[rs, :]], axis=-1).astype(F32)
        gate_b = jnp.concatenate([gb0_ref[rs, :], gb1_ref[rs, :]], axis=-1).astype(F32)
        merged.append((_sigmoid(gate_a) * y_rec
                       + _sigmoid(gate_b) * ya_ref[rs, :].astype(F32)).astype(BF16))

    xns = []
    for rs, mg in zip(subs, merged):
        h_res = x_ref[rs, :] + jnp.dot(mg, wo_ref[...], preferred_element_type=F32)
        h_ref[rs, :] = h_res
        ms = jnp.mean(h_res * h_res, axis=-1, keepdims=True)
        xn = h_res * lax.rsqrt(ms + NORM_EPS) * g2_ref[...]
        _store_tiles(ring.at[slot, rs.start // SUBLANES:rs.stop // SUBLANES], xn)
        xns.append(xn)

    logits = [lax.dot_general(wr_ref[...], xn, NT_DIMS, preferred_element_type=F32,
                              precision=lax.Precision.HIGHEST) + br_ref[...] for xn in xns]
    e_iota = lax.broadcasted_iota(jnp.int32, (N_EXPERTS, sub), 0).astype(F32)
    picked = []
    for rs, vals in zip(subs, logits):
        tops, idxs = [], []
        for _ in range(TOP_K):
            m = jnp.max(vals, axis=0, keepdims=True)
            sel = jnp.min(jnp.where(vals == m, e_iota, float(N_EXPERTS)), axis=0, keepdims=True)
            tops.append(m)
            idxs.append(sel)
            vals = jnp.where(e_iota == sel, -jnp.inf, vals)
        exps = [jnp.exp(t - tops[0]) for t in tops]
        den = exps[0] + exps[1] + exps[2] + exps[3]
        gate_ref[:, rs] = jnp.concatenate([e / den for e in exps], axis=0)
        picked.append((idxs, jnp.where(vals == -jnp.inf, 1.0, 0.0)))

    s_iota = lax.broadcasted_iota(jnp.int32, (sub, sub), 0)
    t_iota = lax.broadcasted_iota(jnp.int32, (sub, sub), 1)
    strict = jnp.where(s_iota < t_iota, 1.0, 0.0).astype(BF16)
    prefixes = [jnp.dot(chosen.astype(BF16), strict, preferred_element_type=F32)
                for _, chosen in picked]
    carry = carry_ref[...]
    for rs, (idxs, chosen), prefix in zip(subs, picked, prefixes):
        prefix = prefix + carry
        dests = [sel * float(cap) + jnp.sum(jnp.where(e_iota == sel, prefix, 0.0), axis=0, keepdims=True)
                 for sel in idxs]
        dest = jnp.concatenate(dests, axis=0).astype(jnp.int32)
        dest_ref[:, rs] = dest
        dest_vmem[slot, :, rs] = dest
        carry = carry + jnp.sum(chosen, axis=1, keepdims=True)
    carry_ref[...] = carry
    counts = jnp.broadcast_to(carry, cnt_ref.shape).astype(jnp.int32)
    cnt_ref[...] = counts

    dest_copy(slot).start()

    @pl.when(i >= 1)
    def _():
        issue_rows(1 - slot)

    @pl.when(i == n_steps - 1)
    def _():
        issue_rows(slot)
        if n_steps >= 2:
            rows_wait(1 - slot)
        rows_wait(slot)
        cnt_vmem[...] = counts
        cp = pltpu.make_async_copy(cnt_vmem, cnt_smem, sem_misc.at[0])
        cp.start()
        cp.wait()
        zero_ref[...] = jnp.zeros_like(zero_ref)

        def tail(e):
            return pltpu.make_async_copy(
                zero_ref, xb_hbm.at[pl.ds(e * cap + cnt_smem[e, 0], MOE_ROWS)], sem_misc.at[0])
        for e in range(N_EXPERTS):
            tail(e).start()
        for e in range(N_EXPERTS):
            tail(e).wait()


def _merge_route(o_f, o_b, proj, y_att, x2d, gn_gain, w_out_bf16, g2, w_router_t, b_router, tm=512):
    n_tok = x2d.shape[0]
    nblk = n_tok // tm
    row_blk = (tm, D_MODEL)
    half = D_MODEL // 2
    half_blk = (tm, half)

    def rows(c):
        return lambda i: (i, c)

    full = lambda i: (0, 0)
    tok_out = pl.BlockSpec((TOP_K, tm), lambda i: (0, i))
    cap = n_tok + MOE_ROWS
    return pl.pallas_call(
        functools.partial(_merge_body, nblk),
        grid=(nblk,),
        in_specs=[
            pl.BlockSpec(row_blk, rows(0)), pl.BlockSpec(row_blk, rows(0)),
            pl.BlockSpec(row_blk, rows(COL_HG // D_MODEL)),
            pl.BlockSpec(half_blk, rows(COL_GA // half)),
            pl.BlockSpec(half_blk, rows(COL_GA // half + 1)),
            pl.BlockSpec(half_blk, rows(COL_GB // half)),
            pl.BlockSpec(half_blk, rows(COL_GB // half + 1)),
            pl.BlockSpec(row_blk, rows(0)), pl.BlockSpec(row_blk, rows(0)),
            pl.BlockSpec((1, HG_DK), full),
            pl.BlockSpec((D_MODEL, D_MODEL), full),
            pl.BlockSpec((1, D_MODEL), full),
            pl.BlockSpec((N_EXPERTS, D_MODEL), full),
            pl.BlockSpec((N_EXPERTS, 1), full),
        ],
        out_specs=[
            pl.BlockSpec(row_blk, rows(0)),
            tok_out, tok_out,
            pl.BlockSpec((N_EXPERTS, LANES), full),
            pl.BlockSpec(memory_space=pl.ANY),
        ],
        out_shape=[
            jax.ShapeDtypeStruct((n_tok, D_MODEL), F32),
            jax.ShapeDtypeStruct((TOP_K, n_tok), F32),
            jax.ShapeDtypeStruct((TOP_K, n_tok), jnp.int32),
            jax.ShapeDtypeStruct((N_EXPERTS, LANES), jnp.int32),
            jax.ShapeDtypeStruct((N_EXPERTS * cap, LANE_TILES, LANES), F32),
        ],
        scratch_shapes=[
            pltpu.VMEM((N_EXPERTS, 1), F32),
            pltpu.VMEM((2, tm // SUBLANES) + TILE_VIEW, F32),
            pltpu.VMEM((2, TOP_K, tm), jnp.int32), pltpu.SMEM((2, TOP_K, tm), jnp.int32),
            pltpu.VMEM((N_EXPERTS, LANES), jnp.int32), pltpu.SMEM((N_EXPERTS, LANES), jnp.int32),
            pltpu.VMEM((MOE_ROWS, LANE_TILES, LANES), F32),
            pltpu.SemaphoreType.DMA((2,)), pltpu.SemaphoreType.DMA((2,)), pltpu.SemaphoreType.DMA((1,)),
        ],
        compiler_params=pltpu.CompilerParams(
            dimension_semantics=("arbitrary",), vmem_limit_bytes=VMEM_LIMIT),
        name="merge_route",
    )(o_f, o_b, proj, proj, proj, proj, proj, y_att, x2d, gn_gain.reshape(1, HG_DK), w_out_bf16,
      g2.reshape(1, D_MODEL), w_router_t, b_router.reshape(N_EXPERTS, 1))


ISSUE_UNROLL = 2


def _expert_body(bend_ref, bgrp_ref, x_hbm, wug_ref, bug_ref, wd_ref, bd_ref, y_hbm,
                 wug_bf, wd_bf, act_ref, xbuf, ybuf, sem_in, sem_out):
    e = pl.program_id(0)
    groups = MOE_ROWS // SUBLANES
    n_used = bend_ref[N_EXPERTS - 1]
    b_first = jnp.where(e == 0, 0, bend_ref[jnp.maximum(e - 1, 0)])
    b_last = bend_ref[e]

    def x_copies(b, slot):
        return [pltpu.make_async_copy(x_hbm.at[pl.ds(bgrp_ref[b], groups), s],
                                      xbuf.at[slot, :, :, s, :], sem_in.at[slot])
                for s in range(SUBLANES)]

    def y_copies(b, slot):
        return [pltpu.make_async_copy(ybuf.at[slot, :, :, s, :],
                                      y_hbm.at[pl.ds(bgrp_ref[b], groups), s], sem_out.at[slot])
                for s in range(SUBLANES)]

    def start_all(copies):
        for cp in copies:
            cp.start(priority=BLOCK_DMA_PRIORITY)

    def wait_all(copies):
        for cp in copies:
            cp.wait()

    @pl.when((e == 0) & (n_used > 0))
    def _():
        start_all(x_copies(0, 0))

    ns = D_FF // FF_SLAB
    for s in range(ns):
        lo, hi = s * FF_SLAB, (s + 1) * FF_SLAB
        wug_bf[:, 2 * lo:2 * lo + FF_SLAB] = wug_ref[0, :, lo:hi].astype(BF16)
        wug_bf[:, 2 * lo + FF_SLAB:2 * hi] = wug_ref[0, :, D_FF + lo:D_FF + hi].astype(BF16)
    wd_bf[...] = wd_ref[0].astype(BF16)
    b_ug = bug_ref[0]
    b_d = bd_ref[0]

    def block(b, carry):
        slot = lax.rem(b, 2)
        wait_all(x_copies(b, slot))

        @pl.when(b + 1 < n_used)
        def _():
            start_all(x_copies(b + 1, 1 - slot))

        @pl.when(b >= 2)
        def _():
            wait_all(y_copies(b - 2, slot))

        x = _load_tiles(xbuf.at[slot]).astype(BF16)
        for s in range(ns):
            lo, hi = s * FF_SLAB, (s + 1) * FF_SLAB
            hu = jnp.dot(x, wug_bf[:, 2 * lo:2 * hi], preferred_element_type=F32)
            g = jnp.minimum(hu[:, :FF_SLAB] + b_ug[:, lo:hi], SWIGLU_LIMIT)
            u = jnp.clip(hu[:, FF_SLAB:] + b_ug[:, D_FF + lo:D_FF + hi], -SWIGLU_LIMIT, SWIGLU_LIMIT)
            act_ref[:, lo:hi] = (g * _sigmoid(SWIGLU_ALPHA * g) * (u + 1.0)).astype(BF16)
        _store_tiles(ybuf.at[slot],
                     jnp.dot(act_ref[...], wd_bf[...], preferred_element_type=F32) + b_d)
        start_all(y_copies(b, slot))
        return carry

    lax.fori_loop(b_first, b_last, block, 0)

    @pl.when(e == N_EXPERTS - 1)
    def _():
        for back in (2, 1):
            @pl.when(n_used >= back)
            def _():
                wait_all(y_copies(n_used - back, lax.rem(n_used - back, 2)))


def _experts(xb, block_end, block_group, w_up_gate, b_up_gate, w_down, b_down):
    ring = (2, MOE_ROWS // SUBLANES) + TILE_VIEW

    def w_map(e, bend, bgrp):
        return (e, 0, 0)

    grid_spec = pltpu.PrefetchScalarGridSpec(
        num_scalar_prefetch=2,
        grid=(N_EXPERTS,),
        in_specs=[
            pl.BlockSpec(memory_space=pl.ANY),
            pl.BlockSpec((1, D_MODEL, 2 * D_FF), w_map),
            pl.BlockSpec((1, 1, 2 * D_FF), w_map),
            pl.BlockSpec((1, D_FF, D_MODEL), w_map),
            pl.BlockSpec((1, 1, D_MODEL), w_map),
        ],
        out_specs=pl.BlockSpec(memory_space=pl.ANY),
        scratch_shapes=[
            pltpu.VMEM((D_MODEL, 2 * D_FF), BF16), pltpu.VMEM((D_FF, D_MODEL), BF16),
            pltpu.VMEM((MOE_ROWS, D_FF), BF16),
            pltpu.VMEM(ring, F32), pltpu.VMEM(ring, F32),
            pltpu.SemaphoreType.DMA((2,)), pltpu.SemaphoreType.DMA((2,)),
        ],
    )
    return pl.pallas_call(
        _expert_body,
        grid_spec=grid_spec,
        out_shape=jax.ShapeDtypeStruct(xb.shape, F32),
        compiler_params=pltpu.CompilerParams(
            dimension_semantics=("arbitrary",), vmem_limit_bytes=VMEM_LIMIT),
        name="moe_experts",
    )(block_end, block_group, xb, w_up_gate, b_up_gate.reshape(N_EXPERTS, 1, -1),
      w_down, b_down.reshape(N_EXPERTS, 1, -1))


COMBINE_TOKENS = 128


def _combine_body(dest_ref, yb_hbm, h_ref, gate_ref, gain_ref, o_ref, buf_ref, sem):
    i = pl.program_id(0)
    n_steps = pl.num_programs(0)
    nt = COMBINE_TOKENS

    def issue(step, slot):
        base = step * nt

        def body(g, c):
            for s in range(SUBLANES):
                for k in range(TOP_K):
                    pltpu.make_async_copy(
                        yb_hbm.at[dest_ref[(base + g * SUBLANES + s) * TOP_K + k]],
                        buf_ref.at[slot, k, g, :, s, :], sem.at[slot]).start(priority=k % 2)
            return c
        lax.fori_loop(0, nt // SUBLANES, body, 0, unroll=ISSUE_UNROLL)

    slot = lax.rem(i, 2)

    @pl.when(i == 0)
    def _():
        issue(0, 0)

    @pl.when(i + 1 < n_steps)
    def _():
        issue(i + 1, 1 - slot)

    pltpu.make_async_copy(buf_ref.at[slot], buf_ref.at[slot], sem.at[slot]).wait()

    gates = gate_ref[...]
    acc = h_ref[...]
    for k in range(TOP_K):
        acc = acc + gates[:, k:k + 1] * _load_tiles(buf_ref.at[slot, k])
    ms = jnp.mean(acc * acc, axis=-1, keepdims=True)
    o_ref[...] = acc * lax.rsqrt(ms + NORM_EPS) * gain_ref[...]


def _combine(yb, dest_flat, h_res, gates_tk, final_gain):
    n_tok = h_res.shape[0]
    nt = COMBINE_TOKENS
    grid_spec = pltpu.PrefetchScalarGridSpec(
        num_scalar_prefetch=1,
        grid=(n_tok // nt,),
        in_specs=[
            pl.BlockSpec(memory_space=pl.ANY),
            pl.BlockSpec((nt, D_MODEL), lambda i, d: (i, 0)),
            pl.BlockSpec((nt, TOP_K), lambda i, d: (i, 0)),
            pl.BlockSpec((1, D_MODEL), lambda i, d: (0, 0)),
        ],
        out_specs=pl.BlockSpec((nt, D_MODEL), lambda i, d: (i, 0)),
        scratch_shapes=[pltpu.VMEM((2, TOP_K, nt // SUBLANES) + TILE_VIEW, F32),
                        pltpu.SemaphoreType.DMA((2,))],
    )
    return pl.pallas_call(
        _combine_body,
        grid_spec=grid_spec,
        out_shape=jax.ShapeDtypeStruct((n_tok, D_MODEL), F32),
        compiler_params=pltpu.CompilerParams(
            dimension_semantics=("arbitrary",), vmem_limit_bytes=VMEM_LIMIT),
        name="moe_combine",
    )(dest_flat, yb, h_res, gates_tk, final_gain.reshape(1, D_MODEL))


def kernel(x, norm_mix_gain, w_in, hg_lb_fwd, hg_lb_bwd, hg_norm_gain, attn_sink, w_out,
           norm_ffn_gain, w_router, b_router, w_up_gate, b_up_gate, w_down, b_down,
           final_norm_gain):
    bsz, seqlen, d = x.shape
    n_tok = bsz * seqlen
    x2d = x.reshape(n_tok, d)

    lb_fwd = jax.nn.softmax(hg_lb_fwd.astype(F32), axis=0)[0]
    lb_bwd = jax.nn.softmax(hg_lb_bwd.astype(F32), axis=0)[0]

    proj = _inproj(x2d, norm_mix_gain[0], w_in[0].astype(BF16))
    o_f, o_b = _hgrn2(proj, lb_fwd, lb_bwd, bsz, seqlen)
    y_att = _attention(proj, attn_sink[0], bsz, seqlen)
    h_res, gate_t, dest_t, counts, xb = _merge_route(
        o_f, o_b, proj, y_att, x2d, hg_norm_gain[0], w_out[0].astype(BF16),
        norm_ffn_gain[0], w_router[0].T.astype(F32), b_router[0].astype(F32))

    n_rows = xb.shape[0]
    cap = n_rows // N_EXPERTS
    counts = counts[:, 0]
    blocks = (counts + MOE_ROWS - 1) // MOE_ROWS
    block_end = jnp.cumsum(blocks).astype(jnp.int32)
    block_start = block_end - blocks
    n_blocks_max = n_tok * TOP_K // MOE_ROWS + N_EXPERTS
    b = jnp.arange(n_blocks_max, dtype=jnp.int32)
    owner = (block_end[None, :] <= b[:, None]).astype(jnp.int32)
    expert_of = jnp.minimum(jnp.sum(owner, axis=-1), N_EXPERTS - 1)
    start_of = jnp.sum(owner * blocks[None, :], axis=-1)
    block_group = ((expert_of * cap + (b - start_of) * MOE_ROWS) // SUBLANES).astype(jnp.int32)
    dest_flat = dest_t.T.reshape(-1)

    yb = _experts(xb.reshape((n_rows // SUBLANES,) + TOKEN_TILES), block_end, block_group,
                  w_up_gate[0], b_up_gate[0], w_down[0], b_down[0])
    out = _combine(yb.reshape(n_rows, LANE_TILES, LANES), dest_flat, h_res, gate_t.T, final_norm_gain)
    return out.reshape(bsz, seqlen, d)
```

```python
import functools

import jax
import jax.numpy as jnp
from jax import lax
from jax.experimental import pallas as pl
from jax.experimental.pallas import tpu as pltpu

F32 = jnp.float32
BF16 = jnp.bfloat16

D_MODEL = 1024
HG_HEADS = 8
HG_DK = 128
HG_CHUNK = 128
HG_GROUP = 4
AT_Q_HEADS = 16
AT_KV_HEADS = 4
AT_GROUP = AT_Q_HEADS // AT_KV_HEADS
AT_HEAD_DIM = 64
WINDOW = 128
AT_BLOCK = WINDOW
AT_KVWIDTH = AT_KV_HEADS * AT_HEAD_DIM
N_EXPERTS = 32
TOP_K = 4
D_FF = 1024
SWIGLU_LIMIT = 7.0
SWIGLU_ALPHA = 1.702
NORM_EPS = 1e-5
LOG2E = 1.4426950408889634
IN_WIDTH = 8704

COL_HQ, COL_FF, COL_FB, COL_HI, COL_HG = 0, 1024, 2048, 3072, 4096
COL_AQ, COL_AK, COL_AV, COL_GA, COL_GB = 5120, 6144, 6400, 6656, 7680

MOE_ROWS = 512
FF_SLAB = 256
MERGE_ISSUE_PARTS = 4
MERGE_SUB = 256
BLOCK_DMA_PRIORITY = 1
VMEM_LIMIT = 48 * 1024 * 1024

NT_DIMS = (((1,), (1,)), ((), ()))

SUBLANES = 8
LANES = 128
LANE_TILES = D_MODEL // LANES
TILE_VIEW = (LANE_TILES, SUBLANES, LANES)
TOKEN_TILES = (SUBLANES, LANE_TILES, LANES)


def _store_tiles(ref, val):
    groups = val.shape[0] // SUBLANES
    for t in range(LANE_TILES):
        ref[:, t] = val[:, t * LANES:(t + 1) * LANES].reshape(groups, SUBLANES, LANES)


def _load_tiles(ref):
    groups = ref.shape[0]
    return jnp.concatenate(
        [ref[:, t].reshape(groups * SUBLANES, LANES) for t in range(LANE_TILES)], axis=1)


def _sigmoid(x):
    return 0.5 * jnp.tanh(0.5 * x) + 0.5


def _inproj_body(x_ref, g_ref, w_ref, o_ref, xn_ref):
    @pl.when(pl.program_id(1) == 0)
    def _():
        x = x_ref[...]
        ms = jnp.mean(x * x, axis=-1, keepdims=True)
        xn_ref[...] = (x * lax.rsqrt(ms + NORM_EPS) * g_ref[...]).astype(BF16)

    o_ref[...] = jnp.dot(xn_ref[...], w_ref[...], preferred_element_type=F32).astype(o_ref.dtype)


def _inproj(x2d, gain, w_bf16, tm=1024, tn=2176):
    n_tok = x2d.shape[0]
    n_out = w_bf16.shape[1]
    return pl.pallas_call(
        _inproj_body,
        grid=(n_tok // tm, n_out // tn),
        in_specs=[
            pl.BlockSpec((tm, D_MODEL), lambda i, j: (i, 0)),
            pl.BlockSpec((1, D_MODEL), lambda i, j: (0, 0)),
            pl.BlockSpec((D_MODEL, tn), lambda i, j: (0, j)),
        ],
        out_specs=pl.BlockSpec((tm, tn), lambda i, j: (i, j)),
        out_shape=jax.ShapeDtypeStruct((n_tok, n_out), BF16),
        scratch_shapes=[pltpu.VMEM((tm, D_MODEL), BF16)],
        compiler_params=pltpu.CompilerParams(
            dimension_semantics=("parallel", "arbitrary"), vmem_limit_bytes=VMEM_LIMIT),
        name="inproj",
    )(x2d, gain.reshape(1, D_MODEL), w_bf16)


class _Dir:
    def __init__(self, q_ref, f_ref, v_ref, o_ref, st_ref, lb, tri, mask, mid_row, last_row):
        self.q_ref, self.f_ref, self.v_ref, self.o_ref, self.st_ref = q_ref, f_ref, v_ref, o_ref, st_ref
        self.c0, self.c1 = 0.5 * (1.0 + lb), 0.5 * (1.0 - lb)
        self.tri, self.mask, self.mid_row, self.last_row = tri, mask, mid_row, last_row


def _gla_group(chains):
    c = HG_CHUNK
    dk = HG_DK
    pre = []
    for d, r0 in chains:
        h = 0.5 * d.q_ref[pl.ds(r0, c), :].astype(F32)
        q = h * jnp.tanh(h) + h
        ct = d.c1 * jnp.tanh(0.5 * d.f_ref[pl.ds(r0, c), :].astype(F32))
        logf = jnp.log2(d.c0 + ct)
        hi = logf.astype(BF16)
        lo = (logf - hi.astype(F32)).astype(BF16)
        pre.append((q, (1.0 - d.c0) - ct, jnp.concatenate([hi, lo], axis=1)))
    cums = []
    for (d, _), (_, _, hl) in zip(chains, pre):
        r = jnp.dot(d.tri, hl, preferred_element_type=F32)
        cums.append(r[:, :dk] + r[:, dk:])
    mids = []
    for (d, r0), (q, k, _), cum in zip(chains, pre, cums):
        mid = cum[d.mid_row:d.mid_row + 1, :]
        last = cum[d.last_row:d.last_row + 1, :]
        a = cum - mid
        qa = q * jnp.exp2(a)
        ka = k * jnp.exp2(-a)
        qd = (qa * jnp.exp2(mid)).astype(BF16)
        kend = (ka * jnp.exp2(last - mid)).astype(BF16)
        dec = jnp.exp2(last)
        scores = lax.dot_general(qa.astype(BF16), ka.astype(BF16), NT_DIMS,
                                 preferred_element_type=F32)
        mids.append((qd, kend, dec, scores))
    outs = []
    for (d, r0), (qd, kend, dec, scores) in zip(chains, mids):
        v = d.v_ref[pl.ds(r0, c), :]
        sm = jnp.where(d.mask, scores, 0.0).astype(BF16)
        o_intra = jnp.dot(sm, v, preferred_element_type=F32)
        upd = jnp.dot(jnp.transpose(v.astype(F32)).astype(BF16), kend, preferred_element_type=F32)
        outs.append((o_intra, upd))
    for (d, r0), (qd, _, dec, _), (o_intra, upd) in zip(chains, mids, outs):
        st = d.st_ref[...]
        d.o_ref[pl.ds(r0, c), :] = (o_intra + lax.dot_general(
            qd, st.astype(BF16), NT_DIMS, preferred_element_type=F32)).astype(d.o_ref.dtype)
        d.st_ref[...] = st * dec + upd


def _hgrn2_body(qf_ref, ff_ref, vf_ref, qb_ref, fb_ref, vb_ref, lbf_ref, lbb_ref,
                of_ref, ob_ref, sf_ref, sb_ref):
    @pl.when(pl.program_id(2) == 0)
    def _():
        sf_ref[...] = jnp.zeros_like(sf_ref)
        sb_ref[...] = jnp.zeros_like(sb_ref)

    c = HG_CHUNK
    g = HG_GROUP
    n_chunks = qf_ref.shape[0] // c
    row = lax.broadcasted_iota(jnp.int32, (c, c), 0)
    col = lax.broadcasted_iota(jnp.int32, (c, c), 1)
    lower = col <= row
    upper = col >= row
    fwd = _Dir(qf_ref, ff_ref, vf_ref, of_ref, sf_ref, lbf_ref[...],
               jnp.where(lower, 1.0, 0.0).astype(BF16), lower, c // 2 - 1, c - 1)
    bwd = _Dir(qb_ref, fb_ref, vb_ref, ob_ref, sb_ref, lbb_ref[...],
               jnp.where(upper, 1.0, 0.0).astype(BF16), upper, c // 2, 0)

    def step(n, carry):
        chains = []
        for u in range(g):
            chains.append((fwd, pl.multiple_of((n * g + u) * c, c)))
        for u in range(g):
            chains.append((bwd, pl.multiple_of((n_chunks - 1 - n * g - u) * c, c)))
        _gla_group(chains)
        return carry

    lax.fori_loop(0, n_chunks // g, step, 0)


def _hgrn2(proj, lb_fwd, lb_bwd, bsz, seqlen, rows=1024):
    n_tok = bsz * seqlen
    rows = min(rows, seqlen)
    nb = seqlen // rows
    dk = HG_DK

    def fwd_map(col0):
        return lambda b, h, j: (b * nb + j, col0 // dk + h)

    def bwd_map(col0):
        return lambda b, h, j: (b * nb + (nb - 1 - j), col0 // dk + h)

    blk = (rows, dk)
    lb_spec = pl.BlockSpec((1, dk), lambda b, h, j: (0, h))
    return pl.pallas_call(
        _hgrn2_body,
        grid=(bsz, HG_HEADS, nb),
        in_specs=[
            pl.BlockSpec(blk, fwd_map(COL_HQ)), pl.BlockSpec(blk, fwd_map(COL_FF)),
            pl.BlockSpec(blk, fwd_map(COL_HI)),
            pl.BlockSpec(blk, bwd_map(COL_HQ)), pl.BlockSpec(blk, bwd_map(COL_FB)),
            pl.BlockSpec(blk, bwd_map(COL_HI)),
            lb_spec, lb_spec,
        ],
        out_specs=[pl.BlockSpec(blk, fwd_map(0)), pl.BlockSpec(blk, bwd_map(0))],
        out_shape=[jax.ShapeDtypeStruct((n_tok, HG_HEADS * dk), BF16)] * 2,
        scratch_shapes=[pltpu.VMEM((dk, dk), F32), pltpu.VMEM((dk, dk), F32)],
        compiler_params=pltpu.CompilerParams(
            dimension_semantics=("parallel", "parallel", "arbitrary"),
            vmem_limit_bytes=VMEM_LIMIT),
        name="hgrn2",
    )(proj, proj, proj, proj, proj, proj, lb_fwd.reshape(1, -1), lb_bwd.reshape(1, -1))


def _attn_body(seqlen, sink_ref, q_ref, kp_ref, ko_ref, kn_ref, vp_ref, vo_ref, vn_ref, o_ref):
    j = pl.program_id(1)
    blk = AT_BLOCK
    span = 3 * blk
    row = lax.broadcasted_iota(jnp.int32, (blk, span), 0)
    col = lax.broadcasted_iota(jnp.int32, (blk, span), 1)
    absrel = jnp.abs(row + blk - col)
    key_pos = j * blk - blk + col
    valid = (absrel <= WINDOW) & (key_pos >= 0) & (key_pos < seqlen)
    base = jnp.where(valid, -absrel.astype(F32), -jnp.inf)
    dh = AT_HEAD_DIM
    lanes = 2 * dh
    lo_half = lax.broadcasted_iota(jnp.int32, (blk, lanes), 1) < dh

    def padded(ref, g):
        t = ref[:, (g // 2) * lanes:(g // 2 + 1) * lanes].astype(F32)
        if g % 2 == 0:
            lo = jnp.where(lo_half, t, 0.0)
            hi = pltpu.roll(lo, dh, axis=1)
        else:
            hi = jnp.where(lo_half, 0.0, t)
            lo = pltpu.roll(hi, dh, axis=1)
        return lo, hi

    def banded(refs, g):
        pads = [padded(r, g) for r in refs]
        return jnp.concatenate([p[0] for p in pads] + [p[1] for p in pads], axis=0).astype(BF16)

    def score(g):
        q2 = q_ref[:, g * 2 * lanes:(g + 1) * 2 * lanes].astype(F32) * (dh ** -0.5 * LOG2E)
        lhs = jnp.concatenate([q2[:, :lanes], q2[:, lanes:]], axis=0).astype(BF16)
        return lax.dot_general(lhs, banded((kp_ref, ko_ref, kn_ref), g), NT_DIMS,
                               preferred_element_type=F32)

    def finish(g, s_all):
        p_rows, d_rows = [], []
        for r in range(2):
            p_cols, d_cols = [], []
            for c in range(2):
                h = g * AT_GROUP + 2 * r + c
                slope = 2.0 ** (-8.0 * (h + 1) / AT_Q_HEADS) * LOG2E
                sink = sink_ref[h] * LOG2E
                s = s_all[r * blk:(r + 1) * blk, c * span:(c + 1) * span] + slope * base
                m = jnp.maximum(jnp.max(s, axis=-1, keepdims=True), sink)
                p = jnp.exp2(s - m)
                d_cols.append(jnp.sum(p, axis=-1, keepdims=True) + jnp.exp2(sink - m))
                p_cols.append(p.astype(BF16))
            p_rows.append(jnp.concatenate(p_cols, axis=1))
            d_rows.append(jnp.where(lo_half, d_cols[0], d_cols[1]))
        o = jnp.dot(jnp.concatenate(p_rows, axis=0), banded((vp_ref, vo_ref, vn_ref), g),
                    preferred_element_type=F32)
        return [o[r * blk:(r + 1) * blk, :] / d_rows[r] for r in range(2)]

    outs = []
    s_next = score(0)
    for g in range(AT_KV_HEADS):
        s_cur = s_next
        if g + 1 < AT_KV_HEADS:
            s_next = score(g + 1)
        outs.extend(finish(g, s_cur))
    o_ref[...] = jnp.concatenate(outs, axis=-1).astype(o_ref.dtype)


def _attention(proj, sink, bsz, seqlen):
    n_tok = bsz * seqlen
    nb = seqlen // AT_BLOCK
    kcol = COL_AK // AT_KVWIDTH
    vcol = COL_AV // AT_KVWIDTH

    def prev(c):
        return lambda b, j: (b * nb + jnp.maximum(j - 1, 0), c)

    def own(c):
        return lambda b, j: (b * nb + j, c)

    def nxt(c):
        return lambda b, j: (b * nb + jnp.minimum(j + 1, nb - 1), c)

    kv_blk = (AT_BLOCK, AT_KVWIDTH)
    return pl.pallas_call(
        functools.partial(_attn_body, seqlen),
        grid=(bsz, nb),
        in_specs=[
            pl.BlockSpec(memory_space=pltpu.SMEM),
            pl.BlockSpec((AT_BLOCK, D_MODEL), own(COL_AQ // D_MODEL)),
            pl.BlockSpec(kv_blk, prev(kcol)), pl.BlockSpec(kv_blk, own(kcol)),
            pl.BlockSpec(kv_blk, nxt(kcol)),
            pl.BlockSpec(kv_blk, prev(vcol)), pl.BlockSpec(kv_blk, own(vcol)),
            pl.BlockSpec(kv_blk, nxt(vcol)),
        ],
        out_specs=pl.BlockSpec((AT_BLOCK, D_MODEL), lambda b, j: (b * nb + j, 0)),
        out_shape=jax.ShapeDtypeStruct((n_tok, D_MODEL), BF16),
        compiler_params=pltpu.CompilerParams(
            dimension_semantics=("parallel", "arbitrary"), vmem_limit_bytes=VMEM_LIMIT),
        name="window_attn",
    )(sink.astype(F32), proj, proj, proj, proj, proj, proj, proj)


def _merge_body(n_steps, of_ref, ob_ref, hg_ref, ga0_ref, ga1_ref, gb0_ref, gb1_ref, ya_ref, x_ref,
                gn_ref, wo_ref, g2_ref, wr_ref, br_ref,
                h_ref, gate_ref, dest_ref, cnt_ref, xb_hbm,
                carry_ref, ring, dest_vmem, dest_smem, cnt_vmem, cnt_smem, zero_ref,
                sem_rows, sem_dest, sem_misc):
    i = pl.program_id(0)
    slot = lax.rem(i, 2)
    tm = x_ref.shape[0]
    cap = xb_hbm.shape[0] // N_EXPERTS

    def rows_wait(sl):
        pltpu.make_async_copy(xb_hbm.at[pl.ds(0, tm * TOP_K)], xb_hbm.at[pl.ds(0, tm * TOP_K)],
                              sem_rows.at[sl]).wait()

    def dest_copy(sl):
        return pltpu.make_async_copy(dest_vmem.at[sl], dest_smem.at[sl], sem_dest.at[sl])

    n_groups = tm // SUBLANES

    def issue_rows(sl, part=0, parts=1):
        def body(g, c):
            for s in range(SUBLANES):
                for k in range(TOP_K):
                    pltpu.make_async_copy(
                        ring.at[sl, g, :, s, :], xb_hbm.at[dest_smem[sl, k, g * SUBLANES + s]],
                        sem_rows.at[sl]).start(priority=k % 2)
            return c
        lax.fori_loop(part * n_groups // parts, (part + 1) * n_groups // parts, body, 0,
                      unroll=ISSUE_UNROLL)

    def issue_previous(part):
        @pl.when(i >= 1)
        def _():
            if part == 0:
                dest_copy(1 - slot).wait()
            issue_rows(1 - slot, part, MERGE_ISSUE_PARTS)

    @pl.when(i == 0)
    def _():
        carry_ref[...] = jnp.zeros_like(carry_ref)

    @pl.when(i >= 2)
    def _():
        rows_wait(slot)

    sub = MERGE_SUB
    subs = [slice(s0, s0 + sub) for s0 in range(0, x_ref.shape[0], sub)]
    gn = gn_ref[...]

    merged = []
    for rs in subs:
        o = of_ref[rs, :].astype(F32) + ob_ref[rs, :].astype(F32)
        parts = []
        for h in range(HG_HEADS):
            oh = o[:, h * HG_DK:(h + 1) * HG_DK]
            ms = jnp.mean(oh * oh, axis=-1, keepdims=True)
            parts.append(oh * lax.rsqrt(ms + NORM_EPS) * gn)
        hg = hg_ref[rs, :].astype(F32)
        y_rec = jnp.concatenate(parts, axis=-1) * (hg * _sigmoid(hg))
        gate_a = jnp.concatenate([ga0_ref[rs, :], ga1_ref[rs, :]], axis=-1).astype(F32)
        gate_b = jnp.concatenate([gb0_ref[rs, :], gb1_ref[rs, :]], axis=-1).astype(F32)
        merged.append((_sigmoid(gate_a) * y_rec
                       + _sigmoid(gate_b) * ya_ref[rs, :].astype(F32)).astype(BF16))
    issue_previous(0)

    xns = []
    for rs, mg in zip(subs, merged):
        h_res = x_ref[rs, :] + jnp.dot(mg, wo_ref[...], preferred_element_type=F32)
        h_ref[rs, :] = h_res
        ms = jnp.mean(h_res * h_res, axis=-1, keepdims=True)
        xn = h_res * lax.rsqrt(ms + NORM_EPS) * g2_ref[...]
        _store_tiles(ring.at[slot, rs.start // SUBLANES:rs.stop // SUBLANES], xn)
        xns.append(xn)
    issue_previous(1)

    logits = [lax.dot_general(wr_ref[...], xn, NT_DIMS, preferred_element_type=F32,
                              precision=lax.Precision.HIGHEST) + br_ref[...] for xn in xns]
    e_iota = lax.broadcasted_iota(jnp.int32, (N_EXPERTS, sub), 0).astype(F32)
    picked = []
    for rs, vals in zip(subs, logits):
        tops, idxs = [], []
        for _ in range(TOP_K):
            m = jnp.max(vals, axis=0, keepdims=True)
            sel = jnp.min(jnp.where(vals == m, e_iota, float(N_EXPERTS)), axis=0, keepdims=True)
            tops.append(m)
            idxs.append(sel)
            vals = jnp.where(e_iota == sel, -jnp.inf, vals)
        exps = [jnp.exp(t - tops[0]) for t in tops]
        den = exps[0] + exps[1] + exps[2] + exps[3]
        gate_ref[:, rs] = jnp.concatenate([e / den for e in exps], axis=0)
        picked.append((idxs, jnp.where(vals == -jnp.inf, 1.0, 0.0)))
    issue_previous(2)

    s_iota = lax.broadcasted_iota(jnp.int32, (sub, sub), 0)
    t_iota = lax.broadcasted_iota(jnp.int32, (sub, sub), 1)
    strict = jnp.where(s_iota < t_iota, 1.0, 0.0).astype(BF16)
    prefixes = [jnp.dot(chosen.astype(BF16), strict, preferred_element_type=F32)
                for _, chosen in picked]
    carry = carry_ref[...]
    for rs, (idxs, chosen), prefix in zip(subs, picked, prefixes):
        prefix = prefix + carry
        dests = [sel * float(cap) + jnp.sum(jnp.where(e_iota == sel, prefix, 0.0), axis=0, keepdims=True)
                 for sel in idxs]
        dest = jnp.concatenate(dests, axis=0).astype(jnp.int32)
        dest_ref[:, rs] = dest
        dest_vmem[slot, :, rs] = dest
        carry = carry + jnp.sum(chosen, axis=1, keepdims=True)
    carry_ref[...] = carry
    counts = jnp.broadcast_to(carry, cnt_ref.shape).astype(jnp.int32)
    cnt_ref[...] = counts

    dest_copy(slot).start()
    issue_previous(3)

    @pl.when(i == n_steps - 1)
    def _():
        dest_copy(slot).wait()
        issue_rows(slot)
        if n_steps >= 2:
            rows_wait(1 - slot)
        rows_wait(slot)
        cnt_vmem[...] = counts
        cp = pltpu.make_async_copy(cnt_vmem, cnt_smem, sem_misc.at[0])
        cp.start()
        cp.wait()
        zero_ref[...] = jnp.zeros_like(zero_ref)

        def tail(e):
            return pltpu.make_async_copy(
                zero_ref, xb_hbm.at[pl.ds(e * cap + cnt_smem[e, 0], MOE_ROWS)], sem_misc.at[0])
        for e in range(N_EXPERTS):
            tail(e).start()
        for e in range(N_EXPERTS):
            tail(e).wait()


def _merge_route(o_f, o_b, proj, y_att, x2d, gn_gain, w_out_bf16, g2, w_router_t, b_router, tm=512):
    n_tok = x2d.shape[0]
    nblk = n_tok // tm
    row_blk = (tm, D_MODEL)
    half = D_MODEL // 2
    half_blk = (tm, half)

    def rows(c):
        return lambda i: (i, c)

    full = lambda i: (0, 0)
    tok_out = pl.BlockSpec((TOP_K, tm), lambda i: (0, i))
    cap = n_tok + MOE_ROWS
    return pl.pallas_call(
        functools.partial(_merge_body, nblk),
        grid=(nblk,),
        in_specs=[
            pl.BlockSpec(row_blk, rows(0)), pl.BlockSpec(row_blk, rows(0)),
            pl.BlockSpec(row_blk, rows(COL_HG // D_MODEL)),
            pl.BlockSpec(half_blk, rows(COL_GA // half)),
            pl.BlockSpec(half_blk, rows(COL_GA // half + 1)),
            pl.BlockSpec(half_blk, rows(COL_GB // half)),
            pl.BlockSpec(half_blk, rows(COL_GB // half + 1)),
            pl.BlockSpec(row_blk, rows(0)), pl.BlockSpec(row_blk, rows(0)),
            pl.BlockSpec((1, HG_DK), full),
            pl.BlockSpec((D_MODEL, D_MODEL), full),
            pl.BlockSpec((1, D_MODEL), full),
            pl.BlockSpec((N_EXPERTS, D_MODEL), full),
            pl.BlockSpec((N_EXPERTS, 1), full),
        ],
        out_specs=[
            pl.BlockSpec(row_blk, rows(0)),
            tok_out, tok_out,
            pl.BlockSpec((N_EXPERTS, LANES), full),
            pl.BlockSpec(memory_space=pl.ANY),
        ],
        out_shape=[
            jax.ShapeDtypeStruct((n_tok, D_MODEL), F32),
            jax.ShapeDtypeStruct((TOP_K, n_tok), F32),
            jax.ShapeDtypeStruct((TOP_K, n_tok), jnp.int32),
            jax.ShapeDtypeStruct((N_EXPERTS, LANES), jnp.int32),
            jax.ShapeDtypeStruct((N_EXPERTS * cap, LANE_TILES, LANES), F32),
        ],
        scratch_shapes=[
            pltpu.VMEM((N_EXPERTS, 1), F32),
            pltpu.VMEM((2, tm // SUBLANES) + TILE_VIEW, F32),
            pltpu.VMEM((2, TOP_K, tm), jnp.int32), pltpu.SMEM((2, TOP_K, tm), jnp.int32),
            pltpu.VMEM((N_EXPERTS, LANES), jnp.int32), pltpu.SMEM((N_EXPERTS, LANES), jnp.int32),
            pltpu.VMEM((MOE_ROWS, LANE_TILES, LANES), F32),
            pltpu.SemaphoreType.DMA((2,)), pltpu.SemaphoreType.DMA((2,)), pltpu.SemaphoreType.DMA((1,)),
        ],
        compiler_params=pltpu.CompilerParams(
            dimension_semantics=("arbitrary",), vmem_limit_bytes=VMEM_LIMIT),
        name="merge_route",
    )(o_f, o_b, proj, proj, proj, proj, proj, y_att, x2d, gn_gain.reshape(1, HG_DK), w_out_bf16,
      g2.reshape(1, D_MODEL), w_router_t, b_router.reshape(N_EXPERTS, 1))


ISSUE_UNROLL = 2


def _expert_body(bend_ref, bgrp_ref, x_hbm, wug_ref, bug_ref, wd_ref, bd_ref, y_hbm,
                 wug_bf, wd_bf, act_ref, xbuf, ybuf, sem_in, sem_out):
    e = pl.program_id(0)
    groups = MOE_ROWS // SUBLANES
    n_used = bend_ref[N_EXPERTS - 1]
    b_first = jnp.where(e == 0, 0, bend_ref[jnp.maximum(e - 1, 0)])
    b_last = bend_ref[e]

    def x_copies(b, slot):
        return [pltpu.make_async_copy(x_hbm.at[pl.ds(bgrp_ref[b], groups), s],
                                      xbuf.at[slot, :, :, s, :], sem_in.at[slot])
                for s in range(SUBLANES)]

    def y_copies(b, slot):
        return [pltpu.make_async_copy(ybuf.at[slot, :, :, s, :],
                                      y_hbm.at[pl.ds(bgrp_ref[b], groups), s], sem_out.at[slot])
                for s in range(SUBLANES)]

    def start_all(copies):
        for cp in copies:
            cp.start(priority=BLOCK_DMA_PRIORITY)

    def wait_all(copies):
        for cp in copies:
            cp.wait()

    @pl.when((e == 0) & (n_used > 0))
    def _():
        start_all(x_copies(0, 0))

    ns = D_FF // FF_SLAB
    for s in range(ns):
        lo, hi = s * FF_SLAB, (s + 1) * FF_SLAB
        wug_bf[:, 2 * lo:2 * lo + FF_SLAB] = wug_ref[0, :, lo:hi].astype(BF16)
        wug_bf[:, 2 * lo + FF_SLAB:2 * hi] = wug_ref[0, :, D_FF + lo:D_FF + hi].astype(BF16)
    wd_bf[...] = wd_ref[0].astype(BF16)
    b_ug = bug_ref[0]
    b_d = bd_ref[0]

    def block(b, carry):
        slot = lax.rem(b, 2)
        wait_all(x_copies(b, slot))

        @pl.when(b + 1 < n_used)
        def _():
            start_all(x_copies(b + 1, 1 - slot))

        @pl.when(b >= 2)
        def _():
            wait_all(y_copies(b - 2, slot))

        x = _load_tiles(xbuf.at[slot]).astype(BF16)
        for s in range(ns):
            lo, hi = s * FF_SLAB, (s + 1) * FF_SLAB
            hu = jnp.dot(x, wug_bf[:, 2 * lo:2 * hi], preferred_element_type=F32)
            g = jnp.minimum(hu[:, :FF_SLAB] + b_ug[:, lo:hi], SWIGLU_LIMIT)
            u = jnp.clip(hu[:, FF_SLAB:] + b_ug[:, D_FF + lo:D_FF + hi], -SWIGLU_LIMIT, SWIGLU_LIMIT)
            act_ref[:, lo:hi] = (g * _sigmoid(SWIGLU_ALPHA * g) * (u + 1.0)).astype(BF16)
        _store_tiles(ybuf.at[slot],
                     jnp.dot(act_ref[...], wd_bf[...], preferred_element_type=F32) + b_d)
        start_all(y_copies(b, slot))
        return carry

    lax.fori_loop(b_first, b_last, block, 0)

    @pl.when(e == N_EXPERTS - 1)
    def _():
        for back in (2, 1):
            @pl.when(n_used >= back)
            def _():
                wait_all(y_copies(n_used - back, lax.rem(n_used - back, 2)))


def _experts(xb, block_end, block_group, w_up_gate, b_up_gate, w_down, b_down):
    ring = (2, MOE_ROWS // SUBLANES) + TILE_VIEW

    def w_map(e, bend, bgrp):
        return (e, 0, 0)

    grid_spec = pltpu.PrefetchScalarGridSpec(
        num_scalar_prefetch=2,
        grid=(N_EXPERTS,),
        in_specs=[
            pl.BlockSpec(memory_space=pl.ANY),
            pl.BlockSpec((1, D_MODEL, 2 * D_FF), w_map),
            pl.BlockSpec((1, 1, 2 * D_FF), w_map),
            pl.BlockSpec((1, D_FF, D_MODEL), w_map),
            pl.BlockSpec((1, 1, D_MODEL), w_map),
        ],
        out_specs=pl.BlockSpec(memory_space=pl.ANY),
        scratch_shapes=[
            pltpu.VMEM((D_MODEL, 2 * D_FF), BF16), pltpu.VMEM((D_FF, D_MODEL), BF16),
            pltpu.VMEM((MOE_ROWS, D_FF), BF16),
            pltpu.VMEM(ring, F32), pltpu.VMEM(ring, F32),
            pltpu.SemaphoreType.DMA((2,)), pltpu.SemaphoreType.DMA((2,)),
        ],
    )
    return pl.pallas_call(
        _expert_body,
        grid_spec=grid_spec,
        out_shape=jax.ShapeDtypeStruct(xb.shape, F32),
        compiler_params=pltpu.CompilerParams(
            dimension_semantics=("arbitrary",), vmem_limit_bytes=VMEM_LIMIT),
        name="moe_experts",
    )(block_end, block_group, xb, w_up_gate, b_up_gate.reshape(N_EXPERTS, 1, -1),
      w_down, b_down.reshape(N_EXPERTS, 1, -1))


COMBINE_TOKENS = 128


def _combine_body(dest_ref, yb_hbm, h_ref, gate_ref, gain_ref, o_ref, buf_ref, sem):
    i = pl.program_id(0)
    n_steps = pl.num_programs(0)
    nt = COMBINE_TOKENS

    def issue(step, slot):
        base = step * nt

        def body(g, c):
            for s in range(SUBLANES):
                for k in range(TOP_K):
                    pltpu.make_async_copy(
                        yb_hbm.at[dest_ref[(base + g * SUBLANES + s) * TOP_K + k]],
                        buf_ref.at[slot, k, g, :, s, :], sem.at[slot]).start(priority=k % 2)
            return c
        lax.fori_loop(0, nt // SUBLANES, body, 0, unroll=ISSUE_UNROLL)

    slot = lax.rem(i, 2)

    @pl.when(i == 0)
    def _():
        issue(0, 0)

    @pl.when(i + 1 < n_steps)
    def _():
        issue(i + 1, 1 - slot)

    pltpu.make_async_copy(buf_ref.at[slot], buf_ref.at[slot], sem.at[slot]).wait()

    gates = gate_ref[...]
    acc = h_ref[...]
    for k in range(TOP_K):
        acc = acc + gates[:, k:k + 1] * _load_tiles(buf_ref.at[slot, k])
    ms = jnp.mean(acc * acc, axis=-1, keepdims=True)
    o_ref[...] = acc * lax.rsqrt(ms + NORM_EPS) * gain_ref[...]


def _combine(yb, dest_flat, h_res, gates_tk, final_gain):
    n_tok = h_res.shape[0]
    nt = COMBINE_TOKENS
    grid_spec = pltpu.PrefetchScalarGridSpec(
        num_scalar_prefetch=1,
        grid=(n_tok // nt,),
        in_specs=[
            pl.BlockSpec(memory_space=pl.ANY),
            pl.BlockSpec((nt, D_MODEL), lambda i, d: (i, 0)),
            pl.BlockSpec((nt, TOP_K), lambda i, d: (i, 0)),
            pl.BlockSpec((1, D_MODEL), lambda i, d: (0, 0)),
        ],
        out_specs=pl.BlockSpec((nt, D_MODEL), lambda i, d: (i, 0)),
        scratch_shapes=[pltpu.VMEM((2, TOP_K, nt // SUBLANES) + TILE_VIEW, F32),
                        pltpu.SemaphoreType.DMA((2,))],
    )
    return pl.pallas_call(
        _combine_body,
        grid_spec=grid_spec,
        out_shape=jax.ShapeDtypeStruct((n_tok, D_MODEL), F32),
        compiler_params=pltpu.CompilerParams(
            dimension_semantics=("arbitrary",), vmem_limit_bytes=VMEM_LIMIT),
        name="moe_combine",
    )(dest_flat, yb, h_res, gates_tk, final_gain.reshape(1, D_MODEL))


def kernel(x, norm_mix_gain, w_in, hg_lb_fwd, hg_lb_bwd, hg_norm_gain, attn_sink, w_out,
           norm_ffn_gain, w_router, b_router, w_up_gate, b_up_gate, w_down, b_down,
           final_norm_gain):
    bsz, seqlen, d = x.shape
    n_tok = bsz * seqlen
    x2d = x.reshape(n_tok, d)

    lb_fwd = jax.nn.softmax(hg_lb_fwd.astype(F32), axis=0)[0]
    lb_bwd = jax.nn.softmax(hg_lb_bwd.astype(F32), axis=0)[0]

    proj = _inproj(x2d, norm_mix_gain[0], w_in[0].astype(BF16))
    o_f, o_b = _hgrn2(proj, lb_fwd, lb_bwd, bsz, seqlen)
    y_att = _attention(proj, attn_sink[0], bsz, seqlen)
    h_res, gate_t, dest_t, counts, xb = _merge_route(
        o_f, o_b, proj, y_att, x2d, hg_norm_gain[0], w_out[0].astype(BF16),
        norm_ffn_gain[0], w_router[0].T.astype(F32), b_router[0].astype(F32))

    n_rows = xb.shape[0]
    cap = n_rows // N_EXPERTS
    counts = counts[:, 0]
    blocks = (counts + MOE_ROWS - 1) // MOE_ROWS
    block_end = jnp.cumsum(blocks).astype(jnp.int32)
    block_start = block_end - blocks
    n_blocks_max = n_tok * TOP_K // MOE_ROWS + N_EXPERTS
    b = jnp.arange(n_blocks_max, dtype=jnp.int32)
    owner = (block_end[None, :] <= b[:, None]).astype(jnp.int32)
    expert_of = jnp.minimum(jnp.sum(owner, axis=-1), N_EXPERTS - 1)
    start_of = jnp.sum(owner * blocks[None, :], axis=-1)
    block_group = ((expert_of * cap + (b - start_of) * MOE_ROWS) // SUBLANES).astype(jnp.int32)
    dest_flat = dest_t.T.reshape(-1)

    yb = _experts(xb.reshape((n_rows // SUBLANES,) + TOKEN_TILES), block_end, block_group,
                  w_up_gate[0], b_up_gate[0], w_down[0], b_down[0])
    out = _combine(yb.reshape(n_rows, LANE_TILES, LANES), dest_flat, h_res, gate_t.T, final_norm_gain)
    return out.reshape(bsz, seqlen, d)
```

```python
import functools

import jax
import jax.numpy as jnp
from jax import lax
from jax.experimental import pallas as pl
from jax.experimental.pallas import tpu as pltpu

F32 = jnp.float32
BF16 = jnp.bfloat16

D_MODEL = 1024
HG_HEADS = 8
HG_DK = 128
HG_CHUNK = 128
HG_GROUP = 4
AT_Q_HEADS = 16
AT_KV_HEADS = 4
AT_GROUP = AT_Q_HEADS // AT_KV_HEADS
AT_HEAD_DIM = 64
WINDOW = 128
AT_BLOCK = WINDOW
AT_KVWIDTH = AT_KV_HEADS * AT_HEAD_DIM
N_EXPERTS = 32
TOP_K = 4
D_FF = 1024
SWIGLU_LIMIT = 7.0
SWIGLU_ALPHA = 1.702
NORM_EPS = 1e-5
IN_WIDTH = 8704

COL_HQ, COL_FF, COL_FB, COL_HI, COL_HG = 0, 1024, 2048, 3072, 4096
COL_AQ, COL_AK, COL_AV, COL_GA, COL_GB = 5120, 6144, 6400, 6656, 7680

MOE_ROWS = 1024
FF_SLAB = 256
MERGE_SUB = 256
BLOCK_DMA_PRIORITY = 1
VMEM_LIMIT = 56 * 1024 * 1024

NT_DIMS = (((1,), (1,)), ((), ()))

SUBLANES = 8
LANES = 128
LANE_TILES = D_MODEL // LANES
TILE_VIEW = (LANE_TILES, SUBLANES, LANES)
TOKEN_TILES = (SUBLANES, LANE_TILES, LANES)


def _store_tiles(ref, val):
    groups = val.shape[0] // SUBLANES
    for t in range(LANE_TILES):
        ref[:, t] = val[:, t * LANES:(t + 1) * LANES].reshape(groups, SUBLANES, LANES)


def _load_tiles(ref):
    groups = ref.shape[0]
    return jnp.concatenate(
        [ref[:, t].reshape(groups * SUBLANES, LANES) for t in range(LANE_TILES)], axis=1)


def _sigmoid(x):
    return 0.5 * jnp.tanh(0.5 * x) + 0.5


def _inproj_body(x_ref, g_ref, w_ref, o_ref, xn_ref):
    @pl.when(pl.program_id(1) == 0)
    def _():
        x = x_ref[...]
        ms = jnp.mean(x * x, axis=-1, keepdims=True)
        xn_ref[...] = (x * lax.rsqrt(ms + NORM_EPS) * g_ref[...]).astype(BF16)

    o_ref[...] = jnp.dot(xn_ref[...], w_ref[...], preferred_element_type=F32).astype(o_ref.dtype)


def _inproj(x2d, gain, w_bf16, tm=1024, tn=2176):
    n_tok = x2d.shape[0]
    n_out = w_bf16.shape[1]
    return pl.pallas_call(
        _inproj_body,
        grid=(n_tok // tm, n_out // tn),
        in_specs=[
            pl.BlockSpec((tm, D_MODEL), lambda i, j: (i, 0)),
            pl.BlockSpec((1, D_MODEL), lambda i, j: (0, 0)),
            pl.BlockSpec((D_MODEL, tn), lambda i, j: (0, j)),
        ],
        out_specs=pl.BlockSpec((tm, tn), lambda i, j: (i, j)),
        out_shape=jax.ShapeDtypeStruct((n_tok, n_out), BF16),
        scratch_shapes=[pltpu.VMEM((tm, D_MODEL), BF16)],
        compiler_params=pltpu.CompilerParams(
            dimension_semantics=("parallel", "arbitrary"), vmem_limit_bytes=VMEM_LIMIT),
        name="inproj",
    )(x2d, gain.reshape(1, D_MODEL), w_bf16)


class _Dir:
    def __init__(self, q_ref, f_ref, v_ref, o_ref, st_ref, lb, tri, mask, mid_row, last_row):
        self.q_ref, self.f_ref, self.v_ref, self.o_ref, self.st_ref = q_ref, f_ref, v_ref, o_ref, st_ref
        self.c0, self.c1 = 0.5 * (1.0 + lb), 0.5 * (1.0 - lb)
        self.tri, self.mask, self.mid_row, self.last_row = tri, mask, mid_row, last_row


def _gla_group(chains):
    c = HG_CHUNK
    dk = HG_DK
    pre = []
    for d, r0 in chains:
        h = 0.5 * d.q_ref[pl.ds(r0, c), :].astype(F32)
        q = h * jnp.tanh(h) + h
        ct = d.c1 * jnp.tanh(0.5 * d.f_ref[pl.ds(r0, c), :].astype(F32))
        logf = jnp.log2(d.c0 + ct)
        hi = logf.astype(BF16)
        lo = (logf - hi.astype(F32)).astype(BF16)
        pre.append((q, (1.0 - d.c0) - ct, jnp.concatenate([hi, lo], axis=1)))
    cums = []
    for (d, _), (_, _, hl) in zip(chains, pre):
        r = jnp.dot(d.tri, hl, preferred_element_type=F32)
        cums.append(r[:, :dk] + r[:, dk:])
    mids = []
    for (d, r0), (q, k, _), cum in zip(chains, pre, cums):
        mid = cum[d.mid_row:d.mid_row + 1, :]
        last = cum[d.last_row:d.last_row + 1, :]
        a = cum - mid
        qa = q * jnp.exp2(a)
        ka = k * jnp.exp2(-a)
        qd = (qa * jnp.exp2(mid)).astype(BF16)
        kend = (ka * jnp.exp2(last - mid)).astype(BF16)
        dec = jnp.exp2(last)
        scores = lax.dot_general(qa.astype(BF16), ka.astype(BF16), NT_DIMS,
                                 preferred_element_type=F32)
        mids.append((qd, kend, dec, scores))
    outs = []
    for (d, r0), (qd, kend, dec, scores) in zip(chains, mids):
        v = d.v_ref[pl.ds(r0, c), :]
        sm = jnp.where(d.mask, scores, 0.0).astype(BF16)
        o_intra = jnp.dot(sm, v, preferred_element_type=F32)
        upd = jnp.dot(jnp.transpose(v.astype(F32)).astype(BF16), kend, preferred_element_type=F32)
        outs.append((o_intra, upd))
    for (d, r0), (qd, _, dec, _), (o_intra, upd) in zip(chains, mids, outs):
        st = d.st_ref[...]
        d.o_ref[pl.ds(r0, c), :] = (o_intra + lax.dot_general(
            qd, st.astype(BF16), NT_DIMS, preferred_element_type=F32)).astype(d.o_ref.dtype)
        d.st_ref[...] = st * dec + upd


def _hgrn2_body(qf_ref, ff_ref, vf_ref, qb_ref, fb_ref, vb_ref, lbf_ref, lbb_ref,
                of_ref, ob_ref, sf_ref, sb_ref):
    @pl.when(pl.program_id(2) == 0)
    def _():
        sf_ref[...] = jnp.zeros_like(sf_ref)
        sb_ref[...] = jnp.zeros_like(sb_ref)

    c = HG_CHUNK
    g = HG_GROUP
    n_chunks = qf_ref.shape[0] // c
    row = lax.broadcasted_iota(jnp.int32, (c, c), 0)
    col = lax.broadcasted_iota(jnp.int32, (c, c), 1)
    lower = col <= row
    upper = col >= row
    fwd = _Dir(qf_ref, ff_ref, vf_ref, of_ref, sf_ref, lbf_ref[...],
               jnp.where(lower, 1.0, 0.0).astype(BF16), lower, c // 2 - 1, c - 1)
    bwd = _Dir(qb_ref, fb_ref, vb_ref, ob_ref, sb_ref, lbb_ref[...],
               jnp.where(upper, 1.0, 0.0).astype(BF16), upper, c // 2, 0)

    def step(n, carry):
        chains = []
        for u in range(g):
            chains.append((fwd, pl.multiple_of((n * g + u) * c, c)))
        for u in range(g):
            chains.append((bwd, pl.multiple_of((n_chunks - 1 - n * g - u) * c, c)))
        _gla_group(chains)
        return carry

    lax.fori_loop(0, n_chunks // g, step, 0)


def _hgrn2(proj, lb_fwd, lb_bwd, bsz, seqlen, rows=1024):
    n_tok = bsz * seqlen
    rows = min(rows, seqlen)
    nb = seqlen // rows
    dk = HG_DK

    def fwd_map(col0):
        return lambda b, h, j: (b * nb + j, col0 // dk + h)

    def bwd_map(col0):
        return lambda b, h, j: (b * nb + (nb - 1 - j), col0 // dk + h)

    blk = (rows, dk)
    lb_spec = pl.BlockSpec((1, dk), lambda b, h, j: (0, h))
    return pl.pallas_call(
        _hgrn2_body,
        grid=(bsz, HG_HEADS, nb),
        in_specs=[
            pl.BlockSpec(blk, fwd_map(COL_HQ)), pl.BlockSpec(blk, fwd_map(COL_FF)),
            pl.BlockSpec(blk, fwd_map(COL_HI)),
            pl.BlockSpec(blk, bwd_map(COL_HQ)), pl.BlockSpec(blk, bwd_map(COL_FB)),
            pl.BlockSpec(blk, bwd_map(COL_HI)),
            lb_spec, lb_spec,
        ],
        out_specs=[pl.BlockSpec(blk, fwd_map(0)), pl.BlockSpec(blk, bwd_map(0))],
        out_shape=[jax.ShapeDtypeStruct((n_tok, HG_HEADS * dk), BF16)] * 2,
        scratch_shapes=[pltpu.VMEM((dk, dk), F32), pltpu.VMEM((dk, dk), F32)],
        compiler_params=pltpu.CompilerParams(
            dimension_semantics=("parallel", "parallel", "arbitrary"),
            vmem_limit_bytes=VMEM_LIMIT),
        name="hgrn2",
    )(proj, proj, proj, proj, proj, proj, lb_fwd.reshape(1, -1), lb_bwd.reshape(1, -1))


def _attn_body(seqlen, sink_ref, q_ref, kp_ref, ko_ref, kn_ref, vp_ref, vo_ref, vn_ref, o_ref):
    j = pl.program_id(1)
    blk = AT_BLOCK
    span = 3 * blk
    row = lax.broadcasted_iota(jnp.int32, (blk, span), 0)
    col = lax.broadcasted_iota(jnp.int32, (blk, span), 1)
    absrel = jnp.abs(row + blk - col)
    key_pos = j * blk - blk + col
    valid = (absrel <= WINDOW) & (key_pos >= 0) & (key_pos < seqlen)
    base = jnp.where(valid, -absrel.astype(F32), -jnp.inf)
    dh = AT_HEAD_DIM
    lanes = 2 * dh
    lo_half = lax.broadcasted_iota(jnp.int32, (blk, lanes), 1) < dh

    def padded(ref, g):
        t = ref[:, (g // 2) * lanes:(g // 2 + 1) * lanes].astype(F32)
        if g % 2 == 0:
            lo = jnp.where(lo_half, t, 0.0)
            hi = pltpu.roll(lo, dh, axis=1)
        else:
            hi = jnp.where(lo_half, 0.0, t)
            lo = pltpu.roll(hi, dh, axis=1)
        return lo, hi

    def banded(refs, g):
        pads = [padded(r, g) for r in refs]
        return jnp.concatenate([p[0] for p in pads] + [p[1] for p in pads], axis=0).astype(BF16)

    scores, values = [], []
    for g in range(AT_KV_HEADS):
        q2 = q_ref[:, g * 2 * lanes:(g + 1) * 2 * lanes] * (dh ** -0.5)
        lhs = jnp.concatenate([q2[:, :lanes], q2[:, lanes:]], axis=0).astype(BF16)
        scores.append(lax.dot_general(lhs, banded((kp_ref, ko_ref, kn_ref), g), NT_DIMS,
                                      preferred_element_type=F32))
        values.append(banded((vp_ref, vo_ref, vn_ref), g))
    probs, denoms = [], []
    for g in range(AT_KV_HEADS):
        p_rows, d_rows = [], []
        for r in range(2):
            p_cols, d_cols = [], []
            for c in range(2):
                h = g * AT_GROUP + 2 * r + c
                slope = 2.0 ** (-8.0 * (h + 1) / AT_Q_HEADS)
                sink = sink_ref[h]
                s = scores[g][r * blk:(r + 1) * blk, c * span:(c + 1) * span] + slope * base
                m = jnp.maximum(jnp.max(s, axis=-1, keepdims=True), sink)
                p = jnp.exp(s - m)
                d_cols.append(jnp.sum(p, axis=-1, keepdims=True) + jnp.exp(sink - m))
                p_cols.append(p.astype(BF16))
            p_rows.append(jnp.concatenate(p_cols, axis=1))
            d_rows.append(jnp.where(lo_half, d_cols[0], d_cols[1]))
        probs.append(jnp.concatenate(p_rows, axis=0))
        denoms.append(d_rows)
    outs = []
    for g in range(AT_KV_HEADS):
        o = jnp.dot(probs[g], values[g], preferred_element_type=F32)
        for r in range(2):
            outs.append(o[r * blk:(r + 1) * blk, :] / denoms[g][r])
    o_ref[...] = jnp.concatenate(outs, axis=-1).astype(o_ref.dtype)


def _attention(proj, sink, bsz, seqlen):
    n_tok = bsz * seqlen
    nb = seqlen // AT_BLOCK
    kcol = COL_AK // AT_KVWIDTH
    vcol = COL_AV // AT_KVWIDTH

    def prev(c):
        return lambda b, j: (b * nb + jnp.maximum(j - 1, 0), c)

    def own(c):
        return lambda b, j: (b * nb + j, c)

    def nxt(c):
        return lambda b, j: (b * nb + jnp.minimum(j + 1, nb - 1), c)

    kv_blk = (AT_BLOCK, AT_KVWIDTH)
    return pl.pallas_call(
        functools.partial(_attn_body, seqlen),
        grid=(bsz, nb),
        in_specs=[
            pl.BlockSpec(memory_space=pltpu.SMEM),
            pl.BlockSpec((AT_BLOCK, D_MODEL), own(COL_AQ // D_MODEL)),
            pl.BlockSpec(kv_blk, prev(kcol)), pl.BlockSpec(kv_blk, own(kcol)),
            pl.BlockSpec(kv_blk, nxt(kcol)),
            pl.BlockSpec(kv_blk, prev(vcol)), pl.BlockSpec(kv_blk, own(vcol)),
            pl.BlockSpec(kv_blk, nxt(vcol)),
        ],
        out_specs=pl.BlockSpec((AT_BLOCK, D_MODEL), lambda b, j: (b * nb + j, 0)),
        out_shape=jax.ShapeDtypeStruct((n_tok, D_MODEL), BF16),
        compiler_params=pltpu.CompilerParams(
            dimension_semantics=("parallel", "arbitrary"), vmem_limit_bytes=VMEM_LIMIT),
        name="window_attn",
    )(sink.astype(F32), proj, proj, proj, proj, proj, proj, proj)


def _merge_body(n_steps, of_ref, ob_ref, hg_ref, ga0_ref, ga1_ref, gb0_ref, gb1_ref, ya_ref, x_ref,
                gn_ref, wo_ref, g2_ref, wr_ref, br_ref,
                h_ref, gate_ref, dest_ref, cnt_ref, xb_hbm,
                carry_ref, ring, dest_vmem, dest_smem, cnt_vmem, cnt_smem, zero_ref,
                sem_rows, sem_dest, sem_misc):
    i = pl.program_id(0)
    slot = lax.rem(i, 2)
    tm = x_ref.shape[0]
    cap = xb_hbm.shape[0] // N_EXPERTS

    def rows_wait(sl):
        pltpu.make_async_copy(xb_hbm.at[pl.ds(0, tm * TOP_K)], xb_hbm.at[pl.ds(0, tm * TOP_K)],
                              sem_rows.at[sl]).wait()

    def dest_copy(sl):
        return pltpu.make_async_copy(dest_vmem.at[sl], dest_smem.at[sl], sem_dest.at[sl])

    def issue_rows(sl):
        dest_copy(sl).wait()

        def body(g, c):
            for s in range(SUBLANES):
                for k in range(TOP_K):
                    pltpu.make_async_copy(
                        ring.at[sl, g, :, s, :], xb_hbm.at[dest_smem[sl, k, g * SUBLANES + s]],
                        sem_rows.at[sl]).start(priority=k % 2)
            return c
        lax.fori_loop(0, tm // SUBLANES, body, 0, unroll=ISSUE_UNROLL)

    @pl.when(i == 0)
    def _():
        carry_ref[...] = jnp.zeros_like(carry_ref)

    @pl.when(i >= 2)
    def _():
        rows_wait(slot)

    sub = MERGE_SUB
    subs = [slice(s0, s0 + sub) for s0 in range(0, x_ref.shape[0], sub)]
    gn = gn_ref[...]

    merged = []
    for rs in subs:
        o = of_ref[rs, :].astype(F32) + ob_ref[rs, :].astype(F32)
        parts = []
        for h in range(HG_HEADS):
            oh = o[:, h * HG_DK:(h + 1) * HG_DK]
            ms = jnp.mean(oh * oh, axis=-1, keepdims=True)
            parts.append(oh * lax.rsqrt(ms + NORM_EPS) * gn)
        hg = hg_ref[rs, :].astype(F32)
        y_rec = jnp.concatenate(parts, axis=-1) * (hg * _sigmoid(hg))
        gate_a = jnp.concatenate([ga0_ref[rs, :], ga1_ref[rs, :]], axis=-1).astype(F32)
        gate_b = jnp.concatenate([gb0_ref[rs, :], gb1_ref[rs, :]], axis=-1).astype(F32)
        merged.append((_sigmoid(gate_a) * y_rec
                       + _sigmoid(gate_b) * ya_ref[rs, :].astype(F32)).astype(BF16))

    xns = []
    for rs, mg in zip(subs, merged):
        h_res = x_ref[rs, :] + jnp.dot(mg, wo_ref[...], preferred_element_type=F32)
        h_ref[rs, :] = h_res
        ms = jnp.mean(h_res * h_res, axis=-1, keepdims=True)
        xn = h_res * lax.rsqrt(ms + NORM_EPS) * g2_ref[...]
        _store_tiles(ring.at[slot, rs.start // SUBLANES:rs.stop // SUBLANES], xn)
        xns.append(xn)

    logits = [lax.dot_general(wr_ref[...], xn, NT_DIMS, preferred_element_type=F32,
                              precision=lax.Precision.HIGHEST) + br_ref[...] for xn in xns]
    e_iota = lax.broadcasted_iota(jnp.int32, (N_EXPERTS, sub), 0).astype(F32)
    picked = []
    for rs, vals in zip(subs, logits):
        tops, idxs = [], []
        for _ in range(TOP_K):
            m = jnp.max(vals, axis=0, keepdims=True)
            sel = jnp.min(jnp.where(vals == m, e_iota, float(N_EXPERTS)), axis=0, keepdims=True)
            tops.append(m)
            idxs.append(sel)
            vals = jnp.where(e_iota == sel, -jnp.inf, vals)
        exps = [jnp.exp(t - tops[0]) for t in tops]
        den = exps[0] + exps[1] + exps[2] + exps[3]
        gate_ref[:, rs] = jnp.concatenate([e / den for e in exps], axis=0)
        picked.append((idxs, jnp.where(vals == -jnp.inf, 1.0, 0.0)))

    s_iota = lax.broadcasted_iota(jnp.int32, (sub, sub), 0)
    t_iota = lax.broadcasted_iota(jnp.int32, (sub, sub), 1)
    strict = jnp.where(s_iota < t_iota, 1.0, 0.0).astype(BF16)
    prefixes = [jnp.dot(chosen.astype(BF16), strict, preferred_element_type=F32)
                for _, chosen in picked]
    carry = carry_ref[...]
    for rs, (idxs, chosen), prefix in zip(subs, picked, prefixes):
        prefix = prefix + carry
        dests = [sel * float(cap) + jnp.sum(jnp.where(e_iota == sel, prefix, 0.0), axis=0, keepdims=True)
                 for sel in idxs]
        dest = jnp.concatenate(dests, axis=0).astype(jnp.int32)
        dest_ref[:, rs] = dest
        dest_vmem[slot, :, rs] = dest
        carry = carry + jnp.sum(chosen, axis=1, keepdims=True)
    carry_ref[...] = carry
    counts = jnp.broadcast_to(carry, cnt_ref.shape).astype(jnp.int32)
    cnt_ref[...] = counts

    dest_copy(slot).start()

    @pl.when(i >= 1)
    def _():
        issue_rows(1 - slot)

    @pl.when(i == n_steps - 1)
    def _():
        issue_rows(slot)
        if n_steps >= 2:
            rows_wait(1 - slot)
        rows_wait(slot)
        cnt_vmem[...] = counts
        cp = pltpu.make_async_copy(cnt_vmem, cnt_smem, sem_misc.at[0])
        cp.start()
        cp.wait()
        zero_ref[...] = jnp.zeros_like(zero_ref)

        def tail(e):
            return pltpu.make_async_copy(
                zero_ref, xb_hbm.at[pl.ds(e * cap + cnt_smem[e, 0], MOE_ROWS)], sem_misc.at[0])
        for e in range(N_EXPERTS):
            tail(e).start()
        for e in range(N_EXPERTS):
            tail(e).wait()


def _merge_route(o_f, o_b, proj, y_att, x2d, gn_gain, w_out_bf16, g2, w_router_t, b_router, tm=512):
    n_tok = x2d.shape[0]
    nblk = n_tok // tm
    row_blk = (tm, D_MODEL)
    half = D_MODEL // 2
    half_blk = (tm, half)

    def rows(c):
        return lambda i: (i, c)

    full = lambda i: (0, 0)
    tok_out = pl.BlockSpec((TOP_K, tm), lambda i: (0, i))
    cap = n_tok + MOE_ROWS
    return pl.pallas_call(
        functools.partial(_merge_body, nblk),
        grid=(nblk,),
        in_specs=[
            pl.BlockSpec(row_blk, rows(0)), pl.BlockSpec(row_blk, rows(0)),
            pl.BlockSpec(row_blk, rows(COL_HG // D_MODEL)),
            pl.BlockSpec(half_blk, rows(COL_GA // half)),
            pl.BlockSpec(half_blk, rows(COL_GA // half + 1)),
            pl.BlockSpec(half_blk, rows(COL_GB // half)),
            pl.BlockSpec(half_blk, rows(COL_GB // half + 1)),
            pl.BlockSpec(row_blk, rows(0)), pl.BlockSpec(row_blk, rows(0)),
            pl.BlockSpec((1, HG_DK), full),
            pl.BlockSpec((D_MODEL, D_MODEL), full),
            pl.BlockSpec((1, D_MODEL), full),
            pl.BlockSpec((N_EXPERTS, D_MODEL), full),
            pl.BlockSpec((N_EXPERTS, 1), full),
        ],
        out_specs=[
            pl.BlockSpec(row_blk, rows(0)),
            tok_out, tok_out,
            pl.BlockSpec((N_EXPERTS, LANES), full),
            pl.BlockSpec(memory_space=pl.ANY),
        ],
        out_shape=[
            jax.ShapeDtypeStruct((n_tok, D_MODEL), F32),
            jax.ShapeDtypeStruct((TOP_K, n_tok), F32),
            jax.ShapeDtypeStruct((TOP_K, n_tok), jnp.int32),
            jax.ShapeDtypeStruct((N_EXPERTS, LANES), jnp.int32),
            jax.ShapeDtypeStruct((N_EXPERTS * cap, LANE_TILES, LANES), F32),
        ],
        scratch_shapes=[
            pltpu.VMEM((N_EXPERTS, 1), F32),
            pltpu.VMEM((2, tm // SUBLANES) + TILE_VIEW, F32),
            pltpu.VMEM((2, TOP_K, tm), jnp.int32), pltpu.SMEM((2, TOP_K, tm), jnp.int32),
            pltpu.VMEM((N_EXPERTS, LANES), jnp.int32), pltpu.SMEM((N_EXPERTS, LANES), jnp.int32),
            pltpu.VMEM((MOE_ROWS, LANE_TILES, LANES), F32),
            pltpu.SemaphoreType.DMA((2,)), pltpu.SemaphoreType.DMA((2,)), pltpu.SemaphoreType.DMA((1,)),
        ],
        compiler_params=pltpu.CompilerParams(
            dimension_semantics=("arbitrary",), vmem_limit_bytes=VMEM_LIMIT),
        name="merge_route",
    )(o_f, o_b, proj, proj, proj, proj, proj, y_att, x2d, gn_gain.reshape(1, HG_DK), w_out_bf16,
      g2.reshape(1, D_MODEL), w_router_t, b_router.reshape(N_EXPERTS, 1))


ISSUE_UNROLL = 2


def _expert_body(bend_ref, bgrp_ref, x_hbm, wug_ref, bug_ref, wd_ref, bd_ref, y_hbm,
                 wug_bf, wd_bf, act_ref, xbuf, ybuf, sem_in, sem_out):
    e = pl.program_id(0)
    groups = MOE_ROWS // SUBLANES
    n_used = bend_ref[N_EXPERTS - 1]
    b_first = jnp.where(e == 0, 0, bend_ref[jnp.maximum(e - 1, 0)])
    b_last = bend_ref[e]

    def x_copies(b, slot):
        return [pltpu.make_async_copy(x_hbm.at[pl.ds(bgrp_ref[b], groups), s],
                                      xbuf.at[slot, :, :, s, :], sem_in.at[slot])
                for s in range(SUBLANES)]

    def y_copies(b, slot):
        return [pltpu.make_async_copy(ybuf.at[slot, :, :, s, :],
                                      y_hbm.at[pl.ds(bgrp_ref[b], groups), s], sem_out.at[slot])
                for s in range(SUBLANES)]

    def start_all(copies):
        for cp in copies:
            cp.start(priority=BLOCK_DMA_PRIORITY)

    def wait_all(copies):
        for cp in copies:
            cp.wait()

    @pl.when((e == 0) & (n_used > 0))
    def _():
        start_all(x_copies(0, 0))

    ns = D_FF // FF_SLAB
    for s in range(ns):
        lo, hi = s * FF_SLAB, (s + 1) * FF_SLAB
        wug_bf[:, 2 * lo:2 * lo + FF_SLAB] = wug_ref[0, :, lo:hi].astype(BF16)
        wug_bf[:, 2 * lo + FF_SLAB:2 * hi] = wug_ref[0, :, D_FF + lo:D_FF + hi].astype(BF16)
    wd_bf[...] = wd_ref[0].astype(BF16)
    b_ug = bug_ref[0]
    b_d = bd_ref[0]

    def block(b, carry):
        slot = lax.rem(b, 2)
        wait_all(x_copies(b, slot))

        @pl.when(b + 1 < n_used)
        def _():
            start_all(x_copies(b + 1, 1 - slot))

        @pl.when(b >= 2)
        def _():
            wait_all(y_copies(b - 2, slot))

        x = _load_tiles(xbuf.at[slot]).astype(BF16)
        for s in range(ns):
            lo, hi = s * FF_SLAB, (s + 1) * FF_SLAB
            hu = jnp.dot(x, wug_bf[:, 2 * lo:2 * hi], preferred_element_type=F32)
            g = jnp.minimum(hu[:, :FF_SLAB] + b_ug[:, lo:hi], SWIGLU_LIMIT)
            u = jnp.clip(hu[:, FF_SLAB:] + b_ug[:, D_FF + lo:D_FF + hi], -SWIGLU_LIMIT, SWIGLU_LIMIT)
            act_ref[:, lo:hi] = (g * _sigmoid(SWIGLU_ALPHA * g) * (u + 1.0)).astype(BF16)
        _store_tiles(ybuf.at[slot],
                     jnp.dot(act_ref[...], wd_bf[...], preferred_element_type=F32) + b_d)
        start_all(y_copies(b, slot))
        return carry

    lax.fori_loop(b_first, b_last, block, 0)

    @pl.when(e == N_EXPERTS - 1)
    def _():
        for back in (2, 1):
            @pl.when(n_used >= back)
            def _():
                wait_all(y_copies(n_used - back, lax.rem(n_used - back, 2)))


def _experts(xb, block_end, block_group, w_up_gate, b_up_gate, w_down, b_down):
    ring = (2, MOE_ROWS // SUBLANES) + TILE_VIEW

    def w_map(e, bend, bgrp):
        return (e, 0, 0)

    grid_spec = pltpu.PrefetchScalarGridSpec(
        num_scalar_prefetch=2,
        grid=(N_EXPERTS,),
        in_specs=[
            pl.BlockSpec(memory_space=pl.ANY),
            pl.BlockSpec((1, D_MODEL, 2 * D_FF), w_map),
            pl.BlockSpec((1, 1, 2 * D_FF), w_map),
            pl.BlockSpec((1, D_FF, D_MODEL), w_map),
            pl.BlockSpec((1, 1, D_MODEL), w_map),
        ],
        out_specs=pl.BlockSpec(memory_space=pl.ANY),
        scratch_shapes=[
            pltpu.VMEM((D_MODEL, 2 * D_FF), BF16), pltpu.VMEM((D_FF, D_MODEL), BF16),
            pltpu.VMEM((MOE_ROWS, D_FF), BF16),
            pltpu.VMEM(ring, F32), pltpu.VMEM(ring, F32),
            pltpu.SemaphoreType.DMA((2,)), pltpu.SemaphoreType.DMA((2,)),
        ],
    )
    return pl.pallas_call(
        _expert_body,
        grid_spec=grid_spec,
        out_shape=jax.ShapeDtypeStruct(xb.shape, F32),
        compiler_params=pltpu.CompilerParams(
            dimension_semantics=("arbitrary",), vmem_limit_bytes=VMEM_LIMIT),
        name="moe_experts",
    )(block_end, block_group, xb, w_up_gate, b_up_gate.reshape(N_EXPERTS, 1, -1),
      w_down, b_down.reshape(N_EXPERTS, 1, -1))


COMBINE_TOKENS = 128


def _combine_body(dest_ref, yb_hbm, h_ref, gate_ref, gain_ref, o_ref, buf_ref, sem):
    i = pl.program_id(0)
    n_steps = pl.num_programs(0)
    nt = COMBINE_TOKENS

    def issue(step, slot):
        base = step * nt

        def body(g, c):
            for s in range(SUBLANES):
                for k in range(TOP_K):
                    pltpu.make_async_copy(
                        yb_hbm.at[dest_ref[(base + g * SUBLANES + s) * TOP_K + k]],
                        buf_ref.at[slot, k, g, :, s, :], sem.at[slot]).start(priority=k % 2)
            return c
        lax.fori_loop(0, nt // SUBLANES, body, 0, unroll=ISSUE_UNROLL)

    slot = lax.rem(i, 2)

    @pl.when(i == 0)
    def _():
        issue(0, 0)

    @pl.when(i + 1 < n_steps)
    def _():
        issue(i + 1, 1 - slot)

    pltpu.make_async_copy(buf_ref.at[slot], buf_ref.at[slot], sem.at[slot]).wait()

    gates = gate_ref[...]
    acc = h_ref[...]
    for k in range(TOP_K):
        acc = acc + gates[:, k:k + 1] * _load_tiles(buf_ref.at[slot, k])
    ms = jnp.mean(acc * acc, axis=-1, keepdims=True)
    o_ref[...] = acc * lax.rsqrt(ms + NORM_EPS) * gain_ref[...]


def _combine(yb, dest_flat, h_res, gates_tk, final_gain):
    n_tok = h_res.shape[0]
    nt = COMBINE_TOKENS
    grid_spec = pltpu.PrefetchScalarGridSpec(
        num_scalar_prefetch=1,
        grid=(n_tok // nt,),
        in_specs=[
            pl.BlockSpec(memory_space=pl.ANY),
            pl.BlockSpec((nt, D_MODEL), lambda i, d: (i, 0)),
            pl.BlockSpec((nt, TOP_K), lambda i, d: (i, 0)),
            pl.BlockSpec((1, D_MODEL), lambda i, d: (0, 0)),
        ],
        out_specs=pl.BlockSpec((nt, D_MODEL), lambda i, d: (i, 0)),
        scratch_shapes=[pltpu.VMEM((2, TOP_K, nt // SUBLANES) + TILE_VIEW, F32),
                        pltpu.SemaphoreType.DMA((2,))],
    )
    return pl.pallas_call(
        _combine_body,
        grid_spec=grid_spec,
        out_shape=jax.ShapeDtypeStruct((n_tok, D_MODEL), F32),
        compiler_params=pltpu.CompilerParams(
            dimension_semantics=("arbitrary",), vmem_limit_bytes=VMEM_LIMIT),
        name="moe_combine",
    )(dest_flat, yb, h_res, gates_tk, final_gain.reshape(1, D_MODEL))


def kernel(x, norm_mix_gain, w_in, hg_lb_fwd, hg_lb_bwd, hg_norm_gain, attn_sink, w_out,
           norm_ffn_gain, w_router, b_router, w_up_gate, b_up_gate, w_down, b_down,
           final_norm_gain):
    bsz, seqlen, d = x.shape
    n_tok = bsz * seqlen
    x2d = x.reshape(n_tok, d)

    lb_fwd = jax.nn.softmax(hg_lb_fwd.astype(F32), axis=0)[0]
    lb_bwd = jax.nn.softmax(hg_lb_bwd.astype(F32), axis=0)[0]

    proj = _inproj(x2d, norm_mix_gain[0], w_in[0].astype(BF16))
    o_f, o_b = _hgrn2(proj, lb_fwd, lb_bwd, bsz, seqlen)
    y_att = _attention(proj, attn_sink[0], bsz, seqlen)
    h_res, gate_t, dest_t, counts, xb = _merge_route(
        o_f, o_b, proj, y_att, x2d, hg_norm_gain[0], w_out[0].astype(BF16),
        norm_ffn_gain[0], w_router[0].T.astype(F32), b_router[0].astype(F32))

    n_rows = xb.shape[0]
    cap = n_rows // N_EXPERTS
    counts = counts[:, 0]
    blocks = (counts + MOE_ROWS - 1) // MOE_ROWS
    block_end = jnp.cumsum(blocks).astype(jnp.int32)
    block_start = block_end - blocks
    n_blocks_max = n_tok * TOP_K // MOE_ROWS + N_EXPERTS
    b = jnp.arange(n_blocks_max, dtype=jnp.int32)
    owner = (block_end[None, :] <= b[:, None]).astype(jnp.int32)
    expert_of = jnp.minimum(jnp.sum(owner, axis=-1), N_EXPERTS - 1)
    start_of = jnp.sum(owner * blocks[None, :], axis=-1)
    block_group = ((expert_of * cap + (b - start_of) * MOE_ROWS) // SUBLANES).astype(jnp.int32)
    dest_flat = dest_t.T.reshape(-1)

    yb = _experts(xb.reshape((n_rows // SUBLANES,) + TOKEN_TILES), block_end, block_group,
                  w_up_gate[0], b_up_gate[0], w_down[0], b_down[0])
    out = _combine(yb.reshape(n_rows, LANE_TILES, LANES), dest_flat, h_res, gate_t.T, final_norm_gain)
    return out.reshape(bsz, seqlen, d)
```

```python
import functools

import jax
import jax.numpy as jnp
from jax import lax
from jax.experimental import pallas as pl
from jax.experimental.pallas import tpu as pltpu

F32 = jnp.float32
BF16 = jnp.bfloat16

D_MODEL = 1024
HG_HEADS = 8
HG_DK = 128
HG_CHUNK = 128
HG_GROUP = 4
AT_Q_HEADS = 16
AT_KV_HEADS = 4
AT_GROUP = AT_Q_HEADS // AT_KV_HEADS
AT_HEAD_DIM = 64
WINDOW = 128
AT_BLOCK = WINDOW
AT_KVWIDTH = AT_KV_HEADS * AT_HEAD_DIM
N_EXPERTS = 32
TOP_K = 4
D_FF = 1024
SWIGLU_LIMIT = 7.0
SWIGLU_ALPHA = 1.702
NORM_EPS = 1e-5
IN_WIDTH = 8704

COL_HQ, COL_FF, COL_FB, COL_HI, COL_HG = 0, 1024, 2048, 3072, 4096
COL_AQ, COL_AK, COL_AV, COL_GA, COL_GB = 5120, 6144, 6400, 6656, 7680

MOE_ROWS = 1024
MOE_PAD = 512
FF_SLAB = 256
MERGE_SUB = 256
BLOCK_DMA_PRIORITY = 1
VMEM_LIMIT = 56 * 1024 * 1024

NT_DIMS = (((1,), (1,)), ((), ()))

SUBLANES = 8
LANES = 128
LANE_TILES = D_MODEL // LANES
TILE_VIEW = (LANE_TILES, SUBLANES, LANES)
TOKEN_TILES = (SUBLANES, LANE_TILES, LANES)


def _store_tiles(ref, val):
    groups = val.shape[0] // SUBLANES
    for t in range(val.shape[1] // LANES):
        ref[:, t] = val[:, t * LANES:(t + 1) * LANES].reshape(groups, SUBLANES, LANES)


def _load_tiles(ref):
    groups = ref.shape[0]
    return jnp.concatenate(
        [ref[:, t].reshape(groups * SUBLANES, LANES) for t in range(ref.shape[1])], axis=1)


PACKED_TILES = LANE_TILES // 2
PACKED_VIEW = (PACKED_TILES, SUBLANES, LANES)
PACKED_TOKEN_TILES = (SUBLANES, PACKED_TILES, LANES)


def _pack_bf16_pairs(x):
    half = x.shape[1] // 2
    lo = pltpu.bitcast(x[:, :half].astype(BF16).astype(F32), jnp.uint32)
    hi = pltpu.bitcast(x[:, half:].astype(BF16).astype(F32), jnp.uint32)
    return hi | lax.shift_right_logical(lo, jnp.uint32(16))


def _unpack_bf16_pairs(w):
    lo = pltpu.bitcast(lax.shift_left(w, jnp.uint32(16)), F32)
    hi = pltpu.bitcast(w & jnp.uint32(0xFFFF0000), F32)
    return jnp.concatenate([lo, hi], axis=1).astype(BF16)


def _sigmoid(x):
    return 0.5 * jnp.tanh(0.5 * x) + 0.5


def _inproj_body(x_ref, g_ref, w_ref, o_ref, xn_ref):
    @pl.when(pl.program_id(1) == 0)
    def _():
        x = x_ref[...]
        ms = jnp.mean(x * x, axis=-1, keepdims=True)
        xn_ref[...] = (x * lax.rsqrt(ms + NORM_EPS) * g_ref[...]).astype(BF16)

    o_ref[...] = jnp.dot(xn_ref[...], w_ref[...], preferred_element_type=F32).astype(o_ref.dtype)


def _inproj(x2d, gain, w_bf16, tm=1024, tn=2176):
    n_tok = x2d.shape[0]
    n_out = w_bf16.shape[1]
    return pl.pallas_call(
        _inproj_body,
        grid=(n_tok // tm, n_out // tn),
        in_specs=[
            pl.BlockSpec((tm, D_MODEL), lambda i, j: (i, 0)),
            pl.BlockSpec((1, D_MODEL), lambda i, j: (0, 0)),
            pl.BlockSpec((D_MODEL, tn), lambda i, j: (0, j)),
        ],
        out_specs=pl.BlockSpec((tm, tn), lambda i, j: (i, j)),
        out_shape=jax.ShapeDtypeStruct((n_tok, n_out), BF16),
        scratch_shapes=[pltpu.VMEM((tm, D_MODEL), BF16)],
        compiler_params=pltpu.CompilerParams(
            dimension_semantics=("parallel", "arbitrary"), vmem_limit_bytes=VMEM_LIMIT),
        name="inproj",
    )(x2d, gain.reshape(1, D_MODEL), w_bf16)


class _Dir:
    def __init__(self, q_ref, f_ref, v_ref, o_ref, st_ref, lb, tri, mask, mid_row, last_row):
        self.q_ref, self.f_ref, self.v_ref, self.o_ref, self.st_ref = q_ref, f_ref, v_ref, o_ref, st_ref
        self.c0, self.c1 = 0.5 * (1.0 + lb), 0.5 * (1.0 - lb)
        self.tri, self.mask, self.mid_row, self.last_row = tri, mask, mid_row, last_row


def _gla_group(chains):
    c = HG_CHUNK
    dk = HG_DK
    pre = []
    for d, r0 in chains:
        h = 0.5 * d.q_ref[pl.ds(r0, c), :].astype(F32)
        q = h * jnp.tanh(h) + h
        ct = d.c1 * jnp.tanh(0.5 * d.f_ref[pl.ds(r0, c), :].astype(F32))
        logf = jnp.log2(d.c0 + ct)
        hi = logf.astype(BF16)
        lo = (logf - hi.astype(F32)).astype(BF16)
        pre.append((q, (1.0 - d.c0) - ct, jnp.concatenate([hi, lo], axis=1)))
    cums = []
    for (d, _), (_, _, hl) in zip(chains, pre):
        r = jnp.dot(d.tri, hl, preferred_element_type=F32)
        cums.append(r[:, :dk] + r[:, dk:])
    mids = []
    for (d, r0), (q, k, _), cum in zip(chains, pre, cums):
        mid = cum[d.mid_row:d.mid_row + 1, :]
        last = cum[d.last_row:d.last_row + 1, :]
        a = cum - mid
        qa = q * jnp.exp2(a)
        ka = k * jnp.exp2(-a)
        qd = (qa * jnp.exp2(mid)).astype(BF16)
        kend = (ka * jnp.exp2(last - mid)).astype(BF16)
        dec = jnp.exp2(last)
        scores = lax.dot_general(qa.astype(BF16), ka.astype(BF16), NT_DIMS,
                                 preferred_element_type=F32)
        mids.append((qd, kend, dec, scores))
    outs = []
    for (d, r0), (qd, kend, dec, scores) in zip(chains, mids):
        v = d.v_ref[pl.ds(r0, c), :]
        sm = jnp.where(d.mask, scores, 0.0).astype(BF16)
        o_intra = jnp.dot(sm, v, preferred_element_type=F32)
        upd = jnp.dot(jnp.transpose(v.astype(F32)).astype(BF16), kend, preferred_element_type=F32)
        outs.append((o_intra, upd))
    for (d, r0), (qd, _, dec, _), (o_intra, upd) in zip(chains, mids, outs):
        st = d.st_ref[...]
        d.o_ref[pl.ds(r0, c), :] = (o_intra + lax.dot_general(
            qd, st.astype(BF16), NT_DIMS, preferred_element_type=F32)).astype(d.o_ref.dtype)
        d.st_ref[...] = st * dec + upd


def _hgrn2_body(qf_ref, ff_ref, vf_ref, qb_ref, fb_ref, vb_ref, lbf_ref, lbb_ref,
                of_ref, ob_ref, sf_ref, sb_ref):
    @pl.when(pl.program_id(2) == 0)
    def _():
        sf_ref[...] = jnp.zeros_like(sf_ref)
        sb_ref[...] = jnp.zeros_like(sb_ref)

    c = HG_CHUNK
    g = HG_GROUP
    n_chunks = qf_ref.shape[0] // c
    row = lax.broadcasted_iota(jnp.int32, (c, c), 0)
    col = lax.broadcasted_iota(jnp.int32, (c, c), 1)
    lower = col <= row
    upper = col >= row
    fwd = _Dir(qf_ref, ff_ref, vf_ref, of_ref, sf_ref, lbf_ref[...],
               jnp.where(lower, 1.0, 0.0).astype(BF16), lower, c // 2 - 1, c - 1)
    bwd = _Dir(qb_ref, fb_ref, vb_ref, ob_ref, sb_ref, lbb_ref[...],
               jnp.where(upper, 1.0, 0.0).astype(BF16), upper, c // 2, 0)

    def step(n, carry):
        chains = []
        for u in range(g):
            chains.append((fwd, pl.multiple_of((n * g + u) * c, c)))
        for u in range(g):
            chains.append((bwd, pl.multiple_of((n_chunks - 1 - n * g - u) * c, c)))
        _gla_group(chains)
        return carry

    lax.fori_loop(0, n_chunks // g, step, 0)


def _hgrn2(proj, lb_fwd, lb_bwd, bsz, seqlen, rows=1024):
    n_tok = bsz * seqlen
    rows = min(rows, seqlen)
    nb = seqlen // rows
    dk = HG_DK

    def fwd_map(col0):
        return lambda b, h, j: (b * nb + j, col0 // dk + h)

    def bwd_map(col0):
        return lambda b, h, j: (b * nb + (nb - 1 - j), col0 // dk + h)

    blk = (rows, dk)
    lb_spec = pl.BlockSpec((1, dk), lambda b, h, j: (0, h))
    return pl.pallas_call(
        _hgrn2_body,
        grid=(bsz, HG_HEADS, nb),
        in_specs=[
            pl.BlockSpec(blk, fwd_map(COL_HQ)), pl.BlockSpec(blk, fwd_map(COL_FF)),
            pl.BlockSpec(blk, fwd_map(COL_HI)),
            pl.BlockSpec(blk, bwd_map(COL_HQ)), pl.BlockSpec(blk, bwd_map(COL_FB)),
            pl.BlockSpec(blk, bwd_map(COL_HI)),
            lb_spec, lb_spec,
        ],
        out_specs=[pl.BlockSpec(blk, fwd_map(0)), pl.BlockSpec(blk, bwd_map(0))],
        out_shape=[jax.ShapeDtypeStruct((n_tok, HG_HEADS * dk), BF16)] * 2,
        scratch_shapes=[pltpu.VMEM((dk, dk), F32), pltpu.VMEM((dk, dk), F32)],
        compiler_params=pltpu.CompilerParams(
            dimension_semantics=("parallel", "parallel", "arbitrary"),
            vmem_limit_bytes=VMEM_LIMIT),
        name="hgrn2",
    )(proj, proj, proj, proj, proj, proj, lb_fwd.reshape(1, -1), lb_bwd.reshape(1, -1))


def _attn_body(seqlen, sink_ref, q_ref, kp_ref, ko_ref, kn_ref, vp_ref, vo_ref, vn_ref, o_ref):
    j = pl.program_id(1)
    blk = AT_BLOCK
    span = 3 * blk
    row = lax.broadcasted_iota(jnp.int32, (blk, span), 0)
    col = lax.broadcasted_iota(jnp.int32, (blk, span), 1)
    absrel = jnp.abs(row + blk - col)
    key_pos = j * blk - blk + col
    valid = (absrel <= WINDOW) & (key_pos >= 0) & (key_pos < seqlen)
    base = jnp.where(valid, -absrel.astype(F32), -jnp.inf)
    dh = AT_HEAD_DIM
    lanes = 2 * dh
    lo_half = lax.broadcasted_iota(jnp.int32, (blk, lanes), 1) < dh

    def padded(ref, g):
        t = ref[:, (g // 2) * lanes:(g // 2 + 1) * lanes].astype(F32)
        if g % 2 == 0:
            lo = jnp.where(lo_half, t, 0.0)
            hi = pltpu.roll(lo, dh, axis=1)
        else:
            hi = jnp.where(lo_half, 0.0, t)
            lo = pltpu.roll(hi, dh, axis=1)
        return lo, hi

    def banded(refs, g):
        pads = [padded(r, g) for r in refs]
        return jnp.concatenate([p[0] for p in pads] + [p[1] for p in pads], axis=0).astype(BF16)

    scores, values = [], []
    for g in range(AT_KV_HEADS):
        q2 = q_ref[:, g * 2 * lanes:(g + 1) * 2 * lanes] * (dh ** -0.5)
        lhs = jnp.concatenate([q2[:, :lanes], q2[:, lanes:]], axis=0).astype(BF16)
        scores.append(lax.dot_general(lhs, banded((kp_ref, ko_ref, kn_ref), g), NT_DIMS,
                                      preferred_element_type=F32))
        values.append(banded((vp_ref, vo_ref, vn_ref), g))
    probs, denoms = [], []
    for g in range(AT_KV_HEADS):
        p_rows, d_rows = [], []
        for r in range(2):
            p_cols, d_cols = [], []
            for c in range(2):
                h = g * AT_GROUP + 2 * r + c
                slope = 2.0 ** (-8.0 * (h + 1) / AT_Q_HEADS)
                sink = sink_ref[h]
                s = scores[g][r * blk:(r + 1) * blk, c * span:(c + 1) * span] + slope * base
                m = jnp.maximum(jnp.max(s, axis=-1, keepdims=True), sink)
                p = jnp.exp(s - m)
                d_cols.append(jnp.sum(p, axis=-1, keepdims=True) + jnp.exp(sink - m))
                p_cols.append(p.astype(BF16))
            p_rows.append(jnp.concatenate(p_cols, axis=1))
            d_rows.append(jnp.where(lo_half, d_cols[0], d_cols[1]))
        probs.append(jnp.concatenate(p_rows, axis=0))
        denoms.append(d_rows)
    outs = []
    for g in range(AT_KV_HEADS):
        o = jnp.dot(probs[g], values[g], preferred_element_type=F32)
        for r in range(2):
            outs.append(o[r * blk:(r + 1) * blk, :] / denoms[g][r])
    o_ref[...] = jnp.concatenate(outs, axis=-1).astype(o_ref.dtype)


def _attention(proj, sink, bsz, seqlen):
    n_tok = bsz * seqlen
    nb = seqlen // AT_BLOCK
    kcol = COL_AK // AT_KVWIDTH
    vcol = COL_AV // AT_KVWIDTH

    def prev(c):
        return lambda b, j: (b * nb + jnp.maximum(j - 1, 0), c)

    def own(c):
        return lambda b, j: (b * nb + j, c)

    def nxt(c):
        return lambda b, j: (b * nb + jnp.minimum(j + 1, nb - 1), c)

    kv_blk = (AT_BLOCK, AT_KVWIDTH)
    return pl.pallas_call(
        functools.partial(_attn_body, seqlen),
        grid=(bsz, nb),
        in_specs=[
            pl.BlockSpec(memory_space=pltpu.SMEM),
            pl.BlockSpec((AT_BLOCK, D_MODEL), own(COL_AQ // D_MODEL)),
            pl.BlockSpec(kv_blk, prev(kcol)), pl.BlockSpec(kv_blk, own(kcol)),
            pl.BlockSpec(kv_blk, nxt(kcol)),
            pl.BlockSpec(kv_blk, prev(vcol)), pl.BlockSpec(kv_blk, own(vcol)),
            pl.BlockSpec(kv_blk, nxt(vcol)),
        ],
        out_specs=pl.BlockSpec((AT_BLOCK, D_MODEL), lambda b, j: (b * nb + j, 0)),
        out_shape=jax.ShapeDtypeStruct((n_tok, D_MODEL), BF16),
        compiler_params=pltpu.CompilerParams(
            dimension_semantics=("parallel", "arbitrary"), vmem_limit_bytes=VMEM_LIMIT),
        name="window_attn",
    )(sink.astype(F32), proj, proj, proj, proj, proj, proj, proj)


def _merge_body(n_steps, of_ref, ob_ref, hg_ref, ga0_ref, ga1_ref, gb0_ref, gb1_ref, ya_ref, x_ref,
                gn_ref, wo_ref, g2_ref, wr_ref, br_ref,
                h_ref, gate_ref, dest_ref, cnt_ref, xb_hbm,
                carry_ref, ring, dest_vmem, dest_smem, cnt_vmem, cnt_smem, zero_ref,
                sem_rows, sem_dest, sem_misc):
    i = pl.program_id(0)
    slot = lax.rem(i, 2)
    tm = x_ref.shape[0]
    cap = xb_hbm.shape[0] // N_EXPERTS

    def rows_wait(sl):
        pltpu.make_async_copy(xb_hbm.at[pl.ds(0, tm * TOP_K)], xb_hbm.at[pl.ds(0, tm * TOP_K)],
                              sem_rows.at[sl]).wait()

    def dest_copy(sl):
        return pltpu.make_async_copy(dest_vmem.at[sl], dest_smem.at[sl], sem_dest.at[sl])

    def issue_rows(sl):
        dest_copy(sl).wait()

        def body(g, c):
            for s in range(SUBLANES):
                for k in range(TOP_K):
                    pltpu.make_async_copy(
                        ring.at[sl, g, :, s, :], xb_hbm.at[dest_smem[sl, k, g * SUBLANES + s]],
                        sem_rows.at[sl]).start(priority=k % 2)
            return c
        lax.fori_loop(0, tm // SUBLANES, body, 0, unroll=ISSUE_UNROLL)

    @pl.when(i == 0)
    def _():
        carry_ref[...] = jnp.zeros_like(carry_ref)

    @pl.when(i >= 2)
    def _():
        rows_wait(slot)

    sub = MERGE_SUB
    subs = [slice(s0, s0 + sub) for s0 in range(0, x_ref.shape[0], sub)]
    gn = gn_ref[...]

    merged = []
    for rs in subs:
        o = of_ref[rs, :].astype(F32) + ob_ref[rs, :].astype(F32)
        parts = []
        for h in range(HG_HEADS):
            oh = o[:, h * HG_DK:(h + 1) * HG_DK]
            ms = jnp.mean(oh * oh, axis=-1, keepdims=True)
            parts.append(oh * lax.rsqrt(ms + NORM_EPS) * gn)
        hg = hg_ref[rs, :].astype(F32)
        y_rec = jnp.concatenate(parts, axis=-1) * (hg * _sigmoid(hg))
        gate_a = jnp.concatenate([ga0_ref[rs, :], ga1_ref[rs, :]], axis=-1).astype(F32)
        gate_b = jnp.concatenate([gb0_ref[rs, :], gb1_ref[rs, :]], axis=-1).astype(F32)
        merged.append((_sigmoid(gate_a) * y_rec
                       + _sigmoid(gate_b) * ya_ref[rs, :].astype(F32)).astype(BF16))

    xns = []
    for rs, mg in zip(subs, merged):
        h_res = x_ref[rs, :] + jnp.dot(mg, wo_ref[...], preferred_element_type=F32)
        h_ref[rs, :] = h_res
        ms = jnp.mean(h_res * h_res, axis=-1, keepdims=True)
        xn = h_res * lax.rsqrt(ms + NORM_EPS) * g2_ref[...]
        _store_tiles(ring.at[slot, rs.start // SUBLANES:rs.stop // SUBLANES], _pack_bf16_pairs(xn))
        xns.append(xn)

    logits = [lax.dot_general(wr_ref[...], xn, NT_DIMS, preferred_element_type=F32,
                              precision=lax.Precision.HIGHEST) + br_ref[...] for xn in xns]
    e_iota = lax.broadcasted_iota(jnp.int32, (N_EXPERTS, sub), 0).astype(F32)
    picked = []
    for rs, vals in zip(subs, logits):
        tops, idxs = [], []
        for _ in range(TOP_K):
            m = jnp.max(vals, axis=0, keepdims=True)
            sel = jnp.min(jnp.where(vals == m, e_iota, float(N_EXPERTS)), axis=0, keepdims=True)
            tops.append(m)
            idxs.append(sel)
            vals = jnp.where(e_iota == sel, -jnp.inf, vals)
        exps = [jnp.exp(t - tops[0]) for t in tops]
        den = exps[0] + exps[1] + exps[2] + exps[3]
        gate_ref[:, rs] = jnp.concatenate([e / den for e in exps], axis=0)
        picked.append((idxs, jnp.where(vals == -jnp.inf, 1.0, 0.0)))

    s_iota = lax.broadcasted_iota(jnp.int32, (sub, sub), 0)
    t_iota = lax.broadcasted_iota(jnp.int32, (sub, sub), 1)
    strict = jnp.where(s_iota < t_iota, 1.0, 0.0).astype(BF16)
    prefixes = [jnp.dot(chosen.astype(BF16), strict, preferred_element_type=F32)
                for _, chosen in picked]
    carry = carry_ref[...]
    for rs, (idxs, chosen), prefix in zip(subs, picked, prefixes):
        prefix = prefix + carry
        dests = [sel * float(cap) + jnp.sum(jnp.where(e_iota == sel, prefix, 0.0), axis=0, keepdims=True)
                 for sel in idxs]
        dest = jnp.concatenate(dests, axis=0).astype(jnp.int32)
        dest_ref[:, rs] = dest
        dest_vmem[slot, :, rs] = dest
        carry = carry + jnp.sum(chosen, axis=1, keepdims=True)
    carry_ref[...] = carry
    counts = jnp.broadcast_to(carry, cnt_ref.shape).astype(jnp.int32)
    cnt_ref[...] = counts

    dest_copy(slot).start()

    @pl.when(i >= 1)
    def _():
        issue_rows(1 - slot)

    @pl.when(i == n_steps - 1)
    def _():
        issue_rows(slot)
        if n_steps >= 2:
            rows_wait(1 - slot)
        rows_wait(slot)
        cnt_vmem[...] = counts
        cp = pltpu.make_async_copy(cnt_vmem, cnt_smem, sem_misc.at[0])
        cp.start()
        cp.wait()
        zero_ref[...] = jnp.zeros_like(zero_ref)

        def tail(e):
            return pltpu.make_async_copy(
                zero_ref, xb_hbm.at[pl.ds(e * cap + cnt_smem[e, 0], MOE_PAD)], sem_misc.at[0])
        for e in range(N_EXPERTS):
            tail(e).start()
        for e in range(N_EXPERTS):
            tail(e).wait()


def _merge_route(o_f, o_b, proj, y_att, x2d, gn_gain, w_out_bf16, g2, w_router_t, b_router, tm=512):
    n_tok = x2d.shape[0]
    nblk = n_tok // tm
    row_blk = (tm, D_MODEL)
    half = D_MODEL // 2
    half_blk = (tm, half)

    def rows(c):
        return lambda i: (i, c)

    full = lambda i: (0, 0)
    tok_out = pl.BlockSpec((TOP_K, tm), lambda i: (0, i))
    cap = n_tok + MOE_PAD
    return pl.pallas_call(
        functools.partial(_merge_body, nblk),
        grid=(nblk,),
        in_specs=[
            pl.BlockSpec(row_blk, rows(0)), pl.BlockSpec(row_blk, rows(0)),
            pl.BlockSpec(row_blk, rows(COL_HG // D_MODEL)),
            pl.BlockSpec(half_blk, rows(COL_GA // half)),
            pl.BlockSpec(half_blk, rows(COL_GA // half + 1)),
            pl.BlockSpec(half_blk, rows(COL_GB // half)),
            pl.BlockSpec(half_blk, rows(COL_GB // half + 1)),
            pl.BlockSpec(row_blk, rows(0)), pl.BlockSpec(row_blk, rows(0)),
            pl.BlockSpec((1, HG_DK), full),
            pl.BlockSpec((D_MODEL, D_MODEL), full),
            pl.BlockSpec((1, D_MODEL), full),
            pl.BlockSpec((N_EXPERTS, D_MODEL), full),
            pl.BlockSpec((N_EXPERTS, 1), full),
        ],
        out_specs=[
            pl.BlockSpec(row_blk, rows(0)),
            tok_out, tok_out,
            pl.BlockSpec((N_EXPERTS, LANES), full),
            pl.BlockSpec(memory_space=pl.ANY),
        ],
        out_shape=[
            jax.ShapeDtypeStruct((n_tok, D_MODEL), F32),
            jax.ShapeDtypeStruct((TOP_K, n_tok), F32),
            jax.ShapeDtypeStruct((TOP_K, n_tok), jnp.int32),
            jax.ShapeDtypeStruct((N_EXPERTS, LANES), jnp.int32),
            jax.ShapeDtypeStruct((N_EXPERTS * cap, PACKED_TILES, LANES), jnp.uint32),
        ],
        scratch_shapes=[
            pltpu.VMEM((N_EXPERTS, 1), F32),
            pltpu.VMEM((2, tm // SUBLANES) + PACKED_VIEW, jnp.uint32),
            pltpu.VMEM((2, TOP_K, tm), jnp.int32), pltpu.SMEM((2, TOP_K, tm), jnp.int32),
            pltpu.VMEM((N_EXPERTS, LANES), jnp.int32), pltpu.SMEM((N_EXPERTS, LANES), jnp.int32),
            pltpu.VMEM((MOE_PAD, PACKED_TILES, LANES), jnp.uint32),
            pltpu.SemaphoreType.DMA((2,)), pltpu.SemaphoreType.DMA((2,)), pltpu.SemaphoreType.DMA((1,)),
        ],
        compiler_params=pltpu.CompilerParams(
            dimension_semantics=("arbitrary",), vmem_limit_bytes=VMEM_LIMIT),
        name="merge_route",
    )(o_f, o_b, proj, proj, proj, proj, proj, y_att, x2d, gn_gain.reshape(1, HG_DK), w_out_bf16,
      g2.reshape(1, D_MODEL), w_router_t, b_router.reshape(N_EXPERTS, 1))


ISSUE_UNROLL = 2


def _expert_body(bend_ref, bgrp_ref, bshort_ref, x_hbm, wug_ref, bug_ref, wd_ref, bd_ref, y_hbm,
                 wug_bf, wd_bf, act_ref, xbuf, ybuf, sem_in, sem_out):
    e = pl.program_id(0)
    n_used = bend_ref[N_EXPERTS - 1]
    b_first = jnp.where(e == 0, 0, bend_ref[jnp.maximum(e - 1, 0)])
    b_last = bend_ref[e]

    def for_size(b, fn):
        short = bshort_ref[b] == 1

        @pl.when(short)
        def _():
            fn(MOE_PAD // SUBLANES)

        @pl.when(jnp.logical_not(short))
        def _():
            fn(MOE_ROWS // SUBLANES)

    def x_copies(b, slot, groups):
        return [pltpu.make_async_copy(x_hbm.at[pl.ds(bgrp_ref[b], groups), s],
                                      xbuf.at[slot, pl.ds(0, groups), :, s, :], sem_in.at[slot])
                for s in range(SUBLANES)]

    def y_copies(b, slot, groups):
        return [pltpu.make_async_copy(ybuf.at[slot, pl.ds(0, groups), :, s, :],
                                      y_hbm.at[pl.ds(bgrp_ref[b], groups), s], sem_out.at[slot])
                for s in range(SUBLANES)]

    def start_all(copies):
        for cp in copies:
            cp.start(priority=BLOCK_DMA_PRIORITY)

    def wait_all(copies):
        for cp in copies:
            cp.wait()

    @pl.when((e == 0) & (n_used > 0))
    def _():
        for_size(0, lambda groups: start_all(x_copies(0, 0, groups)))

    ns = D_FF // FF_SLAB
    for s in range(ns):
        lo, hi = s * FF_SLAB, (s + 1) * FF_SLAB
        wug_bf[:, 2 * lo:2 * lo + FF_SLAB] = wug_ref[0, :, lo:hi].astype(BF16)
        wug_bf[:, 2 * lo + FF_SLAB:2 * hi] = wug_ref[0, :, D_FF + lo:D_FF + hi].astype(BF16)
    wd_bf[...] = wd_ref[0].astype(BF16)
    b_ug = bug_ref[0]
    b_d = bd_ref[0]

    def mlp(slot, groups):
        rows = groups * SUBLANES
        x = _unpack_bf16_pairs(_load_tiles(xbuf.at[slot, pl.ds(0, groups)]))
        for s in range(ns):
            lo, hi = s * FF_SLAB, (s + 1) * FF_SLAB
            hu = jnp.dot(x, wug_bf[:, 2 * lo:2 * hi], preferred_element_type=F32)
            g = jnp.minimum(hu[:, :FF_SLAB] + b_ug[:, lo:hi], SWIGLU_LIMIT)
            u = jnp.clip(hu[:, FF_SLAB:] + b_ug[:, D_FF + lo:D_FF + hi], -SWIGLU_LIMIT, SWIGLU_LIMIT)
            act_ref[0:rows, lo:hi] = (g * _sigmoid(SWIGLU_ALPHA * g) * (u + 1.0)).astype(BF16)
        _store_tiles(ybuf.at[slot, pl.ds(0, groups)],
                     jnp.dot(act_ref[0:rows, :], wd_bf[...], preferred_element_type=F32) + b_d)

    def block(b, carry):
        slot = lax.rem(b, 2)
        for_size(b, lambda groups: wait_all(x_copies(b, slot, groups)))

        @pl.when(b + 1 < n_used)
        def _():
            for_size(b + 1, lambda groups: start_all(x_copies(b + 1, 1 - slot, groups)))

        @pl.when(b >= 2)
        def _():
            for_size(b - 2, lambda groups: wait_all(y_copies(b - 2, slot, groups)))

        def run(groups):
            mlp(slot, groups)
            start_all(y_copies(b, slot, groups))
        for_size(b, run)
        return carry

    lax.fori_loop(b_first, b_last, block, 0)

    @pl.when(e == N_EXPERTS - 1)
    def _():
        for back in (2, 1):
            @pl.when(n_used >= back)
            def _():
                last = n_used - back
                for_size(last, lambda groups: wait_all(y_copies(last, lax.rem(last, 2), groups)))


def _experts(xb, block_end, block_group, block_short, w_up_gate, b_up_gate, w_down, b_down):
    groups = MOE_ROWS // SUBLANES

    def w_map(e, bend, bgrp, bshort):
        return (e, 0, 0)

    grid_spec = pltpu.PrefetchScalarGridSpec(
        num_scalar_prefetch=3,
        grid=(N_EXPERTS,),
        in_specs=[
            pl.BlockSpec(memory_space=pl.ANY),
            pl.BlockSpec((1, D_MODEL, 2 * D_FF), w_map),
            pl.BlockSpec((1, 1, 2 * D_FF), w_map),
            pl.BlockSpec((1, D_FF, D_MODEL), w_map),
            pl.BlockSpec((1, 1, D_MODEL), w_map),
        ],
        out_specs=pl.BlockSpec(memory_space=pl.ANY),
        scratch_shapes=[
            pltpu.VMEM((D_MODEL, 2 * D_FF), BF16), pltpu.VMEM((D_FF, D_MODEL), BF16),
            pltpu.VMEM((MOE_ROWS, D_FF), BF16),
            pltpu.VMEM((2, groups) + PACKED_VIEW, jnp.uint32), pltpu.VMEM((2, groups) + TILE_VIEW, F32),
            pltpu.SemaphoreType.DMA((2,)), pltpu.SemaphoreType.DMA((2,)),
        ],
    )
    return pl.pallas_call(
        _expert_body,
        grid_spec=grid_spec,
        out_shape=jax.ShapeDtypeStruct((xb.shape[0],) + TOKEN_TILES, F32),
        compiler_params=pltpu.CompilerParams(
            dimension_semantics=("arbitrary",), vmem_limit_bytes=VMEM_LIMIT),
        name="moe_experts",
    )(block_end, block_group, block_short, xb, w_up_gate, b_up_gate.reshape(N_EXPERTS, 1, -1),
      w_down, b_down.reshape(N_EXPERTS, 1, -1))


COMBINE_TOKENS = 128


def _combine_body(dest_ref, yb_hbm, h_ref, gate_ref, gain_ref, o_ref, buf_ref, sem):
    i = pl.program_id(0)
    n_steps = pl.num_programs(0)
    nt = COMBINE_TOKENS

    def issue(step, slot):
        base = step * nt

        def body(g, c):
            for s in range(SUBLANES):
                for k in range(TOP_K):
                    pltpu.make_async_copy(
                        yb_hbm.at[dest_ref[(base + g * SUBLANES + s) * TOP_K + k]],
                        buf_ref.at[slot, k, g, :, s, :], sem.at[slot]).start(priority=k % 2)
            return c
        lax.fori_loop(0, nt // SUBLANES, body, 0, unroll=ISSUE_UNROLL)

    slot = lax.rem(i, 2)

    @pl.when(i == 0)
    def _():
        issue(0, 0)

    @pl.when(i + 1 < n_steps)
    def _():
        issue(i + 1, 1 - slot)

    pltpu.make_async_copy(buf_ref.at[slot], buf_ref.at[slot], sem.at[slot]).wait()

    gates = gate_ref[...]
    acc = h_ref[...]
    for k in range(TOP_K):
        acc = acc + gates[:, k:k + 1] * _load_tiles(buf_ref.at[slot, k])
    ms = jnp.mean(acc * acc, axis=-1, keepdims=True)
    o_ref[...] = acc * lax.rsqrt(ms + NORM_EPS) * gain_ref[...]


def _combine(yb, dest_flat, h_res, gates_tk, final_gain):
    n_tok = h_res.shape[0]
    nt = COMBINE_TOKENS
    grid_spec = pltpu.PrefetchScalarGridSpec(
        num_scalar_prefetch=1,
        grid=(n_tok // nt,),
        in_specs=[
            pl.BlockSpec(memory_space=pl.ANY),
            pl.BlockSpec((nt, D_MODEL), lambda i, d: (i, 0)),
            pl.BlockSpec((nt, TOP_K), lambda i, d: (i, 0)),
            pl.BlockSpec((1, D_MODEL), lambda i, d: (0, 0)),
        ],
        out_specs=pl.BlockSpec((nt, D_MODEL), lambda i, d: (i, 0)),
        scratch_shapes=[pltpu.VMEM((2, TOP_K, nt // SUBLANES) + TILE_VIEW, F32),
                        pltpu.SemaphoreType.DMA((2,))],
    )
    return pl.pallas_call(
        _combine_body,
        grid_spec=grid_spec,
        out_shape=jax.ShapeDtypeStruct((n_tok, D_MODEL), F32),
        compiler_params=pltpu.CompilerParams(
            dimension_semantics=("arbitrary",), vmem_limit_bytes=VMEM_LIMIT),
        name="moe_combine",
    )(dest_flat, yb, h_res, gates_tk, final_gain.reshape(1, D_MODEL))


def kernel(x, norm_mix_gain, w_in, hg_lb_fwd, hg_lb_bwd, hg_norm_gain, attn_sink, w_out,
           norm_ffn_gain, w_router, b_router, w_up_gate, b_up_gate, w_down, b_down,
           final_norm_gain):
    bsz, seqlen, d = x.shape
    n_tok = bsz * seqlen
    x2d = x.reshape(n_tok, d)

    lb_fwd = jax.nn.softmax(hg_lb_fwd.astype(F32), axis=0)[0]
    lb_bwd = jax.nn.softmax(hg_lb_bwd.astype(F32), axis=0)[0]

    proj = _inproj(x2d, norm_mix_gain[0], w_in[0].astype(BF16))
    o_f, o_b = _hgrn2(proj, lb_fwd, lb_bwd, bsz, seqlen)
    y_att = _attention(proj, attn_sink[0], bsz, seqlen)
    h_res, gate_t, dest_t, counts, xb = _merge_route(
        o_f, o_b, proj, y_att, x2d, hg_norm_gain[0], w_out[0].astype(BF16),
        norm_ffn_gain[0], w_router[0].T.astype(F32), b_router[0].astype(F32))

    n_rows = xb.shape[0]
    cap = n_rows // N_EXPERTS
    counts = counts[:, 0]
    padded = (counts + MOE_PAD - 1) // MOE_PAD * MOE_PAD
    full = padded // MOE_ROWS
    blocks = full + (padded % MOE_ROWS) // MOE_PAD
    block_end = jnp.cumsum(blocks).astype(jnp.int32)
    n_blocks_max = n_tok * TOP_K // MOE_PAD + N_EXPERTS
    b = jnp.arange(n_blocks_max, dtype=jnp.int32)
    owner = (block_end[None, :] <= b[:, None]).astype(jnp.int32)
    expert_of = jnp.minimum(jnp.sum(owner, axis=-1), N_EXPERTS - 1)
    within = b - jnp.sum(owner * blocks[None, :], axis=-1)
    is_owner = expert_of[:, None] == jnp.arange(N_EXPERTS, dtype=jnp.int32)[None, :]
    block_short = (within == jnp.sum(jnp.where(is_owner, full[None, :], 0), axis=-1)).astype(jnp.int32)
    block_group = ((expert_of * cap + within * MOE_ROWS) // SUBLANES).astype(jnp.int32)
    dest_flat = dest_t.T.reshape(-1)

    yb = _experts(xb.reshape((n_rows // SUBLANES,) + PACKED_TOKEN_TILES), block_end, block_group, block_short,
                  w_up_gate[0], b_up_gate[0], w_down[0], b_down[0])
    out = _combine(yb.reshape(n_rows, LANE_TILES, LANES), dest_flat, h_res, gate_t.T, final_norm_gain)
    return out.reshape(bsz, seqlen, d)
```

```python
import functools

import jax
import jax.numpy as jnp
import numpy as np
from jax import lax
from jax.experimental import pallas as pl
from jax.experimental.pallas import tpu as pltpu

F32 = jnp.float32
BF16 = jnp.bfloat16

D_MODEL = 1024
HG_HEADS = 8
HG_DK = 128
HG_CHUNK = 128
HG_GROUP = 8
AT_Q_HEADS = 16
AT_KV_HEADS = 4
AT_GROUP = AT_Q_HEADS // AT_KV_HEADS
AT_HEAD_DIM = 64
WINDOW = 128
AT_BLOCK = WINDOW
AT_KVWIDTH = AT_KV_HEADS * AT_HEAD_DIM
N_EXPERTS = 32
TOP_K = 4
D_FF = 1024
SWIGLU_LIMIT = 7.0
SWIGLU_ALPHA = 1.702
NORM_EPS = 1e-5
IN_WIDTH = 8704

COL_HQ, COL_FF, COL_FB, COL_HI, COL_HG = 0, 1024, 2048, 3072, 4096
COL_AQ, COL_AK, COL_AV, COL_GA, COL_GB = 5120, 6144, 6400, 6656, 7680

MOE_ROWS = 1024
MOE_PAD = 512
FF_SLAB = 256
MERGE_SUB = 256
BLOCK_DMA_PRIORITY = 1
VMEM_LIMIT = 56 * 1024 * 1024

NT_DIMS = (((1,), (1,)), ((), ()))

SUBLANES = 8
LANES = 128
LANE_TILES = D_MODEL // LANES
TILE_VIEW = (LANE_TILES, SUBLANES, LANES)
TOKEN_TILES = (SUBLANES, LANE_TILES, LANES)


def _store_tiles(ref, val):
    groups = val.shape[0] // SUBLANES
    for t in range(val.shape[1] // LANES):
        ref[:, t] = val[:, t * LANES:(t + 1) * LANES].reshape(groups, SUBLANES, LANES)


def _load_tiles(ref):
    groups = ref.shape[0]
    return jnp.concatenate(
        [ref[:, t].reshape(groups * SUBLANES, LANES) for t in range(ref.shape[1])], axis=1)


PACKED_TILES = LANE_TILES // 2
PACKED_VIEW = (PACKED_TILES, SUBLANES, LANES)
PACKED_TOKEN_TILES = (SUBLANES, PACKED_TILES, LANES)


def _pack_bf16_pairs(x):
    half = x.shape[1] // 2
    lo = pltpu.bitcast(x[:, :half].astype(BF16).astype(F32), jnp.uint32)
    hi = pltpu.bitcast(x[:, half:].astype(BF16).astype(F32), jnp.uint32)
    return hi | lax.shift_right_logical(lo, jnp.uint32(16))


def _unpack_bf16_pairs(w):
    lo = pltpu.bitcast(lax.shift_left(w, jnp.uint32(16)), F32)
    hi = pltpu.bitcast(w & jnp.uint32(0xFFFF0000), F32)
    return jnp.concatenate([lo, hi], axis=1).astype(BF16)


def _sigmoid(x):
    return 0.5 * jnp.tanh(0.5 * x) + 0.5


def _inproj_body(x_ref, g_ref, w_ref, o_ref, xn_ref):
    @pl.when(pl.program_id(1) == 0)
    def _():
        x = x_ref[...]
        ms = jnp.mean(x * x, axis=-1, keepdims=True)
        xn_ref[...] = (x * lax.rsqrt(ms + NORM_EPS) * g_ref[...]).astype(BF16)

    o_ref[...] = jnp.dot(xn_ref[...], w_ref[...].astype(BF16),
                         preferred_element_type=F32).astype(o_ref.dtype)


def _inproj(x2d, gain, w, tm=1024, tn=2176):
    n_tok = x2d.shape[0]
    n_out = w.shape[1]
    return pl.pallas_call(
        _inproj_body,
        grid=(n_tok // tm, n_out // tn),
        in_specs=[
            pl.BlockSpec((tm, D_MODEL), lambda i, j: (i, 0)),
            pl.BlockSpec((1, D_MODEL), lambda i, j: (0, 0)),
            pl.BlockSpec((D_MODEL, tn), lambda i, j: (0, j)),
        ],
        out_specs=pl.BlockSpec((tm, tn), lambda i, j: (i, j)),
        out_shape=jax.ShapeDtypeStruct((n_tok, n_out), BF16),
        scratch_shapes=[pltpu.VMEM((tm, D_MODEL), BF16)],
        compiler_params=pltpu.CompilerParams(
            dimension_semantics=("parallel", "arbitrary"), vmem_limit_bytes=VMEM_LIMIT),
        name="inproj",
    )(x2d, gain.reshape(1, D_MODEL), w)


class _Dir:
    def __init__(self, q_ref, f_ref, v_ref, o_ref, st_ref, lb, tri, mask, mid_row, last_row):
        self.q_ref, self.f_ref, self.v_ref, self.o_ref, self.st_ref = q_ref, f_ref, v_ref, o_ref, st_ref
        self.c0, self.c1 = 0.5 * (1.0 + lb), 0.5 * (1.0 - lb)
        self.tri, self.mask, self.mid_row, self.last_row = tri, mask, mid_row, last_row


def _gla_group(chains):
    c = HG_CHUNK
    dk = HG_DK
    pre = []
    for d, r0 in chains:
        h = 0.5 * d.q_ref[pl.ds(r0, c), :].astype(F32)
        q = h * jnp.tanh(h) + h
        ct = d.c1 * jnp.tanh(0.5 * d.f_ref[pl.ds(r0, c), :].astype(F32))
        logf = jnp.log2(d.c0 + ct)
        hi = logf.astype(BF16)
        lo = (logf - hi.astype(F32)).astype(BF16)
        pre.append((q, (1.0 - d.c0) - ct, jnp.concatenate([hi, lo], axis=1)))
    cums = []
    for (d, _), (_, _, hl) in zip(chains, pre):
        r = jnp.dot(d.tri, hl, preferred_element_type=F32)
        cums.append(r[:, :dk] + r[:, dk:])
    mids = []
    for (d, r0), (q, k, _), cum in zip(chains, pre, cums):
        mid = cum[d.mid_row:d.mid_row + 1, :]
        last = cum[d.last_row:d.last_row + 1, :]
        a = cum - mid
        qa = q * jnp.exp2(a)
        ka = k * jnp.exp2(-a)
        qd = (qa * jnp.exp2(mid)).astype(BF16)
        kend = (ka * jnp.exp2(last - mid)).astype(BF16)
        dec = jnp.exp2(last)
        scores = lax.dot_general(qa.astype(BF16), ka.astype(BF16), NT_DIMS,
                                 preferred_element_type=F32)
        mids.append((qd, kend, dec, scores))
    outs = []
    for (d, r0), (qd, kend, dec, scores) in zip(chains, mids):
        v = d.v_ref[pl.ds(r0, c), :]
        sm = jnp.where(d.mask, scores, 0.0).astype(BF16)
        o_intra = jnp.dot(sm, v, preferred_element_type=F32)
        upd = jnp.dot(jnp.transpose(v.astype(F32)).astype(BF16), kend, preferred_element_type=F32)
        outs.append((o_intra, upd))
    for (d, r0), (qd, _, dec, _), (o_intra, upd) in zip(chains, mids, outs):
        st = d.st_ref[...]
        d.o_ref[pl.ds(r0, c), :] = (o_intra + lax.dot_general(
            qd, st.astype(BF16), NT_DIMS, preferred_element_type=F32)).astype(d.o_ref.dtype)
        d.st_ref[...] = st * dec + upd


def _hgrn2_body(qf_ref, ff_ref, vf_ref, qb_ref, fb_ref, vb_ref, lbf_ref, lbb_ref,
                of_ref, ob_ref, sf_ref, sb_ref):
    @pl.when(pl.program_id(2) == 0)
    def _():
        sf_ref[...] = jnp.zeros_like(sf_ref)
        sb_ref[...] = jnp.zeros_like(sb_ref)

    c = HG_CHUNK
    g = HG_GROUP
    n_chunks = qf_ref.shape[0] // c
    row = lax.broadcasted_iota(jnp.int32, (c, c), 0)
    col = lax.broadcasted_iota(jnp.int32, (c, c), 1)
    lower = col <= row
    upper = col >= row
    fwd = _Dir(qf_ref, ff_ref, vf_ref, of_ref, sf_ref, lbf_ref[...],
               jnp.where(lower, 1.0, 0.0).astype(BF16), lower, c // 2 - 1, c - 1)
    bwd = _Dir(qb_ref, fb_ref, vb_ref, ob_ref, sb_ref, lbb_ref[...],
               jnp.where(upper, 1.0, 0.0).astype(BF16), upper, c // 2, 0)

    def step(n, carry):
        chains = []
        for u in range(g):
            chains.append((fwd, pl.multiple_of((n * g + u) * c, c)))
        for u in range(g):
            chains.append((bwd, pl.multiple_of((n_chunks - 1 - n * g - u) * c, c)))
        _gla_group(chains)
        return carry

    lax.fori_loop(0, n_chunks // g, step, 0)


def _hgrn2(proj, lb_fwd, lb_bwd, bsz, seqlen, rows=1024):
    n_tok = bsz * seqlen
    rows = min(rows, seqlen)
    nb = seqlen // rows
    dk = HG_DK

    def fwd_map(col0):
        return lambda b, h, j: (b * nb + j, col0 // dk + h)

    def bwd_map(col0):
        return lambda b, h, j: (b * nb + (nb - 1 - j), col0 // dk + h)

    blk = (rows, dk)
    lb_spec = pl.BlockSpec((1, dk), lambda b, h, j: (0, h))
    return pl.pallas_call(
        _hgrn2_body,
        grid=(bsz, HG_HEADS, nb),
        in_specs=[
            pl.BlockSpec(blk, fwd_map(COL_HQ)), pl.BlockSpec(blk, fwd_map(COL_FF)),
            pl.BlockSpec(blk, fwd_map(COL_HI)),
            pl.BlockSpec(blk, bwd_map(COL_HQ)), pl.BlockSpec(blk, bwd_map(COL_FB)),
            pl.BlockSpec(blk, bwd_map(COL_HI)),
            lb_spec, lb_spec,
        ],
        out_specs=[pl.BlockSpec(blk, fwd_map(0)), pl.BlockSpec(blk, bwd_map(0))],
        out_shape=[jax.ShapeDtypeStruct((n_tok, HG_HEADS * dk), BF16)] * 2,
        scratch_shapes=[pltpu.VMEM((dk, dk), F32), pltpu.VMEM((dk, dk), F32)],
        compiler_params=pltpu.CompilerParams(
            dimension_semantics=("parallel", "parallel", "arbitrary"),
            vmem_limit_bytes=VMEM_LIMIT),
        name="hgrn2",
    )(proj, proj, proj, proj, proj, proj, lb_fwd.reshape(1, -1), lb_bwd.reshape(1, -1))


def _alibi_tables():
    blk, span = AT_BLOCK, 3 * AT_BLOCK
    row = np.arange(blk)[:, None]
    col = np.arange(span)[None, :]
    absrel = np.abs(row + blk - col)
    slopes = 2.0 ** (-8.0 * np.arange(1, AT_Q_HEADS + 1) / AT_Q_HEADS)
    tables = []
    for sel in range(4):
        valid = absrel <= WINDOW
        if sel & 1:
            valid = valid & (col >= blk)
        if sel & 2:
            valid = valid & (col < 2 * blk)
        base = np.where(valid, -absrel.astype(np.float64), -np.inf)
        tables.append(slopes[:, None, None] * base[None])
    return np.stack(tables).astype(np.float32)


def _attn_body(sink_ref, bias_ref, q_ref, kp_ref, ko_ref, kn_ref, vp_ref, vo_ref, vn_ref, o_ref):
    blk = AT_BLOCK
    span = 3 * blk
    dh = AT_HEAD_DIM
    lanes = 2 * dh
    lo_half = lax.broadcasted_iota(jnp.int32, (blk, lanes), 1) < dh

    def padded(ref, g):
        t = ref[:, (g // 2) * lanes:(g // 2 + 1) * lanes].astype(F32)
        if g % 2 == 0:
            lo = jnp.where(lo_half, t, 0.0)
            hi = pltpu.roll(lo, dh, axis=1)
        else:
            hi = jnp.where(lo_half, 0.0, t)
            lo = pltpu.roll(hi, dh, axis=1)
        return lo, hi

    def banded(refs, g):
        pads = [padded(r, g) for r in refs]
        return jnp.concatenate([p[0] for p in pads] + [p[1] for p in pads], axis=0).astype(BF16)

    scores, values = [], []
    for g in range(AT_KV_HEADS):
        q2 = q_ref[:, g * 2 * lanes:(g + 1) * 2 * lanes] * (dh ** -0.5)
        lhs = jnp.concatenate([q2[:, :lanes], q2[:, lanes:]], axis=0).astype(BF16)
        scores.append(lax.dot_general(lhs, banded((kp_ref, ko_ref, kn_ref), g), NT_DIMS,
                                      preferred_element_type=F32))
        values.append(banded((vp_ref, vo_ref, vn_ref), g))
    probs, denoms = [], []
    for g in range(AT_KV_HEADS):
        p_rows, d_rows = [], []
        for r in range(2):
            p_cols, d_cols = [], []
            for c in range(2):
                h = g * AT_GROUP + 2 * r + c
                sink = sink_ref[h]
                s = scores[g][r * blk:(r + 1) * blk, c * span:(c + 1) * span] + bias_ref[0, h]
                m = jnp.maximum(jnp.max(s, axis=-1, keepdims=True), sink)
                p = jnp.exp(s - m)
                d_cols.append(jnp.sum(p, axis=-1, keepdims=True) + jnp.exp(sink - m))
                p_cols.append(p.astype(BF16))
            p_rows.append(jnp.concatenate(p_cols, axis=1))
            d_rows.append(jnp.where(lo_half, d_cols[0], d_cols[1]))
        probs.append(jnp.concatenate(p_rows, axis=0))
        denoms.append(d_rows)
    outs = []
    for g in range(AT_KV_HEADS):
        o = jnp.dot(probs[g], values[g], preferred_element_type=F32)
        for r in range(2):
            outs.append(o[r * blk:(r + 1) * blk, :] / denoms[g][r])
    o_ref[...] = jnp.concatenate(outs, axis=-1).astype(o_ref.dtype)


def _attention(proj, sink, bsz, seqlen):
    n_tok = bsz * seqlen
    nb = seqlen // AT_BLOCK
    kcol = COL_AK // AT_KVWIDTH
    vcol = COL_AV // AT_KVWIDTH

    def prev(c):
        return lambda b, j: (b * nb + jnp.maximum(j - 1, 0), c)

    def own(c):
        return lambda b, j: (b * nb + j, c)

    def nxt(c):
        return lambda b, j: (b * nb + jnp.minimum(j + 1, nb - 1), c)

    def bias_map(b, j):
        return (jnp.where(j == 0, 1, 0) + jnp.where(j == nb - 1, 2, 0), 0, 0, 0)

    kv_blk = (AT_BLOCK, AT_KVWIDTH)
    return pl.pallas_call(
        _attn_body,
        grid=(bsz, nb),
        in_specs=[
            pl.BlockSpec(memory_space=pltpu.SMEM),
            pl.BlockSpec((1, AT_Q_HEADS, AT_BLOCK, 3 * AT_BLOCK), bias_map),
            pl.BlockSpec((AT_BLOCK, D_MODEL), own(COL_AQ // D_MODEL)),
            pl.BlockSpec(kv_blk, prev(kcol)), pl.BlockSpec(kv_blk, own(kcol)),
            pl.BlockSpec(kv_blk, nxt(kcol)),
            pl.BlockSpec(kv_blk, prev(vcol)), pl.BlockSpec(kv_blk, own(vcol)),
            pl.BlockSpec(kv_blk, nxt(vcol)),
        ],
        out_specs=pl.BlockSpec((AT_BLOCK, D_MODEL), lambda b, j: (b * nb + j, 0)),
        out_shape=jax.ShapeDtypeStruct((n_tok, D_MODEL), BF16),
        compiler_params=pltpu.CompilerParams(
            dimension_semantics=("parallel", "arbitrary"), vmem_limit_bytes=VMEM_LIMIT),
        name="window_attn",
    )(sink.astype(F32), jnp.asarray(_alibi_tables()), proj, proj, proj, proj, proj, proj, proj)


def _merge_body(n_steps, of_ref, ob_ref, hg_ref, ga0_ref, ga1_ref, gb0_ref, gb1_ref, ya_ref, x_ref,
                gn_ref, wo_ref, g2_ref, wr_ref, br_ref,
                h_ref, gate_ref, dest_ref, cnt_ref, xb_hbm,
                carry_ref, ring, dest_vmem, dest_smem, cnt_vmem, cnt_smem, zero_ref,
                sem_rows, sem_dest, sem_misc):
    i = pl.program_id(0)
    slot = lax.rem(i, 2)
    tm = x_ref.shape[0]
    cap = xb_hbm.shape[0] // N_EXPERTS

    def rows_wait(sl):
        pltpu.make_async_copy(xb_hbm.at[pl.ds(0, tm * TOP_K)], xb_hbm.at[pl.ds(0, tm * TOP_K)],
                              sem_rows.at[sl]).wait()

    def dest_copy(sl):
        return pltpu.make_async_copy(dest_vmem.at[sl], dest_smem.at[sl], sem_dest.at[sl])

    def issue_rows(sl):
        dest_copy(sl).wait()

        def body(g, c):
            for s in range(SUBLANES):
                for k in range(TOP_K):
                    pltpu.make_async_copy(
                        ring.at[sl, g, :, s, :], xb_hbm.at[dest_smem[sl, k, g * SUBLANES + s]],
                        sem_rows.at[sl]).start(priority=k % 2)
            return c
        lax.fori_loop(0, tm // SUBLANES, body, 0, unroll=ISSUE_UNROLL)

    @pl.when(i == 0)
    def _():
        carry_ref[...] = jnp.zeros_like(carry_ref)

    @pl.when(i >= 2)
    def _():
        rows_wait(slot)

    sub = MERGE_SUB
    subs = [slice(s0, s0 + sub) for s0 in range(0, x_ref.shape[0], sub)]
    gn = gn_ref[...]

    merged = []
    for rs in subs:
        o = of_ref[rs, :].astype(F32) + ob_ref[rs, :].astype(F32)
        parts = []
        for h in range(HG_HEADS):
            oh = o[:, h * HG_DK:(h + 1) * HG_DK]
            ms = jnp.mean(oh * oh, axis=-1, keepdims=True)
            parts.append(oh * lax.rsqrt(ms + NORM_EPS) * gn)
        hg = hg_ref[rs, :].astype(F32)
        y_rec = jnp.concatenate(parts, axis=-1) * (hg * _sigmoid(hg))
        gate_a = jnp.concatenate([ga0_ref[rs, :], ga1_ref[rs, :]], axis=-1).astype(F32)
        gate_b = jnp.concatenate([gb0_ref[rs, :], gb1_ref[rs, :]], axis=-1).astype(F32)
        merged.append((_sigmoid(gate_a) * y_rec
                       + _sigmoid(gate_b) * ya_ref[rs, :].astype(F32)).astype(BF16))

    w_out = wo_ref[...].astype(BF16)
    xns = []
    for rs, mg in zip(subs, merged):
        h_res = x_ref[rs, :] + jnp.dot(mg, w_out, preferred_element_type=F32)
        h_ref[rs, :] = h_res
        ms = jnp.mean(h_res * h_res, axis=-1, keepdims=True)
        xn = h_res * lax.rsqrt(ms + NORM_EPS) * g2_ref[...]
        _store_tiles(ring.at[slot, rs.start // SUBLANES:rs.stop // SUBLANES], _pack_bf16_pairs(xn))
        xns.append(xn)

    logits = [lax.dot_general(wr_ref[...], xn, NT_DIMS, preferred_element_type=F32,
                              precision=lax.Precision.HIGHEST) + br_ref[...] for xn in xns]
    e_iota = lax.broadcasted_iota(jnp.int32, (N_EXPERTS, sub), 0).astype(F32)
    picked = []
    for rs, vals in zip(subs, logits):
        tops, idxs = [], []
        for _ in range(TOP_K):
            m = jnp.max(vals, axis=0, keepdims=True)
            sel = jnp.min(jnp.where(vals == m, e_iota, float(N_EXPERTS)), axis=0, keepdims=True)
            tops.append(m)
            idxs.append(sel)
            vals = jnp.where(e_iota == sel, -jnp.inf, vals)
        exps = [jnp.exp(t - tops[0]) for t in tops]
        den = exps[0] + exps[1] + exps[2] + exps[3]
        gate_ref[:, rs] = jnp.concatenate([e / den for e in exps], axis=0)
        picked.append((idxs, jnp.where(vals == -jnp.inf, 1.0, 0.0)))

    s_iota = lax.broadcasted_iota(jnp.int32, (sub, sub), 0)
    t_iota = lax.broadcasted_iota(jnp.int32, (sub, sub), 1)
    strict = jnp.where(s_iota < t_iota, 1.0, 0.0).astype(BF16)
    prefixes = [jnp.dot(chosen.astype(BF16), strict, preferred_element_type=F32)
                for _, chosen in picked]
    carry = carry_ref[...]
    for rs, (idxs, chosen), prefix in zip(subs, picked, prefixes):
        prefix = prefix + carry
        dests = [sel * float(cap) + jnp.sum(jnp.where(e_iota == sel, prefix, 0.0), axis=0, keepdims=True)
                 for sel in idxs]
        dest = jnp.concatenate(dests, axis=0).astype(jnp.int32)
        dest_ref[:, rs] = dest
        dest_vmem[slot, :, rs] = dest
        carry = carry + jnp.sum(chosen, axis=1, keepdims=True)
    carry_ref[...] = carry
    counts = jnp.broadcast_to(carry, cnt_ref.shape).astype(jnp.int32)
    cnt_ref[...] = counts

    dest_copy(slot).start()

    @pl.when(i >= 1)
    def _():
        issue_rows(1 - slot)

    @pl.when(i == n_steps - 1)
    def _():
        issue_rows(slot)
        if n_steps >= 2:
            rows_wait(1 - slot)
        rows_wait(slot)
        cnt_vmem[...] = counts
        cp = pltpu.make_async_copy(cnt_vmem, cnt_smem, sem_misc.at[0])
        cp.start()
        cp.wait()
        zero_ref[...] = jnp.zeros_like(zero_ref)

        def tail(e):
            return pltpu.make_async_copy(
                zero_ref, xb_hbm.at[pl.ds(e * cap + cnt_smem[e, 0], MOE_PAD)], sem_misc.at[0])
        for e in range(N_EXPERTS):
            tail(e).start()
        for e in range(N_EXPERTS):
            tail(e).wait()


def _merge_route(o_f, o_b, proj, y_att, x2d, gn_gain, w_out, g2, w_router_t, b_router, tm=512):
    n_tok = x2d.shape[0]
    nblk = n_tok // tm
    row_blk = (tm, D_MODEL)
    half = D_MODEL // 2
    half_blk = (tm, half)

    def rows(c):
        return lambda i: (i, c)

    full = lambda i: (0, 0)
    tok_out = pl.BlockSpec((TOP_K, tm), lambda i: (0, i))
    cap = n_tok + MOE_PAD
    return pl.pallas_call(
        functools.partial(_merge_body, nblk),
        grid=(nblk,),
        in_specs=[
            pl.BlockSpec(row_blk, rows(0)), pl.BlockSpec(row_blk, rows(0)),
            pl.BlockSpec(row_blk, rows(COL_HG // D_MODEL)),
            pl.BlockSpec(half_blk, rows(COL_GA // half)),
            pl.BlockSpec(half_blk, rows(COL_GA // half + 1)),
            pl.BlockSpec(half_blk, rows(COL_GB // half)),
            pl.BlockSpec(half_blk, rows(COL_GB // half + 1)),
            pl.BlockSpec(row_blk, rows(0)), pl.BlockSpec(row_blk, rows(0)),
            pl.BlockSpec((1, HG_DK), full),
            pl.BlockSpec((D_MODEL, D_MODEL), full),
            pl.BlockSpec((1, D_MODEL), full),
            pl.BlockSpec((N_EXPERTS, D_MODEL), full),
            pl.BlockSpec((N_EXPERTS, 1), full),
        ],
        out_specs=[
            pl.BlockSpec(row_blk, rows(0)),
            tok_out, tok_out,
            pl.BlockSpec((N_EXPERTS, LANES), full),
            pl.BlockSpec(memory_space=pl.ANY),
        ],
        out_shape=[
            jax.ShapeDtypeStruct((n_tok, D_MODEL), F32),
            jax.ShapeDtypeStruct((TOP_K, n_tok), F32),
            jax.ShapeDtypeStruct((TOP_K, n_tok), jnp.int32),
            jax.ShapeDtypeStruct((N_EXPERTS, LANES), jnp.int32),
            jax.ShapeDtypeStruct((N_EXPERTS * cap, PACKED_TILES, LANES), jnp.uint32),
        ],
        scratch_shapes=[
            pltpu.VMEM((N_EXPERTS, 1), F32),
            pltpu.VMEM((2, tm // SUBLANES) + PACKED_VIEW, jnp.uint32),
            pltpu.VMEM((2, TOP_K, tm), jnp.int32), pltpu.SMEM((2, TOP_K, tm), jnp.int32),
            pltpu.VMEM((N_EXPERTS, LANES), jnp.int32), pltpu.SMEM((N_EXPERTS, LANES), jnp.int32),
            pltpu.VMEM((MOE_PAD, PACKED_TILES, LANES), jnp.uint32),
            pltpu.SemaphoreType.DMA((2,)), pltpu.SemaphoreType.DMA((2,)), pltpu.SemaphoreType.DMA((1,)),
        ],
        compiler_params=pltpu.CompilerParams(
            dimension_semantics=("arbitrary",), vmem_limit_bytes=VMEM_LIMIT),
        name="merge_route",
    )(o_f, o_b, proj, proj, proj, proj, proj, y_att, x2d, gn_gain.reshape(1, HG_DK), w_out,
      g2.reshape(1, D_MODEL), w_router_t, b_router.reshape(N_EXPERTS, 1))


ISSUE_UNROLL = 2


def _expert_body(bend_ref, bgrp_ref, bshort_ref, x_hbm, wug_ref, bug_ref, wd_ref, bd_ref, y_hbm,
                 wug_bf, wd_bf, act_ref, xbuf, ybuf, sem_in, sem_out):
    e = pl.program_id(0)
    n_used = bend_ref[N_EXPERTS - 1]
    b_first = jnp.where(e == 0, 0, bend_ref[jnp.maximum(e - 1, 0)])
    b_last = bend_ref[e]

    def for_size(b, fn):
        short = bshort_ref[b] == 1

        @pl.when(short)
        def _():
            fn(MOE_PAD // SUBLANES)

        @pl.when(jnp.logical_not(short))
        def _():
            fn(MOE_ROWS // SUBLANES)

    def x_copies(b, slot, groups):
        return [pltpu.make_async_copy(x_hbm.at[pl.ds(bgrp_ref[b], groups), s],
                                      xbuf.at[slot, pl.ds(0, groups), :, s, :], sem_in.at[slot])
                for s in range(SUBLANES)]

    def y_copies(b, slot, groups):
        return [pltpu.make_async_copy(ybuf.at[slot, pl.ds(0, groups), :, s, :],
                                      y_hbm.at[pl.ds(bgrp_ref[b], groups), s], sem_out.at[slot])
                for s in range(SUBLANES)]

    def start_all(copies):
        for cp in copies:
            cp.start(priority=BLOCK_DMA_PRIORITY)

    def wait_all(copies):
        for cp in copies:
            cp.wait()

    @pl.when((e == 0) & (n_used > 0))
    def _():
        for_size(0, lambda groups: start_all(x_copies(0, 0, groups)))

    ns = D_FF // FF_SLAB
    for s in range(ns):
        lo, hi = s * FF_SLAB, (s + 1) * FF_SLAB
        wug_bf[:, 2 * lo:2 * lo + FF_SLAB] = wug_ref[0, :, lo:hi].astype(BF16)
        wug_bf[:, 2 * lo + FF_SLAB:2 * hi] = wug_ref[0, :, D_FF + lo:D_FF + hi].astype(BF16)
    wd_bf[...] = wd_ref[0].astype(BF16)
    b_ug = bug_ref[0]
    b_d = bd_ref[0]

    def mlp(slot, groups):
        rows = groups * SUBLANES
        x = _unpack_bf16_pairs(_load_tiles(xbuf.at[slot, pl.ds(0, groups)]))
        for s in range(ns):
            lo, hi = s * FF_SLAB, (s + 1) * FF_SLAB
            hu = jnp.dot(x, wug_bf[:, 2 * lo:2 * hi], preferred_element_type=F32)
            g = jnp.minimum(hu[:, :FF_SLAB] + b_ug[:, lo:hi], SWIGLU_LIMIT)
            u = jnp.clip(hu[:, FF_SLAB:] + b_ug[:, D_FF + lo:D_FF + hi], -SWIGLU_LIMIT, SWIGLU_LIMIT)
            act_ref[0:rows, lo:hi] = (g * _sigmoid(SWIGLU_ALPHA * g) * (u + 1.0)).astype(BF16)
        _store_tiles(ybuf.at[slot, pl.ds(0, groups)],
                     jnp.dot(act_ref[0:rows, :], wd_bf[...], preferred_element_type=F32) + b_d)

    def block(b, carry):
        slot = lax.rem(b, 2)
        for_size(b, lambda groups: wait_all(x_copies(b, slot, groups)))

        @pl.when(b + 1 < n_used)
        def _():
            for_size(b + 1, lambda groups: start_all(x_copies(b + 1, 1 - slot, groups)))

        @pl.when(b >= 2)
        def _():
            for_size(b - 2, lambda groups: wait_all(y_copies(b - 2, slot, groups)))

        def run(groups):
            mlp(slot, groups)
            start_all(y_copies(b, slot, groups))
        for_size(b, run)
        return carry

    lax.fori_loop(b_first, b_last, block, 0)

    @pl.when(e == N_EXPERTS - 1)
    def _():
        for back in (2, 1):
            @pl.when(n_used >= back)
            def _():
                last = n_used - back
                for_size(last, lambda groups: wait_all(y_copies(last, lax.rem(last, 2), groups)))


def _experts(xb, block_end, block_group, block_short, w_up_gate, b_up_gate, w_down, b_down):
    groups = MOE_ROWS // SUBLANES

    def w_map(e, bend, bgrp, bshort):
        return (e, 0, 0)

    grid_spec = pltpu.PrefetchScalarGridSpec(
        num_scalar_prefetch=3,
        grid=(N_EXPERTS,),
        in_specs=[
            pl.BlockSpec(memory_space=pl.ANY),
            pl.BlockSpec((1, D_MODEL, 2 * D_FF), w_map),
            pl.BlockSpec((1, 1, 2 * D_FF), w_map),
            pl.BlockSpec((1, D_FF, D_MODEL), w_map),
            pl.BlockSpec((1, 1, D_MODEL), w_map),
        ],
        out_specs=pl.BlockSpec(memory_space=pl.ANY),
        scratch_shapes=[
            pltpu.VMEM((D_MODEL, 2 * D_FF), BF16), pltpu.VMEM((D_FF, D_MODEL), BF16),
            pltpu.VMEM((MOE_ROWS, D_FF), BF16),
            pltpu.VMEM((2, groups) + PACKED_VIEW, jnp.uint32), pltpu.VMEM((2, groups) + TILE_VIEW, F32),
            pltpu.SemaphoreType.DMA((2,)), pltpu.SemaphoreType.DMA((2,)),
        ],
    )
    return pl.pallas_call(
        _expert_body,
        grid_spec=grid_spec,
        out_shape=jax.ShapeDtypeStruct((xb.shape[0],) + TOKEN_TILES, F32),
        compiler_params=pltpu.CompilerParams(
            dimension_semantics=("arbitrary",), vmem_limit_bytes=VMEM_LIMIT),
        name="moe_experts",
    )(block_end, block_group, block_short, xb, w_up_gate, b_up_gate.reshape(N_EXPERTS, 1, -1),
      w_down, b_down.reshape(N_EXPERTS, 1, -1))


COMBINE_TOKENS = 128


def _combine_body(dest_ref, yb_hbm, h_ref, gate_ref, gain_ref, o_ref, buf_ref, sem):
    i = pl.program_id(0)
    n_steps = pl.num_programs(0)
    nt = COMBINE_TOKENS

    def issue(step, slot):
        base = step * nt

        def body(g, c):
            for s in range(SUBLANES):
                for k in range(TOP_K):
                    pltpu.make_async_copy(
                        yb_hbm.at[dest_ref[k * (dest_ref.shape[0] // TOP_K) + base + g * SUBLANES + s]],
                        buf_ref.at[slot, k, g, :, s, :], sem.at[slot]).start(priority=k % 2)
            return c
        lax.fori_loop(0, nt // SUBLANES, body, 0, unroll=ISSUE_UNROLL)

    slot = lax.rem(i, 2)

    @pl.when(i == 0)
    def _():
        issue(0, 0)

    @pl.when(i + 1 < n_steps)
    def _():
        issue(i + 1, 1 - slot)

    pltpu.make_async_copy(buf_ref.at[slot], buf_ref.at[slot], sem.at[slot]).wait()

    gates = gate_ref[...]
    acc = h_ref[...]
    for k in range(TOP_K):
        acc = acc + gates[:, k:k + 1] * _load_tiles(buf_ref.at[slot, k])
    ms = jnp.mean(acc * acc, axis=-1, keepdims=True)
    o_ref[...] = acc * lax.rsqrt(ms + NORM_EPS) * gain_ref[...]


def _combine(yb, dest_flat, h_res, gates_tk, final_gain):
    n_tok = h_res.shape[0]
    nt = COMBINE_TOKENS
    grid_spec = pltpu.PrefetchScalarGridSpec(
        num_scalar_prefetch=1,
        grid=(n_tok // nt,),
        in_specs=[
            pl.BlockSpec(memory_space=pl.ANY),
            pl.BlockSpec((nt, D_MODEL), lambda i, d: (i, 0)),
            pl.BlockSpec((nt, TOP_K), lambda i, d: (i, 0)),
            pl.BlockSpec((1, D_MODEL), lambda i, d: (0, 0)),
        ],
        out_specs=pl.BlockSpec((nt, D_MODEL), lambda i, d: (i, 0)),
        scratch_shapes=[pltpu.VMEM((2, TOP_K, nt // SUBLANES) + TILE_VIEW, F32),
                        pltpu.SemaphoreType.DMA((2,))],
    )
    return pl.pallas_call(
        _combine_body,
        grid_spec=grid_spec,
        out_shape=jax.ShapeDtypeStruct((n_tok, D_MODEL), F32),
        compiler_params=pltpu.CompilerParams(
            dimension_semantics=("arbitrary",), vmem_limit_bytes=VMEM_LIMIT),
        name="moe_combine",
    )(dest_flat, yb, h_res, gates_tk, final_gain.reshape(1, D_MODEL))


def kernel(x, norm_mix_gain, w_in, hg_lb_fwd, hg_lb_bwd, hg_norm_gain, attn_sink, w_out,
           norm_ffn_gain, w_router, b_router, w_up_gate, b_up_gate, w_down, b_down,
           final_norm_gain):
    bsz, seqlen, d = x.shape
    n_tok = bsz * seqlen
    x2d = x.reshape(n_tok, d)

    lb_fwd = jax.nn.softmax(hg_lb_fwd.astype(F32), axis=0)[0]
    lb_bwd = jax.nn.softmax(hg_lb_bwd.astype(F32), axis=0)[0]

    proj = _inproj(x2d, norm_mix_gain[0], w_in[0])
    o_f, o_b = _hgrn2(proj, lb_fwd, lb_bwd, bsz, seqlen)
    y_att = _attention(proj, attn_sink[0], bsz, seqlen)
    h_res, gate_t, dest_t, counts, xb = _merge_route(
        o_f, o_b, proj, y_att, x2d, hg_norm_gain[0], w_out[0],
        norm_ffn_gain[0], w_router[0].T.astype(F32), b_router[0].astype(F32))

    n_rows = xb.shape[0]
    cap = n_rows // N_EXPERTS
    counts = counts[:, 0]
    padded = (counts + MOE_PAD - 1) // MOE_PAD * MOE_PAD
    full = padded // MOE_ROWS
    blocks = full + (padded % MOE_ROWS) // MOE_PAD
    block_end = jnp.cumsum(blocks).astype(jnp.int32)
    n_blocks_max = n_tok * TOP_K // MOE_PAD + N_EXPERTS
    b = jnp.arange(n_blocks_max, dtype=jnp.int32)
    owner = (block_end[None, :] <= b[:, None]).astype(jnp.int32)
    expert_of = jnp.minimum(jnp.sum(owner, axis=-1), N_EXPERTS - 1)
    within = b - jnp.sum(owner * blocks[None, :], axis=-1)
    is_owner = expert_of[:, None] == jnp.arange(N_EXPERTS, dtype=jnp.int32)[None, :]
    block_short = (within == jnp.sum(jnp.where(is_owner, full[None, :], 0), axis=-1)).astype(jnp.int32)
    block_group = ((expert_of * cap + within * MOE_ROWS) // SUBLANES).astype(jnp.int32)
    dest_flat = dest_t.reshape(-1)

    yb = _experts(xb.reshape((n_rows // SUBLANES,) + PACKED_TOKEN_TILES), block_end, block_group, block_short,
                  w_up_gate[0], b_up_gate[0], w_down[0], b_down[0])
    out = _combine(yb.reshape(n_rows, LANE_TILES, LANES), dest_flat, h_res, gate_t.T, final_norm_gain)
    return out.reshape(bsz, seqlen, d)
```

```python
import functools

import jax
import jax.numpy as jnp
import numpy as np
from jax import lax
from jax.experimental import pallas as pl
from jax.experimental.pallas import tpu as pltpu

F32 = jnp.float32
BF16 = jnp.bfloat16

D_MODEL = 1024
HG_HEADS = 8
HG_DK = 128
HG_CHUNK = 128
HG_GROUP = 8
AT_Q_HEADS = 16
AT_KV_HEADS = 4
AT_GROUP = AT_Q_HEADS // AT_KV_HEADS
AT_HEAD_DIM = 64
WINDOW = 128
AT_BLOCK = WINDOW
AT_KVWIDTH = AT_KV_HEADS * AT_HEAD_DIM
N_EXPERTS = 32
TOP_K = 4
D_FF = 1024
SWIGLU_LIMIT = 7.0
SWIGLU_ALPHA = 1.702
NORM_EPS = 1e-5
IN_WIDTH = 8704

COL_HQ, COL_FF, COL_FB, COL_HI, COL_HG = 0, 1024, 2048, 3072, 4096
COL_AQ, COL_AK, COL_AV, COL_GA, COL_GB = 5120, 6144, 6400, 6656, 7680

MOE_ROWS = 1024
MOE_PAD = 512
FF_SLAB = 256
MERGE_SUB = 256
BLOCK_DMA_PRIORITY = 1
VMEM_LIMIT = 56 * 1024 * 1024

NT_DIMS = (((1,), (1,)), ((), ()))

SUBLANES = 8
LANES = 128
LANE_TILES = D_MODEL // LANES
TILE_VIEW = (LANE_TILES, SUBLANES, LANES)
TOKEN_TILES = (SUBLANES, LANE_TILES, LANES)


def _store_tiles(ref, val):
    groups = val.shape[0] // SUBLANES
    for t in range(val.shape[1] // LANES):
        ref[:, t] = val[:, t * LANES:(t + 1) * LANES].reshape(groups, SUBLANES, LANES)


def _load_tiles(ref):
    groups = ref.shape[0]
    return jnp.concatenate(
        [ref[:, t].reshape(groups * SUBLANES, LANES) for t in range(ref.shape[1])], axis=1)


PACKED_TILES = LANE_TILES // 2
PACKED_VIEW = (PACKED_TILES, SUBLANES, LANES)
PACKED_TOKEN_TILES = (SUBLANES, PACKED_TILES, LANES)


def _pack_bf16_pairs(x):
    half = x.shape[1] // 2
    lo = pltpu.bitcast(x[:, :half].astype(BF16).astype(F32), jnp.uint32)
    hi = pltpu.bitcast(x[:, half:].astype(BF16).astype(F32), jnp.uint32)
    return hi | lax.shift_right_logical(lo, jnp.uint32(16))


def _unpack_bf16_pairs(w):
    lo = pltpu.bitcast(lax.shift_left(w, jnp.uint32(16)), F32)
    hi = pltpu.bitcast(w & jnp.uint32(0xFFFF0000), F32)
    return jnp.concatenate([lo, hi], axis=1).astype(BF16)


def _sigmoid(x):
    return 0.5 * jnp.tanh(0.5 * x) + 0.5


def _inproj_body(x_ref, g_ref, w_ref, o_ref, xn_ref):
    @pl.when(pl.program_id(1) == 0)
    def _():
        x = x_ref[...]
        ms = jnp.mean(x * x, axis=-1, keepdims=True)
        xn_ref[...] = (x * lax.rsqrt(ms + NORM_EPS) * g_ref[...]).astype(BF16)

    o_ref[...] = jnp.dot(xn_ref[...], w_ref[...].astype(BF16),
                         preferred_element_type=F32).astype(o_ref.dtype)


def _inproj(x2d, gain, w, tm=1024, tn=2176):
    n_tok = x2d.shape[0]
    n_out = w.shape[1]
    return pl.pallas_call(
        _inproj_body,
        grid=(n_tok // tm, n_out // tn),
        in_specs=[
            pl.BlockSpec((tm, D_MODEL), lambda i, j: (i, 0)),
            pl.BlockSpec((1, D_MODEL), lambda i, j: (0, 0)),
            pl.BlockSpec((D_MODEL, tn), lambda i, j: (0, j)),
        ],
        out_specs=pl.BlockSpec((tm, tn), lambda i, j: (i, j)),
        out_shape=jax.ShapeDtypeStruct((n_tok, n_out), BF16),
        scratch_shapes=[pltpu.VMEM((tm, D_MODEL), BF16)],
        compiler_params=pltpu.CompilerParams(
            dimension_semantics=("parallel", "arbitrary"), vmem_limit_bytes=VMEM_LIMIT),
        name="inproj",
    )(x2d, gain.reshape(1, D_MODEL), w)


class _Dir:
    def __init__(self, q_ref, f_ref, v_ref, o_ref, st_ref, lb, tri, mask, mid_row, last_row):
        self.q_ref, self.f_ref, self.v_ref, self.o_ref, self.st_ref = q_ref, f_ref, v_ref, o_ref, st_ref
        self.c0, self.c1 = 0.5 * (1.0 + lb), 0.5 * (1.0 - lb)
        self.tri, self.mask, self.mid_row, self.last_row = tri, mask, mid_row, last_row


def _gla_group(chains):
    c = HG_CHUNK
    dk = HG_DK
    pre = []
    for d, r0 in chains:
        h = 0.5 * d.q_ref[pl.ds(r0, c), :].astype(F32)
        q = h * jnp.tanh(h) + h
        ct = d.c1 * jnp.tanh(0.5 * d.f_ref[pl.ds(r0, c), :].astype(F32))
        logf = jnp.log2(d.c0 + ct)
        hi = logf.astype(BF16)
        lo = (logf - hi.astype(F32)).astype(BF16)
        pre.append((q, (1.0 - d.c0) - ct, jnp.concatenate([hi, lo], axis=1)))
    cums = []
    for (d, _), (_, _, hl) in zip(chains, pre):
        r = jnp.dot(d.tri, hl, preferred_element_type=F32)
        cums.append(r[:, :dk] + r[:, dk:])
    mids = []
    for (d, r0), (q, k, _), cum in zip(chains, pre, cums):
        mid = cum[d.mid_row:d.mid_row + 1, :]
        last = cum[d.last_row:d.last_row + 1, :]
        a = cum - mid
        qa = q * jnp.exp2(a)
        ka = k * jnp.exp2(-a)
        qd = (qa * jnp.exp2(mid)).astype(BF16)
        kend = (ka * jnp.exp2(last - mid)).astype(BF16)
        dec = jnp.exp2(last)
        scores = lax.dot_general(qa.astype(BF16), ka.astype(BF16), NT_DIMS,
                                 preferred_element_type=F32)
        mids.append((qd, kend, dec, scores))
    outs = []
    for (d, r0), (qd, kend, dec, scores) in zip(chains, mids):
        v = d.v_ref[pl.ds(r0, c), :]
        sm = jnp.where(d.mask, scores, 0.0).astype(BF16)
        o_intra = jnp.dot(sm, v, preferred_element_type=F32)
        upd = jnp.dot(jnp.transpose(v.astype(F32)).astype(BF16), kend, preferred_element_type=F32)
        outs.append((o_intra, upd))
    for (d, r0), (qd, _, dec, _), (o_intra, upd) in zip(chains, mids, outs):
        st = d.st_ref[...]
        d.o_ref[pl.ds(r0, c), :] = (o_intra + lax.dot_general(
            qd, st.astype(BF16), NT_DIMS, preferred_element_type=F32)).astype(d.o_ref.dtype)
        d.st_ref[...] = st * dec + upd


def _hgrn2_body(qf_ref, ff_ref, vf_ref, qb_ref, fb_ref, vb_ref, lbf_ref, lbb_ref,
                of_ref, ob_ref, sf_ref, sb_ref):
    @pl.when(pl.program_id(2) == 0)
    def _():
        sf_ref[...] = jnp.zeros_like(sf_ref)
        sb_ref[...] = jnp.zeros_like(sb_ref)

    c = HG_CHUNK
    g = HG_GROUP
    n_chunks = qf_ref.shape[0] // c
    row = lax.broadcasted_iota(jnp.int32, (c, c), 0)
    col = lax.broadcasted_iota(jnp.int32, (c, c), 1)
    lower = col <= row
    upper = col >= row
    fwd = _Dir(qf_ref, ff_ref, vf_ref, of_ref, sf_ref, lbf_ref[...],
               jnp.where(lower, 1.0, 0.0).astype(BF16), lower, c // 2 - 1, c - 1)
    bwd = _Dir(qb_ref, fb_ref, vb_ref, ob_ref, sb_ref, lbb_ref[...],
               jnp.where(upper, 1.0, 0.0).astype(BF16), upper, c // 2, 0)

    def step(n, carry):
        chains = []
        for u in range(g):
            chains.append((fwd, pl.multiple_of((n * g + u) * c, c)))
        for u in range(g):
            chains.append((bwd, pl.multiple_of((n_chunks - 1 - n * g - u) * c, c)))
        _gla_group(chains)
        return carry

    lax.fori_loop(0, n_chunks // g, step, 0)


def _hgrn2(proj, lb_fwd, lb_bwd, bsz, seqlen, rows=1024):
    n_tok = bsz * seqlen
    rows = min(rows, seqlen)
    nb = seqlen // rows
    dk = HG_DK

    def fwd_map(col0):
        return lambda b, h, j: (b * nb + j, col0 // dk + h)

    def bwd_map(col0):
        return lambda b, h, j: (b * nb + (nb - 1 - j), col0 // dk + h)

    blk = (rows, dk)
    lb_spec = pl.BlockSpec((1, dk), lambda b, h, j: (0, h))
    return pl.pallas_call(
        _hgrn2_body,
        grid=(bsz, HG_HEADS, nb),
        in_specs=[
            pl.BlockSpec(blk, fwd_map(COL_HQ)), pl.BlockSpec(blk, fwd_map(COL_FF)),
            pl.BlockSpec(blk, fwd_map(COL_HI)),
            pl.BlockSpec(blk, bwd_map(COL_HQ)), pl.BlockSpec(blk, bwd_map(COL_FB)),
            pl.BlockSpec(blk, bwd_map(COL_HI)),
            lb_spec, lb_spec,
        ],
        out_specs=[pl.BlockSpec(blk, fwd_map(0)), pl.BlockSpec(blk, bwd_map(0))],
        out_shape=[jax.ShapeDtypeStruct((n_tok, HG_HEADS * dk), BF16)] * 2,
        scratch_shapes=[pltpu.VMEM((dk, dk), F32), pltpu.VMEM((dk, dk), F32)],
        compiler_params=pltpu.CompilerParams(
            dimension_semantics=("parallel", "parallel", "arbitrary"),
            vmem_limit_bytes=VMEM_LIMIT),
        name="hgrn2",
    )(proj, proj, proj, proj, proj, proj, lb_fwd.reshape(1, -1), lb_bwd.reshape(1, -1))


def _alibi_tables():
    blk, span = AT_BLOCK, 3 * AT_BLOCK
    row = np.arange(blk)[:, None]
    col = np.arange(span)[None, :]
    absrel = np.abs(row + blk - col)
    slopes = 2.0 ** (-8.0 * np.arange(1, AT_Q_HEADS + 1) / AT_Q_HEADS)
    tables = []
    for sel in range(4):
        valid = absrel <= WINDOW
        if sel & 1:
            valid = valid & (col >= blk)
        if sel & 2:
            valid = valid & (col < 2 * blk)
        base = np.where(valid, -absrel.astype(np.float64), -np.inf)
        tables.append(slopes[:, None, None] * base[None])
    return np.stack(tables).astype(np.float32)


def _attn_body(sink_ref, bias_ref, q_ref, kp_ref, ko_ref, kn_ref, vp_ref, vo_ref, vn_ref, o_ref):
    blk = AT_BLOCK
    span = 3 * blk
    dh = AT_HEAD_DIM
    lanes = 2 * dh
    lo_half = lax.broadcasted_iota(jnp.int32, (blk, lanes), 1) < dh

    def padded(ref, g):
        t = ref[:, (g // 2) * lanes:(g // 2 + 1) * lanes].astype(F32)
        if g % 2 == 0:
            lo = jnp.where(lo_half, t, 0.0)
            hi = pltpu.roll(lo, dh, axis=1)
        else:
            hi = jnp.where(lo_half, 0.0, t)
            lo = pltpu.roll(hi, dh, axis=1)
        return lo, hi

    def banded(refs, g):
        pads = [padded(r, g) for r in refs]
        return jnp.concatenate([p[0] for p in pads] + [p[1] for p in pads], axis=0).astype(BF16)

    scores, values = [], []
    for g in range(AT_KV_HEADS):
        q2 = q_ref[:, g * 2 * lanes:(g + 1) * 2 * lanes] * (dh ** -0.5)
        lhs = jnp.concatenate([q2[:, :lanes], q2[:, lanes:]], axis=0).astype(BF16)
        scores.append(lax.dot_general(lhs, banded((kp_ref, ko_ref, kn_ref), g), NT_DIMS,
                                      preferred_element_type=F32))
        values.append(banded((vp_ref, vo_ref, vn_ref), g))
    probs, denoms = [], []
    for g in range(AT_KV_HEADS):
        p_rows, d_rows = [], []
        for r in range(2):
            p_cols, d_cols = [], []
            for c in range(2):
                h = g * AT_GROUP + 2 * r + c
                sink = sink_ref[h]
                s = scores[g][r * blk:(r + 1) * blk, c * span:(c + 1) * span] + bias_ref[0, h]
                m = jnp.maximum(jnp.max(s, axis=-1, keepdims=True), sink)
                p = jnp.exp(s - m)
                d_cols.append(jnp.sum(p, axis=-1, keepdims=True) + jnp.exp(sink - m))
                p_cols.append(p.astype(BF16))
            p_rows.append(jnp.concatenate(p_cols, axis=1))
            d_rows.append(jnp.where(lo_half, d_cols[0], d_cols[1]))
        probs.append(jnp.concatenate(p_rows, axis=0))
        denoms.append(d_rows)
    outs = []
    for g in range(AT_KV_HEADS):
        o = jnp.dot(probs[g], values[g], preferred_element_type=F32)
        for r in range(2):
            outs.append(o[r * blk:(r + 1) * blk, :] / denoms[g][r])
    o_ref[...] = jnp.concatenate(outs, axis=-1).astype(o_ref.dtype)


def _attention(proj, sink, bsz, seqlen):
    n_tok = bsz * seqlen
    nb = seqlen // AT_BLOCK
    kcol = COL_AK // AT_KVWIDTH
    vcol = COL_AV // AT_KVWIDTH

    def prev(c):
        return lambda b, j: (b * nb + jnp.maximum(j - 1, 0), c)

    def own(c):
        return lambda b, j: (b * nb + j, c)

    def nxt(c):
        return lambda b, j: (b * nb + jnp.minimum(j + 1, nb - 1), c)

    def bias_map(b, j):
        return (jnp.where(j == 0, 1, 0) + jnp.where(j == nb - 1, 2, 0), 0, 0, 0)

    kv_blk = (AT_BLOCK, AT_KVWIDTH)
    return pl.pallas_call(
        _attn_body,
        grid=(bsz, nb),
        in_specs=[
            pl.BlockSpec(memory_space=pltpu.SMEM),
            pl.BlockSpec((1, AT_Q_HEADS, AT_BLOCK, 3 * AT_BLOCK), bias_map),
            pl.BlockSpec((AT_BLOCK, D_MODEL), own(COL_AQ // D_MODEL)),
            pl.BlockSpec(kv_blk, prev(kcol)), pl.BlockSpec(kv_blk, own(kcol)),
            pl.BlockSpec(kv_blk, nxt(kcol)),
            pl.BlockSpec(kv_blk, prev(vcol)), pl.BlockSpec(kv_blk, own(vcol)),
            pl.BlockSpec(kv_blk, nxt(vcol)),
        ],
        out_specs=pl.BlockSpec((AT_BLOCK, D_MODEL), lambda b, j: (b * nb + j, 0)),
        out_shape=jax.ShapeDtypeStruct((n_tok, D_MODEL), BF16),
        compiler_params=pltpu.CompilerParams(
            dimension_semantics=("parallel", "arbitrary"), vmem_limit_bytes=VMEM_LIMIT),
        name="window_attn",
    )(sink.astype(F32), jnp.asarray(_alibi_tables()), proj, proj, proj, proj, proj, proj, proj)


def _merge_body(n_steps, of_ref, ob_ref, hg_ref, ga0_ref, ga1_ref, gb0_ref, gb1_ref, ya_ref, x_ref,
                gn_ref, wo_ref, g2_ref, wr_ref, br_ref,
                h_ref, gate_ref, dest_ref, cnt_ref, xb_hbm,
                carry_ref, ring, dest_vmem, dest_smem, cnt_vmem, cnt_smem, zero_ref,
                sem_rows, sem_dest, sem_misc):
    i = pl.program_id(0)
    slot = lax.rem(i, 2)
    tm = x_ref.shape[0]
    cap = xb_hbm.shape[0] // N_EXPERTS

    def rows_wait(sl):
        pltpu.make_async_copy(xb_hbm.at[pl.ds(0, tm * TOP_K)], xb_hbm.at[pl.ds(0, tm * TOP_K)],
                              sem_rows.at[sl]).wait()

    def dest_copy(sl):
        return pltpu.make_async_copy(dest_vmem.at[sl], dest_smem.at[sl], sem_dest.at[sl])

    def issue_rows(sl):
        dest_copy(sl).wait()

        def body(g, c):
            for s in range(SUBLANES):
                for k in range(TOP_K):
                    pltpu.make_async_copy(
                        ring.at[sl, g, :, s, :], xb_hbm.at[dest_smem[sl, k, g * SUBLANES + s]],
                        sem_rows.at[sl]).start(priority=k % 2)
            return c
        lax.fori_loop(0, tm // SUBLANES, body, 0, unroll=ISSUE_UNROLL)

    @pl.when(i == 0)
    def _():
        carry_ref[...] = jnp.zeros_like(carry_ref)

    @pl.when(i >= 2)
    def _():
        rows_wait(slot)

    sub = MERGE_SUB
    subs = [slice(s0, s0 + sub) for s0 in range(0, x_ref.shape[0], sub)]
    gn = gn_ref[...]

    merged = []
    for rs in subs:
        o = of_ref[rs, :].astype(F32) + ob_ref[rs, :].astype(F32)
        parts = []
        for h in range(HG_HEADS):
            oh = o[:, h * HG_DK:(h + 1) * HG_DK]
            ms = jnp.mean(oh * oh, axis=-1, keepdims=True)
            parts.append(oh * lax.rsqrt(ms + NORM_EPS) * gn)
        hg = hg_ref[rs, :].astype(F32)
        y_rec = jnp.concatenate(parts, axis=-1) * (hg * _sigmoid(hg))
        gate_a = jnp.concatenate([ga0_ref[rs, :], ga1_ref[rs, :]], axis=-1).astype(F32)
        gate_b = jnp.concatenate([gb0_ref[rs, :], gb1_ref[rs, :]], axis=-1).astype(F32)
        merged.append((_sigmoid(gate_a) * y_rec
                       + _sigmoid(gate_b) * ya_ref[rs, :].astype(F32)).astype(BF16))

    w_out = wo_ref[...].astype(BF16)
    xns = []
    for rs, mg in zip(subs, merged):
        h_res = x_ref[rs, :] + jnp.dot(mg, w_out, preferred_element_type=F32)
        h_ref[rs, :] = h_res
        ms = jnp.mean(h_res * h_res, axis=-1, keepdims=True)
        xn = h_res * lax.rsqrt(ms + NORM_EPS) * g2_ref[...]
        _store_tiles(ring.at[slot, rs.start // SUBLANES:rs.stop // SUBLANES], _pack_bf16_pairs(xn))
        xns.append(xn)

    logits = [lax.dot_general(wr_ref[...], xn, NT_DIMS, preferred_element_type=F32,
                              precision=lax.Precision.HIGHEST) + br_ref[...] for xn in xns]
    e_iota = lax.broadcasted_iota(jnp.int32, (N_EXPERTS, sub), 0).astype(F32)
    picked = []
    for rs, vals in zip(subs, logits):
        tops, idxs = [], []
        for _ in range(TOP_K):
            m = jnp.max(vals, axis=0, keepdims=True)
            sel = jnp.min(jnp.where(vals == m, e_iota, float(N_EXPERTS)), axis=0, keepdims=True)
            tops.append(m)
            idxs.append(sel)
            vals = jnp.where(e_iota == sel, -jnp.inf, vals)
        exps = [jnp.exp(t - tops[0]) for t in tops]
        den = exps[0] + exps[1] + exps[2] + exps[3]
        gate_ref[:, rs] = jnp.concatenate([e / den for e in exps], axis=0)
        picked.append((idxs, jnp.where(vals == -jnp.inf, 1.0, 0.0)))

    s_iota = lax.broadcasted_iota(jnp.int32, (sub, sub), 0)
    t_iota = lax.broadcasted_iota(jnp.int32, (sub, sub), 1)
    strict = jnp.where(s_iota < t_iota, 1.0, 0.0).astype(BF16)
    prefixes = [jnp.dot(chosen.astype(BF16), strict, preferred_element_type=F32)
                for _, chosen in picked]
    carry = carry_ref[...]
    for rs, (idxs, chosen), prefix in zip(subs, picked, prefixes):
        prefix = prefix + carry
        dests = [sel * float(cap) + jnp.sum(jnp.where(e_iota == sel, prefix, 0.0), axis=0, keepdims=True)
                 for sel in idxs]
        dest = jnp.concatenate(dests, axis=0).astype(jnp.int32)
        dest_ref[:, rs] = dest
        dest_vmem[slot, :, rs] = dest
        carry = carry + jnp.sum(chosen, axis=1, keepdims=True)
    carry_ref[...] = carry
    counts = jnp.broadcast_to(carry, cnt_ref.shape).astype(jnp.int32)
    cnt_ref[...] = counts

    dest_copy(slot).start()

    @pl.when(i >= 1)
    def _():
        issue_rows(1 - slot)

    @pl.when(i == n_steps - 1)
    def _():
        issue_rows(slot)
        if n_steps >= 2:
            rows_wait(1 - slot)
        rows_wait(slot)
        cnt_vmem[...] = counts
        cp = pltpu.make_async_copy(cnt_vmem, cnt_smem, sem_misc.at[0])
        cp.start()
        cp.wait()
        zero_ref[...] = jnp.zeros_like(zero_ref)

        def tail(e):
            return pltpu.make_async_copy(
                zero_ref, xb_hbm.at[pl.ds(e * cap + cnt_smem[e, 0], MOE_PAD)], sem_misc.at[0])
        for e in range(N_EXPERTS):
            tail(e).start()
        for e in range(N_EXPERTS):
            tail(e).wait()


def _merge_route(o_f, o_b, proj, y_att, x2d, gn_gain, w_out, g2, w_router_t, b_router, tm=512):
    n_tok = x2d.shape[0]
    nblk = n_tok // tm
    row_blk = (tm, D_MODEL)
    half = D_MODEL // 2
    half_blk = (tm, half)

    def rows(c):
        return lambda i: (i, c)

    full = lambda i: (0, 0)
    tok_out = pl.BlockSpec((TOP_K, tm), lambda i: (0, i))
    cap = n_tok + MOE_PAD
    return pl.pallas_call(
        functools.partial(_merge_body, nblk),
        grid=(nblk,),
        in_specs=[
            pl.BlockSpec(row_blk, rows(0)), pl.BlockSpec(row_blk, rows(0)),
            pl.BlockSpec(row_blk, rows(COL_HG // D_MODEL)),
            pl.BlockSpec(half_blk, rows(COL_GA // half)),
            pl.BlockSpec(half_blk, rows(COL_GA // half + 1)),
            pl.BlockSpec(half_blk, rows(COL_GB // half)),
            pl.BlockSpec(half_blk, rows(COL_GB // half + 1)),
            pl.BlockSpec(row_blk, rows(0)), pl.BlockSpec(row_blk, rows(0)),
            pl.BlockSpec((1, HG_DK), full),
            pl.BlockSpec((D_MODEL, D_MODEL), full),
            pl.BlockSpec((1, D_MODEL), full),
            pl.BlockSpec((N_EXPERTS, D_MODEL), full),
            pl.BlockSpec((N_EXPERTS, 1), full),
        ],
        out_specs=[
            pl.BlockSpec(row_blk, rows(0)),
            tok_out, tok_out,
            pl.BlockSpec((N_EXPERTS, LANES), full),
            pl.BlockSpec(memory_space=pl.ANY),
        ],
        out_shape=[
            jax.ShapeDtypeStruct((n_tok, D_MODEL), F32),
            jax.ShapeDtypeStruct((TOP_K, n_tok), F32),
            jax.ShapeDtypeStruct((TOP_K, n_tok), jnp.int32),
            jax.ShapeDtypeStruct((N_EXPERTS, LANES), jnp.int32),
            jax.ShapeDtypeStruct((N_EXPERTS * cap, PACKED_TILES, LANES), jnp.uint32),
        ],
        scratch_shapes=[
            pltpu.VMEM((N_EXPERTS, 1), F32),
            pltpu.VMEM((2, tm // SUBLANES) + PACKED_VIEW, jnp.uint32),
            pltpu.VMEM((2, TOP_K, tm), jnp.int32), pltpu.SMEM((2, TOP_K, tm), jnp.int32),
            pltpu.VMEM((N_EXPERTS, LANES), jnp.int32), pltpu.SMEM((N_EXPERTS, LANES), jnp.int32),
            pltpu.VMEM((MOE_PAD, PACKED_TILES, LANES), jnp.uint32),
            pltpu.SemaphoreType.DMA((2,)), pltpu.SemaphoreType.DMA((2,)), pltpu.SemaphoreType.DMA((1,)),
        ],
        compiler_params=pltpu.CompilerParams(
            dimension_semantics=("arbitrary",), vmem_limit_bytes=VMEM_LIMIT),
        name="merge_route",
    )(o_f, o_b, proj, proj, proj, proj, proj, y_att, x2d, gn_gain.reshape(1, HG_DK), w_out,
      g2.reshape(1, D_MODEL), w_router_t, b_router.reshape(N_EXPERTS, 1))


ISSUE_UNROLL = 2


def _expert_body(bend_ref, bgrp_ref, bshort_ref, x_hbm, wug_ref, bug_ref, wd_ref, bd_ref, y_hbm,
                 wug_bf, wd_bf, act_ref, xbuf, xbf, ybuf, sem_in, sem_out):
    e = pl.program_id(0)
    n_used = bend_ref[N_EXPERTS - 1]
    b_first = jnp.where(e == 0, 0, bend_ref[jnp.maximum(e - 1, 0)])
    b_last = bend_ref[e]

    def for_size(b, fn):
        short = bshort_ref[b] == 1

        @pl.when(short)
        def _():
            fn(MOE_PAD // SUBLANES)

        @pl.when(jnp.logical_not(short))
        def _():
            fn(MOE_ROWS // SUBLANES)

    def x_copies(b, slot, groups):
        return [pltpu.make_async_copy(x_hbm.at[pl.ds(bgrp_ref[b], groups), s],
                                      xbuf.at[slot, pl.ds(0, groups), :, s, :], sem_in.at[slot])
                for s in range(SUBLANES)]

    def y_copies(b, slot, groups):
        return [pltpu.make_async_copy(ybuf.at[slot, pl.ds(0, groups), :, s, :],
                                      y_hbm.at[pl.ds(bgrp_ref[b], groups), s], sem_out.at[slot])
                for s in range(SUBLANES)]

    def start_all(copies):
        for cp in copies:
            cp.start(priority=BLOCK_DMA_PRIORITY)

    def wait_all(copies):
        for cp in copies:
            cp.wait()

    ns = D_FF // FF_SLAB
    for s in range(ns):
        lo, hi = s * FF_SLAB, (s + 1) * FF_SLAB
        wug_bf[:, 2 * lo:2 * lo + FF_SLAB] = wug_ref[0, :, lo:hi].astype(BF16)
        wug_bf[:, 2 * lo + FF_SLAB:2 * hi] = wug_ref[0, :, D_FF + lo:D_FF + hi].astype(BF16)
    wd_bf[...] = wd_ref[0].astype(BF16)
    b_ug = bug_ref[0]
    b_d = bd_ref[0]

    def unpack(slot):
        xbf[slot] = _unpack_bf16_pairs(_load_tiles(xbuf.at[slot]))

    def mlp(slot, groups):
        rows = groups * SUBLANES
        x = xbf[slot, 0:rows, :]
        for s in range(ns):
            lo, hi = s * FF_SLAB, (s + 1) * FF_SLAB
            hu = jnp.dot(x, wug_bf[:, 2 * lo:2 * hi], preferred_element_type=F32)
            g = jnp.minimum(hu[:, :FF_SLAB] + b_ug[:, lo:hi], SWIGLU_LIMIT)
            u = jnp.clip(hu[:, FF_SLAB:] + b_ug[:, D_FF + lo:D_FF + hi], -SWIGLU_LIMIT, SWIGLU_LIMIT)
            act_ref[0:rows, lo:hi] = (g * _sigmoid(SWIGLU_ALPHA * g) * (u + 1.0)).astype(BF16)
        _store_tiles(ybuf.at[slot, pl.ds(0, groups)],
                     jnp.dot(act_ref[0:rows, :], wd_bf[...], preferred_element_type=F32) + b_d)

    def block(b, carry):
        slot = lax.rem(b, 2)
        other = 1 - slot

        @pl.when(b + 1 < n_used)
        def _():
            for_size(b + 1, lambda groups: wait_all(x_copies(b + 1, other, groups)))

        @pl.when(b + 2 < n_used)
        def _():
            for_size(b + 2, lambda groups: start_all(x_copies(b + 2, slot, groups)))

        @pl.when(b >= 2)
        def _():
            for_size(b - 2, lambda groups: wait_all(y_copies(b - 2, slot, groups)))

        @pl.when(b >= 1)
        def _():
            for_size(b - 1, lambda groups: start_all(y_copies(b - 1, other, groups)))

        def run(groups):
            mlp(slot, groups)
            unpack(other)
        for_size(b, run)
        return carry

    @pl.when((e == 0) & (n_used > 0))
    def _():
        xbuf[...] = jnp.zeros(xbuf.shape, xbuf.dtype)
        for_size(0, lambda groups: start_all(x_copies(0, 0, groups)))

        @pl.when(n_used > 1)
        def _():
            for_size(1, lambda groups: start_all(x_copies(1, 1, groups)))
        for_size(0, lambda groups: wait_all(x_copies(0, 0, groups)))
        unpack(0)

    lax.fori_loop(b_first, b_last, block, 0)

    @pl.when(e == N_EXPERTS - 1)
    def _():
        @pl.when(n_used >= 1)
        def _():
            last = n_used - 1
            for_size(last, lambda groups: start_all(y_copies(last, lax.rem(last, 2), groups)))
        for back in (2, 1):
            @pl.when(n_used >= back)
            def _():
                last = n_used - back
                for_size(last, lambda groups: wait_all(y_copies(last, lax.rem(last, 2), groups)))


def _experts(xb, block_end, block_group, block_short, w_up_gate, b_up_gate, w_down, b_down):
    groups = MOE_ROWS // SUBLANES

    def w_map(e, bend, bgrp, bshort):
        return (e, 0, 0)

    grid_spec = pltpu.PrefetchScalarGridSpec(
        num_scalar_prefetch=3,
        grid=(N_EXPERTS,),
        in_specs=[
            pl.BlockSpec(memory_space=pl.ANY),
            pl.BlockSpec((1, D_MODEL, 2 * D_FF), w_map),
            pl.BlockSpec((1, 1, 2 * D_FF), w_map),
            pl.BlockSpec((1, D_FF, D_MODEL), w_map),
            pl.BlockSpec((1, 1, D_MODEL), w_map),
        ],
        out_specs=pl.BlockSpec(memory_space=pl.ANY),
        scratch_shapes=[
            pltpu.VMEM((D_MODEL, 2 * D_FF), BF16), pltpu.VMEM((D_FF, D_MODEL), BF16),
            pltpu.VMEM((MOE_ROWS, D_FF), BF16),
            pltpu.VMEM((2, groups) + PACKED_VIEW, jnp.uint32), pltpu.VMEM((2, MOE_ROWS, D_MODEL), BF16),
            pltpu.VMEM((2, groups) + TILE_VIEW, F32),
            pltpu.SemaphoreType.DMA((2,)), pltpu.SemaphoreType.DMA((2,)),
        ],
    )
    return pl.pallas_call(
        _expert_body,
        grid_spec=grid_spec,
        out_shape=jax.ShapeDtypeStruct((xb.shape[0],) + TOKEN_TILES, F32),
        compiler_params=pltpu.CompilerParams(
            dimension_semantics=("arbitrary",), vmem_limit_bytes=VMEM_LIMIT),
        name="moe_experts",
    )(block_end, block_group, block_short, xb, w_up_gate, b_up_gate.reshape(N_EXPERTS, 1, -1),
      w_down, b_down.reshape(N_EXPERTS, 1, -1))


COMBINE_TOKENS = 128


def _combine_body(dest_ref, yb_hbm, h_ref, gate_ref, gain_ref, o_ref, buf_ref, sem):
    i = pl.program_id(0)
    n_steps = pl.num_programs(0)
    nt = COMBINE_TOKENS

    def issue(step, slot):
        base = step * nt

        def body(g, c):
            for s in range(SUBLANES):
                for k in range(TOP_K):
                    pltpu.make_async_copy(
                        yb_hbm.at[dest_ref[k * (dest_ref.shape[0] // TOP_K) + base + g * SUBLANES + s]],
                        buf_ref.at[slot, k, g, :, s, :], sem.at[slot]).start(priority=k % 2)
            return c
        lax.fori_loop(0, nt // SUBLANES, body, 0, unroll=ISSUE_UNROLL)

    slot = lax.rem(i, 2)

    @pl.when(i == 0)
    def _():
        issue(0, 0)

    @pl.when(i + 1 < n_steps)
    def _():
        issue(i + 1, 1 - slot)

    pltpu.make_async_copy(buf_ref.at[slot], buf_ref.at[slot], sem.at[slot]).wait()

    gates = gate_ref[...]
    acc = h_ref[...]
    for k in range(TOP_K):
        acc = acc + gates[:, k:k + 1] * _load_tiles(buf_ref.at[slot, k])
    ms = jnp.mean(acc * acc, axis=-1, keepdims=True)
    o_ref[...] = acc * lax.rsqrt(ms + NORM_EPS) * gain_ref[...]


def _combine(yb, dest_flat, h_res, gates_tk, final_gain):
    n_tok = h_res.shape[0]
    nt = COMBINE_TOKENS
    grid_spec = pltpu.PrefetchScalarGridSpec(
        num_scalar_prefetch=1,
        grid=(n_tok // nt,),
        in_specs=[
            pl.BlockSpec(memory_space=pl.ANY),
            pl.BlockSpec((nt, D_MODEL), lambda i, d: (i, 0)),
            pl.BlockSpec((nt, TOP_K), lambda i, d: (i, 0)),
            pl.BlockSpec((1, D_MODEL), lambda i, d: (0, 0)),
        ],
        out_specs=pl.BlockSpec((nt, D_MODEL), lambda i, d: (i, 0)),
        scratch_shapes=[pltpu.VMEM((2, TOP_K, nt // SUBLANES) + TILE_VIEW, F32),
                        pltpu.SemaphoreType.DMA((2,))],
    )
    return pl.pallas_call(
        _combine_body,
        grid_spec=grid_spec,
        out_shape=jax.ShapeDtypeStruct((n_tok, D_MODEL), F32),
        compiler_params=pltpu.CompilerParams(
            dimension_semantics=("arbitrary",), vmem_limit_bytes=VMEM_LIMIT),
        name="moe_combine",
    )(dest_flat, yb, h_res, gates_tk, final_gain.reshape(1, D_MODEL))


def kernel(x, norm_mix_gain, w_in, hg_lb_fwd, hg_lb_bwd, hg_norm_gain, attn_sink, w_out,
           norm_ffn_gain, w_router, b_router, w_up_gate, b_up_gate, w_down, b_down,
           final_norm_gain):
    bsz, seqlen, d = x.shape
    n_tok = bsz * seqlen
    x2d = x.reshape(n_tok, d)

    lb_fwd = jax.nn.softmax(hg_lb_fwd.astype(F32), axis=0)[0]
    lb_bwd = jax.nn.softmax(hg_lb_bwd.astype(F32), axis=0)[0]

    proj = _inproj(x2d, norm_mix_gain[0], w_in[0])
    o_f, o_b = _hgrn2(proj, lb_fwd, lb_bwd, bsz, seqlen)
    y_att = _attention(proj, attn_sink[0], bsz, seqlen)
    h_res, gate_t, dest_t, counts, xb = _merge_route(
        o_f, o_b, proj, y_att, x2d, hg_norm_gain[0], w_out[0],
        norm_ffn_gain[0], w_router[0].T.astype(F32), b_router[0].astype(F32))

    n_rows = xb.shape[0]
    cap = n_rows // N_EXPERTS
    counts = counts[:, 0]
    padded = (counts + MOE_PAD - 1) // MOE_PAD * MOE_PAD
    full = padded // MOE_ROWS
    blocks = full + (padded % MOE_ROWS) // MOE_PAD
    block_end = jnp.cumsum(blocks).astype(jnp.int32)
    n_blocks_max = n_tok * TOP_K // MOE_PAD + N_EXPERTS
    b = jnp.arange(n_blocks_max, dtype=jnp.int32)
    owner = (block_end[None, :] <= b[:, None]).astype(jnp.int32)
    expert_of = jnp.minimum(jnp.sum(owner, axis=-1), N_EXPERTS - 1)
    within = b - jnp.sum(owner * blocks[None, :], axis=-1)
    is_owner = expert_of[:, None] == jnp.arange(N_EXPERTS, dtype=jnp.int32)[None, :]
    block_short = (within == jnp.sum(jnp.where(is_owner, full[None, :], 0), axis=-1)).astype(jnp.int32)
    block_group = ((expert_of * cap + within * MOE_ROWS) // SUBLANES).astype(jnp.int32)
    dest_flat = dest_t.reshape(-1)

    yb = _experts(xb.reshape((n_rows // SUBLANES,) + PACKED_TOKEN_TILES), block_end, block_group, block_short,
                  w_up_gate[0], b_up_gate[0], w_down[0], b_down[0])
    out = _combine(yb.reshape(n_rows, LANE_TILES, LANES), dest_flat, h_res, gate_t.T, final_norm_gain)
    return out.reshape(bsz, seqlen, d)
```

```python
import functools

import jax
import jax.numpy as jnp
import numpy as np
from jax import lax
from jax.experimental import pallas as pl
from jax.experimental.pallas import tpu as pltpu

F32 = jnp.float32
BF16 = jnp.bfloat16

D_MODEL = 1024
HG_HEADS = 8
HG_DK = 128
HG_CHUNK = 128
HG_GROUP = 8
AT_Q_HEADS = 16
AT_KV_HEADS = 4
AT_GROUP = AT_Q_HEADS // AT_KV_HEADS
AT_HEAD_DIM = 64
WINDOW = 128
AT_BLOCK = WINDOW
AT_KVWIDTH = AT_KV_HEADS * AT_HEAD_DIM
N_EXPERTS = 32
TOP_K = 4
D_FF = 1024
SWIGLU_LIMIT = 7.0
SWIGLU_ALPHA = 1.702
NORM_EPS = 1e-5
IN_WIDTH = 8704

COL_HQ, COL_FF, COL_FB, COL_HI, COL_HG = 0, 1024, 2048, 3072, 4096
COL_AQ, COL_AK, COL_AV, COL_GA, COL_GB = 5120, 6144, 6400, 6656, 7680

MOE_ROWS = 1024
MOE_PAD = 512
FF_SLAB = 256
MERGE_SUB = 256
BLOCK_DMA_PRIORITY = 1
VMEM_LIMIT = 56 * 1024 * 1024

NT_DIMS = (((1,), (1,)), ((), ()))

SUBLANES = 8
LANES = 128
LANE_TILES = D_MODEL // LANES
TILE_VIEW = (LANE_TILES, SUBLANES, LANES)
TOKEN_TILES = (SUBLANES, LANE_TILES, LANES)


def _store_tiles(ref, val):
    groups = val.shape[0] // SUBLANES
    for t in range(val.shape[1] // LANES):
        ref[:, t] = val[:, t * LANES:(t + 1) * LANES].reshape(groups, SUBLANES, LANES)


def _load_tiles(ref):
    groups = ref.shape[0]
    return jnp.concatenate(
        [ref[:, t].reshape(groups * SUBLANES, LANES) for t in range(ref.shape[1])], axis=1)


PACKED_TILES = LANE_TILES // 2
PACKED_VIEW = (PACKED_TILES, SUBLANES, LANES)
PACKED_TOKEN_TILES = (SUBLANES, PACKED_TILES, LANES)


def _pack_bf16_pairs(x):
    half = x.shape[1] // 2
    lo = pltpu.bitcast(x[:, :half].astype(BF16).astype(F32), jnp.uint32)
    hi = pltpu.bitcast(x[:, half:].astype(BF16).astype(F32), jnp.uint32)
    return hi | lax.shift_right_logical(lo, jnp.uint32(16))


def _unpack_bf16_pairs(w):
    lo = pltpu.bitcast(lax.shift_left(w, jnp.uint32(16)), F32)
    hi = pltpu.bitcast(w & jnp.uint32(0xFFFF0000), F32)
    return jnp.concatenate([lo, hi], axis=1).astype(BF16)


def _sigmoid(x):
    return 0.5 * jnp.tanh(0.5 * x) + 0.5


def _inproj_body(x_ref, g_ref, w_ref, o_ref, xn_ref):
    @pl.when(pl.program_id(1) == 0)
    def _():
        x = x_ref[...]
        ms = jnp.mean(x * x, axis=-1, keepdims=True)
        xn_ref[...] = (x * lax.rsqrt(ms + NORM_EPS) * g_ref[...]).astype(BF16)

    o_ref[...] = jnp.dot(xn_ref[...], w_ref[...].astype(BF16),
                         preferred_element_type=F32).astype(o_ref.dtype)


def _inproj(x2d, gain, w, tm=1024, tn=2176):
    n_tok = x2d.shape[0]
    n_out = w.shape[1]
    return pl.pallas_call(
        _inproj_body,
        grid=(n_tok // tm, n_out // tn),
        in_specs=[
            pl.BlockSpec((tm, D_MODEL), lambda i, j: (i, 0)),
            pl.BlockSpec((1, D_MODEL), lambda i, j: (0, 0)),
            pl.BlockSpec((D_MODEL, tn), lambda i, j: (0, j)),
        ],
        out_specs=pl.BlockSpec((tm, tn), lambda i, j: (i, j)),
        out_shape=jax.ShapeDtypeStruct((n_tok, n_out), BF16),
        scratch_shapes=[pltpu.VMEM((tm, D_MODEL), BF16)],
        compiler_params=pltpu.CompilerParams(
            dimension_semantics=("parallel", "arbitrary"), vmem_limit_bytes=VMEM_LIMIT),
        name="inproj",
    )(x2d, gain.reshape(1, D_MODEL), w)


class _Dir:
    def __init__(self, q_ref, f_ref, v_ref, o_ref, st_ref, lb, tri, mask, mid_row, last_row):
        self.q_ref, self.f_ref, self.v_ref, self.o_ref, self.st_ref = q_ref, f_ref, v_ref, o_ref, st_ref
        self.c0, self.c1 = 0.5 * (1.0 + lb), 0.5 * (1.0 - lb)
        self.tri, self.mask, self.mid_row, self.last_row = tri, mask, mid_row, last_row


def _gla_group(chains):
    c = HG_CHUNK
    dk = HG_DK
    pre = []
    for d, r0 in chains:
        h = 0.5 * d.q_ref[pl.ds(r0, c), :].astype(F32)
        q = h * jnp.tanh(h) + h
        ct = d.c1 * jnp.tanh(0.5 * d.f_ref[pl.ds(r0, c), :].astype(F32))
        logf = jnp.log2(d.c0 + ct)
        hi = logf.astype(BF16)
        lo = (logf - hi.astype(F32)).astype(BF16)
        pre.append((q, (1.0 - d.c0) - ct, jnp.concatenate([hi, lo], axis=1)))
    cums = []
    for (d, _), (_, _, hl) in zip(chains, pre):
        r = jnp.dot(d.tri, hl, preferred_element_type=F32)
        cums.append(r[:, :dk] + r[:, dk:])
    mids = []
    for (d, r0), (q, k, _), cum in zip(chains, pre, cums):
        mid = cum[d.mid_row:d.mid_row + 1, :]
        last = cum[d.last_row:d.last_row + 1, :]
        a = cum - mid
        qa = q * jnp.exp2(a)
        ka = k * jnp.exp2(-a)
        qd = (qa * jnp.exp2(mid)).astype(BF16)
        kend = (ka * jnp.exp2(last - mid)).astype(BF16)
        dec = jnp.exp2(last)
        scores = lax.dot_general(qa.astype(BF16), ka.astype(BF16), NT_DIMS,
                                 preferred_element_type=F32)
        mids.append((qd, kend, dec, scores))
    outs = []
    for (d, r0), (qd, kend, dec, scores) in zip(chains, mids):
        v = d.v_ref[pl.ds(r0, c), :]
        sm = jnp.where(d.mask, scores, 0.0).astype(BF16)
        o_intra = jnp.dot(sm, v, preferred_element_type=F32)
        upd = jnp.dot(jnp.transpose(v.astype(F32)).astype(BF16), kend, preferred_element_type=F32)
        outs.append((o_intra, upd))
    for (d, r0), (qd, _, dec, _), (o_intra, upd) in zip(chains, mids, outs):
        st = d.st_ref[...]
        d.o_ref[pl.ds(r0, c), :] = (o_intra + lax.dot_general(
            qd, st.astype(BF16), NT_DIMS, preferred_element_type=F32)).astype(d.o_ref.dtype)
        d.st_ref[...] = st * dec + upd


def _hgrn2_body(qf_ref, ff_ref, vf_ref, qb_ref, fb_ref, vb_ref, lbf_ref, lbb_ref,
                of_ref, ob_ref, sf_ref, sb_ref):
    @pl.when(pl.program_id(2) == 0)
    def _():
        sf_ref[...] = jnp.zeros_like(sf_ref)
        sb_ref[...] = jnp.zeros_like(sb_ref)

    c = HG_CHUNK
    g = HG_GROUP
    n_chunks = qf_ref.shape[0] // c
    row = lax.broadcasted_iota(jnp.int32, (c, c), 0)
    col = lax.broadcasted_iota(jnp.int32, (c, c), 1)
    lower = col <= row
    upper = col >= row
    fwd = _Dir(qf_ref, ff_ref, vf_ref, of_ref, sf_ref, lbf_ref[...],
               jnp.where(lower, 1.0, 0.0).astype(BF16), lower, c // 2 - 1, c - 1)
    bwd = _Dir(qb_ref, fb_ref, vb_ref, ob_ref, sb_ref, lbb_ref[...],
               jnp.where(upper, 1.0, 0.0).astype(BF16), upper, c // 2, 0)

    def step(n, carry):
        chains = []
        for u in range(g):
            chains.append((fwd, pl.multiple_of((n * g + u) * c, c)))
        for u in range(g):
            chains.append((bwd, pl.multiple_of((n_chunks - 1 - n * g - u) * c, c)))
        _gla_group(chains)
        return carry

    lax.fori_loop(0, n_chunks // g, step, 0)


def _hgrn2(proj, lb_fwd, lb_bwd, bsz, seqlen, rows=4096):
    n_tok = bsz * seqlen
    rows = min(rows, seqlen)
    nb = seqlen // rows
    dk = HG_DK

    def fwd_map(col0):
        return lambda b, h, j: (b * nb + j, col0 // dk + h)

    def bwd_map(col0):
        return lambda b, h, j: (b * nb + (nb - 1 - j), col0 // dk + h)

    blk = (rows, dk)
    lb_spec = pl.BlockSpec((1, dk), lambda b, h, j: (0, h))
    return pl.pallas_call(
        _hgrn2_body,
        grid=(bsz, HG_HEADS, nb),
        in_specs=[
            pl.BlockSpec(blk, fwd_map(COL_HQ)), pl.BlockSpec(blk, fwd_map(COL_FF)),
            pl.BlockSpec(blk, fwd_map(COL_HI)),
            pl.BlockSpec(blk, bwd_map(COL_HQ)), pl.BlockSpec(blk, bwd_map(COL_FB)),
            pl.BlockSpec(blk, bwd_map(COL_HI)),
            lb_spec, lb_spec,
        ],
        out_specs=[pl.BlockSpec(blk, fwd_map(0)), pl.BlockSpec(blk, bwd_map(0))],
        out_shape=[jax.ShapeDtypeStruct((n_tok, HG_HEADS * dk), BF16)] * 2,
        scratch_shapes=[pltpu.VMEM((dk, dk), F32), pltpu.VMEM((dk, dk), F32)],
        compiler_params=pltpu.CompilerParams(
            dimension_semantics=("parallel", "parallel", "arbitrary"),
            vmem_limit_bytes=VMEM_LIMIT),
        name="hgrn2",
    )(proj, proj, proj, proj, proj, proj, lb_fwd.reshape(1, -1), lb_bwd.reshape(1, -1))


def _alibi_tables():
    blk, span = AT_BLOCK, 3 * AT_BLOCK
    row = np.arange(blk)[:, None]
    col = np.arange(span)[None, :]
    absrel = np.abs(row + blk - col)
    slopes = 2.0 ** (-8.0 * np.arange(1, AT_Q_HEADS + 1) / AT_Q_HEADS)
    tables = []
    for sel in range(4):
        valid = absrel <= WINDOW
        if sel & 1:
            valid = valid & (col >= blk)
        if sel & 2:
            valid = valid & (col < 2 * blk)
        base = np.where(valid, -absrel.astype(np.float64), -np.inf)
        tables.append(slopes[:, None, None] * base[None])
    return np.stack(tables).astype(np.float32)


def _attn_body(sink_ref, bias_a_ref, bias_b_ref, q_ref, kp_ref, ko_ref, kn_ref, vp_ref, vo_ref, vn_ref,
               o_ref):
    blk = AT_BLOCK
    span = 3 * blk
    dh = AT_HEAD_DIM
    lanes = 2 * dh
    lo_half = lax.broadcasted_iota(jnp.int32, (blk, lanes), 1) < dh

    def padded(t, g):
        t = t[:, (g // 2) * lanes:(g // 2 + 1) * lanes].astype(F32)
        if g % 2 == 0:
            lo = jnp.where(lo_half, t, 0.0)
            hi = pltpu.roll(lo, dh, axis=1)
        else:
            hi = jnp.where(lo_half, 0.0, t)
            lo = pltpu.roll(hi, dh, axis=1)
        return lo.astype(BF16), hi.astype(BF16)

    def key_blocks(prev_ref, own_ref, next_ref):
        return [prev_ref[...], own_ref[0:blk, :], own_ref[blk:2 * blk, :], next_ref[...]]

    k_blocks = key_blocks(kp_ref, ko_ref, kn_ref)
    v_blocks = key_blocks(vp_ref, vo_ref, vn_ref)
    k_pads = [[padded(t, g) for t in k_blocks] for g in range(AT_KV_HEADS)]
    v_pads = [[padded(t, g) for t in v_blocks] for g in range(AT_KV_HEADS)]

    def banded(pads, first):
        use = pads[first:first + 3]
        return jnp.concatenate([p[0] for p in use] + [p[1] for p in use], axis=0)

    for qb, bias_ref in enumerate((bias_a_ref, bias_b_ref)):
        rows = slice(qb * blk, (qb + 1) * blk)
        scores = []
        for g in range(AT_KV_HEADS):
            q2 = q_ref[rows, g * 2 * lanes:(g + 1) * 2 * lanes] * (dh ** -0.5)
            lhs = jnp.concatenate([q2[:, :lanes], q2[:, lanes:]], axis=0).astype(BF16)
            scores.append(lax.dot_general(lhs, banded(k_pads[g], qb), NT_DIMS,
                                          preferred_element_type=F32))
        probs, denoms = [], []
        for g in range(AT_KV_HEADS):
            p_rows, d_rows = [], []
            for r in range(2):
                p_cols, d_cols = [], []
                for c in range(2):
                    h = g * AT_GROUP + 2 * r + c
                    sink = sink_ref[h]
                    s = scores[g][r * blk:(r + 1) * blk, c * span:(c + 1) * span] + bias_ref[0, h]
                    m = jnp.maximum(jnp.max(s, axis=-1, keepdims=True), sink)
                    p = jnp.exp(s - m)
                    d_cols.append(jnp.sum(p, axis=-1, keepdims=True) + jnp.exp(sink - m))
                    p_cols.append(p.astype(BF16))
                p_rows.append(jnp.concatenate(p_cols, axis=1))
                d_rows.append(jnp.where(lo_half, d_cols[0], d_cols[1]))
            probs.append(jnp.concatenate(p_rows, axis=0))
            denoms.append(d_rows)
        outs = []
        for g in range(AT_KV_HEADS):
            o = jnp.dot(probs[g], banded(v_pads[g], qb), preferred_element_type=F32)
            for r in range(2):
                outs.append(o[r * blk:(r + 1) * blk, :] / denoms[g][r])
        o_ref[rows, :] = jnp.concatenate(outs, axis=-1).astype(o_ref.dtype)


def _attention(proj, sink, bsz, seqlen):
    n_tok = bsz * seqlen
    nb = seqlen // AT_BLOCK
    kcol = COL_AK // AT_KVWIDTH
    vcol = COL_AV // AT_KVWIDTH

    nb2 = nb // 2

    def prev(c):
        return lambda b, j: (b * nb + jnp.maximum(2 * j - 1, 0), c)

    def own(c):
        return lambda b, j: (b * nb2 + j, c)

    def nxt(c):
        return lambda b, j: (b * nb + jnp.minimum(2 * j + 2, nb - 1), c)

    bias_spec_a = pl.BlockSpec((1, AT_Q_HEADS, AT_BLOCK, 3 * AT_BLOCK),
                               lambda b, j: (jnp.where(j == 0, 1, 0), 0, 0, 0))
    bias_spec_b = pl.BlockSpec((1, AT_Q_HEADS, AT_BLOCK, 3 * AT_BLOCK),
                               lambda b, j: (jnp.where(j == nb2 - 1, 2, 0), 0, 0, 0))
    kv_blk = (AT_BLOCK, AT_KVWIDTH)
    kv_pair = (2 * AT_BLOCK, AT_KVWIDTH)
    tables = jnp.asarray(_alibi_tables())
    return pl.pallas_call(
        _attn_body,
        grid=(bsz, nb2),
        in_specs=[
            pl.BlockSpec(memory_space=pltpu.SMEM),
            bias_spec_a, bias_spec_b,
            pl.BlockSpec((2 * AT_BLOCK, D_MODEL), own(COL_AQ // D_MODEL)),
            pl.BlockSpec(kv_blk, prev(kcol)), pl.BlockSpec(kv_pair, own(kcol)),
            pl.BlockSpec(kv_blk, nxt(kcol)),
            pl.BlockSpec(kv_blk, prev(vcol)), pl.BlockSpec(kv_pair, own(vcol)),
            pl.BlockSpec(kv_blk, nxt(vcol)),
        ],
        out_specs=pl.BlockSpec((2 * AT_BLOCK, D_MODEL), lambda b, j: (b * nb2 + j, 0)),
        out_shape=jax.ShapeDtypeStruct((n_tok, D_MODEL), BF16),
        compiler_params=pltpu.CompilerParams(
            dimension_semantics=("parallel", "arbitrary"), vmem_limit_bytes=VMEM_LIMIT),
        name="window_attn",
    )(sink.astype(F32), tables, tables, proj, proj, proj, proj, proj, proj, proj)


def _merge_body(n_steps, of_ref, ob_ref, hg_ref, ga0_ref, ga1_ref, gb0_ref, gb1_ref, ya_ref, x_ref,
                gn_ref, wo_ref, g2_ref, wr_ref, br_ref,
                h_ref, gate_ref, dest_ref, cnt_ref, xb_hbm,
                carry_ref, ring, dest_vmem, dest_smem, cnt_vmem, cnt_smem, zero_ref,
                sem_rows, sem_dest, sem_misc):
    i = pl.program_id(0)
    slot = lax.rem(i, 2)
    tm = x_ref.shape[0]
    cap = xb_hbm.shape[0] // N_EXPERTS

    def rows_wait(sl):
        pltpu.make_async_copy(xb_hbm.at[pl.ds(0, tm * TOP_K)], xb_hbm.at[pl.ds(0, tm * TOP_K)],
                              sem_rows.at[sl]).wait()

    def dest_copy(sl):
        return pltpu.make_async_copy(dest_vmem.at[sl], dest_smem.at[sl], sem_dest.at[sl])

    def issue_rows(sl):
        dest_copy(sl).wait()

        def body(g, c):
            for s in range(SUBLANES):
                for k in range(TOP_K):
                    pltpu.make_async_copy(
                        ring.at[sl, g, :, s, :], xb_hbm.at[dest_smem[sl, k, g * SUBLANES + s]],
                        sem_rows.at[sl]).start(priority=k % 2)
            return c
        lax.fori_loop(0, tm // SUBLANES, body, 0, unroll=ISSUE_UNROLL)

    @pl.when(i == 0)
    def _():
        carry_ref[...] = jnp.zeros_like(carry_ref)

    @pl.when(i >= 2)
    def _():
        rows_wait(slot)

    sub = MERGE_SUB
    subs = [slice(s0, s0 + sub) for s0 in range(0, x_ref.shape[0], sub)]
    gn = gn_ref[...]

    merged = []
    for rs in subs:
        o = of_ref[rs, :].astype(F32) + ob_ref[rs, :].astype(F32)
        parts = []
        for h in range(HG_HEADS):
            oh = o[:, h * HG_DK:(h + 1) * HG_DK]
            ms = jnp.mean(oh * oh, axis=-1, keepdims=True)
            parts.append(oh * lax.rsqrt(ms + NORM_EPS) * gn)
        hg = hg_ref[rs, :].astype(F32)
        y_rec = jnp.concatenate(parts, axis=-1) * (hg * _sigmoid(hg))
        gate_a = jnp.concatenate([ga0_ref[rs, :], ga1_ref[rs, :]], axis=-1).astype(F32)
        gate_b = jnp.concatenate([gb0_ref[rs, :], gb1_ref[rs, :]], axis=-1).astype(F32)
        merged.append((_sigmoid(gate_a) * y_rec
                       + _sigmoid(gate_b) * ya_ref[rs, :].astype(F32)).astype(BF16))

    w_out = wo_ref[...].astype(BF16)
    xns = []
    for rs, mg in zip(subs, merged):
        h_res = x_ref[rs, :] + jnp.dot(mg, w_out, preferred_element_type=F32)
        h_ref[rs, :] = h_res
        ms = jnp.mean(h_res * h_res, axis=-1, keepdims=True)
        xn = h_res * lax.rsqrt(ms + NORM_EPS) * g2_ref[...]
        _store_tiles(ring.at[slot, rs.start // SUBLANES:rs.stop // SUBLANES], _pack_bf16_pairs(xn))
        xns.append(xn)

    logits = [lax.dot_general(wr_ref[...], xn, NT_DIMS, preferred_element_type=F32,
                              precision=lax.Precision.HIGHEST) + br_ref[...] for xn in xns]
    e_iota = lax.broadcasted_iota(jnp.int32, (N_EXPERTS, sub), 0).astype(F32)
    picked = []
    for rs, vals in zip(subs, logits):
        tops, idxs = [], []
        for _ in range(TOP_K):
            m = jnp.max(vals, axis=0, keepdims=True)
            sel = jnp.min(jnp.where(vals == m, e_iota, float(N_EXPERTS)), axis=0, keepdims=True)
            tops.append(m)
            idxs.append(sel)
            vals = jnp.where(e_iota == sel, -jnp.inf, vals)
        exps = [jnp.exp(t - tops[0]) for t in tops]
        den = exps[0] + exps[1] + exps[2] + exps[3]
        gate_ref[:, rs] = jnp.concatenate([e / den for e in exps], axis=0)
        picked.append((idxs, jnp.where(vals == -jnp.inf, 1.0, 0.0)))

    s_iota = lax.broadcasted_iota(jnp.int32, (sub, sub), 0)
    t_iota = lax.broadcasted_iota(jnp.int32, (sub, sub), 1)
    strict = jnp.where(s_iota < t_iota, 1.0, 0.0).astype(BF16)
    prefixes = [jnp.dot(chosen.astype(BF16), strict, preferred_element_type=F32)
                for _, chosen in picked]
    carry = carry_ref[...]
    for rs, (idxs, chosen), prefix in zip(subs, picked, prefixes):
        prefix = prefix + carry
        dests = [sel * float(cap) + jnp.sum(jnp.where(e_iota == sel, prefix, 0.0), axis=0, keepdims=True)
                 for sel in idxs]
        dest = jnp.concatenate(dests, axis=0).astype(jnp.int32)
        dest_ref[:, rs] = dest
        dest_vmem[slot, :, rs] = dest
        carry = carry + jnp.sum(chosen, axis=1, keepdims=True)
    carry_ref[...] = carry
    counts = jnp.broadcast_to(carry, cnt_ref.shape).astype(jnp.int32)
    cnt_ref[...] = counts

    dest_copy(slot).start()

    @pl.when(i >= 1)
    def _():
        issue_rows(1 - slot)

    @pl.when(i == n_steps - 1)
    def _():
        issue_rows(slot)
        if n_steps >= 2:
            rows_wait(1 - slot)
        rows_wait(slot)
        cnt_vmem[...] = counts
        cp = pltpu.make_async_copy(cnt_vmem, cnt_smem, sem_misc.at[0])
        cp.start()
        cp.wait()
        zero_ref[...] = jnp.zeros_like(zero_ref)

        def tail(e):
            return pltpu.make_async_copy(
                zero_ref, xb_hbm.at[pl.ds(e * cap + cnt_smem[e, 0], MOE_PAD)], sem_misc.at[0])
        for e in range(N_EXPERTS):
            tail(e).start()
        for e in range(N_EXPERTS):
            tail(e).wait()


def _merge_route(o_f, o_b, proj, y_att, x2d, gn_gain, w_out, g2, w_router_t, b_router, tm=512):
    n_tok = x2d.shape[0]
    nblk = n_tok // tm
    row_blk = (tm, D_MODEL)
    half = D_MODEL // 2
    half_blk = (tm, half)

    def rows(c):
        return lambda i: (i, c)

    full = lambda i: (0, 0)
    tok_out = pl.BlockSpec((TOP_K, tm), lambda i: (0, i))
    cap = n_tok + MOE_PAD
    return pl.pallas_call(
        functools.partial(_merge_body, nblk),
        grid=(nblk,),
        in_specs=[
            pl.BlockSpec(row_blk, rows(0)), pl.BlockSpec(row_blk, rows(0)),
            pl.BlockSpec(row_blk, rows(COL_HG // D_MODEL)),
            pl.BlockSpec(half_blk, rows(COL_GA // half)),
            pl.BlockSpec(half_blk, rows(COL_GA // half + 1)),
            pl.BlockSpec(half_blk, rows(COL_GB // half)),
            pl.BlockSpec(half_blk, rows(COL_GB // half + 1)),
            pl.BlockSpec(row_blk, rows(0)), pl.BlockSpec(row_blk, rows(0)),
            pl.BlockSpec((1, HG_DK), full),
            pl.BlockSpec((D_MODEL, D_MODEL), full),
            pl.BlockSpec((1, D_MODEL), full),
            pl.BlockSpec((N_EXPERTS, D_MODEL), full),
            pl.BlockSpec((N_EXPERTS, 1), full),
        ],
        out_specs=[
            pl.BlockSpec(row_blk, rows(0)),
            tok_out, tok_out,
            pl.BlockSpec((N_EXPERTS, LANES), full),
            pl.BlockSpec(memory_space=pl.ANY),
        ],
        out_shape=[
            jax.ShapeDtypeStruct((n_tok, D_MODEL), F32),
            jax.ShapeDtypeStruct((TOP_K, n_tok), F32),
            jax.ShapeDtypeStruct((TOP_K, n_tok), jnp.int32),
            jax.ShapeDtypeStruct((N_EXPERTS, LANES), jnp.int32),
            jax.ShapeDtypeStruct((N_EXPERTS * cap, PACKED_TILES, LANES), jnp.uint32),
        ],
        scratch_shapes=[
            pltpu.VMEM((N_EXPERTS, 1), F32),
            pltpu.VMEM((2, tm // SUBLANES) + PACKED_VIEW, jnp.uint32),
            pltpu.VMEM((2, TOP_K, tm), jnp.int32), pltpu.SMEM((2, TOP_K, tm), jnp.int32),
            pltpu.VMEM((N_EXPERTS, LANES), jnp.int32), pltpu.SMEM((N_EXPERTS, LANES), jnp.int32),
            pltpu.VMEM((MOE_PAD, PACKED_TILES, LANES), jnp.uint32),
            pltpu.SemaphoreType.DMA((2,)), pltpu.SemaphoreType.DMA((2,)), pltpu.SemaphoreType.DMA((1,)),
        ],
        compiler_params=pltpu.CompilerParams(
            dimension_semantics=("arbitrary",), vmem_limit_bytes=VMEM_LIMIT),
        name="merge_route",
    )(o_f, o_b, proj, proj, proj, proj, proj, y_att, x2d, gn_gain.reshape(1, HG_DK), w_out,
      g2.reshape(1, D_MODEL), w_router_t, b_router.reshape(N_EXPERTS, 1))


ISSUE_UNROLL = 2


def _expert_body(bend_ref, bgrp_ref, bshort_ref, x_hbm, wug_ref, bug_ref, wd_ref, bd_ref, y_hbm,
                 wug_bf, wd_bf, act_ref, xbuf, ybuf, sem_in, sem_out):
    e = pl.program_id(0)
    n_used = bend_ref[N_EXPERTS - 1]
    b_first = jnp.where(e == 0, 0, bend_ref[jnp.maximum(e - 1, 0)])
    b_last = bend_ref[e]

    def for_size(b, fn):
        short = bshort_ref[b] == 1

        @pl.when(short)
        def _():
            fn(MOE_PAD // SUBLANES)

        @pl.when(jnp.logical_not(short))
        def _():
            fn(MOE_ROWS // SUBLANES)

    def x_copies(b, slot, groups):
        return [pltpu.make_async_copy(x_hbm.at[pl.ds(bgrp_ref[b], groups), s],
                                      xbuf.at[slot, pl.ds(0, groups), :, s, :], sem_in.at[slot])
                for s in range(SUBLANES)]

    def y_copies(b, slot, groups):
        return [pltpu.make_async_copy(ybuf.at[slot, pl.ds(0, groups), :, s, :],
                                      y_hbm.at[pl.ds(bgrp_ref[b], groups), s], sem_out.at[slot])
                for s in range(SUBLANES)]

    def start_all(copies):
        for cp in copies:
            cp.start(priority=BLOCK_DMA_PRIORITY)

    def wait_all(copies):
        for cp in copies:
            cp.wait()

    @pl.when((e == 0) & (n_used > 0))
    def _():
        for_size(0, lambda groups: start_all(x_copies(0, 0, groups)))

    ns = D_FF // FF_SLAB
    for s in range(ns):
        lo, hi = s * FF_SLAB, (s + 1) * FF_SLAB
        wug_bf[:, 2 * lo:2 * lo + FF_SLAB] = wug_ref[0, :, lo:hi].astype(BF16)
        wug_bf[:, 2 * lo + FF_SLAB:2 * hi] = wug_ref[0, :, D_FF + lo:D_FF + hi].astype(BF16)
    wd_bf[...] = wd_ref[0].astype(BF16)
    b_ug = bug_ref[0]
    b_d = bd_ref[0]

    def mlp(slot, groups):
        rows = groups * SUBLANES
        x = _unpack_bf16_pairs(_load_tiles(xbuf.at[slot, pl.ds(0, groups)]))
        for s in range(ns):
            lo, hi = s * FF_SLAB, (s + 1) * FF_SLAB
            hu = jnp.dot(x, wug_bf[:, 2 * lo:2 * hi], preferred_element_type=F32)
            g = jnp.minimum(hu[:, :FF_SLAB] + b_ug[:, lo:hi], SWIGLU_LIMIT)
            u = jnp.clip(hu[:, FF_SLAB:] + b_ug[:, D_FF + lo:D_FF + hi], -SWIGLU_LIMIT, SWIGLU_LIMIT)
            act_ref[0:rows, lo:hi] = (g * _sigmoid(SWIGLU_ALPHA * g) * (u + 1.0)).astype(BF16)
        _store_tiles(ybuf.at[slot, pl.ds(0, groups)],
                     jnp.dot(act_ref[0:rows, :], wd_bf[...], preferred_element_type=F32) + b_d)

    def block(b, carry):
        slot = lax.rem(b, 2)
        for_size(b, lambda groups: wait_all(x_copies(b, slot, groups)))

        @pl.when(b + 1 < n_used)
        def _():
            for_size(b + 1, lambda groups: start_all(x_copies(b + 1, 1 - slot, groups)))

        @pl.when(b >= 2)
        def _():
            for_size(b - 2, lambda groups: wait_all(y_copies(b - 2, slot, groups)))

        def run(groups):
            mlp(slot, groups)
            start_all(y_copies(b, slot, groups))
        for_size(b, run)
        return carry

    lax.fori_loop(b_first, b_last, block, 0)

    @pl.when(e == N_EXPERTS - 1)
    def _():
        for back in (2, 1):
            @pl.when(n_used >= back)
            def _():
                last = n_used - back
                for_size(last, lambda groups: wait_all(y_copies(last, lax.rem(last, 2), groups)))


def _experts(xb, block_end, block_group, block_short, w_up_gate, b_up_gate, w_down, b_down):
    groups = MOE_ROWS // SUBLANES

    def w_map(e, bend, bgrp, bshort):
        return (e, 0, 0)

    grid_spec = pltpu.PrefetchScalarGridSpec(
        num_scalar_prefetch=3,
        grid=(N_EXPERTS,),
        in_specs=[
            pl.BlockSpec(memory_space=pl.ANY),
            pl.BlockSpec((1, D_MODEL, 2 * D_FF), w_map),
            pl.BlockSpec((1, 1, 2 * D_FF), w_map),
            pl.BlockSpec((1, D_FF, D_MODEL), w_map),
            pl.BlockSpec((1, 1, D_MODEL), w_map),
        ],
        out_specs=pl.BlockSpec(memory_space=pl.ANY),
        scratch_shapes=[
            pltpu.VMEM((D_MODEL, 2 * D_FF), BF16), pltpu.VMEM((D_FF, D_MODEL), BF16),
            pltpu.VMEM((MOE_ROWS, D_FF), BF16),
            pltpu.VMEM((2, groups) + PACKED_VIEW, jnp.uint32), pltpu.VMEM((2, groups) + TILE_VIEW, F32),
            pltpu.SemaphoreType.DMA((2,)), pltpu.SemaphoreType.DMA((2,)),
        ],
    )
    return pl.pallas_call(
        _expert_body,
        grid_spec=grid_spec,
        out_shape=jax.ShapeDtypeStruct((xb.shape[0],) + TOKEN_TILES, F32),
        compiler_params=pltpu.CompilerParams(
            dimension_semantics=("arbitrary",), vmem_limit_bytes=VMEM_LIMIT),
        name="moe_experts",
    )(block_end, block_group, block_short, xb, w_up_gate, b_up_gate.reshape(N_EXPERTS, 1, -1),
      w_down, b_down.reshape(N_EXPERTS, 1, -1))


COMBINE_TOKENS = 128


def _combine_body(dest_ref, yb_hbm, h_ref, gate_ref, gain_ref, o_ref, buf_ref, sem):
    i = pl.program_id(0)
    n_steps = pl.num_programs(0)
    nt = COMBINE_TOKENS

    def issue(step, slot):
        base = step * nt

        def body(g, c):
            for s in range(SUBLANES):
                for k in range(TOP_K):
                    pltpu.make_async_copy(
                        yb_hbm.at[dest_ref[k * (dest_ref.shape[0] // TOP_K) + base + g * SUBLANES + s]],
                        buf_ref.at[slot, k, g, :, s, :], sem.at[slot]).start(priority=k % 2)
            return c
        lax.fori_loop(0, nt // SUBLANES, body, 0, unroll=ISSUE_UNROLL)

    slot = lax.rem(i, 2)

    @pl.when(i == 0)
    def _():
        issue(0, 0)

    @pl.when(i + 1 < n_steps)
    def _():
        issue(i + 1, 1 - slot)

    pltpu.make_async_copy(buf_ref.at[slot], buf_ref.at[slot], sem.at[slot]).wait()

    gates = gate_ref[...]
    acc = h_ref[...]
    for k in range(TOP_K):
        acc = acc + gates[:, k:k + 1] * _load_tiles(buf_ref.at[slot, k])
    ms = jnp.mean(acc * acc, axis=-1, keepdims=True)
    o_ref[...] = acc * lax.rsqrt(ms + NORM_EPS) * gain_ref[...]


def _combine(yb, dest_flat, h_res, gates_tk, final_gain):
    n_tok = h_res.shape[0]
    nt = COMBINE_TOKENS
    grid_spec = pltpu.PrefetchScalarGridSpec(
        num_scalar_prefetch=1,
        grid=(n_tok // nt,),
        in_specs=[
            pl.BlockSpec(memory_space=pl.ANY),
            pl.BlockSpec((nt, D_MODEL), lambda i, d: (i, 0)),
            pl.BlockSpec((nt, TOP_K), lambda i, d: (i, 0)),
            pl.BlockSpec((1, D_MODEL), lambda i, d: (0, 0)),
        ],
        out_specs=pl.BlockSpec((nt, D_MODEL), lambda i, d: (i, 0)),
        scratch_shapes=[pltpu.VMEM((2, TOP_K, nt // SUBLANES) + TILE_VIEW, F32),
                        pltpu.SemaphoreType.DMA((2,))],
    )
    return pl.pallas_call(
        _combine_body,
        grid_spec=grid_spec,
        out_shape=jax.ShapeDtypeStruct((n_tok, D_MODEL), F32),
        compiler_params=pltpu.CompilerParams(
            dimension_semantics=("arbitrary",), vmem_limit_bytes=VMEM_LIMIT),
        name="moe_combine",
    )(dest_flat, yb, h_res, gates_tk, final_gain.reshape(1, D_MODEL))


def kernel(x, norm_mix_gain, w_in, hg_lb_fwd, hg_lb_bwd, hg_norm_gain, attn_sink, w_out,
           norm_ffn_gain, w_router, b_router, w_up_gate, b_up_gate, w_down, b_down,
           final_norm_gain):
    bsz, seqlen, d = x.shape
    n_tok = bsz * seqlen
    x2d = x.reshape(n_tok, d)

    lb_fwd = jax.nn.softmax(hg_lb_fwd.astype(F32), axis=0)[0]
    lb_bwd = jax.nn.softmax(hg_lb_bwd.astype(F32), axis=0)[0]

    proj = _inproj(x2d, norm_mix_gain[0], w_in[0])
    o_f, o_b = _hgrn2(proj, lb_fwd, lb_bwd, bsz, seqlen)
    y_att = _attention(proj, attn_sink[0], bsz, seqlen)
    h_res, gate_t, dest_t, counts, xb = _merge_route(
        o_f, o_b, proj, y_att, x2d, hg_norm_gain[0], w_out[0],
        norm_ffn_gain[0], w_router[0].T.astype(F32), b_router[0].astype(F32))

    n_rows = xb.shape[0]
    cap = n_rows // N_EXPERTS
    counts = counts[:, 0]
    padded = (counts + MOE_PAD - 1) // MOE_PAD * MOE_PAD
    full = padded // MOE_ROWS
    blocks = full + (padded % MOE_ROWS) // MOE_PAD
    block_end = jnp.cumsum(blocks).astype(jnp.int32)
    n_blocks_max = n_tok * TOP_K // MOE_PAD + N_EXPERTS
    b = jnp.arange(n_blocks_max, dtype=jnp.int32)
    owner = (block_end[None, :] <= b[:, None]).astype(jnp.int32)
    expert_of = jnp.minimum(jnp.sum(owner, axis=-1), N_EXPERTS - 1)
    within = b - jnp.sum(owner * blocks[None, :], axis=-1)
    is_owner = expert_of[:, None] == jnp.arange(N_EXPERTS, dtype=jnp.int32)[None, :]
    block_short = (within == jnp.sum(jnp.where(is_owner, full[None, :], 0), axis=-1)).astype(jnp.int32)
    block_group = ((expert_of * cap + within * MOE_ROWS) // SUBLANES).astype(jnp.int32)
    dest_flat = dest_t.reshape(-1)

    yb = _experts(xb.reshape((n_rows // SUBLANES,) + PACKED_TOKEN_TILES), block_end, block_group, block_short,
                  w_up_gate[0], b_up_gate[0], w_down[0], b_down[0])
    out = _combine(yb.reshape(n_rows, LANE_TILES, LANES), dest_flat, h_res, gate_t.T, final_norm_gain)
    return out.reshape(bsz, seqlen, d)
```

```python
import functools

import jax
import jax.numpy as jnp
import numpy as np
from jax import lax
from jax.experimental import pallas as pl
from jax.experimental.pallas import tpu as pltpu

F32 = jnp.float32
BF16 = jnp.bfloat16

D_MODEL = 1024
HG_HEADS = 8
HG_DK = 128
HG_CHUNK = 128
HG_GROUP = 8
AT_Q_HEADS = 16
AT_KV_HEADS = 4
AT_GROUP = AT_Q_HEADS // AT_KV_HEADS
AT_HEAD_DIM = 64
WINDOW = 128
AT_BLOCK = WINDOW
AT_KVWIDTH = AT_KV_HEADS * AT_HEAD_DIM
N_EXPERTS = 32
TOP_K = 4
D_FF = 1024
SWIGLU_LIMIT = 7.0
SWIGLU_ALPHA = 1.702
NORM_EPS = 1e-5
IN_WIDTH = 8704

COL_HQ, COL_FF, COL_FB, COL_HI, COL_HG = 0, 1024, 2048, 3072, 4096
COL_AQ, COL_AK, COL_AV, COL_GA, COL_GB = 5120, 6144, 6400, 6656, 7680

MOE_ROWS = 1024
MOE_PAD = 512
FF_SLAB = 256
MERGE_SUB = 256
BLOCK_DMA_PRIORITY = 1
VMEM_LIMIT = 56 * 1024 * 1024

NT_DIMS = (((1,), (1,)), ((), ()))

SUBLANES = 8
LANES = 128
LANE_TILES = D_MODEL // LANES
TILE_VIEW = (LANE_TILES, SUBLANES, LANES)
TOKEN_TILES = (SUBLANES, LANE_TILES, LANES)


def _store_tiles(ref, val):
    groups = val.shape[0] // SUBLANES
    for t in range(val.shape[1] // LANES):
        ref[:, t] = val[:, t * LANES:(t + 1) * LANES].reshape(groups, SUBLANES, LANES)


def _load_tiles(ref):
    groups = ref.shape[0]
    return jnp.concatenate(
        [ref[:, t].reshape(groups * SUBLANES, LANES) for t in range(ref.shape[1])], axis=1)


PACKED_TILES = LANE_TILES // 2
PACKED_VIEW = (PACKED_TILES, SUBLANES, LANES)
PACKED_TOKEN_TILES = (SUBLANES, PACKED_TILES, LANES)


def _pack_bf16_pairs(x):
    half = x.shape[1] // 2
    lo = pltpu.bitcast(x[:, :half].astype(BF16).astype(F32), jnp.uint32)
    hi = pltpu.bitcast(x[:, half:].astype(BF16).astype(F32), jnp.uint32)
    return hi | lax.shift_right_logical(lo, jnp.uint32(16))


def _unpack_bf16_pairs(w):
    lo = pltpu.bitcast(lax.shift_left(w, jnp.uint32(16)), F32)
    hi = pltpu.bitcast(w & jnp.uint32(0xFFFF0000), F32)
    return jnp.concatenate([lo, hi], axis=1).astype(BF16)


def _sigmoid(x):
    return 0.5 * jnp.tanh(0.5 * x) + 0.5


def _inproj_body(x_ref, g_ref, w_ref, o_ref, xn_ref):
    @pl.when(pl.program_id(1) == 0)
    def _():
        x = x_ref[...]
        ms = jnp.mean(x * x, axis=-1, keepdims=True)
        xn_ref[...] = (x * lax.rsqrt(ms + NORM_EPS) * g_ref[...]).astype(BF16)

    o_ref[...] = jnp.dot(xn_ref[...], w_ref[...].astype(BF16),
                         preferred_element_type=F32).astype(o_ref.dtype)


def _inproj(x2d, gain, w, tm=1024, tn=2176):
    n_tok = x2d.shape[0]
    n_out = w.shape[1]
    return pl.pallas_call(
        _inproj_body,
        grid=(n_tok // tm, n_out // tn),
        in_specs=[
            pl.BlockSpec((tm, D_MODEL), lambda i, j: (i, 0)),
            pl.BlockSpec((1, D_MODEL), lambda i, j: (0, 0)),
            pl.BlockSpec((D_MODEL, tn), lambda i, j: (0, j)),
        ],
        out_specs=pl.BlockSpec((tm, tn), lambda i, j: (i, j)),
        out_shape=jax.ShapeDtypeStruct((n_tok, n_out), BF16),
        scratch_shapes=[pltpu.VMEM((tm, D_MODEL), BF16)],
        compiler_params=pltpu.CompilerParams(
            dimension_semantics=("parallel", "arbitrary"), vmem_limit_bytes=VMEM_LIMIT),
        name="inproj",
    )(x2d, gain.reshape(1, D_MODEL), w)


class _Dir:
    def __init__(self, q_ref, f_ref, v_ref, o_ref, st_ref, lb, tri, mask, mid_row, last_row):
        self.q_ref, self.f_ref, self.v_ref, self.o_ref, self.st_ref = q_ref, f_ref, v_ref, o_ref, st_ref
        self.c0, self.c1 = 0.5 * (1.0 + lb), 0.5 * (1.0 - lb)
        self.tri, self.mask, self.mid_row, self.last_row = tri, mask, mid_row, last_row


def _gla_group(chains):
    c = HG_CHUNK
    dk = HG_DK
    pre = []
    for d, r0 in chains:
        h = 0.5 * d.q_ref[pl.ds(r0, c), :].astype(F32)
        q = h * jnp.tanh(h) + h
        ct = d.c1 * jnp.tanh(0.5 * d.f_ref[pl.ds(r0, c), :].astype(F32))
        logf = jnp.log2(d.c0 + ct)
        hi = logf.astype(BF16)
        lo = (logf - hi.astype(F32)).astype(BF16)
        pre.append((q, (1.0 - d.c0) - ct, jnp.concatenate([hi, lo], axis=1)))
    cums = []
    for (d, _), (_, _, hl) in zip(chains, pre):
        r = jnp.dot(d.tri, hl, preferred_element_type=F32)
        cums.append(r[:, :dk] + r[:, dk:])
    mids = []
    for (d, r0), (q, k, _), cum in zip(chains, pre, cums):
        mid = cum[d.mid_row:d.mid_row + 1, :]
        last = cum[d.last_row:d.last_row + 1, :]
        a = cum - mid
        qa = q * jnp.exp2(a)
        ka = k * jnp.exp2(-a)
        qd = (qa * jnp.exp2(mid)).astype(BF16)
        kend = (ka * jnp.exp2(last - mid)).astype(BF16)
        dec = jnp.exp2(last)
        scores = lax.dot_general(qa.astype(BF16), ka.astype(BF16), NT_DIMS,
                                 preferred_element_type=F32)
        mids.append((qd, kend, dec, scores))
    outs = []
    for (d, r0), (qd, kend, dec, scores) in zip(chains, mids):
        v = d.v_ref[pl.ds(r0, c), :]
        sm = jnp.where(d.mask, scores, 0.0).astype(BF16)
        o_intra = jnp.dot(sm, v, preferred_element_type=F32)
        upd = jnp.dot(jnp.transpose(v.astype(F32)).astype(BF16), kend, preferred_element_type=F32)
        outs.append((o_intra, upd))
    for (d, r0), (qd, _, dec, _), (o_intra, upd) in zip(chains, mids, outs):
        st = d.st_ref[...]
        d.o_ref[pl.ds(r0, c), :] = (o_intra + lax.dot_general(
            qd, st.astype(BF16), NT_DIMS, preferred_element_type=F32)).astype(d.o_ref.dtype)
        d.st_ref[...] = st * dec + upd


def _hgrn2_body(qf_ref, ff_ref, vf_ref, qb_ref, fb_ref, vb_ref, lbf_ref, lbb_ref,
                of_ref, ob_ref, sf_ref, sb_ref):
    @pl.when(pl.program_id(2) == 0)
    def _():
        sf_ref[...] = jnp.zeros_like(sf_ref)
        sb_ref[...] = jnp.zeros_like(sb_ref)

    c = HG_CHUNK
    g = HG_GROUP
    n_chunks = qf_ref.shape[0] // c
    row = lax.broadcasted_iota(jnp.int32, (c, c), 0)
    col = lax.broadcasted_iota(jnp.int32, (c, c), 1)
    lower = col <= row
    upper = col >= row
    fwd = _Dir(qf_ref, ff_ref, vf_ref, of_ref, sf_ref, lbf_ref[...],
               jnp.where(lower, 1.0, 0.0).astype(BF16), lower, c // 2 - 1, c - 1)
    bwd = _Dir(qb_ref, fb_ref, vb_ref, ob_ref, sb_ref, lbb_ref[...],
               jnp.where(upper, 1.0, 0.0).astype(BF16), upper, c // 2, 0)

    def step(n, carry):
        chains = []
        for u in range(g):
            chains.append((fwd, pl.multiple_of((n * g + u) * c, c)))
        for u in range(g):
            chains.append((bwd, pl.multiple_of((n_chunks - 1 - n * g - u) * c, c)))
        _gla_group(chains)
        return carry

    lax.fori_loop(0, n_chunks // g, step, 0)


def _hgrn2(proj, lb_fwd, lb_bwd, bsz, seqlen, rows=4096):
    n_tok = bsz * seqlen
    rows = min(rows, seqlen)
    nb = seqlen // rows
    dk = HG_DK

    def fwd_map(col0):
        return lambda b, h, j: (b * nb + j, col0 // dk + h)

    def bwd_map(col0):
        return lambda b, h, j: (b * nb + (nb - 1 - j), col0 // dk + h)

    blk = (rows, dk)
    lb_spec = pl.BlockSpec((1, dk), lambda b, h, j: (0, h))
    return pl.pallas_call(
        _hgrn2_body,
        grid=(bsz, HG_HEADS, nb),
        in_specs=[
            pl.BlockSpec(blk, fwd_map(COL_HQ)), pl.BlockSpec(blk, fwd_map(COL_FF)),
            pl.BlockSpec(blk, fwd_map(COL_HI)),
            pl.BlockSpec(blk, bwd_map(COL_HQ)), pl.BlockSpec(blk, bwd_map(COL_FB)),
            pl.BlockSpec(blk, bwd_map(COL_HI)),
            lb_spec, lb_spec,
        ],
        out_specs=[pl.BlockSpec(blk, fwd_map(0)), pl.BlockSpec(blk, bwd_map(0))],
        out_shape=[jax.ShapeDtypeStruct((n_tok, HG_HEADS * dk), BF16)] * 2,
        scratch_shapes=[pltpu.VMEM((dk, dk), F32), pltpu.VMEM((dk, dk), F32)],
        compiler_params=pltpu.CompilerParams(
            dimension_semantics=("parallel", "parallel", "arbitrary"),
            vmem_limit_bytes=VMEM_LIMIT),
        name="hgrn2",
    )(proj, proj, proj, proj, proj, proj, lb_fwd.reshape(1, -1), lb_bwd.reshape(1, -1))


def _window_attention(sink_ref, q_ref, k_refs, v_refs, starts_sequence, ends_sequence):
    blk = AT_BLOCK
    span = 3 * blk
    dh = AT_HEAD_DIM
    lanes = 2 * dh
    n_q = q_ref.shape[0] // blk
    lo_half = lax.broadcasted_iota(jnp.int32, (blk, lanes), 1) < dh
    row = lax.broadcasted_iota(jnp.int32, (blk, span), 0)
    col = lax.broadcasted_iota(jnp.int32, (blk, span), 1)
    absrel = jnp.abs(row + blk - col)
    base_inner = jnp.where(absrel <= WINDOW, -absrel.astype(F32), -jnp.inf)

    def padded(t, g):
        t = t[:, (g // 2) * lanes:(g // 2 + 1) * lanes].astype(F32)
        if g % 2 == 0:
            lo = jnp.where(lo_half, t, 0.0)
            hi = pltpu.roll(lo, dh, axis=1)
        else:
            hi = jnp.where(lo_half, 0.0, t)
            lo = pltpu.roll(hi, dh, axis=1)
        return lo.astype(BF16), hi.astype(BF16)

    def key_blocks(refs):
        prev_ref, own_ref, next_ref = refs
        return ([prev_ref[...]] + [own_ref[b * blk:(b + 1) * blk, :] for b in range(n_q)]
                + [next_ref[...]])

    k_pads = [[padded(t, g) for t in key_blocks(k_refs)] for g in range(AT_KV_HEADS)]
    v_pads = [[padded(t, g) for t in key_blocks(v_refs)] for g in range(AT_KV_HEADS)]

    def banded(pads, first):
        use = pads[first:first + 3]
        return jnp.concatenate([p[0] for p in use] + [p[1] for p in use], axis=0)

    results = []
    for qb in range(n_q):
        base = base_inner
        if qb == 0:
            base = jnp.where(jnp.logical_and(starts_sequence, col < blk), -jnp.inf, base)
        if qb == n_q - 1:
            base = jnp.where(jnp.logical_and(ends_sequence, col >= 2 * blk), -jnp.inf, base)
        rows = slice(qb * blk, (qb + 1) * blk)
        scores = []
        for g in range(AT_KV_HEADS):
            q2 = q_ref[rows, g * 2 * lanes:(g + 1) * 2 * lanes] * (dh ** -0.5)
            lhs = jnp.concatenate([q2[:, :lanes], q2[:, lanes:]], axis=0).astype(BF16)
            scores.append(lax.dot_general(lhs, banded(k_pads[g], qb), NT_DIMS,
                                          preferred_element_type=F32))
        probs, denoms = [], []
        for g in range(AT_KV_HEADS):
            p_rows, d_rows = [], []
            for r in range(2):
                p_cols, d_cols = [], []
                for c in range(2):
                    h = g * AT_GROUP + 2 * r + c
                    slope = 2.0 ** (-8.0 * (h + 1) / AT_Q_HEADS)
                    sink = sink_ref[h]
                    s = scores[g][r * blk:(r + 1) * blk, c * span:(c + 1) * span] + slope * base
                    m = jnp.maximum(jnp.max(s, axis=-1, keepdims=True), sink)
                    p = jnp.exp(s - m)
                    d_cols.append(jnp.sum(p, axis=-1, keepdims=True) + jnp.exp(sink - m))
                    p_cols.append(p.astype(BF16))
                p_rows.append(jnp.concatenate(p_cols, axis=1))
                d_rows.append(jnp.where(lo_half, d_cols[0], d_cols[1]))
            probs.append(jnp.concatenate(p_rows, axis=0))
            denoms.append(d_rows)
        outs = []
        for g in range(AT_KV_HEADS):
            o = jnp.dot(probs[g], banded(v_pads[g], qb), preferred_element_type=F32)
            for r in range(2):
                outs.append(o[r * blk:(r + 1) * blk, :] / denoms[g][r])
        results.append(jnp.concatenate(outs, axis=-1))
    return results


def _merge_body(n_steps, steps_per_seq, sink_ref, of_ref, ob_ref, hg_ref, ga0_ref, ga1_ref, gb0_ref,
                gb1_ref, q_ref, kp_ref, ko_ref, kn_ref, vp_ref, vo_ref, vn_ref, x_ref,
                gn_ref, wo_ref, g2_ref, wr_ref, br_ref,
                h_ref, gate_ref, dest_ref, cnt_ref, xb_hbm,
                carry_ref, ring, dest_vmem, dest_smem, cnt_vmem, cnt_smem, zero_ref,
                sem_rows, sem_dest, sem_misc):
    i = pl.program_id(0)
    slot = lax.rem(i, 2)
    tm = x_ref.shape[0]
    cap = xb_hbm.shape[0] // N_EXPERTS

    def rows_wait(sl):
        pltpu.make_async_copy(xb_hbm.at[pl.ds(0, tm * TOP_K)], xb_hbm.at[pl.ds(0, tm * TOP_K)],
                              sem_rows.at[sl]).wait()

    def dest_copy(sl):
        return pltpu.make_async_copy(dest_vmem.at[sl], dest_smem.at[sl], sem_dest.at[sl])

    def issue_rows(sl):
        dest_copy(sl).wait()

        def body(g, c):
            for s in range(SUBLANES):
                for k in range(TOP_K):
                    pltpu.make_async_copy(
                        ring.at[sl, g, :, s, :], xb_hbm.at[dest_smem[sl, k, g * SUBLANES + s]],
                        sem_rows.at[sl]).start(priority=k % 2)
            return c
        lax.fori_loop(0, tm // SUBLANES, body, 0, unroll=ISSUE_UNROLL)

    @pl.when(i == 0)
    def _():
        carry_ref[...] = jnp.zeros_like(carry_ref)

    @pl.when(i >= 2)
    def _():
        rows_wait(slot)

    sub = MERGE_SUB
    subs = [slice(s0, s0 + sub) for s0 in range(0, x_ref.shape[0], sub)]
    gn = gn_ref[...]

    step_in_seq = lax.rem(i, steps_per_seq)
    att_blocks = _window_attention(
        sink_ref, q_ref, (kp_ref, ko_ref, kn_ref), (vp_ref, vo_ref, vn_ref),
        step_in_seq == 0, step_in_seq == steps_per_seq - 1)
    per_sub = sub // AT_BLOCK
    y_att = [jnp.concatenate(att_blocks[n * per_sub:(n + 1) * per_sub], axis=0)
             for n in range(len(subs))]

    merged = []
    for rs, ya in zip(subs, y_att):
        o = of_ref[rs, :].astype(F32) + ob_ref[rs, :].astype(F32)
        parts = []
        for h in range(HG_HEADS):
            oh = o[:, h * HG_DK:(h + 1) * HG_DK]
            ms = jnp.mean(oh * oh, axis=-1, keepdims=True)
            parts.append(oh * lax.rsqrt(ms + NORM_EPS) * gn)
        hg = hg_ref[rs, :].astype(F32)
        y_rec = jnp.concatenate(parts, axis=-1) * (hg * _sigmoid(hg))
        gate_a = jnp.concatenate([ga0_ref[rs, :], ga1_ref[rs, :]], axis=-1).astype(F32)
        gate_b = jnp.concatenate([gb0_ref[rs, :], gb1_ref[rs, :]], axis=-1).astype(F32)
        merged.append((_sigmoid(gate_a) * y_rec
                       + _sigmoid(gate_b) * ya).astype(BF16))

    w_out = wo_ref[...].astype(BF16)
    xns = []
    for rs, mg in zip(subs, merged):
        h_res = x_ref[rs, :] + jnp.dot(mg, w_out, preferred_element_type=F32)
        h_ref[rs, :] = h_res
        ms = jnp.mean(h_res * h_res, axis=-1, keepdims=True)
        xn = h_res * lax.rsqrt(ms + NORM_EPS) * g2_ref[...]
        _store_tiles(ring.at[slot, rs.start // SUBLANES:rs.stop // SUBLANES], _pack_bf16_pairs(xn))
        xns.append(xn)

    logits = [lax.dot_general(wr_ref[...], xn, NT_DIMS, preferred_element_type=F32,
                              precision=lax.Precision.HIGHEST) + br_ref[...] for xn in xns]
    e_iota = lax.broadcasted_iota(jnp.int32, (N_EXPERTS, sub), 0).astype(F32)
    picked = []
    for rs, vals in zip(subs, logits):
        tops, idxs = [], []
        for _ in range(TOP_K):
            m = jnp.max(vals, axis=0, keepdims=True)
            sel = jnp.min(jnp.where(vals == m, e_iota, float(N_EXPERTS)), axis=0, keepdims=True)
            tops.append(m)
            idxs.append(sel)
            vals = jnp.where(e_iota == sel, -jnp.inf, vals)
        exps = [jnp.exp(t - tops[0]) for t in tops]
        den = exps[0] + exps[1] + exps[2] + exps[3]
        gate_ref[:, rs] = jnp.concatenate([e / den for e in exps], axis=0)
        picked.append((idxs, jnp.where(vals == -jnp.inf, 1.0, 0.0)))

    s_iota = lax.broadcasted_iota(jnp.int32, (sub, sub), 0)
    t_iota = lax.broadcasted_iota(jnp.int32, (sub, sub), 1)
    strict = jnp.where(s_iota < t_iota, 1.0, 0.0).astype(BF16)
    prefixes = [jnp.dot(chosen.astype(BF16), strict, preferred_element_type=F32)
                for _, chosen in picked]
    carry = carry_ref[...]
    for rs, (idxs, chosen), prefix in zip(subs, picked, prefixes):
        prefix = prefix + carry
        dests = [sel * float(cap) + jnp.sum(jnp.where(e_iota == sel, prefix, 0.0), axis=0, keepdims=True)
                 for sel in idxs]
        dest = jnp.concatenate(dests, axis=0).astype(jnp.int32)
        dest_ref[:, rs] = dest
        dest_vmem[slot, :, rs] = dest
        carry = carry + jnp.sum(chosen, axis=1, keepdims=True)
    carry_ref[...] = carry
    counts = jnp.broadcast_to(carry, cnt_ref.shape).astype(jnp.int32)
    cnt_ref[...] = counts

    dest_copy(slot).start()

    @pl.when(i >= 1)
    def _():
        issue_rows(1 - slot)

    @pl.when(i == n_steps - 1)
    def _():
        issue_rows(slot)
        if n_steps >= 2:
            rows_wait(1 - slot)
        rows_wait(slot)
        cnt_vmem[...] = counts
        cp = pltpu.make_async_copy(cnt_vmem, cnt_smem, sem_misc.at[0])
        cp.start()
        cp.wait()
        zero_ref[...] = jnp.zeros_like(zero_ref)

        def tail(e):
            return pltpu.make_async_copy(
                zero_ref, xb_hbm.at[pl.ds(e * cap + cnt_smem[e, 0], MOE_PAD)], sem_misc.at[0])
        for e in range(N_EXPERTS):
            tail(e).start()
        for e in range(N_EXPERTS):
            tail(e).wait()


def _merge_route(o_f, o_b, proj, sink, x2d, seqlen, gn_gain, w_out, g2, w_router_t, b_router, tm=512):
    n_tok = x2d.shape[0]
    nblk = n_tok // tm
    row_blk = (tm, D_MODEL)
    half = D_MODEL // 2
    half_blk = (tm, half)

    def rows(c):
        return lambda i: (i, c)

    per_step = tm // AT_BLOCK
    n_kv_blocks = n_tok // AT_BLOCK
    kv_blk = (AT_BLOCK, AT_KVWIDTH)
    kv_own = (tm, AT_KVWIDTH)

    def kv_prev(c):
        return lambda i: (jnp.maximum(i * per_step - 1, 0), c)

    def kv_next(c):
        return lambda i: (jnp.minimum((i + 1) * per_step, n_kv_blocks - 1), c)

    kcol = COL_AK // AT_KVWIDTH
    vcol = COL_AV // AT_KVWIDTH
    full = lambda i: (0, 0)
    tok_out = pl.BlockSpec((TOP_K, tm), lambda i: (0, i))
    cap = n_tok + MOE_PAD
    return pl.pallas_call(
        functools.partial(_merge_body, nblk, seqlen // tm),
        grid=(nblk,),
        in_specs=[
            pl.BlockSpec(memory_space=pltpu.SMEM),
            pl.BlockSpec(row_blk, rows(0)), pl.BlockSpec(row_blk, rows(0)),
            pl.BlockSpec(row_blk, rows(COL_HG // D_MODEL)),
            pl.BlockSpec(half_blk, rows(COL_GA // half)),
            pl.BlockSpec(half_blk, rows(COL_GA // half + 1)),
            pl.BlockSpec(half_blk, rows(COL_GB // half)),
            pl.BlockSpec(half_blk, rows(COL_GB // half + 1)),
            pl.BlockSpec(row_blk, rows(COL_AQ // D_MODEL)),
            pl.BlockSpec(kv_blk, kv_prev(kcol)), pl.BlockSpec(kv_own, rows(kcol)),
            pl.BlockSpec(kv_blk, kv_next(kcol)),
            pl.BlockSpec(kv_blk, kv_prev(vcol)), pl.BlockSpec(kv_own, rows(vcol)),
            pl.BlockSpec(kv_blk, kv_next(vcol)),
            pl.BlockSpec(row_blk, rows(0)),
            pl.BlockSpec((1, HG_DK), full),
            pl.BlockSpec((D_MODEL, D_MODEL), full),
            pl.BlockSpec((1, D_MODEL), full),
            pl.BlockSpec((N_EXPERTS, D_MODEL), full),
            pl.BlockSpec((N_EXPERTS, 1), full),
        ],
        out_specs=[
            pl.BlockSpec(row_blk, rows(0)),
            tok_out, tok_out,
            pl.BlockSpec((N_EXPERTS, LANES), full),
            pl.BlockSpec(memory_space=pl.ANY),
        ],
        out_shape=[
            jax.ShapeDtypeStruct((n_tok, D_MODEL), F32),
            jax.ShapeDtypeStruct((TOP_K, n_tok), F32),
            jax.ShapeDtypeStruct((TOP_K, n_tok), jnp.int32),
            jax.ShapeDtypeStruct((N_EXPERTS, LANES), jnp.int32),
            jax.ShapeDtypeStruct((N_EXPERTS * cap, PACKED_TILES, LANES), jnp.uint32),
        ],
        scratch_shapes=[
            pltpu.VMEM((N_EXPERTS, 1), F32),
            pltpu.VMEM((2, tm // SUBLANES) + PACKED_VIEW, jnp.uint32),
            pltpu.VMEM((2, TOP_K, tm), jnp.int32), pltpu.SMEM((2, TOP_K, tm), jnp.int32),
            pltpu.VMEM((N_EXPERTS, LANES), jnp.int32), pltpu.SMEM((N_EXPERTS, LANES), jnp.int32),
            pltpu.VMEM((MOE_PAD, PACKED_TILES, LANES), jnp.uint32),
            pltpu.SemaphoreType.DMA((2,)), pltpu.SemaphoreType.DMA((2,)), pltpu.SemaphoreType.DMA((1,)),
        ],
        compiler_params=pltpu.CompilerParams(
            dimension_semantics=("arbitrary",), vmem_limit_bytes=VMEM_LIMIT),
        name="merge_route",
    )(sink.astype(F32), o_f, o_b, proj, proj, proj, proj, proj, proj, proj, proj, proj, proj, proj, proj,
      x2d, gn_gain.reshape(1, HG_DK), w_out,
      g2.reshape(1, D_MODEL), w_router_t, b_router.reshape(N_EXPERTS, 1))


ISSUE_UNROLL = 2


def _expert_body(bend_ref, bgrp_ref, bshort_ref, x_hbm, wug_ref, bug_ref, wd_ref, bd_ref, y_hbm,
                 wug_bf, wd_bf, act_ref, xbuf, ybuf, sem_in, sem_out):
    e = pl.program_id(0)
    n_used = bend_ref[N_EXPERTS - 1]
    b_first = jnp.where(e == 0, 0, bend_ref[jnp.maximum(e - 1, 0)])
    b_last = bend_ref[e]

    def for_size(b, fn):
        short = bshort_ref[b] == 1

        @pl.when(short)
        def _():
            fn(MOE_PAD // SUBLANES)

        @pl.when(jnp.logical_not(short))
        def _():
            fn(MOE_ROWS // SUBLANES)

    def x_copies(b, slot, groups):
        return [pltpu.make_async_copy(x_hbm.at[pl.ds(bgrp_ref[b], groups), s],
                                      xbuf.at[slot, pl.ds(0, groups), :, s, :], sem_in.at[slot])
                for s in range(SUBLANES)]

    def y_copies(b, slot, groups):
        return [pltpu.make_async_copy(ybuf.at[slot, pl.ds(0, groups), :, s, :],
                                      y_hbm.at[pl.ds(bgrp_ref[b], groups), s], sem_out.at[slot])
                for s in range(SUBLANES)]

    def start_all(copies):
        for cp in copies:
            cp.start(priority=BLOCK_DMA_PRIORITY)

    def wait_all(copies):
        for cp in copies:
            cp.wait()

    @pl.when((e == 0) & (n_used > 0))
    def _():
        for_size(0, lambda groups: start_all(x_copies(0, 0, groups)))

    ns = D_FF // FF_SLAB
    for s in range(ns):
        lo, hi = s * FF_SLAB, (s + 1) * FF_SLAB
        wug_bf[:, 2 * lo:2 * lo + FF_SLAB] = wug_ref[0, :, lo:hi].astype(BF16)
        wug_bf[:, 2 * lo + FF_SLAB:2 * hi] = wug_ref[0, :, D_FF + lo:D_FF + hi].astype(BF16)
    wd_bf[...] = wd_ref[0].astype(BF16)
    b_ug = bug_ref[0]
    b_d = bd_ref[0]

    def mlp(slot, groups):
        rows = groups * SUBLANES
        x = _unpack_bf16_pairs(_load_tiles(xbuf.at[slot, pl.ds(0, groups)]))
        for s in range(ns):
            lo, hi = s * FF_SLAB, (s + 1) * FF_SLAB
            hu = jnp.dot(x, wug_bf[:, 2 * lo:2 * hi], preferred_element_type=F32)
            g = jnp.minimum(hu[:, :FF_SLAB] + b_ug[:, lo:hi], SWIGLU_LIMIT)
            u = jnp.clip(hu[:, FF_SLAB:] + b_ug[:, D_FF + lo:D_FF + hi], -SWIGLU_LIMIT, SWIGLU_LIMIT)
            act_ref[0:rows, lo:hi] = (g * _sigmoid(SWIGLU_ALPHA * g) * (u + 1.0)).astype(BF16)
        _store_tiles(ybuf.at[slot, pl.ds(0, groups)],
                     jnp.dot(act_ref[0:rows, :], wd_bf[...], preferred_element_type=F32) + b_d)

    def block(b, carry):
        slot = lax.rem(b, 2)
        for_size(b, lambda groups: wait_all(x_copies(b, slot, groups)))

        @pl.when(b + 1 < n_used)
        def _():
            for_size(b + 1, lambda groups: start_all(x_copies(b + 1, 1 - slot, groups)))

        @pl.when(b >= 2)
        def _():
            for_size(b - 2, lambda groups: wait_all(y_copies(b - 2, slot, groups)))

        def run(groups):
            mlp(slot, groups)
            start_all(y_copies(b, slot, groups))
        for_size(b, run)
        return carry

    lax.fori_loop(b_first, b_last, block, 0)

    @pl.when(e == N_EXPERTS - 1)
    def _():
        for back in (2, 1):
            @pl.when(n_used >= back)
            def _():
                last = n_used - back
                for_size(last, lambda groups: wait_all(y_copies(last, lax.rem(last, 2), groups)))


def _experts(xb, block_end, block_group, block_short, w_up_gate, b_up_gate, w_down, b_down):
    groups = MOE_ROWS // SUBLANES

    def w_map(e, bend, bgrp, bshort):
        return (e, 0, 0)

    grid_spec = pltpu.PrefetchScalarGridSpec(
        num_scalar_prefetch=3,
        grid=(N_EXPERTS,),
        in_specs=[
            pl.BlockSpec(memory_space=pl.ANY),
            pl.BlockSpec((1, D_MODEL, 2 * D_FF), w_map),
            pl.BlockSpec((1, 1, 2 * D_FF), w_map),
            pl.BlockSpec((1, D_FF, D_MODEL), w_map),
            pl.BlockSpec((1, 1, D_MODEL), w_map),
        ],
        out_specs=pl.BlockSpec(memory_space=pl.ANY),
        scratch_shapes=[
            pltpu.VMEM((D_MODEL, 2 * D_FF), BF16), pltpu.VMEM((D_FF, D_MODEL), BF16),
            pltpu.VMEM((MOE_ROWS, D_FF), BF16),
            pltpu.VMEM((2, groups) + PACKED_VIEW, jnp.uint32), pltpu.VMEM((2, groups) + TILE_VIEW, F32),
            pltpu.SemaphoreType.DMA((2,)), pltpu.SemaphoreType.DMA((2,)),
        ],
    )
    return pl.pallas_call(
        _expert_body,
        grid_spec=grid_spec,
        out_shape=jax.ShapeDtypeStruct((xb.shape[0],) + TOKEN_TILES, F32),
        compiler_params=pltpu.CompilerParams(
            dimension_semantics=("arbitrary",), vmem_limit_bytes=VMEM_LIMIT),
        name="moe_experts",
    )(block_end, block_group, block_short, xb, w_up_gate, b_up_gate.reshape(N_EXPERTS, 1, -1),
      w_down, b_down.reshape(N_EXPERTS, 1, -1))


COMBINE_TOKENS = 128


def _combine_body(dest_ref, yb_hbm, h_ref, gate_ref, gain_ref, o_ref, buf_ref, sem):
    i = pl.program_id(0)
    n_steps = pl.num_programs(0)
    nt = COMBINE_TOKENS

    def issue(step, slot):
        base = step * nt

        def body(g, c):
            for s in range(SUBLANES):
                for k in range(TOP_K):
                    pltpu.make_async_copy(
                        yb_hbm.at[dest_ref[k * (dest_ref.shape[0] // TOP_K) + base + g * SUBLANES + s]],
                        buf_ref.at[slot, k, g, :, s, :], sem.at[slot]).start(priority=k % 2)
            return c
        lax.fori_loop(0, nt // SUBLANES, body, 0, unroll=ISSUE_UNROLL)

    slot = lax.rem(i, 2)

    @pl.when(i == 0)
    def _():
        issue(0, 0)

    @pl.when(i + 1 < n_steps)
    def _():
        issue(i + 1, 1 - slot)

    pltpu.make_async_copy(buf_ref.at[slot], buf_ref.at[slot], sem.at[slot]).wait()

    gates = gate_ref[...]
    acc = h_ref[...]
    for k in range(TOP_K):
        acc = acc + gates[:, k:k + 1] * _load_tiles(buf_ref.at[slot, k])
    ms = jnp.mean(acc * acc, axis=-1, keepdims=True)
    o_ref[...] = acc * lax.rsqrt(ms + NORM_EPS) * gain_ref[...]


def _combine(yb, dest_flat, h_res, gates_tk, final_gain):
    n_tok = h_res.shape[0]
    nt = COMBINE_TOKENS
    grid_spec = pltpu.PrefetchScalarGridSpec(
        num_scalar_prefetch=1,
        grid=(n_tok // nt,),
        in_specs=[
            pl.BlockSpec(memory_space=pl.ANY),
            pl.BlockSpec((nt, D_MODEL), lambda i, d: (i, 0)),
            pl.BlockSpec((nt, TOP_K), lambda i, d: (i, 0)),
            pl.BlockSpec((1, D_MODEL), lambda i, d: (0, 0)),
        ],
        out_specs=pl.BlockSpec((nt, D_MODEL), lambda i, d: (i, 0)),
        scratch_shapes=[pltpu.VMEM((2, TOP_K, nt // SUBLANES) + TILE_VIEW, F32),
                        pltpu.SemaphoreType.DMA((2,))],
    )
    return pl.pallas_call(
        _combine_body,
        grid_spec=grid_spec,
        out_shape=jax.ShapeDtypeStruct((n_tok, D_MODEL), F32),
        compiler_params=pltpu.CompilerParams(
            dimension_semantics=("arbitrary",), vmem_limit_bytes=VMEM_LIMIT),
        name="moe_combine",
    )(dest_flat, yb, h_res, gates_tk, final_gain.reshape(1, D_MODEL))


def kernel(x, norm_mix_gain, w_in, hg_lb_fwd, hg_lb_bwd, hg_norm_gain, attn_sink, w_out,
           norm_ffn_gain, w_router, b_router, w_up_gate, b_up_gate, w_down, b_down,
           final_norm_gain):
    bsz, seqlen, d = x.shape
    n_tok = bsz * seqlen
    x2d = x.reshape(n_tok, d)

    lb_fwd = jax.nn.softmax(hg_lb_fwd.astype(F32), axis=0)[0]
    lb_bwd = jax.nn.softmax(hg_lb_bwd.astype(F32), axis=0)[0]

    proj = _inproj(x2d, norm_mix_gain[0], w_in[0])
    o_f, o_b = _hgrn2(proj, lb_fwd, lb_bwd, bsz, seqlen)
    h_res, gate_t, dest_t, counts, xb = _merge_route(
        o_f, o_b, proj, attn_sink[0], x2d, seqlen, hg_norm_gain[0], w_out[0],
        norm_ffn_gain[0], w_router[0].T.astype(F32), b_router[0].astype(F32))

    n_rows = xb.shape[0]
    cap = n_rows // N_EXPERTS
    counts = counts[:, 0]
    padded = (counts + MOE_PAD - 1) // MOE_PAD * MOE_PAD
    full = padded // MOE_ROWS
    blocks = full + (padded % MOE_ROWS) // MOE_PAD
    block_end = jnp.cumsum(blocks).astype(jnp.int32)
    n_blocks_max = n_tok * TOP_K // MOE_PAD + N_EXPERTS
    b = jnp.arange(n_blocks_max, dtype=jnp.int32)
    owner = (block_end[None, :] <= b[:, None]).astype(jnp.int32)
    expert_of = jnp.minimum(jnp.sum(owner, axis=-1), N_EXPERTS - 1)
    within = b - jnp.sum(owner * blocks[None, :], axis=-1)
    is_owner = expert_of[:, None] == jnp.arange(N_EXPERTS, dtype=jnp.int32)[None, :]
    block_short = (within == jnp.sum(jnp.where(is_owner, full[None, :], 0), axis=-1)).astype(jnp.int32)
    block_group = ((expert_of * cap + within * MOE_ROWS) // SUBLANES).astype(jnp.int32)
    dest_flat = dest_t.reshape(-1)

    yb = _experts(xb.reshape((n_rows // SUBLANES,) + PACKED_TOKEN_TILES), block_end, block_group, block_short,
                  w_up_gate[0], b_up_gate[0], w_down[0], b_down[0])
    out = _combine(yb.reshape(n_rows, LANE_TILES, LANES), dest_flat, h_res, gate_t.T, final_norm_gain)
    return out.reshape(bsz, seqlen, d)
```

```python
import functools

import jax
import jax.numpy as jnp
import numpy as np
from jax import lax
from jax.experimental import pallas as pl
from jax.experimental.pallas import tpu as pltpu

F32 = jnp.float32
BF16 = jnp.bfloat16

D_MODEL = 1024
HG_HEADS = 8
HG_DK = 128
HG_CHUNK = 128
HG_GROUP = 8
AT_Q_HEADS = 16
AT_KV_HEADS = 4
AT_GROUP = AT_Q_HEADS // AT_KV_HEADS
AT_HEAD_DIM = 64
WINDOW = 128
AT_BLOCK = WINDOW
AT_KVWIDTH = AT_KV_HEADS * AT_HEAD_DIM
N_EXPERTS = 32
TOP_K = 4
D_FF = 1024
SWIGLU_LIMIT = 7.0
SWIGLU_ALPHA = 1.702
NORM_EPS = 1e-5
IN_WIDTH = 8704

COL_HQ, COL_FF, COL_FB, COL_HI, COL_HG = 0, 1024, 2048, 3072, 4096
COL_AQ, COL_AK, COL_AV, COL_GA, COL_GB = 5120, 6144, 6400, 6656, 7680

MOE_ROWS = 1024
MOE_PAD = 512
FF_SLAB = 256
MERGE_SUB = 256
BLOCK_DMA_PRIORITY = 1
VMEM_LIMIT = 56 * 1024 * 1024

NT_DIMS = (((1,), (1,)), ((), ()))

SUBLANES = 8
LANES = 128
LANE_TILES = D_MODEL // LANES
TILE_VIEW = (LANE_TILES, SUBLANES, LANES)
TOKEN_TILES = (SUBLANES, LANE_TILES, LANES)


def _store_tiles(ref, val):
    groups = val.shape[0] // SUBLANES
    for t in range(val.shape[1] // LANES):
        ref[:, t] = val[:, t * LANES:(t + 1) * LANES].reshape(groups, SUBLANES, LANES)


def _load_tiles(ref):
    groups = ref.shape[0]
    return jnp.concatenate(
        [ref[:, t].reshape(groups * SUBLANES, LANES) for t in range(ref.shape[1])], axis=1)


PACKED_TILES = LANE_TILES // 2
PACKED_VIEW = (PACKED_TILES, SUBLANES, LANES)
PACKED_TOKEN_TILES = (SUBLANES, PACKED_TILES, LANES)


def _pack_bf16_pairs(x):
    half = x.shape[1] // 2
    lo = pltpu.bitcast(x[:, :half].astype(BF16).astype(F32), jnp.uint32)
    hi = pltpu.bitcast(x[:, half:].astype(BF16).astype(F32), jnp.uint32)
    return hi | lax.shift_right_logical(lo, jnp.uint32(16))


def _unpack_bf16_pairs(w):
    lo = pltpu.bitcast(lax.shift_left(w, jnp.uint32(16)), F32)
    hi = pltpu.bitcast(w & jnp.uint32(0xFFFF0000), F32)
    return jnp.concatenate([lo, hi], axis=1).astype(BF16)


def _sigmoid(x):
    return 0.5 * jnp.tanh(0.5 * x) + 0.5


def _inproj_body(x_ref, g_ref, w_ref, o_ref, xn_ref):
    @pl.when(pl.program_id(1) == 0)
    def _():
        x = x_ref[...]
        ms = jnp.mean(x * x, axis=-1, keepdims=True)
        xn_ref[...] = (x * lax.rsqrt(ms + NORM_EPS) * g_ref[...]).astype(BF16)

    o_ref[...] = jnp.dot(xn_ref[...], w_ref[...].astype(BF16),
                         preferred_element_type=F32).astype(o_ref.dtype)


def _inproj(x2d, gain, w, tm=1024, tn=2176):
    n_tok = x2d.shape[0]
    n_out = w.shape[1]
    return pl.pallas_call(
        _inproj_body,
        grid=(n_tok // tm, n_out // tn),
        in_specs=[
            pl.BlockSpec((tm, D_MODEL), lambda i, j: (i, 0)),
            pl.BlockSpec((1, D_MODEL), lambda i, j: (0, 0)),
            pl.BlockSpec((D_MODEL, tn), lambda i, j: (0, j)),
        ],
        out_specs=pl.BlockSpec((tm, tn), lambda i, j: (i, j)),
        out_shape=jax.ShapeDtypeStruct((n_tok, n_out), BF16),
        scratch_shapes=[pltpu.VMEM((tm, D_MODEL), BF16)],
        compiler_params=pltpu.CompilerParams(
            dimension_semantics=("parallel", "arbitrary"), vmem_limit_bytes=VMEM_LIMIT),
        name="inproj",
    )(x2d, gain.reshape(1, D_MODEL), w)


class _Dir:
    def __init__(self, q_ref, f_ref, v_ref, o_ref, st_ref, lb, tri, mask, mid_row, last_row):
        self.q_ref, self.f_ref, self.v_ref, self.o_ref, self.st_ref = q_ref, f_ref, v_ref, o_ref, st_ref
        self.c0, self.c1 = 0.5 * (1.0 + lb), 0.5 * (1.0 - lb)
        self.tri, self.mask, self.mid_row, self.last_row = tri, mask, mid_row, last_row


def _gla_group(chains):
    c = HG_CHUNK
    dk = HG_DK
    pre = []
    for d, r0 in chains:
        h = 0.5 * d.q_ref[pl.ds(r0, c), :].astype(F32)
        q = h * jnp.tanh(h) + h
        ct = d.c1 * jnp.tanh(0.5 * d.f_ref[pl.ds(r0, c), :].astype(F32))
        logf = jnp.log2(d.c0 + ct)
        hi = logf.astype(BF16)
        lo = (logf - hi.astype(F32)).astype(BF16)
        pre.append((q, (1.0 - d.c0) - ct, jnp.concatenate([hi, lo], axis=1)))
    cums = []
    for (d, _), (_, _, hl) in zip(chains, pre):
        r = jnp.dot(d.tri, hl, preferred_element_type=F32)
        cums.append(r[:, :dk] + r[:, dk:])
    mids = []
    for (d, r0), (q, k, _), cum in zip(chains, pre, cums):
        mid = cum[d.mid_row:d.mid_row + 1, :]
        last = cum[d.last_row:d.last_row + 1, :]
        a = cum - mid
        qa = q * jnp.exp2(a)
        ka = k * jnp.exp2(-a)
        qd = (qa * jnp.exp2(mid)).astype(BF16)
        kend = (ka * jnp.exp2(last - mid)).astype(BF16)
        dec = jnp.exp2(last)
        scores = lax.dot_general(qa.astype(BF16), ka.astype(BF16), NT_DIMS,
                                 preferred_element_type=F32)
        mids.append((qd, kend, dec, scores))
    outs = []
    for (d, r0), (qd, kend, dec, scores) in zip(chains, mids):
        v = d.v_ref[pl.ds(r0, c), :]
        sm = jnp.where(d.mask, scores, 0.0).astype(BF16)
        o_intra = jnp.dot(sm, v, preferred_element_type=F32)
        upd = jnp.dot(jnp.transpose(v.astype(F32)).astype(BF16), kend, preferred_element_type=F32)
        outs.append((o_intra, upd))
    for (d, r0), (qd, _, dec, _), (o_intra, upd) in zip(chains, mids, outs):
        st = d.st_ref[...]
        d.o_ref[pl.ds(r0, c), :] = (o_intra + lax.dot_general(
            qd, st.astype(BF16), NT_DIMS, preferred_element_type=F32)).astype(d.o_ref.dtype)
        d.st_ref[...] = st * dec + upd


def _hgrn2_body(qf_ref, ff_ref, vf_ref, qb_ref, fb_ref, vb_ref, lbf_ref, lbb_ref,
                of_ref, ob_ref, sf_ref, sb_ref):
    @pl.when(pl.program_id(2) == 0)
    def _():
        sf_ref[...] = jnp.zeros_like(sf_ref)
        sb_ref[...] = jnp.zeros_like(sb_ref)

    c = HG_CHUNK
    g = HG_GROUP
    n_chunks = qf_ref.shape[0] // c
    row = lax.broadcasted_iota(jnp.int32, (c, c), 0)
    col = lax.broadcasted_iota(jnp.int32, (c, c), 1)
    lower = col <= row
    upper = col >= row
    fwd = _Dir(qf_ref, ff_ref, vf_ref, of_ref, sf_ref, lbf_ref[...],
               jnp.where(lower, 1.0, 0.0).astype(BF16), lower, c // 2 - 1, c - 1)
    bwd = _Dir(qb_ref, fb_ref, vb_ref, ob_ref, sb_ref, lbb_ref[...],
               jnp.where(upper, 1.0, 0.0).astype(BF16), upper, c // 2, 0)

    def step(n, carry):
        chains = []
        for u in range(g):
            chains.append((fwd, pl.multiple_of((n * g + u) * c, c)))
        for u in range(g):
            chains.append((bwd, pl.multiple_of((n_chunks - 1 - n * g - u) * c, c)))
        _gla_group(chains)
        return carry

    lax.fori_loop(0, n_chunks // g, step, 0)


def _hgrn2(proj, lb_fwd, lb_bwd, bsz, seqlen, rows=4096):
    n_tok = bsz * seqlen
    rows = min(rows, seqlen)
    nb = seqlen // rows
    dk = HG_DK

    def fwd_map(col0):
        return lambda b, h, j: (b * nb + j, col0 // dk + h)

    def bwd_map(col0):
        return lambda b, h, j: (b * nb + (nb - 1 - j), col0 // dk + h)

    blk = (rows, dk)
    lb_spec = pl.BlockSpec((1, dk), lambda b, h, j: (0, h))
    return pl.pallas_call(
        _hgrn2_body,
        grid=(bsz, HG_HEADS, nb),
        in_specs=[
            pl.BlockSpec(blk, fwd_map(COL_HQ)), pl.BlockSpec(blk, fwd_map(COL_FF)),
            pl.BlockSpec(blk, fwd_map(COL_HI)),
            pl.BlockSpec(blk, bwd_map(COL_HQ)), pl.BlockSpec(blk, bwd_map(COL_FB)),
            pl.BlockSpec(blk, bwd_map(COL_HI)),
            lb_spec, lb_spec,
        ],
        out_specs=[pl.BlockSpec(blk, fwd_map(0)), pl.BlockSpec(blk, bwd_map(0))],
        out_shape=[jax.ShapeDtypeStruct((n_tok, HG_HEADS * dk), BF16)] * 2,
        scratch_shapes=[pltpu.VMEM((dk, dk), F32), pltpu.VMEM((dk, dk), F32)],
        compiler_params=pltpu.CompilerParams(
            dimension_semantics=("parallel", "parallel", "arbitrary"),
            vmem_limit_bytes=VMEM_LIMIT),
        name="hgrn2",
    )(proj, proj, proj, proj, proj, proj, lb_fwd.reshape(1, -1), lb_bwd.reshape(1, -1))


def _window_attention(sink_ref, q_ref, k_refs, v_refs, starts_sequence, ends_sequence, out_ref):
    blk = AT_BLOCK
    span = 3 * blk
    dh = AT_HEAD_DIM
    lanes = 2 * dh
    n_q = q_ref.shape[0] // blk
    lo_half = lax.broadcasted_iota(jnp.int32, (blk, lanes), 1) < dh
    row = lax.broadcasted_iota(jnp.int32, (blk, span), 0)
    col = lax.broadcasted_iota(jnp.int32, (blk, span), 1)
    absrel = jnp.abs(row + blk - col)
    base_inner = jnp.where(absrel <= WINDOW, -absrel.astype(F32), -jnp.inf)

    def padded(t, g):
        t = t[:, (g // 2) * lanes:(g // 2 + 1) * lanes].astype(F32)
        if g % 2 == 0:
            lo = jnp.where(lo_half, t, 0.0)
            hi = pltpu.roll(lo, dh, axis=1)
        else:
            hi = jnp.where(lo_half, 0.0, t)
            lo = pltpu.roll(hi, dh, axis=1)
        return lo.astype(BF16), hi.astype(BF16)

    def key_blocks(refs):
        prev_ref, own_ref, next_ref = refs
        return ([prev_ref[...]] + [own_ref[b * blk:(b + 1) * blk, :] for b in range(n_q)]
                + [next_ref[...]])

    k_pads = [[padded(t, g) for t in key_blocks(k_refs)] for g in range(AT_KV_HEADS)]
    v_pads = [[padded(t, g) for t in key_blocks(v_refs)] for g in range(AT_KV_HEADS)]

    def banded(pads, first):
        use = pads[first:first + 3]
        return jnp.concatenate([p[0] for p in use] + [p[1] for p in use], axis=0)

    for qb in range(n_q):
        base = base_inner
        if qb == 0:
            base = jnp.where(jnp.logical_and(starts_sequence, col < blk), -jnp.inf, base)
        if qb == n_q - 1:
            base = jnp.where(jnp.logical_and(ends_sequence, col >= 2 * blk), -jnp.inf, base)
        rows = slice(qb * blk, (qb + 1) * blk)
        scores = []
        for g in range(AT_KV_HEADS):
            q2 = q_ref[rows, g * 2 * lanes:(g + 1) * 2 * lanes] * (dh ** -0.5)
            lhs = jnp.concatenate([q2[:, :lanes], q2[:, lanes:]], axis=0).astype(BF16)
            scores.append(lax.dot_general(lhs, banded(k_pads[g], qb), NT_DIMS,
                                          preferred_element_type=F32))
        probs, denoms = [], []
        for g in range(AT_KV_HEADS):
            p_rows, d_rows = [], []
            for r in range(2):
                p_cols, d_cols = [], []
                for c in range(2):
                    h = g * AT_GROUP + 2 * r + c
                    slope = 2.0 ** (-8.0 * (h + 1) / AT_Q_HEADS)
                    sink = sink_ref[h]
                    s = scores[g][r * blk:(r + 1) * blk, c * span:(c + 1) * span] + slope * base
                    m = jnp.maximum(jnp.max(s, axis=-1, keepdims=True), sink)
                    p = jnp.exp(s - m)
                    d_cols.append(jnp.sum(p, axis=-1, keepdims=True) + jnp.exp(sink - m))
                    p_cols.append(p.astype(BF16))
                p_rows.append(jnp.concatenate(p_cols, axis=1))
                d_rows.append(jnp.where(lo_half, d_cols[0], d_cols[1]))
            probs.append(jnp.concatenate(p_rows, axis=0))
            denoms.append(d_rows)
        outs = []
        for g in range(AT_KV_HEADS):
            o = jnp.dot(probs[g], banded(v_pads[g], qb), preferred_element_type=F32)
            for r in range(2):
                outs.append(o[r * blk:(r + 1) * blk, :] / denoms[g][r])
        out_ref[rows, :] = jnp.concatenate(outs, axis=-1).astype(out_ref.dtype)


def _merge_body(n_steps, steps_per_seq, sink_ref, of_ref, ob_ref, hg_ref, ga0_ref, ga1_ref, gb0_ref,
                gb1_ref, q_ref, kp_ref, ko_ref, kn_ref, vp_ref, vo_ref, vn_ref, x_ref,
                gn_ref, wo_ref, g2_ref, wr_ref, br_ref,
                h_ref, gate_ref, dest_ref, cnt_ref, xb_hbm,
                carry_ref, att_ref, ring, dest_vmem, dest_smem, cnt_vmem, cnt_smem, zero_ref,
                sem_rows, sem_dest, sem_misc):
    i = pl.program_id(0)
    slot = lax.rem(i, 2)
    tm = x_ref.shape[0]
    cap = xb_hbm.shape[0] // N_EXPERTS

    def rows_wait(sl):
        pltpu.make_async_copy(xb_hbm.at[pl.ds(0, tm * TOP_K)], xb_hbm.at[pl.ds(0, tm * TOP_K)],
                              sem_rows.at[sl]).wait()

    def dest_copy(sl):
        return pltpu.make_async_copy(dest_vmem.at[sl], dest_smem.at[sl], sem_dest.at[sl])

    def issue_rows(sl):
        dest_copy(sl).wait()

        def body(g, c):
            for s in range(SUBLANES):
                for k in range(TOP_K):
                    pltpu.make_async_copy(
                        ring.at[sl, g, :, s, :], xb_hbm.at[dest_smem[sl, k, g * SUBLANES + s]],
                        sem_rows.at[sl]).start(priority=k % 2)
            return c
        lax.fori_loop(0, tm // SUBLANES, body, 0, unroll=ISSUE_UNROLL)

    @pl.when(i == 0)
    def _():
        carry_ref[...] = jnp.zeros_like(carry_ref)

    @pl.when(i >= 2)
    def _():
        rows_wait(slot)

    sub = MERGE_SUB
    subs = [slice(s0, s0 + sub) for s0 in range(0, x_ref.shape[0], sub)]
    gn = gn_ref[...]

    step_in_seq = lax.rem(i, steps_per_seq)
    _window_attention(
        sink_ref, q_ref, (kp_ref, ko_ref, kn_ref), (vp_ref, vo_ref, vn_ref),
        step_in_seq == 0, step_in_seq == steps_per_seq - 1, att_ref)

    merged = []
    for rs in subs:
        o = of_ref[rs, :].astype(F32) + ob_ref[rs, :].astype(F32)
        parts = []
        for h in range(HG_HEADS):
            oh = o[:, h * HG_DK:(h + 1) * HG_DK]
            ms = jnp.mean(oh * oh, axis=-1, keepdims=True)
            parts.append(oh * lax.rsqrt(ms + NORM_EPS) * gn)
        hg = hg_ref[rs, :].astype(F32)
        y_rec = jnp.concatenate(parts, axis=-1) * (hg * _sigmoid(hg))
        gate_a = jnp.concatenate([ga0_ref[rs, :], ga1_ref[rs, :]], axis=-1).astype(F32)
        gate_b = jnp.concatenate([gb0_ref[rs, :], gb1_ref[rs, :]], axis=-1).astype(F32)
        merged.append((_sigmoid(gate_a) * y_rec
                       + _sigmoid(gate_b) * att_ref[rs, :].astype(F32)).astype(BF16))

    w_out = wo_ref[...].astype(BF16)
    xns = []
    for rs, mg in zip(subs, merged):
        h_res = x_ref[rs, :] + jnp.dot(mg, w_out, preferred_element_type=F32)
        h_ref[rs, :] = h_res
        ms = jnp.mean(h_res * h_res, axis=-1, keepdims=True)
        xn = h_res * lax.rsqrt(ms + NORM_EPS) * g2_ref[...]
        _store_tiles(ring.at[slot, rs.start // SUBLANES:rs.stop // SUBLANES], _pack_bf16_pairs(xn))
        xns.append(xn)

    logits = [lax.dot_general(wr_ref[...], xn, NT_DIMS, preferred_element_type=F32,
                              precision=lax.Precision.HIGHEST) + br_ref[...] for xn in xns]
    e_iota = lax.broadcasted_iota(jnp.int32, (N_EXPERTS, sub), 0).astype(F32)
    picked = []
    for rs, vals in zip(subs, logits):
        tops, idxs = [], []
        for _ in range(TOP_K):
            m = jnp.max(vals, axis=0, keepdims=True)
            sel = jnp.min(jnp.where(vals == m, e_iota, float(N_EXPERTS)), axis=0, keepdims=True)
            tops.append(m)
            idxs.append(sel)
            vals = jnp.where(e_iota == sel, -jnp.inf, vals)
        exps = [jnp.exp(t - tops[0]) for t in tops]
        den = exps[0] + exps[1] + exps[2] + exps[3]
        gate_ref[:, rs] = jnp.concatenate([e / den for e in exps], axis=0)
        picked.append((idxs, jnp.where(vals == -jnp.inf, 1.0, 0.0)))

    s_iota = lax.broadcasted_iota(jnp.int32, (sub, sub), 0)
    t_iota = lax.broadcasted_iota(jnp.int32, (sub, sub), 1)
    strict = jnp.where(s_iota < t_iota, 1.0, 0.0).astype(BF16)
    prefixes = [jnp.dot(chosen.astype(BF16), strict, preferred_element_type=F32)
                for _, chosen in picked]
    carry = carry_ref[...]
    for rs, (idxs, chosen), prefix in zip(subs, picked, prefixes):
        prefix = prefix + carry
        dests = [sel * float(cap) + jnp.sum(jnp.where(e_iota == sel, prefix, 0.0), axis=0, keepdims=True)
                 for sel in idxs]
        dest = jnp.concatenate(dests, axis=0).astype(jnp.int32)
        dest_ref[:, rs] = dest
        dest_vmem[slot, :, rs] = dest
        carry = carry + jnp.sum(chosen, axis=1, keepdims=True)
    carry_ref[...] = carry
    counts = jnp.broadcast_to(carry, cnt_ref.shape).astype(jnp.int32)
    cnt_ref[...] = counts

    dest_copy(slot).start()

    @pl.when(i >= 1)
    def _():
        issue_rows(1 - slot)

    @pl.when(i == n_steps - 1)
    def _():
        issue_rows(slot)
        if n_steps >= 2:
            rows_wait(1 - slot)
        rows_wait(slot)
        cnt_vmem[...] = counts
        cp = pltpu.make_async_copy(cnt_vmem, cnt_smem, sem_misc.at[0])
        cp.start()
        cp.wait()
        zero_ref[...] = jnp.zeros_like(zero_ref)

        def tail(e):
            return pltpu.make_async_copy(
                zero_ref, xb_hbm.at[pl.ds(e * cap + cnt_smem[e, 0], MOE_PAD)], sem_misc.at[0])
        for e in range(N_EXPERTS):
            tail(e).start()
        for e in range(N_EXPERTS):
            tail(e).wait()


def _merge_route(o_f, o_b, proj, sink, x2d, seqlen, gn_gain, w_out, g2, w_router_t, b_router, tm=512):
    n_tok = x2d.shape[0]
    nblk = n_tok // tm
    row_blk = (tm, D_MODEL)
    half = D_MODEL // 2
    half_blk = (tm, half)

    def rows(c):
        return lambda i: (i, c)

    per_step = tm // AT_BLOCK
    n_kv_blocks = n_tok // AT_BLOCK
    kv_blk = (AT_BLOCK, AT_KVWIDTH)
    kv_own = (tm, AT_KVWIDTH)

    def kv_prev(c):
        return lambda i: (jnp.maximum(i * per_step - 1, 0), c)

    def kv_next(c):
        return lambda i: (jnp.minimum((i + 1) * per_step, n_kv_blocks - 1), c)

    kcol = COL_AK // AT_KVWIDTH
    vcol = COL_AV // AT_KVWIDTH
    full = lambda i: (0, 0)
    tok_out = pl.BlockSpec((TOP_K, tm), lambda i: (0, i))
    cap = n_tok + MOE_PAD
    return pl.pallas_call(
        functools.partial(_merge_body, nblk, seqlen // tm),
        grid=(nblk,),
        in_specs=[
            pl.BlockSpec(memory_space=pltpu.SMEM),
            pl.BlockSpec(row_blk, rows(0)), pl.BlockSpec(row_blk, rows(0)),
            pl.BlockSpec(row_blk, rows(COL_HG // D_MODEL)),
            pl.BlockSpec(half_blk, rows(COL_GA // half)),
            pl.BlockSpec(half_blk, rows(COL_GA // half + 1)),
            pl.BlockSpec(half_blk, rows(COL_GB // half)),
            pl.BlockSpec(half_blk, rows(COL_GB // half + 1)),
            pl.BlockSpec(row_blk, rows(COL_AQ // D_MODEL)),
            pl.BlockSpec(kv_blk, kv_prev(kcol)), pl.BlockSpec(kv_own, rows(kcol)),
            pl.BlockSpec(kv_blk, kv_next(kcol)),
            pl.BlockSpec(kv_blk, kv_prev(vcol)), pl.BlockSpec(kv_own, rows(vcol)),
            pl.BlockSpec(kv_blk, kv_next(vcol)),
            pl.BlockSpec(row_blk, rows(0)),
            pl.BlockSpec((1, HG_DK), full),
            pl.BlockSpec((D_MODEL, D_MODEL), full),
            pl.BlockSpec((1, D_MODEL), full),
            pl.BlockSpec((N_EXPERTS, D_MODEL), full),
            pl.BlockSpec((N_EXPERTS, 1), full),
        ],
        out_specs=[
            pl.BlockSpec(row_blk, rows(0)),
            tok_out, tok_out,
            pl.BlockSpec((N_EXPERTS, LANES), full),
            pl.BlockSpec(memory_space=pl.ANY),
        ],
        out_shape=[
            jax.ShapeDtypeStruct((n_tok, D_MODEL), F32),
            jax.ShapeDtypeStruct((TOP_K, n_tok), F32),
            jax.ShapeDtypeStruct((TOP_K, n_tok), jnp.int32),
            jax.ShapeDtypeStruct((N_EXPERTS, LANES), jnp.int32),
            jax.ShapeDtypeStruct((N_EXPERTS * cap, PACKED_TILES, LANES), jnp.uint32),
        ],
        scratch_shapes=[
            pltpu.VMEM((N_EXPERTS, 1), F32),
            pltpu.VMEM((tm, D_MODEL), BF16),
            pltpu.VMEM((2, tm // SUBLANES) + PACKED_VIEW, jnp.uint32),
            pltpu.VMEM((2, TOP_K, tm), jnp.int32), pltpu.SMEM((2, TOP_K, tm), jnp.int32),
            pltpu.VMEM((N_EXPERTS, LANES), jnp.int32), pltpu.SMEM((N_EXPERTS, LANES), jnp.int32),
            pltpu.VMEM((MOE_PAD, PACKED_TILES, LANES), jnp.uint32),
            pltpu.SemaphoreType.DMA((2,)), pltpu.SemaphoreType.DMA((2,)), pltpu.SemaphoreType.DMA((1,)),
        ],
        compiler_params=pltpu.CompilerParams(
            dimension_semantics=("arbitrary",), vmem_limit_bytes=VMEM_LIMIT),
        name="merge_route",
    )(sink.astype(F32), o_f, o_b, proj, proj, proj, proj, proj, proj, proj, proj, proj, proj, proj, proj,
      x2d, gn_gain.reshape(1, HG_DK), w_out,
      g2.reshape(1, D_MODEL), w_router_t, b_router.reshape(N_EXPERTS, 1))


ISSUE_UNROLL = 2


def _expert_body(bend_ref, bgrp_ref, bshort_ref, x_hbm, wug_ref, bug_ref, wd_ref, bd_ref, y_hbm,
                 wug_bf, wd_bf, act_ref, xbuf, ybuf, sem_in, sem_out):
    e = pl.program_id(0)
    n_used = bend_ref[N_EXPERTS - 1]
    b_first = jnp.where(e == 0, 0, bend_ref[jnp.maximum(e - 1, 0)])
    b_last = bend_ref[e]

    def for_size(b, fn):
        short = bshort_ref[b] == 1

        @pl.when(short)
        def _():
            fn(MOE_PAD // SUBLANES)

        @pl.when(jnp.logical_not(short))
        def _():
            fn(MOE_ROWS // SUBLANES)

    def x_copies(b, slot, groups):
        return [pltpu.make_async_copy(x_hbm.at[pl.ds(bgrp_ref[b], groups), s],
                                      xbuf.at[slot, pl.ds(0, groups), :, s, :], sem_in.at[slot])
                for s in range(SUBLANES)]

    def y_copies(b, slot, groups):
        return [pltpu.make_async_copy(ybuf.at[slot, pl.ds(0, groups), :, s, :],
                                      y_hbm.at[pl.ds(bgrp_ref[b], groups), s], sem_out.at[slot])
                for s in range(SUBLANES)]

    def start_all(copies):
        for cp in copies:
            cp.start(priority=BLOCK_DMA_PRIORITY)

    def wait_all(copies):
        for cp in copies:
            cp.wait()

    @pl.when((e == 0) & (n_used > 0))
    def _():
        for_size(0, lambda groups: start_all(x_copies(0, 0, groups)))

    ns = D_FF // FF_SLAB
    for s in range(ns):
        lo, hi = s * FF_SLAB, (s + 1) * FF_SLAB
        wug_bf[:, 2 * lo:2 * lo + FF_SLAB] = wug_ref[0, :, lo:hi].astype(BF16)
        wug_bf[:, 2 * lo + FF_SLAB:2 * hi] = wug_ref[0, :, D_FF + lo:D_FF + hi].astype(BF16)
    wd_bf[...] = wd_ref[0].astype(BF16)
    b_ug = bug_ref[0]
    b_d = bd_ref[0]

    def mlp(slot, groups):
        rows = groups * SUBLANES
        x = _unpack_bf16_pairs(_load_tiles(xbuf.at[slot, pl.ds(0, groups)]))
        for s in range(ns):
            lo, hi = s * FF_SLAB, (s + 1) * FF_SLAB
            hu = jnp.dot(x, wug_bf[:, 2 * lo:2 * hi], preferred_element_type=F32)
            g = jnp.minimum(hu[:, :FF_SLAB] + b_ug[:, lo:hi], SWIGLU_LIMIT)
            u = jnp.clip(hu[:, FF_SLAB:] + b_ug[:, D_FF + lo:D_FF + hi], -SWIGLU_LIMIT, SWIGLU_LIMIT)
            act_ref[0:rows, lo:hi] = (g * _sigmoid(SWIGLU_ALPHA * g) * (u + 1.0)).astype(BF16)
        _store_tiles(ybuf.at[slot, pl.ds(0, groups)],
                     jnp.dot(act_ref[0:rows, :], wd_bf[...], preferred_element_type=F32) + b_d)

    def block(b, carry):
        slot = lax.rem(b, 2)
        for_size(b, lambda groups: wait_all(x_copies(b, slot, groups)))

        @pl.when(b + 1 < n_used)
        def _():
            for_size(b + 1, lambda groups: start_all(x_copies(b + 1, 1 - slot, groups)))

        @pl.when(b >= 2)
        def _():
            for_size(b - 2, lambda groups: wait_all(y_copies(b - 2, slot, groups)))

        def run(groups):
            mlp(slot, groups)
            start_all(y_copies(b, slot, groups))
        for_size(b, run)
        return carry

    lax.fori_loop(b_first, b_last, block, 0)

    @pl.when(e == N_EXPERTS - 1)
    def _():
        for back in (2, 1):
            @pl.when(n_used >= back)
            def _():
                last = n_used - back
                for_size(last, lambda groups: wait_all(y_copies(last, lax.rem(last, 2), groups)))


def _experts(xb, block_end, block_group, block_short, w_up_gate, b_up_gate, w_down, b_down):
    groups = MOE_ROWS // SUBLANES

    def w_map(e, bend, bgrp, bshort):
        return (e, 0, 0)

    grid_spec = pltpu.PrefetchScalarGridSpec(
        num_scalar_prefetch=3,
        grid=(N_EXPERTS,),
        in_specs=[
            pl.BlockSpec(memory_space=pl.ANY),
            pl.BlockSpec((1, D_MODEL, 2 * D_FF), w_map),
            pl.BlockSpec((1, 1, 2 * D_FF), w_map),
            pl.BlockSpec((1, D_FF, D_MODEL), w_map),
            pl.BlockSpec((1, 1, D_MODEL), w_map),
        ],
        out_specs=pl.BlockSpec(memory_space=pl.ANY),
        scratch_shapes=[
            pltpu.VMEM((D_MODEL, 2 * D_FF), BF16), pltpu.VMEM((D_FF, D_MODEL), BF16),
            pltpu.VMEM((MOE_ROWS, D_FF), BF16),
            pltpu.VMEM((2, groups) + PACKED_VIEW, jnp.uint32), pltpu.VMEM((2, groups) + TILE_VIEW, F32),
            pltpu.SemaphoreType.DMA((2,)), pltpu.SemaphoreType.DMA((2,)),
        ],
    )
    return pl.pallas_call(
        _expert_body,
        grid_spec=grid_spec,
        out_shape=jax.ShapeDtypeStruct((xb.shape[0],) + TOKEN_TILES, F32),
        compiler_params=pltpu.CompilerParams(
            dimension_semantics=("arbitrary",), vmem_limit_bytes=VMEM_LIMIT),
        name="moe_experts",
    )(block_end, block_group, block_short, xb, w_up_gate, b_up_gate.reshape(N_EXPERTS, 1, -1),
      w_down, b_down.reshape(N_EXPERTS, 1, -1))


COMBINE_TOKENS = 128


def _combine_body(dest_ref, yb_hbm, h_ref, gate_ref, gain_ref, o_ref, buf_ref, sem):
    i = pl.program_id(0)
    n_steps = pl.num_programs(0)
    nt = COMBINE_TOKENS

    def issue(step, slot):
        base = step * nt

        def body(g, c):
            for s in range(SUBLANES):
                for k in range(TOP_K):
                    pltpu.make_async_copy(
                        yb_hbm.at[dest_ref[k * (dest_ref.shape[0] // TOP_K) + base + g * SUBLANES + s]],
                        buf_ref.at[slot, k, g, :, s, :], sem.at[slot]).start(priority=k % 2)
            return c
        lax.fori_loop(0, nt // SUBLANES, body, 0, unroll=ISSUE_UNROLL)

    slot = lax.rem(i, 2)

    @pl.when(i == 0)
    def _():
        issue(0, 0)

    @pl.when(i + 1 < n_steps)
    def _():
        issue(i + 1, 1 - slot)

    pltpu.make_async_copy(buf_ref.at[slot], buf_ref.at[slot], sem.at[slot]).wait()

    gates = gate_ref[...]
    acc = h_ref[...]
    for k in range(TOP_K):
        acc = acc + gates[:, k:k + 1] * _load_tiles(buf_ref.at[slot, k])
    ms = jnp.mean(acc * acc, axis=-1, keepdims=True)
    o_ref[...] = acc * lax.rsqrt(ms + NORM_EPS) * gain_ref[...]


def _combine(yb, dest_flat, h_res, gates_tk, final_gain):
    n_tok = h_res.shape[0]
    nt = COMBINE_TOKENS
    grid_spec = pltpu.PrefetchScalarGridSpec(
        num_scalar_prefetch=1,
        grid=(n_tok // nt,),
        in_specs=[
            pl.BlockSpec(memory_space=pl.ANY),
            pl.BlockSpec((nt, D_MODEL), lambda i, d: (i, 0)),
            pl.BlockSpec((nt, TOP_K), lambda i, d: (i, 0)),
            pl.BlockSpec((1, D_MODEL), lambda i, d: (0, 0)),
        ],
        out_specs=pl.BlockSpec((nt, D_MODEL), lambda i, d: (i, 0)),
        scratch_shapes=[pltpu.VMEM((2, TOP_K, nt // SUBLANES) + TILE_VIEW, F32),
                        pltpu.SemaphoreType.DMA((2,))],
    )
    return pl.pallas_call(
        _combine_body,
        grid_spec=grid_spec,
        out_shape=jax.ShapeDtypeStruct((n_tok, D_MODEL), F32),
        compiler_params=pltpu.CompilerParams(
            dimension_semantics=("arbitrary",), vmem_limit_bytes=VMEM_LIMIT),
        name="moe_combine",
    )(dest_flat, yb, h_res, gates_tk, final_gain.reshape(1, D_MODEL))


def kernel(x, norm_mix_gain, w_in, hg_lb_fwd, hg_lb_bwd, hg_norm_gain, attn_sink, w_out,
           norm_ffn_gain, w_router, b_router, w_up_gate, b_up_gate, w_down, b_down,
           final_norm_gain):
    bsz, seqlen, d = x.shape
    n_tok = bsz * seqlen
    x2d = x.reshape(n_tok, d)

    lb_fwd = jax.nn.softmax(hg_lb_fwd.astype(F32), axis=0)[0]
    lb_bwd = jax.nn.softmax(hg_lb_bwd.astype(F32), axis=0)[0]

    proj = _inproj(x2d, norm_mix_gain[0], w_in[0])
    o_f, o_b = _hgrn2(proj, lb_fwd, lb_bwd, bsz, seqlen)
    h_res, gate_t, dest_t, counts, xb = _merge_route(
        o_f, o_b, proj, attn_sink[0], x2d, seqlen, hg_norm_gain[0], w_out[0],
        norm_ffn_gain[0], w_router[0].T.astype(F32), b_router[0].astype(F32))

    n_rows = xb.shape[0]
    cap = n_rows // N_EXPERTS
    counts = counts[:, 0]
    padded = (counts + MOE_PAD - 1) // MOE_PAD * MOE_PAD
    full = padded // MOE_ROWS
    blocks = full + (padded % MOE_ROWS) // MOE_PAD
    block_end = jnp.cumsum(blocks).astype(jnp.int32)
    n_blocks_max = n_tok * TOP_K // MOE_PAD + N_EXPERTS
    b = jnp.arange(n_blocks_max, dtype=jnp.int32)
    owner = (block_end[None, :] <= b[:, None]).astype(jnp.int32)
    expert_of = jnp.minimum(jnp.sum(owner, axis=-1), N_EXPERTS - 1)
    within = b - jnp.sum(owner * blocks[None, :], axis=-1)
    is_owner = expert_of[:, None] == jnp.arange(N_EXPERTS, dtype=jnp.int32)[None, :]
    block_short = (within == jnp.sum(jnp.where(is_owner, full[None, :], 0), axis=-1)).astype(jnp.int32)
    block_group = ((expert_of * cap + within * MOE_ROWS) // SUBLANES).astype(jnp.int32)
    dest_flat = dest_t.reshape(-1)

    yb = _experts(xb.reshape((n_rows // SUBLANES,) + PACKED_TOKEN_TILES), block_end, block_group, block_short,
                  w_up_gate[0], b_up_gate[0], w_down[0], b_down[0])
    out = _combine(yb.reshape(n_rows, LANE_TILES, LANES), dest_flat, h_res, gate_t.T, final_norm_gain)
    return out.reshape(bsz, seqlen, d)
```

```python
import functools

import jax
import jax.numpy as jnp
import numpy as np
from jax import lax
from jax.experimental import pallas as pl
from jax.experimental.pallas import tpu as pltpu

F32 = jnp.float32
BF16 = jnp.bfloat16

D_MODEL = 1024
HG_HEADS = 8
HG_DK = 128
HG_CHUNK = 128
HG_GROUP = 8
AT_Q_HEADS = 16
AT_KV_HEADS = 4
AT_GROUP = AT_Q_HEADS // AT_KV_HEADS
AT_HEAD_DIM = 64
WINDOW = 128
AT_BLOCK = WINDOW
AT_KVWIDTH = AT_KV_HEADS * AT_HEAD_DIM
N_EXPERTS = 32
TOP_K = 4
D_FF = 1024
SWIGLU_LIMIT = 7.0
SWIGLU_ALPHA = 1.702
NORM_EPS = 1e-5
IN_WIDTH = 8704

COL_HQ, COL_FF, COL_FB, COL_HI, COL_HG = 0, 1024, 2048, 3072, 4096
COL_AQ, COL_AK, COL_AV, COL_GA, COL_GB = 5120, 6144, 6400, 6656, 7680

MOE_ROWS = 1024
MOE_PAD = 512
FF_SLAB = 256
MERGE_SUB = 256
BLOCK_DMA_PRIORITY = 1
VMEM_LIMIT = 56 * 1024 * 1024

NT_DIMS = (((1,), (1,)), ((), ()))

SUBLANES = 8
LANES = 128
LANE_TILES = D_MODEL // LANES
TILE_VIEW = (LANE_TILES, SUBLANES, LANES)
TOKEN_TILES = (SUBLANES, LANE_TILES, LANES)


def _store_tiles(ref, val):
    groups = val.shape[0] // SUBLANES
    for t in range(val.shape[1] // LANES):
        ref[:, t] = val[:, t * LANES:(t + 1) * LANES].reshape(groups, SUBLANES, LANES)


def _load_tiles(ref):
    groups = ref.shape[0]
    return jnp.concatenate(
        [ref[:, t].reshape(groups * SUBLANES, LANES) for t in range(ref.shape[1])], axis=1)


PACKED_TILES = LANE_TILES // 2
PACKED_VIEW = (PACKED_TILES, SUBLANES, LANES)
PACKED_TOKEN_TILES = (SUBLANES, PACKED_TILES, LANES)


def _pack_bf16_pairs(x):
    half = x.shape[1] // 2
    lo = pltpu.bitcast(x[:, :half].astype(BF16).astype(F32), jnp.uint32)
    hi = pltpu.bitcast(x[:, half:].astype(BF16).astype(F32), jnp.uint32)
    return hi | lax.shift_right_logical(lo, jnp.uint32(16))


def _unpack_bf16_pairs(w):
    lo = pltpu.bitcast(lax.shift_left(w, jnp.uint32(16)), F32)
    hi = pltpu.bitcast(w & jnp.uint32(0xFFFF0000), F32)
    return jnp.concatenate([lo, hi], axis=1).astype(BF16)


def _sigmoid(x):
    return 0.5 * jnp.tanh(0.5 * x) + 0.5


def _inproj_body(x_ref, g_ref, w_hbm, o_ref, xn_ref, w_stage, w_bf, sem):
    i = pl.program_id(0)
    j = pl.program_id(1)
    tn = w_stage.shape[1]

    @pl.when(i == 0)
    def _():
        cp = pltpu.make_async_copy(w_hbm.at[:, pl.ds(pl.multiple_of(j * tn, LANES), tn)], w_stage,
                                   sem.at[0])
        cp.start()
        cp.wait()
        w_bf[j] = w_stage[...].astype(BF16)

    @pl.when(j == 0)
    def _():
        x = x_ref[...]
        ms = jnp.mean(x * x, axis=-1, keepdims=True)
        xn_ref[...] = (x * lax.rsqrt(ms + NORM_EPS) * g_ref[...]).astype(BF16)

    o_ref[...] = jnp.dot(xn_ref[...], w_bf[j], preferred_element_type=F32).astype(o_ref.dtype)


def _inproj(x2d, gain, w, tm=1024, tn=2176):
    n_tok = x2d.shape[0]
    n_out = w.shape[1]
    n_col = n_out // tn
    return pl.pallas_call(
        _inproj_body,
        grid=(n_tok // tm, n_col),
        in_specs=[
            pl.BlockSpec((tm, D_MODEL), lambda i, j: (i, 0)),
            pl.BlockSpec((1, D_MODEL), lambda i, j: (0, 0)),
            pl.BlockSpec(memory_space=pl.ANY),
        ],
        out_specs=pl.BlockSpec((tm, tn), lambda i, j: (i, j)),
        out_shape=jax.ShapeDtypeStruct((n_tok, n_out), BF16),
        scratch_shapes=[pltpu.VMEM((tm, D_MODEL), BF16), pltpu.VMEM((D_MODEL, tn), F32),
                        pltpu.VMEM((n_col, D_MODEL, tn), BF16), pltpu.SemaphoreType.DMA((1,))],
        compiler_params=pltpu.CompilerParams(
            dimension_semantics=("arbitrary", "arbitrary"), vmem_limit_bytes=VMEM_LIMIT),
        name="inproj",
    )(x2d, gain.reshape(1, D_MODEL), w)


class _Dir:
    def __init__(self, q_ref, f_ref, v_ref, o_ref, st_ref, lb, tri, mask, mid_row, last_row):
        self.q_ref, self.f_ref, self.v_ref, self.o_ref, self.st_ref = q_ref, f_ref, v_ref, o_ref, st_ref
        self.c0, self.c1 = 0.5 * (1.0 + lb), 0.5 * (1.0 - lb)
        self.tri, self.mask, self.mid_row, self.last_row = tri, mask, mid_row, last_row


def _gla_group(chains):
    c = HG_CHUNK
    dk = HG_DK
    pre = []
    for d, r0 in chains:
        h = 0.5 * d.q_ref[pl.ds(r0, c), :].astype(F32)
        q = h * jnp.tanh(h) + h
        ct = d.c1 * jnp.tanh(0.5 * d.f_ref[pl.ds(r0, c), :].astype(F32))
        logf = jnp.log2(d.c0 + ct)
        hi = logf.astype(BF16)
        lo = (logf - hi.astype(F32)).astype(BF16)
        pre.append((q, (1.0 - d.c0) - ct, jnp.concatenate([hi, lo], axis=1)))
    cums = []
    for (d, _), (_, _, hl) in zip(chains, pre):
        r = jnp.dot(d.tri, hl, preferred_element_type=F32)
        cums.append(r[:, :dk] + r[:, dk:])
    mids = []
    for (d, r0), (q, k, _), cum in zip(chains, pre, cums):
        mid = cum[d.mid_row:d.mid_row + 1, :]
        last = cum[d.last_row:d.last_row + 1, :]
        a = cum - mid
        qa = q * jnp.exp2(a)
        ka = k * jnp.exp2(-a)
        qd = (qa * jnp.exp2(mid)).astype(BF16)
        kend = (ka * jnp.exp2(last - mid)).astype(BF16)
        dec = jnp.exp2(last)
        scores = lax.dot_general(qa.astype(BF16), ka.astype(BF16), NT_DIMS,
                                 preferred_element_type=F32)
        mids.append((qd, kend, dec, scores))
    outs = []
    for (d, r0), (qd, kend, dec, scores) in zip(chains, mids):
        v = d.v_ref[pl.ds(r0, c), :]
        sm = jnp.where(d.mask, scores, 0.0).astype(BF16)
        o_intra = jnp.dot(sm, v, preferred_element_type=F32)
        upd = jnp.dot(jnp.transpose(v.astype(F32)).astype(BF16), kend, preferred_element_type=F32)
        outs.append((o_intra, upd))
    for (d, r0), (qd, _, dec, _), (o_intra, upd) in zip(chains, mids, outs):
        st = d.st_ref[...]
        d.o_ref[pl.ds(r0, c), :] = (o_intra + lax.dot_general(
            qd, st.astype(BF16), NT_DIMS, preferred_element_type=F32)).astype(d.o_ref.dtype)
        d.st_ref[...] = st * dec + upd


def _hgrn2_body(qf_ref, ff_ref, vf_ref, qb_ref, fb_ref, vb_ref, lbf_ref, lbb_ref,
                of_ref, ob_ref, sf_ref, sb_ref):
    @pl.when(pl.program_id(2) == 0)
    def _():
        sf_ref[...] = jnp.zeros_like(sf_ref)
        sb_ref[...] = jnp.zeros_like(sb_ref)

    c = HG_CHUNK
    g = HG_GROUP
    n_chunks = qf_ref.shape[0] // c
    row = lax.broadcasted_iota(jnp.int32, (c, c), 0)
    col = lax.broadcasted_iota(jnp.int32, (c, c), 1)
    lower = col <= row
    upper = col >= row
    fwd = _Dir(qf_ref, ff_ref, vf_ref, of_ref, sf_ref, lbf_ref[...],
               jnp.where(lower, 1.0, 0.0).astype(BF16), lower, c // 2 - 1, c - 1)
    bwd = _Dir(qb_ref, fb_ref, vb_ref, ob_ref, sb_ref, lbb_ref[...],
               jnp.where(upper, 1.0, 0.0).astype(BF16), upper, c // 2, 0)

    def step(n, carry):
        chains = []
        for u in range(g):
            chains.append((fwd, pl.multiple_of((n * g + u) * c, c)))
        for u in range(g):
            chains.append((bwd, pl.multiple_of((n_chunks - 1 - n * g - u) * c, c)))
        _gla_group(chains)
        return carry

    lax.fori_loop(0, n_chunks // g, step, 0)


def _hgrn2(proj, lb_fwd, lb_bwd, bsz, seqlen, rows=4096):
    n_tok = bsz * seqlen
    rows = min(rows, seqlen)
    nb = seqlen // rows
    dk = HG_DK

    def fwd_map(col0):
        return lambda b, h, j: (b * nb + j, col0 // dk + h)

    def bwd_map(col0):
        return lambda b, h, j: (b * nb + (nb - 1 - j), col0 // dk + h)

    blk = (rows, dk)
    lb_spec = pl.BlockSpec((1, dk), lambda b, h, j: (0, h))
    return pl.pallas_call(
        _hgrn2_body,
        grid=(bsz, HG_HEADS, nb),
        in_specs=[
            pl.BlockSpec(blk, fwd_map(COL_HQ)), pl.BlockSpec(blk, fwd_map(COL_FF)),
            pl.BlockSpec(blk, fwd_map(COL_HI)),
            pl.BlockSpec(blk, bwd_map(COL_HQ)), pl.BlockSpec(blk, bwd_map(COL_FB)),
            pl.BlockSpec(blk, bwd_map(COL_HI)),
            lb_spec, lb_spec,
        ],
        out_specs=[pl.BlockSpec(blk, fwd_map(0)), pl.BlockSpec(blk, bwd_map(0))],
        out_shape=[jax.ShapeDtypeStruct((n_tok, HG_HEADS * dk), BF16)] * 2,
        scratch_shapes=[pltpu.VMEM((dk, dk), F32), pltpu.VMEM((dk, dk), F32)],
        compiler_params=pltpu.CompilerParams(
            dimension_semantics=("parallel", "parallel", "arbitrary"),
            vmem_limit_bytes=VMEM_LIMIT),
        name="hgrn2",
    )(proj, proj, proj, proj, proj, proj, lb_fwd.reshape(1, -1), lb_bwd.reshape(1, -1))


def _window_attention(sink_ref, q_ref, k_refs, v_refs, starts_sequence, ends_sequence):
    blk = AT_BLOCK
    span = 3 * blk
    dh = AT_HEAD_DIM
    lanes = 2 * dh
    n_q = q_ref.shape[0] // blk
    lo_half = lax.broadcasted_iota(jnp.int32, (blk, lanes), 1) < dh
    row = lax.broadcasted_iota(jnp.int32, (blk, span), 0)
    col = lax.broadcasted_iota(jnp.int32, (blk, span), 1)
    absrel = jnp.abs(row + blk - col)
    base_inner = jnp.where(absrel <= WINDOW, -absrel.astype(F32), -jnp.inf)

    def padded(t, g):
        t = t[:, (g // 2) * lanes:(g // 2 + 1) * lanes].astype(F32)
        if g % 2 == 0:
            lo = jnp.where(lo_half, t, 0.0)
            hi = pltpu.roll(lo, dh, axis=1)
        else:
            hi = jnp.where(lo_half, 0.0, t)
            lo = pltpu.roll(hi, dh, axis=1)
        return lo.astype(BF16), hi.astype(BF16)

    def key_blocks(refs):
        prev_ref, own_ref, next_ref = refs
        return ([prev_ref[...]] + [own_ref[b * blk:(b + 1) * blk, :] for b in range(n_q)]
                + [next_ref[...]])

    k_pads = [[padded(t, g) for t in key_blocks(k_refs)] for g in range(AT_KV_HEADS)]
    v_pads = [[padded(t, g) for t in key_blocks(v_refs)] for g in range(AT_KV_HEADS)]

    def banded(pads, first):
        use = pads[first:first + 3]
        return jnp.concatenate([p[0] for p in use] + [p[1] for p in use], axis=0)

    results = []
    for qb in range(n_q):
        base = base_inner
        if qb == 0:
            base = jnp.where(jnp.logical_and(starts_sequence, col < blk), -jnp.inf, base)
        if qb == n_q - 1:
            base = jnp.where(jnp.logical_and(ends_sequence, col >= 2 * blk), -jnp.inf, base)
        rows = slice(qb * blk, (qb + 1) * blk)
        scores = []
        for g in range(AT_KV_HEADS):
            q2 = q_ref[rows, g * 2 * lanes:(g + 1) * 2 * lanes] * (dh ** -0.5)
            lhs = jnp.concatenate([q2[:, :lanes], q2[:, lanes:]], axis=0).astype(BF16)
            scores.append(lax.dot_general(lhs, banded(k_pads[g], qb), NT_DIMS,
                                          preferred_element_type=F32))
        probs, denoms = [], []
        for g in range(AT_KV_HEADS):
            p_rows, d_rows = [], []
            for r in range(2):
                p_cols, d_cols = [], []
                for c in range(2):
                    h = g * AT_GROUP + 2 * r + c
                    slope = 2.0 ** (-8.0 * (h + 1) / AT_Q_HEADS)
                    sink = sink_ref[h]
                    s = scores[g][r * blk:(r + 1) * blk, c * span:(c + 1) * span] + slope * base
                    m = jnp.maximum(jnp.max(s, axis=-1, keepdims=True), sink)
                    p = jnp.exp(s - m)
                    d_cols.append(jnp.sum(p, axis=-1, keepdims=True) + jnp.exp(sink - m))
                    p_cols.append(p.astype(BF16))
                p_rows.append(jnp.concatenate(p_cols, axis=1))
                d_rows.append(jnp.where(lo_half, d_cols[0], d_cols[1]))
            probs.append(jnp.concatenate(p_rows, axis=0))
            denoms.append(d_rows)
        outs = []
        for g in range(AT_KV_HEADS):
            o = jnp.dot(probs[g], banded(v_pads[g], qb), preferred_element_type=F32)
            for r in range(2):
                outs.append(o[r * blk:(r + 1) * blk, :] / denoms[g][r])
        results.append(jnp.concatenate(outs, axis=-1))
    return results


def _merge_body(n_steps, steps_per_seq, sink_ref, of_ref, ob_ref, hg_ref, ga0_ref, ga1_ref, gb0_ref,
                gb1_ref, q_ref, kp_ref, ko_ref, kn_ref, vp_ref, vo_ref, vn_ref, x_ref,
                gn_ref, wo_ref, g2_ref, wr_ref, br_ref,
                h_ref, gate_ref, dest_ref, cnt_ref, xb_hbm,
                carry_ref, ring, dest_vmem, dest_smem, cnt_vmem, cnt_smem, zero_ref,
                sem_rows, sem_dest, sem_misc):
    i = pl.program_id(0)
    slot = lax.rem(i, 2)
    tm = x_ref.shape[0]
    cap = xb_hbm.shape[0] // N_EXPERTS

    def rows_wait(sl):
        pltpu.make_async_copy(xb_hbm.at[pl.ds(0, tm * TOP_K)], xb_hbm.at[pl.ds(0, tm * TOP_K)],
                              sem_rows.at[sl]).wait()

    def dest_copy(sl):
        return pltpu.make_async_copy(dest_vmem.at[sl], dest_smem.at[sl], sem_dest.at[sl])

    def issue_rows(sl):
        dest_copy(sl).wait()

        def body(g, c):
            for s in range(SUBLANES):
                for k in range(TOP_K):
                    pltpu.make_async_copy(
                        ring.at[sl, g, :, s, :], xb_hbm.at[dest_smem[sl, k, g * SUBLANES + s]],
                        sem_rows.at[sl]).start(priority=k % 2)
            return c
        lax.fori_loop(0, tm // SUBLANES, body, 0, unroll=ISSUE_UNROLL)

    @pl.when(i == 0)
    def _():
        carry_ref[...] = jnp.zeros_like(carry_ref)

    @pl.when(i >= 2)
    def _():
        rows_wait(slot)

    sub = MERGE_SUB
    subs = [slice(s0, s0 + sub) for s0 in range(0, x_ref.shape[0], sub)]
    gn = gn_ref[...]

    step_in_seq = lax.rem(i, steps_per_seq)
    att_blocks = _window_attention(
        sink_ref, q_ref, (kp_ref, ko_ref, kn_ref), (vp_ref, vo_ref, vn_ref),
        step_in_seq == 0, step_in_seq == steps_per_seq - 1)
    per_sub = sub // AT_BLOCK
    y_att = [jnp.concatenate(att_blocks[n * per_sub:(n + 1) * per_sub], axis=0)
             for n in range(len(subs))]

    merged = []
    for rs, ya in zip(subs, y_att):
        o = of_ref[rs, :].astype(F32) + ob_ref[rs, :].astype(F32)
        parts = []
        for h in range(HG_HEADS):
            oh = o[:, h * HG_DK:(h + 1) * HG_DK]
            ms = jnp.mean(oh * oh, axis=-1, keepdims=True)
            parts.append(oh * lax.rsqrt(ms + NORM_EPS) * gn)
        hg = hg_ref[rs, :].astype(F32)
        y_rec = jnp.concatenate(parts, axis=-1) * (hg * _sigmoid(hg))
        gate_a = jnp.concatenate([ga0_ref[rs, :], ga1_ref[rs, :]], axis=-1).astype(F32)
        gate_b = jnp.concatenate([gb0_ref[rs, :], gb1_ref[rs, :]], axis=-1).astype(F32)
        merged.append((_sigmoid(gate_a) * y_rec
                       + _sigmoid(gate_b) * ya).astype(BF16))

    w_out = wo_ref[...].astype(BF16)
    xns = []
    for rs, mg in zip(subs, merged):
        h_res = x_ref[rs, :] + jnp.dot(mg, w_out, preferred_element_type=F32)
        h_ref[rs, :] = h_res
        ms = jnp.mean(h_res * h_res, axis=-1, keepdims=True)
        xn = h_res * lax.rsqrt(ms + NORM_EPS) * g2_ref[...]
        _store_tiles(ring.at[slot, rs.start // SUBLANES:rs.stop // SUBLANES], _pack_bf16_pairs(xn))
        xns.append(xn)

    logits = [lax.dot_general(wr_ref[...], xn, NT_DIMS, preferred_element_type=F32,
                              precision=lax.Precision.HIGHEST) + br_ref[...] for xn in xns]
    e_iota = lax.broadcasted_iota(jnp.int32, (N_EXPERTS, sub), 0).astype(F32)
    picked = []
    for rs, vals in zip(subs, logits):
        tops, idxs = [], []
        for _ in range(TOP_K):
            m = jnp.max(vals, axis=0, keepdims=True)
            sel = jnp.min(jnp.where(vals == m, e_iota, float(N_EXPERTS)), axis=0, keepdims=True)
            tops.append(m)
            idxs.append(sel)
            vals = jnp.where(e_iota == sel, -jnp.inf, vals)
        exps = [jnp.exp(t - tops[0]) for t in tops]
        den = exps[0] + exps[1] + exps[2] + exps[3]
        gate_ref[:, rs] = jnp.concatenate([e / den for e in exps], axis=0)
        picked.append((idxs, jnp.where(vals == -jnp.inf, 1.0, 0.0)))

    s_iota = lax.broadcasted_iota(jnp.int32, (sub, sub), 0)
    t_iota = lax.broadcasted_iota(jnp.int32, (sub, sub), 1)
    strict = jnp.where(s_iota < t_iota, 1.0, 0.0).astype(BF16)
    prefixes = [jnp.dot(chosen.astype(BF16), strict, preferred_element_type=F32)
                for _, chosen in picked]
    carry = carry_ref[...]
    for rs, (idxs, chosen), prefix in zip(subs, picked, prefixes):
        prefix = prefix + carry
        dests = [sel * float(cap) + jnp.sum(jnp.where(e_iota == sel, prefix, 0.0), axis=0, keepdims=True)
                 for sel in idxs]
        dest = jnp.concatenate(dests, axis=0).astype(jnp.int32)
        dest_ref[:, rs] = dest
        dest_vmem[slot, :, rs] = dest
        carry = carry + jnp.sum(chosen, axis=1, keepdims=True)
    carry_ref[...] = carry
    counts = jnp.broadcast_to(carry, cnt_ref.shape).astype(jnp.int32)
    cnt_ref[...] = counts

    dest_copy(slot).start()

    @pl.when(i >= 1)
    def _():
        issue_rows(1 - slot)

    @pl.when(i == n_steps - 1)
    def _():
        issue_rows(slot)
        if n_steps >= 2:
            rows_wait(1 - slot)
        rows_wait(slot)
        cnt_vmem[...] = counts
        cp = pltpu.make_async_copy(cnt_vmem, cnt_smem, sem_misc.at[0])
        cp.start()
        cp.wait()
        zero_ref[...] = jnp.zeros_like(zero_ref)

        def tail(e):
            return pltpu.make_async_copy(
                zero_ref, xb_hbm.at[pl.ds(e * cap + cnt_smem[e, 0], MOE_PAD)], sem_misc.at[0])
        for e in range(N_EXPERTS):
            tail(e).start()
        for e in range(N_EXPERTS):
            tail(e).wait()


def _merge_route(o_f, o_b, proj, sink, x2d, seqlen, gn_gain, w_out, g2, w_router_t, b_router, tm=512):
    n_tok = x2d.shape[0]
    nblk = n_tok // tm
    row_blk = (tm, D_MODEL)
    half = D_MODEL // 2
    half_blk = (tm, half)

    def rows(c):
        return lambda i: (i, c)

    per_step = tm // AT_BLOCK
    n_kv_blocks = n_tok // AT_BLOCK
    kv_blk = (AT_BLOCK, AT_KVWIDTH)
    kv_own = (tm, AT_KVWIDTH)

    def kv_prev(c):
        return lambda i: (jnp.maximum(i * per_step - 1, 0), c)

    def kv_next(c):
        return lambda i: (jnp.minimum((i + 1) * per_step, n_kv_blocks - 1), c)

    kcol = COL_AK // AT_KVWIDTH
    vcol = COL_AV // AT_KVWIDTH
    full = lambda i: (0, 0)
    tok_out = pl.BlockSpec((TOP_K, tm), lambda i: (0, i))
    cap = n_tok + MOE_PAD
    return pl.pallas_call(
        functools.partial(_merge_body, nblk, seqlen // tm),
        grid=(nblk,),
        in_specs=[
            pl.BlockSpec(memory_space=pltpu.SMEM),
            pl.BlockSpec(row_blk, rows(0)), pl.BlockSpec(row_blk, rows(0)),
            pl.BlockSpec(row_blk, rows(COL_HG // D_MODEL)),
            pl.BlockSpec(half_blk, rows(COL_GA // half)),
            pl.BlockSpec(half_blk, rows(COL_GA // half + 1)),
            pl.BlockSpec(half_blk, rows(COL_GB // half)),
            pl.BlockSpec(half_blk, rows(COL_GB // half + 1)),
            pl.BlockSpec(row_blk, rows(COL_AQ // D_MODEL)),
            pl.BlockSpec(kv_blk, kv_prev(kcol)), pl.BlockSpec(kv_own, rows(kcol)),
            pl.BlockSpec(kv_blk, kv_next(kcol)),
            pl.BlockSpec(kv_blk, kv_prev(vcol)), pl.BlockSpec(kv_own, rows(vcol)),
            pl.BlockSpec(kv_blk, kv_next(vcol)),
            pl.BlockSpec(row_blk, rows(0)),
            pl.BlockSpec((1, HG_DK), full),
            pl.BlockSpec((D_MODEL, D_MODEL), full),
            pl.BlockSpec((1, D_MODEL), full),
            pl.BlockSpec((N_EXPERTS, D_MODEL), full),
            pl.BlockSpec((N_EXPERTS, 1), full),
        ],
        out_specs=[
            pl.BlockSpec(row_blk, rows(0)),
            tok_out, tok_out,
            pl.BlockSpec((N_EXPERTS, LANES), full),
            pl.BlockSpec(memory_space=pl.ANY),
        ],
        out_shape=[
            jax.ShapeDtypeStruct((n_tok, D_MODEL), F32),
            jax.ShapeDtypeStruct((TOP_K, n_tok), F32),
            jax.ShapeDtypeStruct((TOP_K, n_tok), jnp.int32),
            jax.ShapeDtypeStruct((N_EXPERTS, LANES), jnp.int32),
            jax.ShapeDtypeStruct((N_EXPERTS * cap, PACKED_TILES, LANES), jnp.uint32),
        ],
        scratch_shapes=[
            pltpu.VMEM((N_EXPERTS, 1), F32),
            pltpu.VMEM((2, tm // SUBLANES) + PACKED_VIEW, jnp.uint32),
            pltpu.VMEM((2, TOP_K, tm), jnp.int32), pltpu.SMEM((2, TOP_K, tm), jnp.int32),
            pltpu.VMEM((N_EXPERTS, LANES), jnp.int32), pltpu.SMEM((N_EXPERTS, LANES), jnp.int32),
            pltpu.VMEM((MOE_PAD, PACKED_TILES, LANES), jnp.uint32),
            pltpu.SemaphoreType.DMA((2,)), pltpu.SemaphoreType.DMA((2,)), pltpu.SemaphoreType.DMA((1,)),
        ],
        compiler_params=pltpu.CompilerParams(
            dimension_semantics=("arbitrary",), vmem_limit_bytes=VMEM_LIMIT),
        name="merge_route",
    )(sink.astype(F32), o_f, o_b, proj, proj, proj, proj, proj, proj, proj, proj, proj, proj, proj, proj,
      x2d, gn_gain.reshape(1, HG_DK), w_out,
      g2.reshape(1, D_MODEL), w_router_t, b_router.reshape(N_EXPERTS, 1))


ISSUE_UNROLL = 2


def _expert_body(bend_ref, bgrp_ref, bshort_ref, x_hbm, wug_ref, bug_ref, wd_ref, bd_ref, y_hbm,
                 wug_bf, wd_bf, act_ref, xbuf, ybuf, sem_in, sem_out):
    e = pl.program_id(0)
    n_used = bend_ref[N_EXPERTS - 1]
    b_first = jnp.where(e == 0, 0, bend_ref[jnp.maximum(e - 1, 0)])
    b_last = bend_ref[e]

    def for_size(b, fn):
        short = bshort_ref[b] == 1

        @pl.when(short)
        def _():
            fn(MOE_PAD // SUBLANES)

        @pl.when(jnp.logical_not(short))
        def _():
            fn(MOE_ROWS // SUBLANES)

    def x_copies(b, slot, groups):
        return [pltpu.make_async_copy(x_hbm.at[pl.ds(bgrp_ref[b], groups), s],
                                      xbuf.at[slot, pl.ds(0, groups), :, s, :], sem_in.at[slot])
                for s in range(SUBLANES)]

    def y_copies(b, slot, groups):
        return [pltpu.make_async_copy(ybuf.at[slot, pl.ds(0, groups), :, s, :],
                                      y_hbm.at[pl.ds(bgrp_ref[b], groups), s], sem_out.at[slot])
                for s in range(SUBLANES)]

    def start_all(copies):
        for cp in copies:
            cp.start(priority=BLOCK_DMA_PRIORITY)

    def wait_all(copies):
        for cp in copies:
            cp.wait()

    @pl.when((e == 0) & (n_used > 0))
    def _():
        for_size(0, lambda groups: start_all(x_copies(0, 0, groups)))

    ns = D_FF // FF_SLAB
    for s in range(ns):
        lo, hi = s * FF_SLAB, (s + 1) * FF_SLAB
        wug_bf[:, 2 * lo:2 * lo + FF_SLAB] = wug_ref[0, :, lo:hi].astype(BF16)
        wug_bf[:, 2 * lo + FF_SLAB:2 * hi] = wug_ref[0, :, D_FF + lo:D_FF + hi].astype(BF16)
    wd_bf[...] = wd_ref[0].astype(BF16)
    b_ug = bug_ref[0]
    b_d = bd_ref[0]

    def mlp(slot, groups):
        rows = groups * SUBLANES
        x = _unpack_bf16_pairs(_load_tiles(xbuf.at[slot, pl.ds(0, groups)]))
        for s in range(ns):
            lo, hi = s * FF_SLAB, (s + 1) * FF_SLAB
            hu = jnp.dot(x, wug_bf[:, 2 * lo:2 * hi], preferred_element_type=F32)
            g = jnp.minimum(hu[:, :FF_SLAB] + b_ug[:, lo:hi], SWIGLU_LIMIT)
            u = jnp.clip(hu[:, FF_SLAB:] + b_ug[:, D_FF + lo:D_FF + hi], -SWIGLU_LIMIT, SWIGLU_LIMIT)
            act_ref[0:rows, lo:hi] = (g * _sigmoid(SWIGLU_ALPHA * g) * (u + 1.0)).astype(BF16)
        _store_tiles(ybuf.at[slot, pl.ds(0, groups)],
                     jnp.dot(act_ref[0:rows, :], wd_bf[...], preferred_element_type=F32) + b_d)

    def block(b, carry):
        slot = lax.rem(b, 2)
        for_size(b, lambda groups: wait_all(x_copies(b, slot, groups)))

        @pl.when(b + 1 < n_used)
        def _():
            for_size(b + 1, lambda groups: start_all(x_copies(b + 1, 1 - slot, groups)))

        @pl.when(b >= 2)
        def _():
            for_size(b - 2, lambda groups: wait_all(y_copies(b - 2, slot, groups)))

        def run(groups):
            mlp(slot, groups)
            start_all(y_copies(b, slot, groups))
        for_size(b, run)
        return carry

    lax.fori_loop(b_first, b_last, block, 0)

    @pl.when(e == N_EXPERTS - 1)
    def _():
        for back in (2, 1):
            @pl.when(n_used >= back)
            def _():
                last = n_used - back
                for_size(last, lambda groups: wait_all(y_copies(last, lax.rem(last, 2), groups)))


def _experts(xb, block_end, block_group, block_short, w_up_gate, b_up_gate, w_down, b_down):
    groups = MOE_ROWS // SUBLANES

    def w_map(e, bend, bgrp, bshort):
        return (e, 0, 0)

    grid_spec = pltpu.PrefetchScalarGridSpec(
        num_scalar_prefetch=3,
        grid=(N_EXPERTS,),
        in_specs=[
            pl.BlockSpec(memory_space=pl.ANY),
            pl.BlockSpec((1, D_MODEL, 2 * D_FF), w_map),
            pl.BlockSpec((1, 1, 2 * D_FF), w_map),
            pl.BlockSpec((1, D_FF, D_MODEL), w_map),
            pl.BlockSpec((1, 1, D_MODEL), w_map),
        ],
        out_specs=pl.BlockSpec(memory_space=pl.ANY),
        scratch_shapes=[
            pltpu.VMEM((D_MODEL, 2 * D_FF), BF16), pltpu.VMEM((D_FF, D_MODEL), BF16),
            pltpu.VMEM((MOE_ROWS, D_FF), BF16),
            pltpu.VMEM((2, groups) + PACKED_VIEW, jnp.uint32), pltpu.VMEM((2, groups) + TILE_VIEW, F32),
            pltpu.SemaphoreType.DMA((2,)), pltpu.SemaphoreType.DMA((2,)),
        ],
    )
    return pl.pallas_call(
        _expert_body,
        grid_spec=grid_spec,
        out_shape=jax.ShapeDtypeStruct((xb.shape[0],) + TOKEN_TILES, F32),
        compiler_params=pltpu.CompilerParams(
            dimension_semantics=("arbitrary",), vmem_limit_bytes=VMEM_LIMIT),
        name="moe_experts",
    )(block_end, block_group, block_short, xb, w_up_gate, b_up_gate.reshape(N_EXPERTS, 1, -1),
      w_down, b_down.reshape(N_EXPERTS, 1, -1))


COMBINE_TOKENS = 128


def _combine_body(dest_ref, yb_hbm, h_ref, gate_ref, gain_ref, o_ref, buf_ref, sem):
    i = pl.program_id(0)
    n_steps = pl.num_programs(0)
    nt = COMBINE_TOKENS

    def issue(step, slot):
        base = step * nt

        def body(g, c):
            for s in range(SUBLANES):
                for k in range(TOP_K):
                    pltpu.make_async_copy(
                        yb_hbm.at[dest_ref[k * (dest_ref.shape[0] // TOP_K) + base + g * SUBLANES + s]],
                        buf_ref.at[slot, k, g, :, s, :], sem.at[slot]).start(priority=k % 2)
            return c
        lax.fori_loop(0, nt // SUBLANES, body, 0, unroll=ISSUE_UNROLL)

    slot = lax.rem(i, 2)

    @pl.when(i == 0)
    def _():
        issue(0, 0)

    @pl.when(i + 1 < n_steps)
    def _():
        issue(i + 1, 1 - slot)

    pltpu.make_async_copy(buf_ref.at[slot], buf_ref.at[slot], sem.at[slot]).wait()

    gates = gate_ref[...]
    acc = h_ref[...]
    for k in range(TOP_K):
        acc = acc + gates[:, k:k + 1] * _load_tiles(buf_ref.at[slot, k])
    ms = jnp.mean(acc * acc, axis=-1, keepdims=True)
    o_ref[...] = acc * lax.rsqrt(ms + NORM_EPS) * gain_ref[...]


def _combine(yb, dest_flat, h_res, gates_tk, final_gain):
    n_tok = h_res.shape[0]
    nt = COMBINE_TOKENS
    grid_spec = pltpu.PrefetchScalarGridSpec(
        num_scalar_prefetch=1,
        grid=(n_tok // nt,),
        in_specs=[
            pl.BlockSpec(memory_space=pl.ANY),
            pl.BlockSpec((nt, D_MODEL), lambda i, d: (i, 0)),
            pl.BlockSpec((nt, TOP_K), lambda i, d: (i, 0)),
            pl.BlockSpec((1, D_MODEL), lambda i, d: (0, 0)),
        ],
        out_specs=pl.BlockSpec((nt, D_MODEL), lambda i, d: (i, 0)),
        scratch_shapes=[pltpu.VMEM((2, TOP_K, nt // SUBLANES) + TILE_VIEW, F32),
                        pltpu.SemaphoreType.DMA((2,))],
    )
    return pl.pallas_call(
        _combine_body,
        grid_spec=grid_spec,
        out_shape=jax.ShapeDtypeStruct((n_tok, D_MODEL), F32),
        compiler_params=pltpu.CompilerParams(
            dimension_semantics=("arbitrary",), vmem_limit_bytes=VMEM_LIMIT),
        name="moe_combine",
    )(dest_flat, yb, h_res, gates_tk, final_gain.reshape(1, D_MODEL))


def kernel(x, norm_mix_gain, w_in, hg_lb_fwd, hg_lb_bwd, hg_norm_gain, attn_sink, w_out,
           norm_ffn_gain, w_router, b_router, w_up_gate, b_up_gate, w_down, b_down,
           final_norm_gain):
    bsz, seqlen, d = x.shape
    n_tok = bsz * seqlen
    x2d = x.reshape(n_tok, d)

    lb_fwd = jax.nn.softmax(hg_lb_fwd.astype(F32), axis=0)[0]
    lb_bwd = jax.nn.softmax(hg_lb_bwd.astype(F32), axis=0)[0]

    proj = _inproj(x2d, norm_mix_gain[0], w_in[0])
    o_f, o_b = _hgrn2(proj, lb_fwd, lb_bwd, bsz, seqlen)
    h_res, gate_t, dest_t, counts, xb = _merge_route(
        o_f, o_b, proj, attn_sink[0], x2d, seqlen, hg_norm_gain[0], w_out[0],
        norm_ffn_gain[0], w_router[0].T.astype(F32), b_router[0].astype(F32))

    n_rows = xb.shape[0]
    cap = n_rows // N_EXPERTS
    counts = counts[:, 0]
    padded = (counts + MOE_PAD - 1) // MOE_PAD * MOE_PAD
    full = padded // MOE_ROWS
    blocks = full + (padded % MOE_ROWS) // MOE_PAD
    block_end = jnp.cumsum(blocks).astype(jnp.int32)
    n_blocks_max = n_tok * TOP_K // MOE_PAD + N_EXPERTS
    b = jnp.arange(n_blocks_max, dtype=jnp.int32)
    owner = (block_end[None, :] <= b[:, None]).astype(jnp.int32)
    expert_of = jnp.minimum(jnp.sum(owner, axis=-1), N_EXPERTS - 1)
    within = b - jnp.sum(owner * blocks[None, :], axis=-1)
    is_owner = expert_of[:, None] == jnp.arange(N_EXPERTS, dtype=jnp.int32)[None, :]
    block_short = (within == jnp.sum(jnp.where(is_owner, full[None, :], 0), axis=-1)).astype(jnp.int32)
    block_group = ((expert_of * cap + within * MOE_ROWS) // SUBLANES).astype(jnp.int32)
    dest_flat = dest_t.reshape(-1)

    yb = _experts(xb.reshape((n_rows // SUBLANES,) + PACKED_TOKEN_TILES), block_end, block_group, block_short,
                  w_up_gate[0], b_up_gate[0], w_down[0], b_down[0])
    out = _combine(yb.reshape(n_rows, LANE_TILES, LANES), dest_flat, h_res, gate_t.T, final_norm_gain)
    return out.reshape(bsz, seqlen, d)
```

```python
import functools

import jax
import jax.numpy as jnp
import numpy as np
from jax import lax
from jax.experimental import pallas as pl
from jax.experimental.pallas import tpu as pltpu

F32 = jnp.float32
BF16 = jnp.bfloat16

D_MODEL = 1024
HG_HEADS = 8
HG_DK = 128
HG_CHUNK = 128
HG_GROUP = 8
AT_Q_HEADS = 16
AT_KV_HEADS = 4
AT_GROUP = AT_Q_HEADS // AT_KV_HEADS
AT_HEAD_DIM = 64
WINDOW = 128
AT_BLOCK = WINDOW
AT_KVWIDTH = AT_KV_HEADS * AT_HEAD_DIM
N_EXPERTS = 32
TOP_K = 4
D_FF = 1024
SWIGLU_LIMIT = 7.0
SWIGLU_ALPHA = 1.702
NORM_EPS = 1e-5
IN_WIDTH = 8704

COL_HQ, COL_FF, COL_FB, COL_HI, COL_HG = 0, 1024, 2048, 3072, 4096
COL_AQ, COL_AK, COL_AV, COL_GA, COL_GB = 5120, 6144, 6400, 6656, 7680

MOE_ROWS = 1024
MOE_PAD = 512
FF_SLAB = 256
MERGE_SUB = 256
BLOCK_DMA_PRIORITY = 1
VMEM_LIMIT = 56 * 1024 * 1024

NT_DIMS = (((1,), (1,)), ((), ()))

SUBLANES = 8
LANES = 128
LANE_TILES = D_MODEL // LANES
TILE_VIEW = (LANE_TILES, SUBLANES, LANES)
TOKEN_TILES = (SUBLANES, LANE_TILES, LANES)


def _store_tiles(ref, val):
    groups = val.shape[0] // SUBLANES
    for t in range(val.shape[1] // LANES):
        ref[:, t] = val[:, t * LANES:(t + 1) * LANES].reshape(groups, SUBLANES, LANES)


def _load_tiles(ref):
    groups = ref.shape[0]
    return jnp.concatenate(
        [ref[:, t].reshape(groups * SUBLANES, LANES) for t in range(ref.shape[1])], axis=1)


PACKED_TILES = LANE_TILES // 2
PACKED_VIEW = (PACKED_TILES, SUBLANES, LANES)
PACKED_TOKEN_TILES = (SUBLANES, PACKED_TILES, LANES)


def _pack_bf16_pairs(x):
    half = x.shape[1] // 2
    lo = pltpu.bitcast(x[:, :half].astype(BF16).astype(F32), jnp.uint32)
    hi = pltpu.bitcast(x[:, half:].astype(BF16).astype(F32), jnp.uint32)
    return hi | lax.shift_right_logical(lo, jnp.uint32(16))


def _unpack_bf16_pairs(w):
    lo = pltpu.bitcast(lax.shift_left(w, jnp.uint32(16)), F32)
    hi = pltpu.bitcast(w & jnp.uint32(0xFFFF0000), F32)
    return jnp.concatenate([lo, hi], axis=1).astype(BF16)


def _sigmoid(x):
    return 0.5 * jnp.tanh(0.5 * x) + 0.5


def _inproj_body(x_ref, g_ref, w_ref, o_ref, xn_ref):
    @pl.when(pl.program_id(1) == 0)
    def _():
        x = x_ref[...]
        ms = jnp.mean(x * x, axis=-1, keepdims=True)
        xn_ref[...] = (x * lax.rsqrt(ms + NORM_EPS) * g_ref[...]).astype(BF16)

    o_ref[...] = jnp.dot(xn_ref[...], w_ref[...].astype(BF16),
                         preferred_element_type=F32).astype(o_ref.dtype)


def _inproj(x2d, gain, w, tm=1024, tn=4352):
    n_tok = x2d.shape[0]
    n_out = w.shape[1]
    return pl.pallas_call(
        _inproj_body,
        grid=(n_tok // tm, n_out // tn),
        in_specs=[
            pl.BlockSpec((tm, D_MODEL), lambda i, j: (i, 0)),
            pl.BlockSpec((1, D_MODEL), lambda i, j: (0, 0)),
            pl.BlockSpec((D_MODEL, tn), lambda i, j: (0, j)),
        ],
        out_specs=pl.BlockSpec((tm, tn), lambda i, j: (i, j)),
        out_shape=jax.ShapeDtypeStruct((n_tok, n_out), BF16),
        scratch_shapes=[pltpu.VMEM((tm, D_MODEL), BF16)],
        compiler_params=pltpu.CompilerParams(
            dimension_semantics=("parallel", "arbitrary"), vmem_limit_bytes=VMEM_LIMIT),
        name="inproj",
    )(x2d, gain.reshape(1, D_MODEL), w)


class _Dir:
    def __init__(self, q_ref, f_ref, v_ref, o_ref, st_ref, lb, tri, mask, mid_row, last_row):
        self.q_ref, self.f_ref, self.v_ref, self.o_ref, self.st_ref = q_ref, f_ref, v_ref, o_ref, st_ref
        self.c0, self.c1 = 0.5 * (1.0 + lb), 0.5 * (1.0 - lb)
        self.tri, self.mask, self.mid_row, self.last_row = tri, mask, mid_row, last_row


def _gla_group(chains):
    c = HG_CHUNK
    dk = HG_DK
    pre = []
    for d, r0 in chains:
        h = 0.5 * d.q_ref[pl.ds(r0, c), :].astype(F32)
        q = h * jnp.tanh(h) + h
        ct = d.c1 * jnp.tanh(0.5 * d.f_ref[pl.ds(r0, c), :].astype(F32))
        logf = jnp.log2(d.c0 + ct)
        hi = logf.astype(BF16)
        lo = (logf - hi.astype(F32)).astype(BF16)
        pre.append((q, (1.0 - d.c0) - ct, jnp.concatenate([hi, lo], axis=1)))
    cums = []
    for (d, _), (_, _, hl) in zip(chains, pre):
        r = jnp.dot(d.tri, hl, preferred_element_type=F32)
        cums.append(r[:, :dk] + r[:, dk:])
    mids = []
    for (d, r0), (q, k, _), cum in zip(chains, pre, cums):
        mid = cum[d.mid_row:d.mid_row + 1, :]
        last = cum[d.last_row:d.last_row + 1, :]
        a = cum - mid
        qa = q * jnp.exp2(a)
        ka = k * jnp.exp2(-a)
        qd = (qa * jnp.exp2(mid)).astype(BF16)
        kend = (ka * jnp.exp2(last - mid)).astype(BF16)
        dec = jnp.exp2(last)
        scores = lax.dot_general(qa.astype(BF16), ka.astype(BF16), NT_DIMS,
                                 preferred_element_type=F32)
        mids.append((qd, kend, dec, scores))
    outs = []
    for (d, r0), (qd, kend, dec, scores) in zip(chains, mids):
        v = d.v_ref[pl.ds(r0, c), :]
        sm = jnp.where(d.mask, scores, 0.0).astype(BF16)
        o_intra = jnp.dot(sm, v, preferred_element_type=F32)
        upd = jnp.dot(jnp.transpose(v.astype(F32)).astype(BF16), kend, preferred_element_type=F32)
        outs.append((o_intra, upd))
    for (d, r0), (qd, _, dec, _), (o_intra, upd) in zip(chains, mids, outs):
        st = d.st_ref[...]
        d.o_ref[pl.ds(r0, c), :] = (o_intra + lax.dot_general(
            qd, st.astype(BF16), NT_DIMS, preferred_element_type=F32)).astype(d.o_ref.dtype)
        d.st_ref[...] = st * dec + upd


def _hgrn2_body(qf_ref, ff_ref, vf_ref, qb_ref, fb_ref, vb_ref, lbf_ref, lbb_ref,
                of_ref, ob_ref, sf_ref, sb_ref):
    @pl.when(pl.program_id(2) == 0)
    def _():
        sf_ref[...] = jnp.zeros_like(sf_ref)
        sb_ref[...] = jnp.zeros_like(sb_ref)

    c = HG_CHUNK
    g = HG_GROUP
    n_chunks = qf_ref.shape[0] // c
    row = lax.broadcasted_iota(jnp.int32, (c, c), 0)
    col = lax.broadcasted_iota(jnp.int32, (c, c), 1)
    lower = col <= row
    upper = col >= row
    fwd = _Dir(qf_ref, ff_ref, vf_ref, of_ref, sf_ref, lbf_ref[...],
               jnp.where(lower, 1.0, 0.0).astype(BF16), lower, c // 2 - 1, c - 1)
    bwd = _Dir(qb_ref, fb_ref, vb_ref, ob_ref, sb_ref, lbb_ref[...],
               jnp.where(upper, 1.0, 0.0).astype(BF16), upper, c // 2, 0)

    def step(n, carry):
        chains = []
        for u in range(g):
            chains.append((fwd, pl.multiple_of((n * g + u) * c, c)))
        for u in range(g):
            chains.append((bwd, pl.multiple_of((n_chunks - 1 - n * g - u) * c, c)))
        _gla_group(chains)
        return carry

    lax.fori_loop(0, n_chunks // g, step, 0)


def _hgrn2(proj, lb_fwd, lb_bwd, bsz, seqlen, rows=4096):
    n_tok = bsz * seqlen
    rows = min(rows, seqlen)
    nb = seqlen // rows
    dk = HG_DK

    def fwd_map(col0):
        return lambda b, h, j: (b * nb + j, col0 // dk + h)

    def bwd_map(col0):
        return lambda b, h, j: (b * nb + (nb - 1 - j), col0 // dk + h)

    blk = (rows, dk)
    lb_spec = pl.BlockSpec((1, dk), lambda b, h, j: (0, h))
    return pl.pallas_call(
        _hgrn2_body,
        grid=(bsz, HG_HEADS, nb),
        in_specs=[
            pl.BlockSpec(blk, fwd_map(COL_HQ)), pl.BlockSpec(blk, fwd_map(COL_FF)),
            pl.BlockSpec(blk, fwd_map(COL_HI)),
            pl.BlockSpec(blk, bwd_map(COL_HQ)), pl.BlockSpec(blk, bwd_map(COL_FB)),
            pl.BlockSpec(blk, bwd_map(COL_HI)),
            lb_spec, lb_spec,
        ],
        out_specs=[pl.BlockSpec(blk, fwd_map(0)), pl.BlockSpec(blk, bwd_map(0))],
        out_shape=[jax.ShapeDtypeStruct((n_tok, HG_HEADS * dk), BF16)] * 2,
        scratch_shapes=[pltpu.VMEM((dk, dk), F32), pltpu.VMEM((dk, dk), F32)],
        compiler_params=pltpu.CompilerParams(
            dimension_semantics=("parallel", "parallel", "arbitrary"),
            vmem_limit_bytes=VMEM_LIMIT),
        name="hgrn2",
    )(proj, proj, proj, proj, proj, proj, lb_fwd.reshape(1, -1), lb_bwd.reshape(1, -1))


def _window_attention(sink_ref, q_ref, k_refs, v_refs, starts_sequence, ends_sequence):
    blk = AT_BLOCK
    span = 3 * blk
    dh = AT_HEAD_DIM
    lanes = 2 * dh
    n_q = q_ref.shape[0] // blk
    lo_half = lax.broadcasted_iota(jnp.int32, (blk, lanes), 1) < dh
    row = lax.broadcasted_iota(jnp.int32, (blk, span), 0)
    col = lax.broadcasted_iota(jnp.int32, (blk, span), 1)
    absrel = jnp.abs(row + blk - col)
    base_inner = jnp.where(absrel <= WINDOW, -absrel.astype(F32), -jnp.inf)

    def padded(t, g):
        t = t[:, (g // 2) * lanes:(g // 2 + 1) * lanes].astype(F32)
        if g % 2 == 0:
            lo = jnp.where(lo_half, t, 0.0)
            hi = pltpu.roll(lo, dh, axis=1)
        else:
            hi = jnp.where(lo_half, 0.0, t)
            lo = pltpu.roll(hi, dh, axis=1)
        return lo.astype(BF16), hi.astype(BF16)

    def key_blocks(refs):
        prev_ref, own_ref, next_ref = refs
        return ([prev_ref[...]] + [own_ref[b * blk:(b + 1) * blk, :] for b in range(n_q)]
                + [next_ref[...]])

    k_pads = [[padded(t, g) for t in key_blocks(k_refs)] for g in range(AT_KV_HEADS)]
    v_pads = [[padded(t, g) for t in key_blocks(v_refs)] for g in range(AT_KV_HEADS)]

    def banded(pads, first):
        use = pads[first:first + 3]
        return jnp.concatenate([p[0] for p in use] + [p[1] for p in use], axis=0)

    results = []
    for qb in range(n_q):
        base = base_inner
        if qb == 0:
            base = jnp.where(jnp.logical_and(starts_sequence, col < blk), -jnp.inf, base)
        if qb == n_q - 1:
            base = jnp.where(jnp.logical_and(ends_sequence, col >= 2 * blk), -jnp.inf, base)
        rows = slice(qb * blk, (qb + 1) * blk)
        scores = []
        for g in range(AT_KV_HEADS):
            q2 = q_ref[rows, g * 2 * lanes:(g + 1) * 2 * lanes] * (dh ** -0.5)
            lhs = jnp.concatenate([q2[:, :lanes], q2[:, lanes:]], axis=0).astype(BF16)
            scores.append(lax.dot_general(lhs, banded(k_pads[g], qb), NT_DIMS,
                                          preferred_element_type=F32))
        probs, denoms = [], []
        for g in range(AT_KV_HEADS):
            p_rows, d_rows = [], []
            for r in range(2):
                p_cols, d_cols = [], []
                for c in range(2):
                    h = g * AT_GROUP + 2 * r + c
                    slope = 2.0 ** (-8.0 * (h + 1) / AT_Q_HEADS)
                    sink = sink_ref[h]
                    s = scores[g][r * blk:(r + 1) * blk, c * span:(c + 1) * span] + slope * base
                    m = jnp.maximum(jnp.max(s, axis=-1, keepdims=True), sink)
                    p = jnp.exp(s - m)
                    d_cols.append(jnp.sum(p, axis=-1, keepdims=True) + jnp.exp(sink - m))
                    p_cols.append(p.astype(BF16))
                p_rows.append(jnp.concatenate(p_cols, axis=1))
                d_rows.append(jnp.where(lo_half, d_cols[0], d_cols[1]))
            probs.append(jnp.concatenate(p_rows, axis=0))
            denoms.append(d_rows)
        outs = []
        for g in range(AT_KV_HEADS):
            o = jnp.dot(probs[g], banded(v_pads[g], qb), preferred_element_type=F32)
            for r in range(2):
                outs.append(o[r * blk:(r + 1) * blk, :] / denoms[g][r])
        results.append(jnp.concatenate(outs, axis=-1))
    return results


def _merge_body(n_steps, steps_per_seq, sink_ref, of_ref, ob_ref, hg_ref, ga0_ref, ga1_ref, gb0_ref,
                gb1_ref, q_ref, kp_ref, ko_ref, kn_ref, vp_ref, vo_ref, vn_ref, x_ref,
                gn_ref, wo_ref, g2_ref, wr_ref, br_ref,
                h_ref, gate_ref, dest_ref, cnt_ref, xb_hbm,
                carry_ref, ring, dest_vmem, dest_smem, cnt_vmem, cnt_smem, zero_ref,
                sem_rows, sem_dest, sem_misc):
    i = pl.program_id(0)
    slot = lax.rem(i, 2)
    tm = x_ref.shape[0]
    cap = xb_hbm.shape[0] // N_EXPERTS

    def rows_wait(sl):
        pltpu.make_async_copy(xb_hbm.at[pl.ds(0, tm * TOP_K)], xb_hbm.at[pl.ds(0, tm * TOP_K)],
                              sem_rows.at[sl]).wait()

    def dest_copy(sl):
        return pltpu.make_async_copy(dest_vmem.at[sl], dest_smem.at[sl], sem_dest.at[sl])

    def issue_rows(sl):
        dest_copy(sl).wait()

        def body(g, c):
            for s in range(SUBLANES):
                for k in range(TOP_K):
                    pltpu.make_async_copy(
                        ring.at[sl, g, :, s, :], xb_hbm.at[dest_smem[sl, k, g * SUBLANES + s]],
                        sem_rows.at[sl]).start(priority=k % 2)
            return c
        lax.fori_loop(0, tm // SUBLANES, body, 0, unroll=ISSUE_UNROLL)

    @pl.when(i == 0)
    def _():
        carry_ref[...] = jnp.zeros_like(carry_ref)

    @pl.when(i >= 2)
    def _():
        rows_wait(slot)

    sub = MERGE_SUB
    subs = [slice(s0, s0 + sub) for s0 in range(0, x_ref.shape[0], sub)]
    gn = gn_ref[...]

    step_in_seq = lax.rem(i, steps_per_seq)
    att_blocks = _window_attention(
        sink_ref, q_ref, (kp_ref, ko_ref, kn_ref), (vp_ref, vo_ref, vn_ref),
        step_in_seq == 0, step_in_seq == steps_per_seq - 1)
    per_sub = sub // AT_BLOCK
    y_att = [jnp.concatenate(att_blocks[n * per_sub:(n + 1) * per_sub], axis=0)
             for n in range(len(subs))]

    merged = []
    for rs, ya in zip(subs, y_att):
        o = of_ref[rs, :].astype(F32) + ob_ref[rs, :].astype(F32)
        parts = []
        for h in range(HG_HEADS):
            oh = o[:, h * HG_DK:(h + 1) * HG_DK]
            ms = jnp.mean(oh * oh, axis=-1, keepdims=True)
            parts.append(oh * lax.rsqrt(ms + NORM_EPS) * gn)
        hg = hg_ref[rs, :].astype(F32)
        y_rec = jnp.concatenate(parts, axis=-1) * (hg * _sigmoid(hg))
        gate_a = jnp.concatenate([ga0_ref[rs, :], ga1_ref[rs, :]], axis=-1).astype(F32)
        gate_b = jnp.concatenate([gb0_ref[rs, :], gb1_ref[rs, :]], axis=-1).astype(F32)
        merged.append((_sigmoid(gate_a) * y_rec
                       + _sigmoid(gate_b) * ya).astype(BF16))

    w_out = wo_ref[...].astype(BF16)
    xns = []
    for rs, mg in zip(subs, merged):
        h_res = x_ref[rs, :] + jnp.dot(mg, w_out, preferred_element_type=F32)
        h_ref[rs, :] = h_res
        ms = jnp.mean(h_res * h_res, axis=-1, keepdims=True)
        xn = h_res * lax.rsqrt(ms + NORM_EPS) * g2_ref[...]
        _store_tiles(ring.at[slot, rs.start // SUBLANES:rs.stop // SUBLANES], _pack_bf16_pairs(xn))
        xns.append(xn)

    logits = [lax.dot_general(wr_ref[...], xn, NT_DIMS, preferred_element_type=F32,
                              precision=lax.Precision.HIGHEST) + br_ref[...] for xn in xns]
    e_iota = lax.broadcasted_iota(jnp.int32, (N_EXPERTS, sub), 0).astype(F32)
    picked = []
    for rs, vals in zip(subs, logits):
        tops, idxs = [], []
        for _ in range(TOP_K):
            m = jnp.max(vals, axis=0, keepdims=True)
            sel = jnp.min(jnp.where(vals == m, e_iota, float(N_EXPERTS)), axis=0, keepdims=True)
            tops.append(m)
            idxs.append(sel)
            vals = jnp.where(e_iota == sel, -jnp.inf, vals)
        exps = [jnp.exp(t - tops[0]) for t in tops]
        den = exps[0] + exps[1] + exps[2] + exps[3]
        gate_ref[:, rs] = jnp.concatenate([e / den for e in exps], axis=0)
        picked.append((idxs, jnp.where(vals == -jnp.inf, 1.0, 0.0)))

    s_iota = lax.broadcasted_iota(jnp.int32, (sub, sub), 0)
    t_iota = lax.broadcasted_iota(jnp.int32, (sub, sub), 1)
    strict = jnp.where(s_iota < t_iota, 1.0, 0.0).astype(BF16)
    prefixes = [jnp.dot(chosen.astype(BF16), strict, preferred_element_type=F32)
                for _, chosen in picked]
    carry = carry_ref[...]
    for rs, (idxs, chosen), prefix in zip(subs, picked, prefixes):
        prefix = prefix + carry
        dests = [sel * float(cap) + jnp.sum(jnp.where(e_iota == sel, prefix, 0.0), axis=0, keepdims=True)
                 for sel in idxs]
        dest = jnp.concatenate(dests, axis=0).astype(jnp.int32)
        dest_ref[:, rs] = dest
        dest_vmem[slot, :, rs] = dest
        carry = carry + jnp.sum(chosen, axis=1, keepdims=True)
    carry_ref[...] = carry
    counts = jnp.broadcast_to(carry, cnt_ref.shape).astype(jnp.int32)
    cnt_ref[...] = counts

    dest_copy(slot).start()

    @pl.when(i >= 1)
    def _():
        issue_rows(1 - slot)

    @pl.when(i == n_steps - 1)
    def _():
        issue_rows(slot)
        if n_steps >= 2:
            rows_wait(1 - slot)
        rows_wait(slot)
        cnt_vmem[...] = counts
        cp = pltpu.make_async_copy(cnt_vmem, cnt_smem, sem_misc.at[0])
        cp.start()
        cp.wait()
        zero_ref[...] = jnp.zeros_like(zero_ref)

        def tail(e):
            return pltpu.make_async_copy(
                zero_ref, xb_hbm.at[pl.ds(e * cap + cnt_smem[e, 0], MOE_PAD)], sem_misc.at[0])
        for e in range(N_EXPERTS):
            tail(e).start()
        for e in range(N_EXPERTS):
            tail(e).wait()


def _merge_route(o_f, o_b, proj, sink, x2d, seqlen, gn_gain, w_out, g2, w_router_t, b_router, tm=512):
    n_tok = x2d.shape[0]
    nblk = n_tok // tm
    row_blk = (tm, D_MODEL)
    half = D_MODEL // 2
    half_blk = (tm, half)

    def rows(c):
        return lambda i: (i, c)

    per_step = tm // AT_BLOCK
    n_kv_blocks = n_tok // AT_BLOCK
    kv_blk = (AT_BLOCK, AT_KVWIDTH)
    kv_own = (tm, AT_KVWIDTH)

    def kv_prev(c):
        return lambda i: (jnp.maximum(i * per_step - 1, 0), c)

    def kv_next(c):
        return lambda i: (jnp.minimum((i + 1) * per_step, n_kv_blocks - 1), c)

    kcol = COL_AK // AT_KVWIDTH
    vcol = COL_AV // AT_KVWIDTH
    full = lambda i: (0, 0)
    tok_out = pl.BlockSpec((TOP_K, tm), lambda i: (0, i))
    cap = n_tok + MOE_PAD
    return pl.pallas_call(
        functools.partial(_merge_body, nblk, seqlen // tm),
        grid=(nblk,),
        in_specs=[
            pl.BlockSpec(memory_space=pltpu.SMEM),
            pl.BlockSpec(row_blk, rows(0)), pl.BlockSpec(row_blk, rows(0)),
            pl.BlockSpec(row_blk, rows(COL_HG // D_MODEL)),
            pl.BlockSpec(half_blk, rows(COL_GA // half)),
            pl.BlockSpec(half_blk, rows(COL_GA // half + 1)),
            pl.BlockSpec(half_blk, rows(COL_GB // half)),
            pl.BlockSpec(half_blk, rows(COL_GB // half + 1)),
            pl.BlockSpec(row_blk, rows(COL_AQ // D_MODEL)),
            pl.BlockSpec(kv_blk, kv_prev(kcol)), pl.BlockSpec(kv_own, rows(kcol)),
            pl.BlockSpec(kv_blk, kv_next(kcol)),
            pl.BlockSpec(kv_blk, kv_prev(vcol)), pl.BlockSpec(kv_own, rows(vcol)),
            pl.BlockSpec(kv_blk, kv_next(vcol)),
            pl.BlockSpec(row_blk, rows(0)),
            pl.BlockSpec((1, HG_DK), full),
            pl.BlockSpec((D_MODEL, D_MODEL), full),
            pl.BlockSpec((1, D_MODEL), full),
            pl.BlockSpec((N_EXPERTS, D_MODEL), full),
            pl.BlockSpec((N_EXPERTS, 1), full),
        ],
        out_specs=[
            pl.BlockSpec(row_blk, rows(0)),
            tok_out, tok_out,
            pl.BlockSpec((N_EXPERTS, LANES), full),
            pl.BlockSpec(memory_space=pl.ANY),
        ],
        out_shape=[
            jax.ShapeDtypeStruct((n_tok, D_MODEL), F32),
            jax.ShapeDtypeStruct((TOP_K, n_tok), F32),
            jax.ShapeDtypeStruct((TOP_K, n_tok), jnp.int32),
            jax.ShapeDtypeStruct((N_EXPERTS, LANES), jnp.int32),
            jax.ShapeDtypeStruct((N_EXPERTS * cap, PACKED_TILES, LANES), jnp.uint32),
        ],
        scratch_shapes=[
            pltpu.VMEM((N_EXPERTS, 1), F32),
            pltpu.VMEM((2, tm // SUBLANES) + PACKED_VIEW, jnp.uint32),
            pltpu.VMEM((2, TOP_K, tm), jnp.int32), pltpu.SMEM((2, TOP_K, tm), jnp.int32),
            pltpu.VMEM((N_EXPERTS, LANES), jnp.int32), pltpu.SMEM((N_EXPERTS, LANES), jnp.int32),
            pltpu.VMEM((MOE_PAD, PACKED_TILES, LANES), jnp.uint32),
            pltpu.SemaphoreType.DMA((2,)), pltpu.SemaphoreType.DMA((2,)), pltpu.SemaphoreType.DMA((1,)),
        ],
        compiler_params=pltpu.CompilerParams(
            dimension_semantics=("arbitrary",), vmem_limit_bytes=VMEM_LIMIT),
        name="merge_route",
    )(sink.astype(F32), o_f, o_b, proj, proj, proj, proj, proj, proj, proj, proj, proj, proj, proj, proj,
      x2d, gn_gain.reshape(1, HG_DK), w_out,
      g2.reshape(1, D_MODEL), w_router_t, b_router.reshape(N_EXPERTS, 1))


ISSUE_UNROLL = 2


def _expert_body(bend_ref, bgrp_ref, bshort_ref, x_hbm, wug_ref, bug_ref, wd_ref, bd_ref, y_hbm,
                 wug_bf, wd_bf, act_ref, xbuf, ybuf, sem_in, sem_out):
    e = pl.program_id(0)
    n_used = bend_ref[N_EXPERTS - 1]
    b_first = jnp.where(e == 0, 0, bend_ref[jnp.maximum(e - 1, 0)])
    b_last = bend_ref[e]

    def for_size(b, fn):
        short = bshort_ref[b] == 1

        @pl.when(short)
        def _():
            fn(MOE_PAD // SUBLANES)

        @pl.when(jnp.logical_not(short))
        def _():
            fn(MOE_ROWS // SUBLANES)

    def x_copies(b, slot, groups):
        return [pltpu.make_async_copy(x_hbm.at[pl.ds(bgrp_ref[b], groups), s],
                                      xbuf.at[slot, pl.ds(0, groups), :, s, :], sem_in.at[slot])
                for s in range(SUBLANES)]

    def y_copies(b, slot, groups):
        return [pltpu.make_async_copy(ybuf.at[slot, pl.ds(0, groups), :, s, :],
                                      y_hbm.at[pl.ds(bgrp_ref[b], groups), s], sem_out.at[slot])
                for s in range(SUBLANES)]

    def start_all(copies):
        for cp in copies:
            cp.start(priority=BLOCK_DMA_PRIORITY)

    def wait_all(copies):
        for cp in copies:
            cp.wait()

    @pl.when((e == 0) & (n_used > 0))
    def _():
        for_size(0, lambda groups: start_all(x_copies(0, 0, groups)))

    ns = D_FF // FF_SLAB
    for s in range(ns):
        lo, hi = s * FF_SLAB, (s + 1) * FF_SLAB
        wug_bf[:, 2 * lo:2 * lo + FF_SLAB] = wug_ref[0, :, lo:hi].astype(BF16)
        wug_bf[:, 2 * lo + FF_SLAB:2 * hi] = wug_ref[0, :, D_FF + lo:D_FF + hi].astype(BF16)
    wd_bf[...] = wd_ref[0].astype(BF16)
    b_ug = bug_ref[0]
    b_d = bd_ref[0]

    def mlp(slot, groups):
        rows = groups * SUBLANES
        x = _unpack_bf16_pairs(_load_tiles(xbuf.at[slot, pl.ds(0, groups)]))
        for s in range(ns):
            lo, hi = s * FF_SLAB, (s + 1) * FF_SLAB
            hu = jnp.dot(x, wug_bf[:, 2 * lo:2 * hi], preferred_element_type=F32)
            g = jnp.minimum(hu[:, :FF_SLAB] + b_ug[:, lo:hi], SWIGLU_LIMIT)
            u = jnp.clip(hu[:, FF_SLAB:] + b_ug[:, D_FF + lo:D_FF + hi], -SWIGLU_LIMIT, SWIGLU_LIMIT)
            act_ref[0:rows, lo:hi] = (g * _sigmoid(SWIGLU_ALPHA * g) * (u + 1.0)).astype(BF16)
        _store_tiles(ybuf.at[slot, pl.ds(0, groups)],
                     jnp.dot(act_ref[0:rows, :], wd_bf[...], preferred_element_type=F32) + b_d)

    def block(b, carry):
        slot = lax.rem(b, 2)
        for_size(b, lambda groups: wait_all(x_copies(b, slot, groups)))

        @pl.when(b + 1 < n_used)
        def _():
            for_size(b + 1, lambda groups: start_all(x_copies(b + 1, 1 - slot, groups)))

        @pl.when(b >= 2)
        def _():
            for_size(b - 2, lambda groups: wait_all(y_copies(b - 2, slot, groups)))

        def run(groups):
            mlp(slot, groups)
            start_all(y_copies(b, slot, groups))
        for_size(b, run)
        return carry

    lax.fori_loop(b_first, b_last, block, 0)

    @pl.when(e == N_EXPERTS - 1)
    def _():
        for back in (2, 1):
            @pl.when(n_used >= back)
            def _():
                last = n_used - back
                for_size(last, lambda groups: wait_all(y_copies(last, lax.rem(last, 2), groups)))


def _experts(xb, block_end, block_group, block_short, w_up_gate, b_up_gate, w_down, b_down):
    groups = MOE_ROWS // SUBLANES

    def w_map(e, bend, bgrp, bshort):
        return (e, 0, 0)

    grid_spec = pltpu.PrefetchScalarGridSpec(
        num_scalar_prefetch=3,
        grid=(N_EXPERTS,),
        in_specs=[
            pl.BlockSpec(memory_space=pl.ANY),
            pl.BlockSpec((1, D_MODEL, 2 * D_FF), w_map),
            pl.BlockSpec((1, 1, 2 * D_FF), w_map),
            pl.BlockSpec((1, D_FF, D_MODEL), w_map),
            pl.BlockSpec((1, 1, D_MODEL), w_map),
        ],
        out_specs=pl.BlockSpec(memory_space=pl.ANY),
        scratch_shapes=[
            pltpu.VMEM((D_MODEL, 2 * D_FF), BF16), pltpu.VMEM((D_FF, D_MODEL), BF16),
            pltpu.VMEM((MOE_ROWS, D_FF), BF16),
            pltpu.VMEM((2, groups) + PACKED_VIEW, jnp.uint32), pltpu.VMEM((2, groups) + TILE_VIEW, F32),
            pltpu.SemaphoreType.DMA((2,)), pltpu.SemaphoreType.DMA((2,)),
        ],
    )
    return pl.pallas_call(
        _expert_body,
        grid_spec=grid_spec,
        out_shape=jax.ShapeDtypeStruct((xb.shape[0],) + TOKEN_TILES, F32),
        compiler_params=pltpu.CompilerParams(
            dimension_semantics=("arbitrary",), vmem_limit_bytes=VMEM_LIMIT),
        name="moe_experts",
    )(block_end, block_group, block_short, xb, w_up_gate, b_up_gate.reshape(N_EXPERTS, 1, -1),
      w_down, b_down.reshape(N_EXPERTS, 1, -1))


COMBINE_TOKENS = 128


def _combine_body(dest_ref, yb_hbm, h_ref, gate_ref, gain_ref, o_ref, buf_ref, sem):
    i = pl.program_id(0)
    n_steps = pl.num_programs(0)
    nt = COMBINE_TOKENS

    def issue(step, slot):
        base = step * nt

        def body(g, c):
            for s in range(SUBLANES):
                for k in range(TOP_K):
                    pltpu.make_async_copy(
                        yb_hbm.at[dest_ref[k * (dest_ref.shape[0] // TOP_K) + base + g * SUBLANES + s]],
                        buf_ref.at[slot, k, g, :, s, :], sem.at[slot]).start(priority=k % 2)
            return c
        lax.fori_loop(0, nt // SUBLANES, body, 0, unroll=ISSUE_UNROLL)

    slot = lax.rem(i, 2)

    @pl.when(i == 0)
    def _():
        issue(0, 0)

    @pl.when(i + 1 < n_steps)
    def _():
        issue(i + 1, 1 - slot)

    pltpu.make_async_copy(buf_ref.at[slot], buf_ref.at[slot], sem.at[slot]).wait()

    gates = gate_ref[...]
    acc = h_ref[...]
    for k in range(TOP_K):
        acc = acc + gates[:, k:k + 1] * _load_tiles(buf_ref.at[slot, k])
    ms = jnp.mean(acc * acc, axis=-1, keepdims=True)
    o_ref[...] = acc * lax.rsqrt(ms + NORM_EPS) * gain_ref[...]


def _combine(yb, dest_flat, h_res, gates_tk, final_gain):
    n_tok = h_res.shape[0]
    nt = COMBINE_TOKENS
    grid_spec = pltpu.PrefetchScalarGridSpec(
        num_scalar_prefetch=1,
        grid=(n_tok // nt,),
        in_specs=[
            pl.BlockSpec(memory_space=pl.ANY),
            pl.BlockSpec((nt, D_MODEL), lambda i, d: (i, 0)),
            pl.BlockSpec((nt, TOP_K), lambda i, d: (i, 0)),
            pl.BlockSpec((1, D_MODEL), lambda i, d: (0, 0)),
        ],
        out_specs=pl.BlockSpec((nt, D_MODEL), lambda i, d: (i, 0)),
        scratch_shapes=[pltpu.VMEM((2, TOP_K, nt // SUBLANES) + TILE_VIEW, F32),
                        pltpu.SemaphoreType.DMA((2,))],
    )
    return pl.pallas_call(
        _combine_body,
        grid_spec=grid_spec,
        out_shape=jax.ShapeDtypeStruct((n_tok, D_MODEL), F32),
        compiler_params=pltpu.CompilerParams(
            dimension_semantics=("arbitrary",), vmem_limit_bytes=VMEM_LIMIT),
        name="moe_combine",
    )(dest_flat, yb, h_res, gates_tk, final_gain.reshape(1, D_MODEL))


def kernel(x, norm_mix_gain, w_in, hg_lb_fwd, hg_lb_bwd, hg_norm_gain, attn_sink, w_out,
           norm_ffn_gain, w_router, b_router, w_up_gate, b_up_gate, w_down, b_down,
           final_norm_gain):
    bsz, seqlen, d = x.shape
    n_tok = bsz * seqlen
    x2d = x.reshape(n_tok, d)

    lb_fwd = jax.nn.softmax(hg_lb_fwd.astype(F32), axis=0)[0]
    lb_bwd = jax.nn.softmax(hg_lb_bwd.astype(F32), axis=0)[0]

    proj = _inproj(x2d, norm_mix_gain[0], w_in[0].astype(BF16))
    o_f, o_b = _hgrn2(proj, lb_fwd, lb_bwd, bsz, seqlen)
    h_res, gate_t, dest_t, counts, xb = _merge_route(
        o_f, o_b, proj, attn_sink[0], x2d, seqlen, hg_norm_gain[0], w_out[0],
        norm_ffn_gain[0], w_router[0].T.astype(F32), b_router[0].astype(F32))

    n_rows = xb.shape[0]
    cap = n_rows // N_EXPERTS
    counts = counts[:, 0]
    padded = (counts + MOE_PAD - 1) // MOE_PAD * MOE_PAD
    full = padded // MOE_ROWS
    blocks = full + (padded % MOE_ROWS) // MOE_PAD
    block_end = jnp.cumsum(blocks).astype(jnp.int32)
    n_blocks_max = n_tok * TOP_K // MOE_PAD + N_EXPERTS
    b = jnp.arange(n_blocks_max, dtype=jnp.int32)
    owner = (block_end[None, :] <= b[:, None]).astype(jnp.int32)
    expert_of = jnp.minimum(jnp.sum(owner, axis=-1), N_EXPERTS - 1)
    within = b - jnp.sum(owner * blocks[None, :], axis=-1)
    is_owner = expert_of[:, None] == jnp.arange(N_EXPERTS, dtype=jnp.int32)[None, :]
    block_short = (within == jnp.sum(jnp.where(is_owner, full[None, :], 0), axis=-1)).astype(jnp.int32)
    block_group = ((expert_of * cap + within * MOE_ROWS) // SUBLANES).astype(jnp.int32)
    dest_flat = dest_t.reshape(-1)

    yb = _experts(xb.reshape((n_rows // SUBLANES,) + PACKED_TOKEN_TILES), block_end, block_group, block_short,
                  w_up_gate[0], b_up_gate[0], w_down[0], b_down[0])
    out = _combine(yb.reshape(n_rows, LANE_TILES, LANES), dest_flat, h_res, gate_t.T, final_norm_gain)
    return out.reshape(bsz, seqlen, d)
```

```python
import functools

import jax
import jax.numpy as jnp
import numpy as np
from jax import lax
from jax.experimental import pallas as pl
from jax.experimental.pallas import tpu as pltpu

F32 = jnp.float32
BF16 = jnp.bfloat16

D_MODEL = 1024
HG_HEADS = 8
HG_DK = 128
HG_CHUNK = 128
HG_GROUP = 8
AT_Q_HEADS = 16
AT_KV_HEADS = 4
AT_GROUP = AT_Q_HEADS // AT_KV_HEADS
AT_HEAD_DIM = 64
WINDOW = 128
AT_BLOCK = WINDOW
AT_KVWIDTH = AT_KV_HEADS * AT_HEAD_DIM
N_EXPERTS = 32
TOP_K = 4
D_FF = 1024
SWIGLU_LIMIT = 7.0
SWIGLU_ALPHA = 1.702
NORM_EPS = 1e-5
IN_WIDTH = 8704

COL_HQ, COL_FF, COL_FB, COL_HI, COL_HG = 0, 1024, 2048, 3072, 4096
COL_AQ, COL_AK, COL_AV, COL_GA, COL_GB = 5120, 6144, 6400, 6656, 7680

MOE_ROWS = 1024
MOE_PAD = 512
FF_SLAB = 256
MERGE_SUB = 256
BLOCK_DMA_PRIORITY = 1
VMEM_LIMIT = 56 * 1024 * 1024

NT_DIMS = (((1,), (1,)), ((), ()))

SUBLANES = 8
LANES = 128
LANE_TILES = D_MODEL // LANES
TILE_VIEW = (LANE_TILES, SUBLANES, LANES)
TOKEN_TILES = (SUBLANES, LANE_TILES, LANES)


def _store_tiles(ref, val):
    groups = val.shape[0] // SUBLANES
    for t in range(val.shape[1] // LANES):
        ref[:, t] = val[:, t * LANES:(t + 1) * LANES].reshape(groups, SUBLANES, LANES)


def _load_tiles(ref):
    groups = ref.shape[0]
    return jnp.concatenate(
        [ref[:, t].reshape(groups * SUBLANES, LANES) for t in range(ref.shape[1])], axis=1)


PACKED_TILES = LANE_TILES // 2
PACKED_VIEW = (PACKED_TILES, SUBLANES, LANES)
PACKED_TOKEN_TILES = (SUBLANES, PACKED_TILES, LANES)


def _pack_bf16_pairs(x):
    half = x.shape[1] // 2
    lo = pltpu.bitcast(x[:, :half].astype(BF16).astype(F32), jnp.uint32)
    hi = pltpu.bitcast(x[:, half:].astype(BF16).astype(F32), jnp.uint32)
    return hi | lax.shift_right_logical(lo, jnp.uint32(16))


def _unpack_bf16_pairs(w):
    lo = pltpu.bitcast(lax.shift_left(w, jnp.uint32(16)), F32)
    hi = pltpu.bitcast(w & jnp.uint32(0xFFFF0000), F32)
    return jnp.concatenate([lo, hi], axis=1).astype(BF16)


def _sigmoid(x):
    return 0.5 * jnp.tanh(0.5 * x) + 0.5


def _inproj_body(x_ref, g_ref, w_ref, o_ref, xn_ref):
    @pl.when(pl.program_id(1) == 0)
    def _():
        x = x_ref[...]
        ms = jnp.mean(x * x, axis=-1, keepdims=True)
        xn_ref[...] = (x * lax.rsqrt(ms + NORM_EPS) * g_ref[...]).astype(BF16)

    o_ref[...] = jnp.dot(xn_ref[...], w_ref[...].astype(BF16),
                         preferred_element_type=F32).astype(o_ref.dtype)


def _inproj(x2d, gain, w, tm=1024, tn=4352):
    n_tok = x2d.shape[0]
    n_out = w.shape[1]
    return pl.pallas_call(
        _inproj_body,
        grid=(n_tok // tm, n_out // tn),
        in_specs=[
            pl.BlockSpec((tm, D_MODEL), lambda i, j: (i, 0)),
            pl.BlockSpec((1, D_MODEL), lambda i, j: (0, 0)),
            pl.BlockSpec((D_MODEL, tn), lambda i, j: (0, j)),
        ],
        out_specs=pl.BlockSpec((tm, tn), lambda i, j: (i, j)),
        out_shape=jax.ShapeDtypeStruct((n_tok, n_out), BF16),
        scratch_shapes=[pltpu.VMEM((tm, D_MODEL), BF16)],
        compiler_params=pltpu.CompilerParams(
            dimension_semantics=("parallel", "arbitrary"), vmem_limit_bytes=VMEM_LIMIT),
        name="inproj",
    )(x2d, gain.reshape(1, D_MODEL), w)


class _Dir:
    def __init__(self, q_ref, f_ref, v_ref, o_ref, st_ref, lb, tri, mask, mid_row, last_row):
        self.q_ref, self.f_ref, self.v_ref, self.o_ref, self.st_ref = q_ref, f_ref, v_ref, o_ref, st_ref
        self.c0, self.c1 = 0.5 * (1.0 + lb), 0.5 * (1.0 - lb)
        self.tri, self.mask, self.mid_row, self.last_row = tri, mask, mid_row, last_row


def _gla_group(chains):
    c = HG_CHUNK
    dk = HG_DK
    pre = []
    for d, r0 in chains:
        h = 0.5 * d.q_ref[pl.ds(r0, c), :].astype(F32)
        q = h * jnp.tanh(h) + h
        ct = d.c1 * jnp.tanh(0.5 * d.f_ref[pl.ds(r0, c), :].astype(F32))
        logf = jnp.log2(d.c0 + ct)
        hi = logf.astype(BF16)
        lo = (logf - hi.astype(F32)).astype(BF16)
        pre.append((q, (1.0 - d.c0) - ct, jnp.concatenate([hi, lo], axis=1)))
    cums = []
    for (d, _), (_, _, hl) in zip(chains, pre):
        r = jnp.dot(d.tri, hl, preferred_element_type=F32)
        cums.append(r[:, :dk] + r[:, dk:])
    mids = []
    for (d, r0), (q, k, _), cum in zip(chains, pre, cums):
        mid = cum[d.mid_row:d.mid_row + 1, :]
        last = cum[d.last_row:d.last_row + 1, :]
        a = cum - mid
        qa = q * jnp.exp2(a)
        ka = k * jnp.exp2(-a)
        qd = (qa * jnp.exp2(mid)).astype(BF16)
        kend = (ka * jnp.exp2(last - mid)).astype(BF16)
        dec = jnp.exp2(last)
        scores = lax.dot_general(qa.astype(BF16), ka.astype(BF16), NT_DIMS,
                                 preferred_element_type=F32)
        mids.append((qd, kend, dec, scores))
    outs = []
    for (d, r0), (qd, kend, dec, scores) in zip(chains, mids):
        v = d.v_ref[pl.ds(r0, c), :]
        sm = jnp.where(d.mask, scores, 0.0).astype(BF16)
        o_intra = jnp.dot(sm, v, preferred_element_type=F32)
        upd = jnp.dot(jnp.transpose(v.astype(F32)).astype(BF16), kend, preferred_element_type=F32)
        outs.append((o_intra, upd))
    for (d, r0), (qd, _, dec, _), (o_intra, upd) in zip(chains, mids, outs):
        st = d.st_ref[...]
        d.o_ref[pl.ds(r0, c), :] = (o_intra + lax.dot_general(
            qd, st.astype(BF16), NT_DIMS, preferred_element_type=F32)).astype(d.o_ref.dtype)
        d.st_ref[...] = st * dec + upd


def _hgrn2_body(qf_ref, ff_ref, vf_ref, qb_ref, fb_ref, vb_ref, lbf_ref, lbb_ref,
                of_ref, ob_ref, sf_ref, sb_ref):
    @pl.when(pl.program_id(2) == 0)
    def _():
        sf_ref[...] = jnp.zeros_like(sf_ref)
        sb_ref[...] = jnp.zeros_like(sb_ref)

    c = HG_CHUNK
    g = HG_GROUP
    n_chunks = qf_ref.shape[0] // c
    row = lax.broadcasted_iota(jnp.int32, (c, c), 0)
    col = lax.broadcasted_iota(jnp.int32, (c, c), 1)
    lower = col <= row
    upper = col >= row
    fwd = _Dir(qf_ref, ff_ref, vf_ref, of_ref, sf_ref, lbf_ref[...],
               jnp.where(lower, 1.0, 0.0).astype(BF16), lower, c // 2 - 1, c - 1)
    bwd = _Dir(qb_ref, fb_ref, vb_ref, ob_ref, sb_ref, lbb_ref[...],
               jnp.where(upper, 1.0, 0.0).astype(BF16), upper, c // 2, 0)

    def step(n, carry):
        chains = []
        for u in range(g):
            chains.append((fwd, pl.multiple_of((n * g + u) * c, c)))
        for u in range(g):
            chains.append((bwd, pl.multiple_of((n_chunks - 1 - n * g - u) * c, c)))
        _gla_group(chains)
        return carry

    lax.fori_loop(0, n_chunks // g, step, 0)


def _hgrn2(proj, lb_fwd, lb_bwd, bsz, seqlen, rows=8192):
    n_tok = bsz * seqlen
    rows = min(rows, seqlen)
    nb = seqlen // rows
    dk = HG_DK

    def fwd_map(col0):
        return lambda b, h, j: (b * nb + j, col0 // dk + h)

    def bwd_map(col0):
        return lambda b, h, j: (b * nb + (nb - 1 - j), col0 // dk + h)

    blk = (rows, dk)
    lb_spec = pl.BlockSpec((1, dk), lambda b, h, j: (0, h))
    return pl.pallas_call(
        _hgrn2_body,
        grid=(bsz, HG_HEADS, nb),
        in_specs=[
            pl.BlockSpec(blk, fwd_map(COL_HQ)), pl.BlockSpec(blk, fwd_map(COL_FF)),
            pl.BlockSpec(blk, fwd_map(COL_HI)),
            pl.BlockSpec(blk, bwd_map(COL_HQ)), pl.BlockSpec(blk, bwd_map(COL_FB)),
            pl.BlockSpec(blk, bwd_map(COL_HI)),
            lb_spec, lb_spec,
        ],
        out_specs=[pl.BlockSpec(blk, fwd_map(0)), pl.BlockSpec(blk, bwd_map(0))],
        out_shape=[jax.ShapeDtypeStruct((n_tok, HG_HEADS * dk), BF16)] * 2,
        scratch_shapes=[pltpu.VMEM((dk, dk), F32), pltpu.VMEM((dk, dk), F32)],
        compiler_params=pltpu.CompilerParams(
            dimension_semantics=("parallel", "parallel", "arbitrary"),
            vmem_limit_bytes=VMEM_LIMIT),
        name="hgrn2",
    )(proj, proj, proj, proj, proj, proj, lb_fwd.reshape(1, -1), lb_bwd.reshape(1, -1))


def _window_attention(sink_ref, q_ref, k_refs, v_refs, starts_sequence, ends_sequence):
    blk = AT_BLOCK
    span = 3 * blk
    dh = AT_HEAD_DIM
    lanes = 2 * dh
    n_q = q_ref.shape[0] // blk
    lo_half = lax.broadcasted_iota(jnp.int32, (blk, lanes), 1) < dh
    row = lax.broadcasted_iota(jnp.int32, (blk, span), 0)
    col = lax.broadcasted_iota(jnp.int32, (blk, span), 1)
    absrel = jnp.abs(row + blk - col)
    base_inner = jnp.where(absrel <= WINDOW, -absrel.astype(F32), -jnp.inf)

    def padded(t, g):
        t = t[:, (g // 2) * lanes:(g // 2 + 1) * lanes].astype(F32)
        if g % 2 == 0:
            lo = jnp.where(lo_half, t, 0.0)
            hi = pltpu.roll(lo, dh, axis=1)
        else:
            hi = jnp.where(lo_half, 0.0, t)
            lo = pltpu.roll(hi, dh, axis=1)
        return lo.astype(BF16), hi.astype(BF16)

    def key_blocks(refs):
        prev_ref, own_ref, next_ref = refs
        return ([prev_ref[...]] + [own_ref[b * blk:(b + 1) * blk, :] for b in range(n_q)]
                + [next_ref[...]])

    k_pads = [[padded(t, g) for t in key_blocks(k_refs)] for g in range(AT_KV_HEADS)]
    v_pads = [[padded(t, g) for t in key_blocks(v_refs)] for g in range(AT_KV_HEADS)]

    def banded(pads, first):
        use = pads[first:first + 3]
        return jnp.concatenate([p[0] for p in use] + [p[1] for p in use], axis=0)

    results = []
    for qb in range(n_q):
        base = base_inner
        if qb == 0:
            base = jnp.where(jnp.logical_and(starts_sequence, col < blk), -jnp.inf, base)
        if qb == n_q - 1:
            base = jnp.where(jnp.logical_and(ends_sequence, col >= 2 * blk), -jnp.inf, base)
        rows = slice(qb * blk, (qb + 1) * blk)
        scores = []
        for g in range(AT_KV_HEADS):
            q2 = q_ref[rows, g * 2 * lanes:(g + 1) * 2 * lanes] * (dh ** -0.5)
            lhs = jnp.concatenate([q2[:, :lanes], q2[:, lanes:]], axis=0).astype(BF16)
            scores.append(lax.dot_general(lhs, banded(k_pads[g], qb), NT_DIMS,
                                          preferred_element_type=F32))
        probs, denoms = [], []
        for g in range(AT_KV_HEADS):
            p_rows, d_rows = [], []
            for r in range(2):
                p_cols, d_cols = [], []
                for c in range(2):
                    h = g * AT_GROUP + 2 * r + c
                    slope = 2.0 ** (-8.0 * (h + 1) / AT_Q_HEADS)
                    sink = sink_ref[h]
                    s = scores[g][r * blk:(r + 1) * blk, c * span:(c + 1) * span] + slope * base
                    m = jnp.maximum(jnp.max(s, axis=-1, keepdims=True), sink)
                    p = jnp.exp(s - m)
                    d_cols.append(jnp.sum(p, axis=-1, keepdims=True) + jnp.exp(sink - m))
                    p_cols.append(p.astype(BF16))
                p_rows.append(jnp.concatenate(p_cols, axis=1))
                d_rows.append(jnp.where(lo_half, d_cols[0], d_cols[1]))
            probs.append(jnp.concatenate(p_rows, axis=0))
            denoms.append(d_rows)
        outs = []
        for g in range(AT_KV_HEADS):
            o = jnp.dot(probs[g], banded(v_pads[g], qb), preferred_element_type=F32)
            for r in range(2):
                outs.append(o[r * blk:(r + 1) * blk, :] / denoms[g][r])
        results.append(jnp.concatenate(outs, axis=-1))
    return results


def _merge_body(n_steps, steps_per_seq, sink_ref, of_ref, ob_ref, hg_ref, ga0_ref, ga1_ref, gb0_ref,
                gb1_ref, q_ref, kp_ref, ko_ref, kn_ref, vp_ref, vo_ref, vn_ref, x_ref,
                gn_ref, wo_ref, g2_ref, wr_ref, br_ref,
                h_ref, gate_ref, dest_ref, cnt_ref, xb_hbm,
                carry_ref, ring, dest_vmem, dest_smem, cnt_vmem, cnt_smem, zero_ref,
                sem_rows, sem_dest, sem_misc):
    i = pl.program_id(0)
    slot = lax.rem(i, 2)
    tm = x_ref.shape[0]
    cap = xb_hbm.shape[0] // N_EXPERTS

    def rows_wait(sl):
        pltpu.make_async_copy(xb_hbm.at[pl.ds(0, tm * TOP_K)], xb_hbm.at[pl.ds(0, tm * TOP_K)],
                              sem_rows.at[sl]).wait()

    def dest_copy(sl):
        return pltpu.make_async_copy(dest_vmem.at[sl], dest_smem.at[sl], sem_dest.at[sl])

    def issue_rows(sl):
        dest_copy(sl).wait()

        def body(g, c):
            for s in range(SUBLANES):
                for k in range(TOP_K):
                    pltpu.make_async_copy(
                        ring.at[sl, g, :, s, :], xb_hbm.at[dest_smem[sl, k, g * SUBLANES + s]],
                        sem_rows.at[sl]).start(priority=k % 2)
            return c
        lax.fori_loop(0, tm // SUBLANES, body, 0, unroll=ISSUE_UNROLL)

    @pl.when(i == 0)
    def _():
        carry_ref[...] = jnp.zeros_like(carry_ref)

    @pl.when(i >= 2)
    def _():
        rows_wait(slot)

    sub = MERGE_SUB
    subs = [slice(s0, s0 + sub) for s0 in range(0, x_ref.shape[0], sub)]
    gn = gn_ref[...]

    step_in_seq = lax.rem(i, steps_per_seq)
    att_blocks = _window_attention(
        sink_ref, q_ref, (kp_ref, ko_ref, kn_ref), (vp_ref, vo_ref, vn_ref),
        step_in_seq == 0, step_in_seq == steps_per_seq - 1)
    per_sub = sub // AT_BLOCK
    y_att = [jnp.concatenate(att_blocks[n * per_sub:(n + 1) * per_sub], axis=0)
             for n in range(len(subs))]

    merged = []
    for rs, ya in zip(subs, y_att):
        o = of_ref[rs, :].astype(F32) + ob_ref[rs, :].astype(F32)
        parts = []
        for h in range(HG_HEADS):
            oh = o[:, h * HG_DK:(h + 1) * HG_DK]
            ms = jnp.mean(oh * oh, axis=-1, keepdims=True)
            parts.append(oh * lax.rsqrt(ms + NORM_EPS) * gn)
        hg = hg_ref[rs, :].astype(F32)
        y_rec = jnp.concatenate(parts, axis=-1) * (hg * _sigmoid(hg))
        gate_a = jnp.concatenate([ga0_ref[rs, :], ga1_ref[rs, :]], axis=-1).astype(F32)
        gate_b = jnp.concatenate([gb0_ref[rs, :], gb1_ref[rs, :]], axis=-1).astype(F32)
        merged.append((_sigmoid(gate_a) * y_rec
                       + _sigmoid(gate_b) * ya).astype(BF16))

    w_out = wo_ref[...].astype(BF16)
    xns = []
    for rs, mg in zip(subs, merged):
        h_res = x_ref[rs, :] + jnp.dot(mg, w_out, preferred_element_type=F32)
        h_ref[rs, :] = h_res
        ms = jnp.mean(h_res * h_res, axis=-1, keepdims=True)
        xn = h_res * lax.rsqrt(ms + NORM_EPS) * g2_ref[...]
        _store_tiles(ring.at[slot, rs.start // SUBLANES:rs.stop // SUBLANES], _pack_bf16_pairs(xn))
        xns.append(xn)

    logits = [lax.dot_general(wr_ref[...], xn, NT_DIMS, preferred_element_type=F32,
                              precision=lax.Precision.HIGHEST) + br_ref[...] for xn in xns]
    e_iota = lax.broadcasted_iota(jnp.int32, (N_EXPERTS, sub), 0).astype(F32)
    picked = []
    for rs, vals in zip(subs, logits):
        tops, idxs = [], []
        for _ in range(TOP_K):
            m = jnp.max(vals, axis=0, keepdims=True)
            sel = jnp.min(jnp.where(vals == m, e_iota, float(N_EXPERTS)), axis=0, keepdims=True)
            tops.append(m)
            idxs.append(sel)
            vals = jnp.where(e_iota == sel, -jnp.inf, vals)
        exps = [jnp.exp(t - tops[0]) for t in tops]
        den = exps[0] + exps[1] + exps[2] + exps[3]
        gate_ref[:, rs] = jnp.concatenate([e / den for e in exps], axis=0)
        picked.append((idxs, jnp.where(vals == -jnp.inf, 1.0, 0.0)))

    s_iota = lax.broadcasted_iota(jnp.int32, (sub, sub), 0)
    t_iota = lax.broadcasted_iota(jnp.int32, (sub, sub), 1)
    strict = jnp.where(s_iota < t_iota, 1.0, 0.0).astype(BF16)
    prefixes = [jnp.dot(chosen.astype(BF16), strict, preferred_element_type=F32)
                for _, chosen in picked]
    carry = carry_ref[...]
    for rs, (idxs, chosen), prefix in zip(subs, picked, prefixes):
        prefix = prefix + carry
        dests = [sel * float(cap) + jnp.sum(jnp.where(e_iota == sel, prefix, 0.0), axis=0, keepdims=True)
                 for sel in idxs]
        dest = jnp.concatenate(dests, axis=0).astype(jnp.int32)
        dest_ref[:, rs] = dest
        dest_vmem[slot, :, rs] = dest
        carry = carry + jnp.sum(chosen, axis=1, keepdims=True)
    carry_ref[...] = carry
    counts = jnp.broadcast_to(carry, cnt_ref.shape).astype(jnp.int32)
    cnt_ref[...] = counts

    dest_copy(slot).start()

    @pl.when(i >= 1)
    def _():
        issue_rows(1 - slot)

    @pl.when(i == n_steps - 1)
    def _():
        issue_rows(slot)
        if n_steps >= 2:
            rows_wait(1 - slot)
        rows_wait(slot)
        cnt_vmem[...] = counts
        cp = pltpu.make_async_copy(cnt_vmem, cnt_smem, sem_misc.at[0])
        cp.start()
        cp.wait()
        zero_ref[...] = jnp.zeros_like(zero_ref)

        def tail(e):
            return pltpu.make_async_copy(
                zero_ref, xb_hbm.at[pl.ds(e * cap + cnt_smem[e, 0], MOE_PAD)], sem_misc.at[0])
        for e in range(N_EXPERTS):
            tail(e).start()
        for e in range(N_EXPERTS):
            tail(e).wait()


def _merge_route(o_f, o_b, proj, sink, x2d, seqlen, gn_gain, w_out, g2, w_router_t, b_router, tm=512):
    n_tok = x2d.shape[0]
    nblk = n_tok // tm
    row_blk = (tm, D_MODEL)
    half = D_MODEL // 2
    half_blk = (tm, half)

    def rows(c):
        return lambda i: (i, c)

    per_step = tm // AT_BLOCK
    n_kv_blocks = n_tok // AT_BLOCK
    kv_blk = (AT_BLOCK, AT_KVWIDTH)
    kv_own = (tm, AT_KVWIDTH)

    def kv_prev(c):
        return lambda i: (jnp.maximum(i * per_step - 1, 0), c)

    def kv_next(c):
        return lambda i: (jnp.minimum((i + 1) * per_step, n_kv_blocks - 1), c)

    kcol = COL_AK // AT_KVWIDTH
    vcol = COL_AV // AT_KVWIDTH
    full = lambda i: (0, 0)
    tok_out = pl.BlockSpec((TOP_K, tm), lambda i: (0, i))
    cap = n_tok + MOE_PAD
    return pl.pallas_call(
        functools.partial(_merge_body, nblk, seqlen // tm),
        grid=(nblk,),
        in_specs=[
            pl.BlockSpec(memory_space=pltpu.SMEM),
            pl.BlockSpec(row_blk, rows(0)), pl.BlockSpec(row_blk, rows(0)),
            pl.BlockSpec(row_blk, rows(COL_HG // D_MODEL)),
            pl.BlockSpec(half_blk, rows(COL_GA // half)),
            pl.BlockSpec(half_blk, rows(COL_GA // half + 1)),
            pl.BlockSpec(half_blk, rows(COL_GB // half)),
            pl.BlockSpec(half_blk, rows(COL_GB // half + 1)),
            pl.BlockSpec(row_blk, rows(COL_AQ // D_MODEL)),
            pl.BlockSpec(kv_blk, kv_prev(kcol)), pl.BlockSpec(kv_own, rows(kcol)),
            pl.BlockSpec(kv_blk, kv_next(kcol)),
            pl.BlockSpec(kv_blk, kv_prev(vcol)), pl.BlockSpec(kv_own, rows(vcol)),
            pl.BlockSpec(kv_blk, kv_next(vcol)),
            pl.BlockSpec(row_blk, rows(0)),
            pl.BlockSpec((1, HG_DK), full),
            pl.BlockSpec((D_MODEL, D_MODEL), full),
            pl.BlockSpec((1, D_MODEL), full),
            pl.BlockSpec((N_EXPERTS, D_MODEL), full),
            pl.BlockSpec((N_EXPERTS, 1), full),
        ],
        out_specs=[
            pl.BlockSpec(row_blk, rows(0)),
            tok_out, tok_out,
            pl.BlockSpec((N_EXPERTS, LANES), full),
            pl.BlockSpec(memory_space=pl.ANY),
        ],
        out_shape=[
            jax.ShapeDtypeStruct((n_tok, D_MODEL), F32),
            jax.ShapeDtypeStruct((TOP_K, n_tok), F32),
            jax.ShapeDtypeStruct((TOP_K, n_tok), jnp.int32),
            jax.ShapeDtypeStruct((N_EXPERTS, LANES), jnp.int32),
            jax.ShapeDtypeStruct((N_EXPERTS * cap, PACKED_TILES, LANES), jnp.uint32),
        ],
        scratch_shapes=[
            pltpu.VMEM((N_EXPERTS, 1), F32),
            pltpu.VMEM((2, tm // SUBLANES) + PACKED_VIEW, jnp.uint32),
            pltpu.VMEM((2, TOP_K, tm), jnp.int32), pltpu.SMEM((2, TOP_K, tm), jnp.int32),
            pltpu.VMEM((N_EXPERTS, LANES), jnp.int32), pltpu.SMEM((N_EXPERTS, LANES), jnp.int32),
            pltpu.VMEM((MOE_PAD, PACKED_TILES, LANES), jnp.uint32),
            pltpu.SemaphoreType.DMA((2,)), pltpu.SemaphoreType.DMA((2,)), pltpu.SemaphoreType.DMA((1,)),
        ],
        compiler_params=pltpu.CompilerParams(
            dimension_semantics=("arbitrary",), vmem_limit_bytes=VMEM_LIMIT),
        name="merge_route",
    )(sink.astype(F32), o_f, o_b, proj, proj, proj, proj, proj, proj, proj, proj, proj, proj, proj, proj,
      x2d, gn_gain.reshape(1, HG_DK), w_out,
      g2.reshape(1, D_MODEL), w_router_t, b_router.reshape(N_EXPERTS, 1))


ISSUE_UNROLL = 2


def _expert_body(bend_ref, bgrp_ref, bshort_ref, x_hbm, wug_ref, bug_ref, wd_ref, bd_ref, y_hbm,
                 wug_bf, wd_bf, act_ref, xbuf, ybuf, sem_in, sem_out):
    e = pl.program_id(0)
    n_used = bend_ref[N_EXPERTS - 1]
    b_first = jnp.where(e == 0, 0, bend_ref[jnp.maximum(e - 1, 0)])
    b_last = bend_ref[e]

    def for_size(b, fn):
        short = bshort_ref[b] == 1

        @pl.when(short)
        def _():
            fn(MOE_PAD // SUBLANES)

        @pl.when(jnp.logical_not(short))
        def _():
            fn(MOE_ROWS // SUBLANES)

    def x_copies(b, slot, groups):
        return [pltpu.make_async_copy(x_hbm.at[pl.ds(bgrp_ref[b], groups), s],
                                      xbuf.at[slot, pl.ds(0, groups), :, s, :], sem_in.at[slot])
                for s in range(SUBLANES)]

    def y_copies(b, slot, groups):
        return [pltpu.make_async_copy(ybuf.at[slot, pl.ds(0, groups), :, s, :],
                                      y_hbm.at[pl.ds(bgrp_ref[b], groups), s], sem_out.at[slot])
                for s in range(SUBLANES)]

    def start_all(copies):
        for cp in copies:
            cp.start(priority=BLOCK_DMA_PRIORITY)

    def wait_all(copies):
        for cp in copies:
            cp.wait()

    @pl.when((e == 0) & (n_used > 0))
    def _():
        for_size(0, lambda groups: start_all(x_copies(0, 0, groups)))

    ns = D_FF // FF_SLAB
    for s in range(ns):
        lo, hi = s * FF_SLAB, (s + 1) * FF_SLAB
        wug_bf[:, 2 * lo:2 * lo + FF_SLAB] = wug_ref[0, :, lo:hi].astype(BF16)
        wug_bf[:, 2 * lo + FF_SLAB:2 * hi] = wug_ref[0, :, D_FF + lo:D_FF + hi].astype(BF16)
    wd_bf[...] = wd_ref[0].astype(BF16)
    b_ug = bug_ref[0]
    b_d = bd_ref[0]

    def mlp(slot, groups):
        rows = groups * SUBLANES
        x = _unpack_bf16_pairs(_load_tiles(xbuf.at[slot, pl.ds(0, groups)]))
        for s in range(ns):
            lo, hi = s * FF_SLAB, (s + 1) * FF_SLAB
            hu = jnp.dot(x, wug_bf[:, 2 * lo:2 * hi], preferred_element_type=F32)
            g = jnp.minimum(hu[:, :FF_SLAB] + b_ug[:, lo:hi], SWIGLU_LIMIT)
            u = jnp.clip(hu[:, FF_SLAB:] + b_ug[:, D_FF + lo:D_FF + hi], -SWIGLU_LIMIT, SWIGLU_LIMIT)
            act_ref[0:rows, lo:hi] = (g * _sigmoid(SWIGLU_ALPHA * g) * (u + 1.0)).astype(BF16)
        _store_tiles(ybuf.at[slot, pl.ds(0, groups)],
                     jnp.dot(act_ref[0:rows, :], wd_bf[...], preferred_element_type=F32) + b_d)

    def block(b, carry):
        slot = lax.rem(b, 2)
        for_size(b, lambda groups: wait_all(x_copies(b, slot, groups)))

        @pl.when(b + 1 < n_used)
        def _():
            for_size(b + 1, lambda groups: start_all(x_copies(b + 1, 1 - slot, groups)))

        @pl.when(b >= 2)
        def _():
            for_size(b - 2, lambda groups: wait_all(y_copies(b - 2, slot, groups)))

        def run(groups):
            mlp(slot, groups)
            start_all(y_copies(b, slot, groups))
        for_size(b, run)
        return carry

    lax.fori_loop(b_first, b_last, block, 0)

    @pl.when(e == N_EXPERTS - 1)
    def _():
        for back in (2, 1):
            @pl.when(n_used >= back)
            def _():
                last = n_used - back
                for_size(last, lambda groups: wait_all(y_copies(last, lax.rem(last, 2), groups)))


def _experts(xb, block_end, block_group, block_short, w_up_gate, b_up_gate, w_down, b_down):
    groups = MOE_ROWS // SUBLANES

    def w_map(e, bend, bgrp, bshort):
        return (e, 0, 0)

    grid_spec = pltpu.PrefetchScalarGridSpec(
        num_scalar_prefetch=3,
        grid=(N_EXPERTS,),
        in_specs=[
            pl.BlockSpec(memory_space=pl.ANY),
            pl.BlockSpec((1, D_MODEL, 2 * D_FF), w_map),
            pl.BlockSpec((1, 1, 2 * D_FF), w_map),
            pl.BlockSpec((1, D_FF, D_MODEL), w_map),
            pl.BlockSpec((1, 1, D_MODEL), w_map),
        ],
        out_specs=pl.BlockSpec(memory_space=pl.ANY),
        scratch_shapes=[
            pltpu.VMEM((D_MODEL, 2 * D_FF), BF16), pltpu.VMEM((D_FF, D_MODEL), BF16),
            pltpu.VMEM((MOE_ROWS, D_FF), BF16),
            pltpu.VMEM((2, groups) + PACKED_VIEW, jnp.uint32), pltpu.VMEM((2, groups) + TILE_VIEW, F32),
            pltpu.SemaphoreType.DMA((2,)), pltpu.SemaphoreType.DMA((2,)),
        ],
    )
    return pl.pallas_call(
        _expert_body,
        grid_spec=grid_spec,
        out_shape=jax.ShapeDtypeStruct((xb.shape[0],) + TOKEN_TILES, F32),
        compiler_params=pltpu.CompilerParams(
            dimension_semantics=("arbitrary",), vmem_limit_bytes=VMEM_LIMIT),
        name="moe_experts",
    )(block_end, block_group, block_short, xb, w_up_gate, b_up_gate.reshape(N_EXPERTS, 1, -1),
      w_down, b_down.reshape(N_EXPERTS, 1, -1))


COMBINE_TOKENS = 256


def _combine_body(dest_ref, yb_hbm, h_ref, gate_ref, gain_ref, o_ref, buf_ref, sem):
    i = pl.program_id(0)
    n_steps = pl.num_programs(0)
    nt = COMBINE_TOKENS

    def issue(step, slot):
        base = step * nt

        def body(g, c):
            for s in range(SUBLANES):
                for k in range(TOP_K):
                    pltpu.make_async_copy(
                        yb_hbm.at[dest_ref[k * (dest_ref.shape[0] // TOP_K) + base + g * SUBLANES + s]],
                        buf_ref.at[slot, k, g, :, s, :], sem.at[slot]).start(priority=k % 2)
            return c
        lax.fori_loop(0, nt // SUBLANES, body, 0, unroll=ISSUE_UNROLL)

    slot = lax.rem(i, 2)

    @pl.when(i == 0)
    def _():
        issue(0, 0)

    @pl.when(i + 1 < n_steps)
    def _():
        issue(i + 1, 1 - slot)

    pltpu.make_async_copy(buf_ref.at[slot], buf_ref.at[slot], sem.at[slot]).wait()

    gates = gate_ref[...]
    acc = h_ref[...]
    for k in range(TOP_K):
        acc = acc + gates[:, k:k + 1] * _load_tiles(buf_ref.at[slot, k])
    ms = jnp.mean(acc * acc, axis=-1, keepdims=True)
    o_ref[...] = acc * lax.rsqrt(ms + NORM_EPS) * gain_ref[...]


def _combine(yb, dest_flat, h_res, gates_tk, final_gain):
    n_tok = h_res.shape[0]
    nt = COMBINE_TOKENS
    grid_spec = pltpu.PrefetchScalarGridSpec(
        num_scalar_prefetch=1,
        grid=(n_tok // nt,),
        in_specs=[
            pl.BlockSpec(memory_space=pl.ANY),
            pl.BlockSpec((nt, D_MODEL), lambda i, d: (i, 0)),
            pl.BlockSpec((nt, TOP_K), lambda i, d: (i, 0)),
            pl.BlockSpec((1, D_MODEL), lambda i, d: (0, 0)),
        ],
        out_specs=pl.BlockSpec((nt, D_MODEL), lambda i, d: (i, 0)),
        scratch_shapes=[pltpu.VMEM((2, TOP_K, nt // SUBLANES) + TILE_VIEW, F32),
                        pltpu.SemaphoreType.DMA((2,))],
    )
    return pl.pallas_call(
        _combine_body,
        grid_spec=grid_spec,
        out_shape=jax.ShapeDtypeStruct((n_tok, D_MODEL), F32),
        compiler_params=pltpu.CompilerParams(
            dimension_semantics=("arbitrary",), vmem_limit_bytes=VMEM_LIMIT),
        name="moe_combine",
    )(dest_flat, yb, h_res, gates_tk, final_gain.reshape(1, D_MODEL))


def kernel(x, norm_mix_gain, w_in, hg_lb_fwd, hg_lb_bwd, hg_norm_gain, attn_sink, w_out,
           norm_ffn_gain, w_router, b_router, w_up_gate, b_up_gate, w_down, b_down,
           final_norm_gain):
    bsz, seqlen, d = x.shape
    n_tok = bsz * seqlen
    x2d = x.reshape(n_tok, d)

    lb_fwd = jax.nn.softmax(hg_lb_fwd.astype(F32), axis=0)[0]
    lb_bwd = jax.nn.softmax(hg_lb_bwd.astype(F32), axis=0)[0]

    proj = _inproj(x2d, norm_mix_gain[0], w_in[0].astype(BF16))
    o_f, o_b = _hgrn2(proj, lb_fwd, lb_bwd, bsz, seqlen)
    h_res, gate_t, dest_t, counts, xb = _merge_route(
        o_f, o_b, proj, attn_sink[0], x2d, seqlen, hg_norm_gain[0], w_out[0],
        norm_ffn_gain[0], w_router[0].T.astype(F32), b_router[0].astype(F32))

    n_rows = xb.shape[0]
    cap = n_rows // N_EXPERTS
    counts = counts[:, 0]
    padded = (counts + MOE_PAD - 1) // MOE_PAD * MOE_PAD
    full = padded // MOE_ROWS
    blocks = full + (padded % MOE_ROWS) // MOE_PAD
    block_end = jnp.cumsum(blocks).astype(jnp.int32)
    n_blocks_max = n_tok * TOP_K // MOE_PAD + N_EXPERTS
    b = jnp.arange(n_blocks_max, dtype=jnp.int32)
    owner = (block_end[None, :] <= b[:, None]).astype(jnp.int32)
    expert_of = jnp.minimum(jnp.sum(owner, axis=-1), N_EXPERTS - 1)
    within = b - jnp.sum(owner * blocks[None, :], axis=-1)
    is_owner = expert_of[:, None] == jnp.arange(N_EXPERTS, dtype=jnp.int32)[None, :]
    block_short = (within == jnp.sum(jnp.where(is_owner, full[None, :], 0), axis=-1)).astype(jnp.int32)
    block_group = ((expert_of * cap + within * MOE_ROWS) // SUBLANES).astype(jnp.int32)
    dest_flat = dest_t.reshape(-1)

    yb = _experts(xb.reshape((n_rows // SUBLANES,) + PACKED_TOKEN_TILES), block_end, block_group, block_short,
                  w_up_gate[0], b_up_gate[0], w_down[0], b_down[0])
    out = _combine(yb.reshape(n_rows, LANE_TILES, LANES), dest_flat, h_res, gate_t.T, final_norm_gain)
    return out.reshape(bsz, seqlen, d)
```

```python
import functools

import jax
import jax.numpy as jnp
import numpy as np
from jax import lax
from jax.experimental import pallas as pl
from jax.experimental.pallas import tpu as pltpu

F32 = jnp.float32
BF16 = jnp.bfloat16

D_MODEL = 1024
HG_HEADS = 8
HG_DK = 128
HG_CHUNK = 128
HG_GROUP = 16
AT_Q_HEADS = 16
AT_KV_HEADS = 4
AT_GROUP = AT_Q_HEADS // AT_KV_HEADS
AT_HEAD_DIM = 64
WINDOW = 128
AT_BLOCK = WINDOW
AT_KVWIDTH = AT_KV_HEADS * AT_HEAD_DIM
N_EXPERTS = 32
TOP_K = 4
D_FF = 1024
SWIGLU_LIMIT = 7.0
SWIGLU_ALPHA = 1.702
NORM_EPS = 1e-5
IN_WIDTH = 8704

COL_HQ, COL_FF, COL_FB, COL_HI, COL_HG = 0, 1024, 2048, 3072, 4096
COL_AQ, COL_AK, COL_AV, COL_GA, COL_GB = 5120, 6144, 6400, 6656, 7680

MOE_ROWS = 1024
MOE_PAD = 512
FF_SLAB = 256
MERGE_SUB = 256
BLOCK_DMA_PRIORITY = 1
VMEM_LIMIT = 56 * 1024 * 1024

NT_DIMS = (((1,), (1,)), ((), ()))

SUBLANES = 8
LANES = 128
LANE_TILES = D_MODEL // LANES
TILE_VIEW = (LANE_TILES, SUBLANES, LANES)
TOKEN_TILES = (SUBLANES, LANE_TILES, LANES)


def _store_tiles(ref, val):
    groups = val.shape[0] // SUBLANES
    for t in range(val.shape[1] // LANES):
        ref[:, t] = val[:, t * LANES:(t + 1) * LANES].reshape(groups, SUBLANES, LANES)


def _load_tiles(ref):
    groups = ref.shape[0]
    return jnp.concatenate(
        [ref[:, t].reshape(groups * SUBLANES, LANES) for t in range(ref.shape[1])], axis=1)


PACKED_TILES = LANE_TILES // 2
PACKED_VIEW = (PACKED_TILES, SUBLANES, LANES)
PACKED_TOKEN_TILES = (SUBLANES, PACKED_TILES, LANES)


def _pack_bf16_pairs(x):
    half = x.shape[1] // 2
    lo = pltpu.bitcast(x[:, :half].astype(BF16).astype(F32), jnp.uint32)
    hi = pltpu.bitcast(x[:, half:].astype(BF16).astype(F32), jnp.uint32)
    return hi | lax.shift_right_logical(lo, jnp.uint32(16))


def _unpack_bf16_pairs(w):
    lo = pltpu.bitcast(lax.shift_left(w, jnp.uint32(16)), F32)
    hi = pltpu.bitcast(w & jnp.uint32(0xFFFF0000), F32)
    return jnp.concatenate([lo, hi], axis=1).astype(BF16)


def _sigmoid(x):
    return 0.5 * jnp.tanh(0.5 * x) + 0.5


def _inproj_body(x_ref, g_ref, w_ref, o_ref, xn_ref):
    @pl.when(pl.program_id(1) == 0)
    def _():
        x = x_ref[...]
        ms = jnp.mean(x * x, axis=-1, keepdims=True)
        xn_ref[...] = (x * lax.rsqrt(ms + NORM_EPS) * g_ref[...]).astype(BF16)

    o_ref[...] = jnp.dot(xn_ref[...], w_ref[...].astype(BF16),
                         preferred_element_type=F32).astype(o_ref.dtype)


def _inproj(x2d, gain, w, tm=1024, tn=4352):
    n_tok = x2d.shape[0]
    n_out = w.shape[1]
    return pl.pallas_call(
        _inproj_body,
        grid=(n_tok // tm, n_out // tn),
        in_specs=[
            pl.BlockSpec((tm, D_MODEL), lambda i, j: (i, 0)),
            pl.BlockSpec((1, D_MODEL), lambda i, j: (0, 0)),
            pl.BlockSpec((D_MODEL, tn), lambda i, j: (0, j)),
        ],
        out_specs=pl.BlockSpec((tm, tn), lambda i, j: (i, j)),
        out_shape=jax.ShapeDtypeStruct((n_tok, n_out), BF16),
        scratch_shapes=[pltpu.VMEM((tm, D_MODEL), BF16)],
        compiler_params=pltpu.CompilerParams(
            dimension_semantics=("parallel", "arbitrary"), vmem_limit_bytes=VMEM_LIMIT),
        name="inproj",
    )(x2d, gain.reshape(1, D_MODEL), w)


class _Dir:
    def __init__(self, q_ref, f_ref, v_ref, o_ref, st_ref, lb, tri, mask, mid_row, last_row):
        self.q_ref, self.f_ref, self.v_ref, self.o_ref, self.st_ref = q_ref, f_ref, v_ref, o_ref, st_ref
        self.c0, self.c1 = 0.5 * (1.0 + lb), 0.5 * (1.0 - lb)
        self.tri, self.mask, self.mid_row, self.last_row = tri, mask, mid_row, last_row


def _gla_group(chains):
    c = HG_CHUNK
    dk = HG_DK
    pre = []
    for d, r0 in chains:
        h = 0.5 * d.q_ref[pl.ds(r0, c), :].astype(F32)
        q = h * jnp.tanh(h) + h
        ct = d.c1 * jnp.tanh(0.5 * d.f_ref[pl.ds(r0, c), :].astype(F32))
        logf = jnp.log2(d.c0 + ct)
        hi = logf.astype(BF16)
        lo = (logf - hi.astype(F32)).astype(BF16)
        pre.append((q, (1.0 - d.c0) - ct, jnp.concatenate([hi, lo], axis=1)))
    cums = []
    for (d, _), (_, _, hl) in zip(chains, pre):
        r = jnp.dot(d.tri, hl, preferred_element_type=F32)
        cums.append(r[:, :dk] + r[:, dk:])
    mids = []
    for (d, r0), (q, k, _), cum in zip(chains, pre, cums):
        mid = cum[d.mid_row:d.mid_row + 1, :]
        last = cum[d.last_row:d.last_row + 1, :]
        a = cum - mid
        qa = (q * jnp.exp2(a)).astype(BF16)
        ka = (k * jnp.exp2(-a)).astype(BF16)
        to_start = jnp.exp2(mid)
        to_end = jnp.exp2(last - mid)
        dec = jnp.exp2(last)
        scores = lax.dot_general(qa, ka, NT_DIMS, preferred_element_type=F32)
        mids.append((qa, ka, to_start, to_end, dec, scores))
    outs = []
    for (d, r0), (qa, ka, to_start, to_end, dec, scores) in zip(chains, mids):
        v = d.v_ref[pl.ds(r0, c), :]
        sm = jnp.where(d.mask, scores, 0.0).astype(BF16)
        o_intra = jnp.dot(sm, v, preferred_element_type=F32)
        upd = jnp.dot(jnp.transpose(v.astype(F32)).astype(BF16), ka,
                      preferred_element_type=F32) * to_end
        outs.append((o_intra, upd))
    for (d, r0), (qa, _, to_start, _, dec, _), (o_intra, upd) in zip(chains, mids, outs):
        st = d.st_ref[...]
        d.o_ref[pl.ds(r0, c), :] = (o_intra + lax.dot_general(
            qa, (st * to_start).astype(BF16), NT_DIMS,
            preferred_element_type=F32)).astype(d.o_ref.dtype)
        d.st_ref[...] = st * dec + upd


def _hgrn2_body(qf_ref, ff_ref, vf_ref, qb_ref, fb_ref, vb_ref, lbf_ref, lbb_ref,
                of_ref, ob_ref, sf_ref, sb_ref):
    @pl.when(pl.program_id(2) == 0)
    def _():
        sf_ref[...] = jnp.zeros_like(sf_ref)
        sb_ref[...] = jnp.zeros_like(sb_ref)

    c = HG_CHUNK
    n_chunks = qf_ref.shape[0] // c
    g = min(HG_GROUP, n_chunks)
    row = lax.broadcasted_iota(jnp.int32, (c, c), 0)
    col = lax.broadcasted_iota(jnp.int32, (c, c), 1)
    lower = col <= row
    upper = col >= row
    fwd = _Dir(qf_ref, ff_ref, vf_ref, of_ref, sf_ref, lbf_ref[...],
               jnp.where(lower, 1.0, 0.0).astype(BF16), lower, c // 2 - 1, c - 1)
    bwd = _Dir(qb_ref, fb_ref, vb_ref, ob_ref, sb_ref, lbb_ref[...],
               jnp.where(upper, 1.0, 0.0).astype(BF16), upper, c // 2, 0)

    def step(n, carry):
        chains = []
        for u in range(g):
            chains.append((fwd, pl.multiple_of((n * g + u) * c, c)))
        for u in range(g):
            chains.append((bwd, pl.multiple_of((n_chunks - 1 - n * g - u) * c, c)))
        _gla_group(chains)
        return carry

    lax.fori_loop(0, n_chunks // g, step, 0)


def _hgrn2(proj, lb_fwd, lb_bwd, bsz, seqlen, rows=8192):
    n_tok = bsz * seqlen
    rows = min(rows, seqlen)
    nb = seqlen // rows
    dk = HG_DK

    def fwd_map(col0):
        return lambda b, h, j: (b * nb + j, col0 // dk + h)

    def bwd_map(col0):
        return lambda b, h, j: (b * nb + (nb - 1 - j), col0 // dk + h)

    blk = (rows, dk)
    lb_spec = pl.BlockSpec((1, dk), lambda b, h, j: (0, h))
    return pl.pallas_call(
        _hgrn2_body,
        grid=(bsz, HG_HEADS, nb),
        in_specs=[
            pl.BlockSpec(blk, fwd_map(COL_HQ)), pl.BlockSpec(blk, fwd_map(COL_FF)),
            pl.BlockSpec(blk, fwd_map(COL_HI)),
            pl.BlockSpec(blk, bwd_map(COL_HQ)), pl.BlockSpec(blk, bwd_map(COL_FB)),
            pl.BlockSpec(blk, bwd_map(COL_HI)),
            lb_spec, lb_spec,
        ],
        out_specs=[pl.BlockSpec(blk, fwd_map(0)), pl.BlockSpec(blk, bwd_map(0))],
        out_shape=[jax.ShapeDtypeStruct((n_tok, HG_HEADS * dk), BF16)] * 2,
        scratch_shapes=[pltpu.VMEM((dk, dk), F32), pltpu.VMEM((dk, dk), F32)],
        compiler_params=pltpu.CompilerParams(
            dimension_semantics=("parallel", "parallel", "arbitrary"),
            vmem_limit_bytes=VMEM_LIMIT),
        name="hgrn2",
    )(proj, proj, proj, proj, proj, proj, lb_fwd.reshape(1, -1), lb_bwd.reshape(1, -1))


def _window_attention(sink_ref, q_ref, k_refs, v_refs, starts_sequence, ends_sequence):
    blk = AT_BLOCK
    span = 3 * blk
    dh = AT_HEAD_DIM
    lanes = 2 * dh
    n_q = q_ref.shape[0] // blk
    lo_half = lax.broadcasted_iota(jnp.int32, (blk, lanes), 1) < dh
    row = lax.broadcasted_iota(jnp.int32, (blk, span), 0)
    col = lax.broadcasted_iota(jnp.int32, (blk, span), 1)
    absrel = jnp.abs(row + blk - col)
    base_inner = jnp.where(absrel <= WINDOW, -absrel.astype(F32), -jnp.inf)

    def padded(t, g):
        t = t[:, (g // 2) * lanes:(g // 2 + 1) * lanes].astype(F32)
        if g % 2 == 0:
            lo = jnp.where(lo_half, t, 0.0)
            hi = pltpu.roll(lo, dh, axis=1)
        else:
            hi = jnp.where(lo_half, 0.0, t)
            lo = pltpu.roll(hi, dh, axis=1)
        return lo.astype(BF16), hi.astype(BF16)

    def key_blocks(refs):
        prev_ref, own_ref, next_ref = refs
        return ([prev_ref[...]] + [own_ref[b * blk:(b + 1) * blk, :] for b in range(n_q)]
                + [next_ref[...]])

    k_pads = [[padded(t, g) for t in key_blocks(k_refs)] for g in range(AT_KV_HEADS)]
    v_pads = [[padded(t, g) for t in key_blocks(v_refs)] for g in range(AT_KV_HEADS)]

    def banded(pads, first):
        use = pads[first:first + 3]
        return jnp.concatenate([p[0] for p in use] + [p[1] for p in use], axis=0)

    results = []
    for qb in range(n_q):
        base = base_inner
        if qb == 0:
            base = jnp.where(jnp.logical_and(starts_sequence, col < blk), -jnp.inf, base)
        if qb == n_q - 1:
            base = jnp.where(jnp.logical_and(ends_sequence, col >= 2 * blk), -jnp.inf, base)
        rows = slice(qb * blk, (qb + 1) * blk)
        scores = []
        for g in range(AT_KV_HEADS):
            q2 = q_ref[rows, g * 2 * lanes:(g + 1) * 2 * lanes] * (dh ** -0.5)
            lhs = jnp.concatenate([q2[:, :lanes], q2[:, lanes:]], axis=0).astype(BF16)
            scores.append(lax.dot_general(lhs, banded(k_pads[g], qb), NT_DIMS,
                                          preferred_element_type=F32))
        probs, denoms = [], []
        for g in range(AT_KV_HEADS):
            p_rows, d_rows = [], []
            for r in range(2):
                p_cols, d_cols = [], []
                for c in range(2):
                    h = g * AT_GROUP + 2 * r + c
                    slope = 2.0 ** (-8.0 * (h + 1) / AT_Q_HEADS)
                    sink = sink_ref[h]
                    s = scores[g][r * blk:(r + 1) * blk, c * span:(c + 1) * span] + slope * base
                    m = jnp.maximum(jnp.max(s, axis=-1, keepdims=True), sink)
                    p = jnp.exp(s - m)
                    d_cols.append(jnp.sum(p, axis=-1, keepdims=True) + jnp.exp(sink - m))
                    p_cols.append(p.astype(BF16))
                p_rows.append(jnp.concatenate(p_cols, axis=1))
                d_rows.append(jnp.where(lo_half, d_cols[0], d_cols[1]))
            probs.append(jnp.concatenate(p_rows, axis=0))
            denoms.append(d_rows)
        outs = []
        for g in range(AT_KV_HEADS):
            o = jnp.dot(probs[g], banded(v_pads[g], qb), preferred_element_type=F32)
            for r in range(2):
                outs.append(o[r * blk:(r + 1) * blk, :] / denoms[g][r])
        results.append(jnp.concatenate(outs, axis=-1))
    return results


def _merge_body(n_steps, steps_per_seq, sink_ref, of_ref, ob_ref, hg_ref, ga0_ref, ga1_ref, gb0_ref,
                gb1_ref, q_ref, kp_ref, ko_ref, kn_ref, vp_ref, vo_ref, vn_ref, x_ref,
                gn_ref, wo_ref, g2_ref, wr_ref, br_ref,
                h_ref, gate_ref, dest_ref, cnt_ref, xb_hbm,
                carry_ref, ring, dest_vmem, dest_smem, cnt_vmem, cnt_smem, zero_ref,
                sem_rows, sem_dest, sem_misc):
    i = pl.program_id(0)
    slot = lax.rem(i, 2)
    tm = x_ref.shape[0]
    cap = xb_hbm.shape[0] // N_EXPERTS

    def rows_wait(sl):
        pltpu.make_async_copy(xb_hbm.at[pl.ds(0, tm * TOP_K)], xb_hbm.at[pl.ds(0, tm * TOP_K)],
                              sem_rows.at[sl]).wait()

    def dest_copy(sl):
        return pltpu.make_async_copy(dest_vmem.at[sl], dest_smem.at[sl], sem_dest.at[sl])

    def issue_rows(sl):
        dest_copy(sl).wait()

        def body(g, c):
            for s in range(SUBLANES):
                for k in range(TOP_K):
                    pltpu.make_async_copy(
                        ring.at[sl, g, :, s, :], xb_hbm.at[dest_smem[sl, k, g * SUBLANES + s]],
                        sem_rows.at[sl]).start(priority=k % 2)
            return c
        lax.fori_loop(0, tm // SUBLANES, body, 0, unroll=ISSUE_UNROLL)

    @pl.when(i == 0)
    def _():
        carry_ref[...] = jnp.zeros_like(carry_ref)

    @pl.when(i >= 2)
    def _():
        rows_wait(slot)

    sub = MERGE_SUB
    subs = [slice(s0, s0 + sub) for s0 in range(0, x_ref.shape[0], sub)]
    gn = gn_ref[...]

    step_in_seq = lax.rem(i, steps_per_seq)
    att_blocks = _window_attention(
        sink_ref, q_ref, (kp_ref, ko_ref, kn_ref), (vp_ref, vo_ref, vn_ref),
        step_in_seq == 0, step_in_seq == steps_per_seq - 1)
    per_sub = sub // AT_BLOCK
    y_att = [jnp.concatenate(att_blocks[n * per_sub:(n + 1) * per_sub], axis=0)
             for n in range(len(subs))]

    merged = []
    for rs, ya in zip(subs, y_att):
        o = of_ref[rs, :].astype(F32) + ob_ref[rs, :].astype(F32)
        parts = []
        for h in range(HG_HEADS):
            oh = o[:, h * HG_DK:(h + 1) * HG_DK]
            ms = jnp.mean(oh * oh, axis=-1, keepdims=True)
            parts.append(oh * lax.rsqrt(ms + NORM_EPS) * gn)
        hg = hg_ref[rs, :].astype(F32)
        y_rec = jnp.concatenate(parts, axis=-1) * (hg * _sigmoid(hg))
        gate_a = jnp.concatenate([ga0_ref[rs, :], ga1_ref[rs, :]], axis=-1).astype(F32)
        gate_b = jnp.concatenate([gb0_ref[rs, :], gb1_ref[rs, :]], axis=-1).astype(F32)
        merged.append((_sigmoid(gate_a) * y_rec
                       + _sigmoid(gate_b) * ya).astype(BF16))

    w_out = wo_ref[...].astype(BF16)
    xns = []
    for rs, mg in zip(subs, merged):
        h_res = x_ref[rs, :] + jnp.dot(mg, w_out, preferred_element_type=F32)
        h_ref[rs, :] = h_res
        ms = jnp.mean(h_res * h_res, axis=-1, keepdims=True)
        xn = h_res * lax.rsqrt(ms + NORM_EPS) * g2_ref[...]
        _store_tiles(ring.at[slot, rs.start // SUBLANES:rs.stop // SUBLANES], _pack_bf16_pairs(xn))
        xns.append(xn)

    logits = [lax.dot_general(wr_ref[...], xn, NT_DIMS, preferred_element_type=F32,
                              precision=lax.Precision.HIGHEST) + br_ref[...] for xn in xns]
    e_iota = lax.broadcasted_iota(jnp.int32, (N_EXPERTS, sub), 0).astype(F32)
    picked = []
    for rs, vals in zip(subs, logits):
        tops, idxs = [], []
        for _ in range(TOP_K):
            m = jnp.max(vals, axis=0, keepdims=True)
            sel = jnp.min(jnp.where(vals == m, e_iota, float(N_EXPERTS)), axis=0, keepdims=True)
            tops.append(m)
            idxs.append(sel)
            vals = jnp.where(e_iota == sel, -jnp.inf, vals)
        exps = [jnp.exp(t - tops[0]) for t in tops]
        den = exps[0] + exps[1] + exps[2] + exps[3]
        gate_ref[:, rs] = jnp.concatenate([e / den for e in exps], axis=0)
        picked.append((idxs, jnp.where(vals == -jnp.inf, 1.0, 0.0)))

    s_iota = lax.broadcasted_iota(jnp.int32, (sub, sub), 0)
    t_iota = lax.broadcasted_iota(jnp.int32, (sub, sub), 1)
    strict = jnp.where(s_iota < t_iota, 1.0, 0.0).astype(BF16)
    prefixes = [jnp.dot(chosen.astype(BF16), strict, preferred_element_type=F32)
                for _, chosen in picked]
    carry = carry_ref[...]
    for rs, (idxs, chosen), prefix in zip(subs, picked, prefixes):
        prefix = prefix + carry
        dests = [sel * float(cap) + jnp.sum(jnp.where(e_iota == sel, prefix, 0.0), axis=0, keepdims=True)
                 for sel in idxs]
        dest = jnp.concatenate(dests, axis=0).astype(jnp.int32)
        dest_ref[:, rs] = dest
        dest_vmem[slot, :, rs] = dest
        carry = carry + jnp.sum(chosen, axis=1, keepdims=True)
    carry_ref[...] = carry
    counts = jnp.broadcast_to(carry, cnt_ref.shape).astype(jnp.int32)
    cnt_ref[...] = counts

    dest_copy(slot).start()

    @pl.when(i >= 1)
    def _():
        issue_rows(1 - slot)

    @pl.when(i == n_steps - 1)
    def _():
        issue_rows(slot)
        if n_steps >= 2:
            rows_wait(1 - slot)
        rows_wait(slot)
        cnt_vmem[...] = counts
        cp = pltpu.make_async_copy(cnt_vmem, cnt_smem, sem_misc.at[0])
        cp.start()
        cp.wait()
        zero_ref[...] = jnp.zeros_like(zero_ref)

        def tail(e):
            return pltpu.make_async_copy(
                zero_ref, xb_hbm.at[pl.ds(e * cap + cnt_smem[e, 0], MOE_PAD)], sem_misc.at[0])
        for e in range(N_EXPERTS):
            tail(e).start()
        for e in range(N_EXPERTS):
            tail(e).wait()


def _merge_route(o_f, o_b, proj, sink, x2d, seqlen, gn_gain, w_out, g2, w_router_t, b_router, tm=512):
    n_tok = x2d.shape[0]
    nblk = n_tok // tm
    row_blk = (tm, D_MODEL)
    half = D_MODEL // 2
    half_blk = (tm, half)

    def rows(c):
        return lambda i: (i, c)

    per_step = tm // AT_BLOCK
    n_kv_blocks = n_tok // AT_BLOCK
    kv_blk = (AT_BLOCK, AT_KVWIDTH)
    kv_own = (tm, AT_KVWIDTH)

    def kv_prev(c):
        return lambda i: (jnp.maximum(i * per_step - 1, 0), c)

    def kv_next(c):
        return lambda i: (jnp.minimum((i + 1) * per_step, n_kv_blocks - 1), c)

    kcol = COL_AK // AT_KVWIDTH
    vcol = COL_AV // AT_KVWIDTH
    full = lambda i: (0, 0)
    tok_out = pl.BlockSpec((TOP_K, tm), lambda i: (0, i))
    cap = n_tok + MOE_PAD
    return pl.pallas_call(
        functools.partial(_merge_body, nblk, seqlen // tm),
        grid=(nblk,),
        in_specs=[
            pl.BlockSpec(memory_space=pltpu.SMEM),
            pl.BlockSpec(row_blk, rows(0)), pl.BlockSpec(row_blk, rows(0)),
            pl.BlockSpec(row_blk, rows(COL_HG // D_MODEL)),
            pl.BlockSpec(half_blk, rows(COL_GA // half)),
            pl.BlockSpec(half_blk, rows(COL_GA // half + 1)),
            pl.BlockSpec(half_blk, rows(COL_GB // half)),
            pl.BlockSpec(half_blk, rows(COL_GB // half + 1)),
            pl.BlockSpec(row_blk, rows(COL_AQ // D_MODEL)),
            pl.BlockSpec(kv_blk, kv_prev(kcol)), pl.BlockSpec(kv_own, rows(kcol)),
            pl.BlockSpec(kv_blk, kv_next(kcol)),
            pl.BlockSpec(kv_blk, kv_prev(vcol)), pl.BlockSpec(kv_own, rows(vcol)),
            pl.BlockSpec(kv_blk, kv_next(vcol)),
            pl.BlockSpec(row_blk, rows(0)),
            pl.BlockSpec((1, HG_DK), full),
            pl.BlockSpec((D_MODEL, D_MODEL), full),
            pl.BlockSpec((1, D_MODEL), full),
            pl.BlockSpec((N_EXPERTS, D_MODEL), full),
            pl.BlockSpec((N_EXPERTS, 1), full),
        ],
        out_specs=[
            pl.BlockSpec(row_blk, rows(0)),
            tok_out, tok_out,
            pl.BlockSpec((N_EXPERTS, LANES), full),
            pl.BlockSpec(memory_space=pl.ANY),
        ],
        out_shape=[
            jax.ShapeDtypeStruct((n_tok, D_MODEL), F32),
            jax.ShapeDtypeStruct((TOP_K, n_tok), F32),
            jax.ShapeDtypeStruct((TOP_K, n_tok), jnp.int32),
            jax.ShapeDtypeStruct((N_EXPERTS, LANES), jnp.int32),
            jax.ShapeDtypeStruct((N_EXPERTS * cap, PACKED_TILES, LANES), jnp.uint32),
        ],
        scratch_shapes=[
            pltpu.VMEM((N_EXPERTS, 1), F32),
            pltpu.VMEM((2, tm // SUBLANES) + PACKED_VIEW, jnp.uint32),
            pltpu.VMEM((2, TOP_K, tm), jnp.int32), pltpu.SMEM((2, TOP_K, tm), jnp.int32),
            pltpu.VMEM((N_EXPERTS, LANES), jnp.int32), pltpu.SMEM((N_EXPERTS, LANES), jnp.int32),
            pltpu.VMEM((MOE_PAD, PACKED_TILES, LANES), jnp.uint32),
            pltpu.SemaphoreType.DMA((2,)), pltpu.SemaphoreType.DMA((2,)), pltpu.SemaphoreType.DMA((1,)),
        ],
        compiler_params=pltpu.CompilerParams(
            dimension_semantics=("arbitrary",), vmem_limit_bytes=VMEM_LIMIT),
        name="merge_route",
    )(sink.astype(F32), o_f, o_b, proj, proj, proj, proj, proj, proj, proj, proj, proj, proj, proj, proj,
      x2d, gn_gain.reshape(1, HG_DK), w_out,
      g2.reshape(1, D_MODEL), w_router_t, b_router.reshape(N_EXPERTS, 1))


ISSUE_UNROLL = 2


def _expert_body(bend_ref, bgrp_ref, bshort_ref, x_hbm, wug_ref, bug_ref, wd_ref, bd_ref, y_hbm,
                 wug_bf, wd_bf, act_ref, xbuf, ybuf, sem_in, sem_out):
    e = pl.program_id(0)
    n_used = bend_ref[N_EXPERTS - 1]
    b_first = jnp.where(e == 0, 0, bend_ref[jnp.maximum(e - 1, 0)])
    b_last = bend_ref[e]

    def for_size(b, fn):
        short = bshort_ref[b] == 1

        @pl.when(short)
        def _():
            fn(MOE_PAD // SUBLANES)

        @pl.when(jnp.logical_not(short))
        def _():
            fn(MOE_ROWS // SUBLANES)

    def x_copies(b, slot, groups):
        return [pltpu.make_async_copy(x_hbm.at[pl.ds(bgrp_ref[b], groups), s],
                                      xbuf.at[slot, pl.ds(0, groups), :, s, :], sem_in.at[slot])
                for s in range(SUBLANES)]

    def y_copies(b, slot, groups):
        return [pltpu.make_async_copy(ybuf.at[slot, pl.ds(0, groups), :, s, :],
                                      y_hbm.at[pl.ds(bgrp_ref[b], groups), s], sem_out.at[slot])
                for s in range(SUBLANES)]

    def start_all(copies):
        for cp in copies:
            cp.start(priority=BLOCK_DMA_PRIORITY)

    def wait_all(copies):
        for cp in copies:
            cp.wait()

    @pl.when((e == 0) & (n_used > 0))
    def _():
        for_size(0, lambda groups: start_all(x_copies(0, 0, groups)))

    ns = D_FF // FF_SLAB
    for s in range(ns):
        lo, hi = s * FF_SLAB, (s + 1) * FF_SLAB
        wug_bf[:, 2 * lo:2 * lo + FF_SLAB] = wug_ref[0, :, lo:hi].astype(BF16)
        wug_bf[:, 2 * lo + FF_SLAB:2 * hi] = wug_ref[0, :, D_FF + lo:D_FF + hi].astype(BF16)
    wd_bf[...] = wd_ref[0].astype(BF16)
    b_ug = bug_ref[0]
    b_d = bd_ref[0]

    def mlp(slot, groups):
        rows = groups * SUBLANES
        x = _unpack_bf16_pairs(_load_tiles(xbuf.at[slot, pl.ds(0, groups)]))
        for s in range(ns):
            lo, hi = s * FF_SLAB, (s + 1) * FF_SLAB
            hu = jnp.dot(x, wug_bf[:, 2 * lo:2 * hi], preferred_element_type=F32)
            g = jnp.minimum(hu[:, :FF_SLAB] + b_ug[:, lo:hi], SWIGLU_LIMIT)
            u = jnp.clip(hu[:, FF_SLAB:] + b_ug[:, D_FF + lo:D_FF + hi], -SWIGLU_LIMIT, SWIGLU_LIMIT)
            act_ref[0:rows, lo:hi] = (g * _sigmoid(SWIGLU_ALPHA * g) * (u + 1.0)).astype(BF16)
        _store_tiles(ybuf.at[slot, pl.ds(0, groups)],
                     jnp.dot(act_ref[0:rows, :], wd_bf[...], preferred_element_type=F32) + b_d)

    def block(b, carry):
        slot = lax.rem(b, 2)
        for_size(b, lambda groups: wait_all(x_copies(b, slot, groups)))

        @pl.when(b + 1 < n_used)
        def _():
            for_size(b + 1, lambda groups: start_all(x_copies(b + 1, 1 - slot, groups)))

        @pl.when(b >= 2)
        def _():
            for_size(b - 2, lambda groups: wait_all(y_copies(b - 2, slot, groups)))

        def run(groups):
            mlp(slot, groups)
            start_all(y_copies(b, slot, groups))
        for_size(b, run)
        return carry

    lax.fori_loop(b_first, b_last, block, 0)

    @pl.when(e == N_EXPERTS - 1)
    def _():
        for back in (2, 1):
            @pl.when(n_used >= back)
            def _():
                last = n_used - back
                for_size(last, lambda groups: wait_all(y_copies(last, lax.rem(last, 2), groups)))


def _experts(xb, block_end, block_group, block_short, w_up_gate, b_up_gate, w_down, b_down):
    groups = MOE_ROWS // SUBLANES

    def w_map(e, bend, bgrp, bshort):
        return (e, 0, 0)

    grid_spec = pltpu.PrefetchScalarGridSpec(
        num_scalar_prefetch=3,
        grid=(N_EXPERTS,),
        in_specs=[
            pl.BlockSpec(memory_space=pl.ANY),
            pl.BlockSpec((1, D_MODEL, 2 * D_FF), w_map),
            pl.BlockSpec((1, 1, 2 * D_FF), w_map),
            pl.BlockSpec((1, D_FF, D_MODEL), w_map),
            pl.BlockSpec((1, 1, D_MODEL), w_map),
        ],
        out_specs=pl.BlockSpec(memory_space=pl.ANY),
        scratch_shapes=[
            pltpu.VMEM((D_MODEL, 2 * D_FF), BF16), pltpu.VMEM((D_FF, D_MODEL), BF16),
            pltpu.VMEM((MOE_ROWS, D_FF), BF16),
            pltpu.VMEM((2, groups) + PACKED_VIEW, jnp.uint32), pltpu.VMEM((2, groups) + TILE_VIEW, F32),
            pltpu.SemaphoreType.DMA((2,)), pltpu.SemaphoreType.DMA((2,)),
        ],
    )
    return pl.pallas_call(
        _expert_body,
        grid_spec=grid_spec,
        out_shape=jax.ShapeDtypeStruct((xb.shape[0],) + TOKEN_TILES, F32),
        compiler_params=pltpu.CompilerParams(
            dimension_semantics=("arbitrary",), vmem_limit_bytes=VMEM_LIMIT),
        name="moe_experts",
    )(block_end, block_group, block_short, xb, w_up_gate, b_up_gate.reshape(N_EXPERTS, 1, -1),
      w_down, b_down.reshape(N_EXPERTS, 1, -1))


COMBINE_TOKENS = 256


def _combine_body(dest_ref, yb_hbm, h_ref, gate_ref, gain_ref, o_ref, buf_ref, sem):
    i = pl.program_id(0)
    n_steps = pl.num_programs(0)
    nt = COMBINE_TOKENS

    def issue(step, slot):
        base = step * nt

        def body(g, c):
            for s in range(SUBLANES):
                for k in range(TOP_K):
                    pltpu.make_async_copy(
                        yb_hbm.at[dest_ref[k * (dest_ref.shape[0] // TOP_K) + base + g * SUBLANES + s]],
                        buf_ref.at[slot, k, g, :, s, :], sem.at[slot]).start(priority=k % 2)
            return c
        lax.fori_loop(0, nt // SUBLANES, body, 0, unroll=ISSUE_UNROLL)

    slot = lax.rem(i, 2)

    @pl.when(i == 0)
    def _():
        issue(0, 0)

    @pl.when(i + 1 < n_steps)
    def _():
        issue(i + 1, 1 - slot)

    pltpu.make_async_copy(buf_ref.at[slot], buf_ref.at[slot], sem.at[slot]).wait()

    gates = gate_ref[...]
    acc = h_ref[...]
    for k in range(TOP_K):
        acc = acc + gates[:, k:k + 1] * _load_tiles(buf_ref.at[slot, k])
    ms = jnp.mean(acc * acc, axis=-1, keepdims=True)
    o_ref[...] = acc * lax.rsqrt(ms + NORM_EPS) * gain_ref[...]


def _combine(yb, dest_flat, h_res, gates_tk, final_gain):
    n_tok = h_res.shape[0]
    nt = COMBINE_TOKENS
    grid_spec = pltpu.PrefetchScalarGridSpec(
        num_scalar_prefetch=1,
        grid=(n_tok // nt,),
        in_specs=[
            pl.BlockSpec(memory_space=pl.ANY),
            pl.BlockSpec((nt, D_MODEL), lambda i, d: (i, 0)),
            pl.BlockSpec((nt, TOP_K), lambda i, d: (i, 0)),
            pl.BlockSpec((1, D_MODEL), lambda i, d: (0, 0)),
        ],
        out_specs=pl.BlockSpec((nt, D_MODEL), lambda i, d: (i, 0)),
        scratch_shapes=[pltpu.VMEM((2, TOP_K, nt // SUBLANES) + TILE_VIEW, F32),
                        pltpu.SemaphoreType.DMA((2,))],
    )
    return pl.pallas_call(
        _combine_body,
        grid_spec=grid_spec,
        out_shape=jax.ShapeDtypeStruct((n_tok, D_MODEL), F32),
        compiler_params=pltpu.CompilerParams(
            dimension_semantics=("arbitrary",), vmem_limit_bytes=VMEM_LIMIT),
        name="moe_combine",
    )(dest_flat, yb, h_res, gates_tk, final_gain.reshape(1, D_MODEL))


def kernel(x, norm_mix_gain, w_in, hg_lb_fwd, hg_lb_bwd, hg_norm_gain, attn_sink, w_out,
           norm_ffn_gain, w_router, b_router, w_up_gate, b_up_gate, w_down, b_down,
           final_norm_gain):
    bsz, seqlen, d = x.shape
    n_tok = bsz * seqlen
    x2d = x.reshape(n_tok, d)

    lb_fwd = jax.nn.softmax(hg_lb_fwd.astype(F32), axis=0)[0]
    lb_bwd = jax.nn.softmax(hg_lb_bwd.astype(F32), axis=0)[0]

    proj = _inproj(x2d, norm_mix_gain[0], w_in[0].astype(BF16))
    o_f, o_b = _hgrn2(proj, lb_fwd, lb_bwd, bsz, seqlen)
    h_res, gate_t, dest_t, counts, xb = _merge_route(
        o_f, o_b, proj, attn_sink[0], x2d, seqlen, hg_norm_gain[0], w_out[0],
        norm_ffn_gain[0], w_router[0].T.astype(F32), b_router[0].astype(F32))

    n_rows = xb.shape[0]
    cap = n_rows // N_EXPERTS
    counts = counts[:, 0]
    padded = (counts + MOE_PAD - 1) // MOE_PAD * MOE_PAD
    full = padded // MOE_ROWS
    blocks = full + (padded % MOE_ROWS) // MOE_PAD
    block_end = jnp.cumsum(blocks).astype(jnp.int32)
    n_blocks_max = n_tok * TOP_K // MOE_PAD + N_EXPERTS
    b = jnp.arange(n_blocks_max, dtype=jnp.int32)
    owner = (block_end[None, :] <= b[:, None]).astype(jnp.int32)
    expert_of = jnp.minimum(jnp.sum(owner, axis=-1), N_EXPERTS - 1)
    within = b - jnp.sum(owner * blocks[None, :], axis=-1)
    is_owner = expert_of[:, None] == jnp.arange(N_EXPERTS, dtype=jnp.int32)[None, :]
    block_short = (within == jnp.sum(jnp.where(is_owner, full[None, :], 0), axis=-1)).astype(jnp.int32)
    block_group = ((expert_of * cap + within * MOE_ROWS) // SUBLANES).astype(jnp.int32)
    dest_flat = dest_t.reshape(-1)

    yb = _experts(xb.reshape((n_rows // SUBLANES,) + PACKED_TOKEN_TILES), block_end, block_group, block_short,
                  w_up_gate[0], b_up_gate[0], w_down[0], b_down[0])
    out = _combine(yb.reshape(n_rows, LANE_TILES, LANES), dest_flat, h_res, gate_t.T, final_norm_gain)
    return out.reshape(bsz, seqlen, d)
```

```python
import functools

import jax
import jax.numpy as jnp
from jax import lax
from jax.experimental import pallas as pl
from jax.experimental.pallas import tpu as pltpu

F32 = jnp.float32
BF16 = jnp.bfloat16

D_MODEL = 1024
HG_HEADS = 8
HG_DK = 128
HG_CHUNK = 128
HG_GROUP = 16
AT_Q_HEADS = 16
AT_KV_HEADS = 4
AT_GROUP = AT_Q_HEADS // AT_KV_HEADS
AT_HEAD_DIM = 64
WINDOW = 128
AT_BLOCK = WINDOW
AT_KVWIDTH = AT_KV_HEADS * AT_HEAD_DIM
N_EXPERTS = 32
TOP_K = 4
D_FF = 1024
SWIGLU_LIMIT = 7.0
SWIGLU_ALPHA = 1.702
NORM_EPS = 1e-5

COL_HQ, COL_FF, COL_FB, COL_HI, COL_HG = 0, 1024, 2048, 3072, 4096
COL_AQ, COL_AK, COL_AV, COL_GA, COL_GB = 5120, 6144, 6400, 6656, 7680

MOE_ROWS = 1024
MOE_PAD = 512
FF_SLAB = 256
MERGE_SUB = 256
VMEM_LIMIT = 56 * 1024 * 1024

NT_DIMS = (((1,), (1,)), ((), ()))

SUBLANES = 8
LANES = 128
LANE_TILES = D_MODEL // LANES
TILE_VIEW = (LANE_TILES, SUBLANES, LANES)
TOKEN_TILES = (SUBLANES, LANE_TILES, LANES)


def _store_tiles(ref, val):
    groups = val.shape[0] // SUBLANES
    for t in range(val.shape[1] // LANES):
        ref[:, t] = val[:, t * LANES:(t + 1) * LANES].reshape(groups, SUBLANES, LANES)


def _load_tiles(ref):
    groups = ref.shape[0]
    return jnp.concatenate(
        [ref[:, t].reshape(groups * SUBLANES, LANES) for t in range(ref.shape[1])], axis=1)


PACKED_TILES = LANE_TILES // 2
PACKED_VIEW = (PACKED_TILES, SUBLANES, LANES)
PACKED_TOKEN_TILES = (SUBLANES, PACKED_TILES, LANES)


def _pack_bf16_pairs(x):
    half = x.shape[1] // 2
    lo = pltpu.bitcast(x[:, :half].astype(BF16).astype(F32), jnp.uint32)
    hi = pltpu.bitcast(x[:, half:].astype(BF16).astype(F32), jnp.uint32)
    return hi | lax.shift_right_logical(lo, jnp.uint32(16))


def _unpack_bf16_pairs(w):
    lo = pltpu.bitcast(lax.shift_left(w, jnp.uint32(16)), F32)
    hi = pltpu.bitcast(w & jnp.uint32(0xFFFF0000), F32)
    return jnp.concatenate([lo, hi], axis=1).astype(BF16)


def _sigmoid(x):
    return 0.5 * jnp.tanh(0.5 * x) + 0.5


def _inproj_body(x_ref, g_ref, w_ref, o_ref, xn_ref):
    @pl.when(pl.program_id(1) == 0)
    def _():
        x = x_ref[...]
        ms = jnp.mean(x * x, axis=-1, keepdims=True)
        xn_ref[...] = (x * lax.rsqrt(ms + NORM_EPS) * g_ref[...]).astype(BF16)

    o_ref[...] = jnp.dot(xn_ref[...], w_ref[...], preferred_element_type=F32).astype(o_ref.dtype)


def _inproj(x2d, gain, w, tm=1024, tn=4352):
    n_tok = x2d.shape[0]
    n_out = w.shape[1]
    return pl.pallas_call(
        _inproj_body,
        grid=(n_tok // tm, n_out // tn),
        in_specs=[
            pl.BlockSpec((tm, D_MODEL), lambda i, j: (i, 0)),
            pl.BlockSpec((1, D_MODEL), lambda i, j: (0, 0)),
            pl.BlockSpec((D_MODEL, tn), lambda i, j: (0, j)),
        ],
        out_specs=pl.BlockSpec((tm, tn), lambda i, j: (i, j)),
        out_shape=jax.ShapeDtypeStruct((n_tok, n_out), BF16),
        scratch_shapes=[pltpu.VMEM((tm, D_MODEL), BF16)],
        compiler_params=pltpu.CompilerParams(
            dimension_semantics=("parallel", "arbitrary"), vmem_limit_bytes=VMEM_LIMIT),
        name="inproj",
    )(x2d, gain.reshape(1, D_MODEL), w)


class _Dir:
    def __init__(self, q_ref, f_ref, v_ref, o_ref, st_ref, lb, tri, mask, mid_row, last_row):
        self.q_ref, self.f_ref, self.v_ref, self.o_ref, self.st_ref = q_ref, f_ref, v_ref, o_ref, st_ref
        self.c0, self.c1 = 0.5 * (1.0 + lb), 0.5 * (1.0 - lb)
        self.tri, self.mask, self.mid_row, self.last_row = tri, mask, mid_row, last_row


def _gla_group(chains):
    c = HG_CHUNK
    dk = HG_DK
    pre = []
    for d, r0 in chains:
        h = 0.5 * d.q_ref[pl.ds(r0, c), :].astype(F32)
        q = h * jnp.tanh(h) + h
        ct = d.c1 * jnp.tanh(0.5 * d.f_ref[pl.ds(r0, c), :].astype(F32))
        logf = jnp.log2(d.c0 + ct)
        pre.append((q, (1.0 - d.c0) - ct, logf.astype(BF16)))
    cums = []
    for (d, _), (_, _, hl) in zip(chains, pre):
        cums.append(jnp.dot(d.tri, hl, preferred_element_type=F32))
    mids = []
    for (d, r0), (q, k, _), cum in zip(chains, pre, cums):
        mid = cum[d.mid_row:d.mid_row + 1, :]
        last = cum[d.last_row:d.last_row + 1, :]
        a = cum - mid
        grow = jnp.exp2(a)
        qa = (q * grow).astype(BF16)
        ka = (k / grow).astype(BF16)
        to_start = jnp.exp2(mid)
        to_end = jnp.exp2(last - mid)
        dec = jnp.exp2(last)
        scores = lax.dot_general(qa, ka, NT_DIMS, preferred_element_type=F32)
        mids.append((qa, ka, to_start, to_end, dec, scores))
    outs = []
    for (d, r0), (qa, ka, to_start, to_end, dec, scores) in zip(chains, mids):
        v = d.v_ref[pl.ds(r0, c), :]
        sm = jnp.where(d.mask, scores, 0.0).astype(BF16)
        o_intra = jnp.dot(sm, v, preferred_element_type=F32)
        upd = jnp.dot(jnp.transpose(v), ka, preferred_element_type=F32) * to_end
        outs.append((o_intra, upd))
    for (d, r0), (qa, _, to_start, _, dec, _), (o_intra, upd) in zip(chains, mids, outs):
        st = d.st_ref[...]
        d.o_ref[pl.ds(r0, c), :] = (o_intra + lax.dot_general(
            qa, (st * to_start).astype(BF16), NT_DIMS,
            preferred_element_type=F32)).astype(d.o_ref.dtype)
        d.st_ref[...] = st * dec + upd


def _hgrn2_body(qf_ref, ff_ref, vf_ref, qb_ref, fb_ref, vb_ref, lbf_ref, lbb_ref,
                of_ref, ob_ref, sf_ref, sb_ref):
    @pl.when(pl.program_id(2) == 0)
    def _():
        sf_ref[...] = jnp.zeros_like(sf_ref)
        sb_ref[...] = jnp.zeros_like(sb_ref)

    c = HG_CHUNK
    n_chunks = qf_ref.shape[0] // c
    g = min(HG_GROUP, n_chunks)
    row = lax.broadcasted_iota(jnp.int32, (c, c), 0)
    col = lax.broadcasted_iota(jnp.int32, (c, c), 1)
    lower = col <= row
    upper = col >= row
    fwd = _Dir(qf_ref, ff_ref, vf_ref, of_ref, sf_ref, lbf_ref[...],
               jnp.where(lower, 1.0, 0.0).astype(BF16), lower, c // 2 - 1, c - 1)
    bwd = _Dir(qb_ref, fb_ref, vb_ref, ob_ref, sb_ref, lbb_ref[...],
               jnp.where(upper, 1.0, 0.0).astype(BF16), upper, c // 2, 0)

    def step(n, carry):
        chains = []
        for u in range(g):
            chains.append((fwd, pl.multiple_of((n * g + u) * c, c)))
        for u in range(g):
            chains.append((bwd, pl.multiple_of((n_chunks - 1 - n * g - u) * c, c)))
        _gla_group(chains)
        return carry

    lax.fori_loop(0, n_chunks // g, step, 0)


def _hgrn2(proj, lb_fwd, lb_bwd, bsz, seqlen, rows=8192):
    n_tok = bsz * seqlen
    rows = min(rows, seqlen)
    nb = seqlen // rows
    dk = HG_DK

    def fwd_map(col0):
        return lambda b, h, j: (b * nb + j, col0 // dk + h)

    def bwd_map(col0):
        return lambda b, h, j: (b * nb + (nb - 1 - j), col0 // dk + h)

    blk = (rows, dk)
    lb_spec = pl.BlockSpec((1, dk), lambda b, h, j: (0, h))
    return pl.pallas_call(
        _hgrn2_body,
        grid=(bsz, HG_HEADS, nb),
        in_specs=[
            pl.BlockSpec(blk, fwd_map(COL_HQ)), pl.BlockSpec(blk, fwd_map(COL_FF)),
            pl.BlockSpec(blk, fwd_map(COL_HI)),
            pl.BlockSpec(blk, bwd_map(COL_HQ)), pl.BlockSpec(blk, bwd_map(COL_FB)),
            pl.BlockSpec(blk, bwd_map(COL_HI)),
            lb_spec, lb_spec,
        ],
        out_specs=[pl.BlockSpec(blk, fwd_map(0)), pl.BlockSpec(blk, bwd_map(0))],
        out_shape=[jax.ShapeDtypeStruct((n_tok, HG_HEADS * dk), BF16)] * 2,
        scratch_shapes=[pltpu.VMEM((dk, dk), F32), pltpu.VMEM((dk, dk), F32)],
        compiler_params=pltpu.CompilerParams(
            dimension_semantics=("parallel", "parallel", "arbitrary"),
            vmem_limit_bytes=VMEM_LIMIT),
        name="hgrn2",
    )(proj, proj, proj, proj, proj, proj, lb_fwd.reshape(1, -1), lb_bwd.reshape(1, -1))


def _window_attention(sink_ref, q_ref, k_refs, v_refs, starts_sequence, ends_sequence):
    blk = AT_BLOCK
    span = 3 * blk
    dh = AT_HEAD_DIM
    lanes = 2 * dh
    n_q = q_ref.shape[0] // blk
    lo_half = lax.broadcasted_iota(jnp.int32, (blk, lanes), 1) < dh
    row = lax.broadcasted_iota(jnp.int32, (blk, span), 0)
    col = lax.broadcasted_iota(jnp.int32, (blk, span), 1)
    absrel = jnp.abs(row + blk - col)
    base_inner = jnp.where(absrel <= WINDOW, -absrel.astype(F32), -jnp.inf)

    def padded(t, g):
        t = t[:, (g // 2) * lanes:(g // 2 + 1) * lanes].astype(F32)
        if g % 2 == 0:
            lo = jnp.where(lo_half, t, 0.0)
            hi = pltpu.roll(lo, dh, axis=1)
        else:
            hi = jnp.where(lo_half, 0.0, t)
            lo = pltpu.roll(hi, dh, axis=1)
        return lo.astype(BF16), hi.astype(BF16)

    def key_blocks(refs):
        prev_ref, own_ref, next_ref = refs
        return ([prev_ref[...]] + [own_ref[b * blk:(b + 1) * blk, :] for b in range(n_q)]
                + [next_ref[...]])

    k_pads = [[padded(t, g) for t in key_blocks(k_refs)] for g in range(AT_KV_HEADS)]
    v_pads = [[padded(t, g) for t in key_blocks(v_refs)] for g in range(AT_KV_HEADS)]

    def banded(pads, first):
        use = pads[first:first + 3]
        return jnp.concatenate([p[0] for p in use] + [p[1] for p in use], axis=0)

    results = []
    for qb in range(n_q):
        base = base_inner
        if qb == 0:
            base = jnp.where(jnp.logical_and(starts_sequence, col < blk), -jnp.inf, base)
        if qb == n_q - 1:
            base = jnp.where(jnp.logical_and(ends_sequence, col >= 2 * blk), -jnp.inf, base)
        rows = slice(qb * blk, (qb + 1) * blk)
        scores = []
        for g in range(AT_KV_HEADS):
            q2 = q_ref[rows, g * 2 * lanes:(g + 1) * 2 * lanes] * (dh ** -0.5)
            lhs = jnp.concatenate([q2[:, :lanes], q2[:, lanes:]], axis=0).astype(BF16)
            scores.append(lax.dot_general(lhs, banded(k_pads[g], qb), NT_DIMS,
                                          preferred_element_type=F32))
        probs, denoms = [], []
        for g in range(AT_KV_HEADS):
            p_rows, d_rows = [], []
            for r in range(2):
                p_cols, d_cols = [], []
                for c in range(2):
                    h = g * AT_GROUP + 2 * r + c
                    slope = 2.0 ** (-8.0 * (h + 1) / AT_Q_HEADS)
                    sink = sink_ref[h]
                    s = scores[g][r * blk:(r + 1) * blk, c * span:(c + 1) * span] + slope * base
                    m = jnp.maximum(jnp.max(s, axis=-1, keepdims=True), sink)
                    p = jnp.exp(s - m)
                    d_cols.append(jnp.sum(p, axis=-1, keepdims=True) + jnp.exp(sink - m))
                    p_cols.append(p.astype(BF16))
                p_rows.append(jnp.concatenate(p_cols, axis=1))
                d_rows.append(jnp.where(lo_half, d_cols[0], d_cols[1]))
            probs.append(jnp.concatenate(p_rows, axis=0))
            denoms.append(d_rows)
        outs = []
        for g in range(AT_KV_HEADS):
            o = jnp.dot(probs[g], banded(v_pads[g], qb), preferred_element_type=F32)
            for r in range(2):
                outs.append(o[r * blk:(r + 1) * blk, :] / denoms[g][r])
        results.append(jnp.concatenate(outs, axis=-1))
    return results


def _merge_body(n_steps, steps_per_seq, sink_ref, of_ref, ob_ref, hg_ref, ga0_ref, ga1_ref, gb0_ref,
                gb1_ref, q_ref, kp_ref, ko_ref, kn_ref, vp_ref, vo_ref, vn_ref, x_ref,
                gn_ref, wo_ref, g2_ref, wr_ref, br_ref,
                h_ref, gate_ref, dest_ref, cnt_ref, xb_hbm,
                carry_ref, ring, dest_vmem, dest_smem, cnt_vmem, cnt_smem, zero_ref,
                sem_rows, sem_dest, sem_misc):
    i = pl.program_id(0)
    slot = lax.rem(i, 2)
    tm = x_ref.shape[0]
    cap = xb_hbm.shape[0] // N_EXPERTS

    def rows_wait(sl):
        pltpu.make_async_copy(xb_hbm.at[pl.ds(0, tm * TOP_K)], xb_hbm.at[pl.ds(0, tm * TOP_K)],
                              sem_rows.at[sl]).wait()

    def dest_copy(sl):
        return pltpu.make_async_copy(dest_vmem.at[sl], dest_smem.at[sl], sem_dest.at[sl])

    def issue_rows(sl):
        dest_copy(sl).wait()

        def body(g, c):
            for s in range(SUBLANES):
                for k in range(TOP_K):
                    pltpu.make_async_copy(
                        ring.at[sl, g, :, s, :], xb_hbm.at[dest_smem[sl, k, g * SUBLANES + s]],
                        sem_rows.at[sl]).start(priority=k % 2)
            return c
        lax.fori_loop(0, tm // SUBLANES, body, 0, unroll=ISSUE_UNROLL)

    @pl.when(i == 0)
    def _():
        carry_ref[...] = jnp.zeros_like(carry_ref)

    @pl.when(i >= 2)
    def _():
        rows_wait(slot)

    sub = MERGE_SUB
    subs = [slice(s0, s0 + sub) for s0 in range(0, x_ref.shape[0], sub)]
    gn = gn_ref[...]

    step_in_seq = lax.rem(i, steps_per_seq)
    att_blocks = _window_attention(
        sink_ref, q_ref, (kp_ref, ko_ref, kn_ref), (vp_ref, vo_ref, vn_ref),
        step_in_seq == 0, step_in_seq == steps_per_seq - 1)
    per_sub = sub // AT_BLOCK
    y_att = [jnp.concatenate(att_blocks[n * per_sub:(n + 1) * per_sub], axis=0)
             for n in range(len(subs))]

    merged = []
    for rs, ya in zip(subs, y_att):
        o = of_ref[rs, :].astype(F32) + ob_ref[rs, :].astype(F32)
        parts = []
        for h in range(HG_HEADS):
            oh = o[:, h * HG_DK:(h + 1) * HG_DK]
            ms = jnp.mean(oh * oh, axis=-1, keepdims=True)
            parts.append(oh * lax.rsqrt(ms + NORM_EPS) * gn)
        hg = hg_ref[rs, :].astype(F32)
        y_rec = jnp.concatenate(parts, axis=-1) * (hg * _sigmoid(hg))
        gate_a = jnp.concatenate([ga0_ref[rs, :], ga1_ref[rs, :]], axis=-1).astype(F32)
        gate_b = jnp.concatenate([gb0_ref[rs, :], gb1_ref[rs, :]], axis=-1).astype(F32)
        merged.append((_sigmoid(gate_a) * y_rec
                       + _sigmoid(gate_b) * ya).astype(BF16))

    w_out = wo_ref[...].astype(BF16)
    xns = []
    for rs, mg in zip(subs, merged):
        h_res = x_ref[rs, :] + jnp.dot(mg, w_out, preferred_element_type=F32)
        h_ref[rs, :] = h_res
        ms = jnp.mean(h_res * h_res, axis=-1, keepdims=True)
        xn = h_res * lax.rsqrt(ms + NORM_EPS) * g2_ref[...]
        _store_tiles(ring.at[slot, rs.start // SUBLANES:rs.stop // SUBLANES], _pack_bf16_pairs(xn))
        xns.append(xn)

    logits = [lax.dot_general(wr_ref[...], xn, NT_DIMS, preferred_element_type=F32,
                              precision=lax.Precision.HIGHEST) + br_ref[...] for xn in xns]
    e_iota = lax.broadcasted_iota(jnp.int32, (N_EXPERTS, sub), 0).astype(F32)
    picked = []
    for rs, vals in zip(subs, logits):
        tops, idxs = [], []
        for _ in range(TOP_K):
            m = jnp.max(vals, axis=0, keepdims=True)
            sel = jnp.min(jnp.where(vals == m, e_iota, float(N_EXPERTS)), axis=0, keepdims=True)
            tops.append(m)
            idxs.append(sel)
            vals = jnp.where(e_iota == sel, -jnp.inf, vals)
        exps = [jnp.exp(t - tops[0]) for t in tops]
        den = exps[0] + exps[1] + exps[2] + exps[3]
        gate_ref[:, rs] = jnp.concatenate([e / den for e in exps], axis=0)
        picked.append((idxs, jnp.where(vals == -jnp.inf, 1.0, 0.0)))

    s_iota = lax.broadcasted_iota(jnp.int32, (sub, sub), 0)
    t_iota = lax.broadcasted_iota(jnp.int32, (sub, sub), 1)
    strict = jnp.where(s_iota < t_iota, 1.0, 0.0).astype(BF16)
    prefixes = [jnp.dot(chosen.astype(BF16), strict, preferred_element_type=F32)
                for _, chosen in picked]
    carry = carry_ref[...]
    for rs, (idxs, chosen), prefix in zip(subs, picked, prefixes):
        prefix = prefix + carry
        dests = [sel * float(cap) + jnp.sum(jnp.where(e_iota == sel, prefix, 0.0), axis=0, keepdims=True)
                 for sel in idxs]
        dest = jnp.concatenate(dests, axis=0).astype(jnp.int32)
        dest_ref[:, rs] = dest
        dest_vmem[slot, :, rs] = dest
        carry = carry + jnp.sum(chosen, axis=1, keepdims=True)
    carry_ref[...] = carry
    counts = jnp.broadcast_to(carry, cnt_ref.shape).astype(jnp.int32)
    cnt_ref[...] = counts

    dest_copy(slot).start()

    @pl.when(i >= 1)
    def _():
        issue_rows(1 - slot)

    @pl.when(i == n_steps - 1)
    def _():
        issue_rows(slot)
        if n_steps >= 2:
            rows_wait(1 - slot)
        rows_wait(slot)
        cnt_vmem[...] = counts
        cp = pltpu.make_async_copy(cnt_vmem, cnt_smem, sem_misc.at[0])
        cp.start()
        cp.wait()
        zero_ref[...] = jnp.zeros_like(zero_ref)

        def tail(e):
            return pltpu.make_async_copy(
                zero_ref, xb_hbm.at[pl.ds(e * cap + cnt_smem[e, 0], MOE_PAD)], sem_misc.at[0])
        for e in range(N_EXPERTS):
            tail(e).start()
        for e in range(N_EXPERTS):
            tail(e).wait()


def _merge_route(o_f, o_b, proj, sink, x2d, seqlen, gn_gain, w_out, g2, w_router_t, b_router, tm=512):
    n_tok = x2d.shape[0]
    nblk = n_tok // tm
    row_blk = (tm, D_MODEL)
    half = D_MODEL // 2
    half_blk = (tm, half)

    def rows(c):
        return lambda i: (i, c)

    per_step = tm // AT_BLOCK
    n_kv_blocks = n_tok // AT_BLOCK
    kv_blk = (AT_BLOCK, AT_KVWIDTH)
    kv_own = (tm, AT_KVWIDTH)

    def kv_prev(c):
        return lambda i: (jnp.maximum(i * per_step - 1, 0), c)

    def kv_next(c):
        return lambda i: (jnp.minimum((i + 1) * per_step, n_kv_blocks - 1), c)

    kcol = COL_AK // AT_KVWIDTH
    vcol = COL_AV // AT_KVWIDTH
    full = lambda i: (0, 0)
    tok_out = pl.BlockSpec((TOP_K, tm), lambda i: (0, i))
    cap = n_tok + MOE_PAD
    return pl.pallas_call(
        functools.partial(_merge_body, nblk, seqlen // tm),
        grid=(nblk,),
        in_specs=[
            pl.BlockSpec(memory_space=pltpu.SMEM),
            pl.BlockSpec(row_blk, rows(0)), pl.BlockSpec(row_blk, rows(0)),
            pl.BlockSpec(row_blk, rows(COL_HG // D_MODEL)),
            pl.BlockSpec(half_blk, rows(COL_GA // half)),
            pl.BlockSpec(half_blk, rows(COL_GA // half + 1)),
            pl.BlockSpec(half_blk, rows(COL_GB // half)),
            pl.BlockSpec(half_blk, rows(COL_GB // half + 1)),
            pl.BlockSpec(row_blk, rows(COL_AQ // D_MODEL)),
            pl.BlockSpec(kv_blk, kv_prev(kcol)), pl.BlockSpec(kv_own, rows(kcol)),
            pl.BlockSpec(kv_blk, kv_next(kcol)),
            pl.BlockSpec(kv_blk, kv_prev(vcol)), pl.BlockSpec(kv_own, rows(vcol)),
            pl.BlockSpec(kv_blk, kv_next(vcol)),
            pl.BlockSpec(row_blk, rows(0)),
            pl.BlockSpec((1, HG_DK), full),
            pl.BlockSpec((D_MODEL, D_MODEL), full),
            pl.BlockSpec((1, D_MODEL), full),
            pl.BlockSpec((N_EXPERTS, D_MODEL), full),
            pl.BlockSpec((N_EXPERTS, 1), full),
        ],
        out_specs=[
            pl.BlockSpec(row_blk, rows(0)),
            tok_out, tok_out,
            pl.BlockSpec((N_EXPERTS, LANES), full),
            pl.BlockSpec(memory_space=pl.ANY),
        ],
        out_shape=[
            jax.ShapeDtypeStruct((n_tok, D_MODEL), F32),
            jax.ShapeDtypeStruct((TOP_K, n_tok), F32),
            jax.ShapeDtypeStruct((TOP_K, n_tok), jnp.int32),
            jax.ShapeDtypeStruct((N_EXPERTS, LANES), jnp.int32),
            jax.ShapeDtypeStruct((N_EXPERTS * cap, PACKED_TILES, LANES), jnp.uint32),
        ],
        scratch_shapes=[
            pltpu.VMEM((N_EXPERTS, 1), F32),
            pltpu.VMEM((2, tm // SUBLANES) + PACKED_VIEW, jnp.uint32),
            pltpu.VMEM((2, TOP_K, tm), jnp.int32), pltpu.SMEM((2, TOP_K, tm), jnp.int32),
            pltpu.VMEM((N_EXPERTS, LANES), jnp.int32), pltpu.SMEM((N_EXPERTS, LANES), jnp.int32),
            pltpu.VMEM((MOE_PAD, PACKED_TILES, LANES), jnp.uint32),
            pltpu.SemaphoreType.DMA((2,)), pltpu.SemaphoreType.DMA((2,)), pltpu.SemaphoreType.DMA((1,)),
        ],
        compiler_params=pltpu.CompilerParams(
            dimension_semantics=("arbitrary",), vmem_limit_bytes=VMEM_LIMIT),
        name="merge_route",
    )(sink.astype(F32), o_f, o_b, proj, proj, proj, proj, proj, proj, proj, proj, proj, proj, proj, proj,
      x2d, gn_gain.reshape(1, HG_DK), w_out,
      g2.reshape(1, D_MODEL), w_router_t, b_router.reshape(N_EXPERTS, 1))


ISSUE_UNROLL = 2


def _expert_body(bend_ref, bgrp_ref, bshort_ref, x_hbm, wug_ref, bug_ref, wd_ref, bd_ref, y_hbm,
                 wug_bf, wd_bf, act_ref, xbuf, ybuf, sem_in, sem_out):
    e = pl.program_id(0)
    n_used = bend_ref[N_EXPERTS - 1]
    b_first = jnp.where(e == 0, 0, bend_ref[jnp.maximum(e - 1, 0)])
    b_last = bend_ref[e]

    def for_size(b, fn):
        short = bshort_ref[b] == 1

        @pl.when(short)
        def _():
            fn(MOE_PAD // SUBLANES)

        @pl.when(jnp.logical_not(short))
        def _():
            fn(MOE_ROWS // SUBLANES)

    def x_copies(b, slot, groups):
        return [pltpu.make_async_copy(x_hbm.at[pl.ds(bgrp_ref[b], groups), s],
                                      xbuf.at[slot, pl.ds(0, groups), :, s, :], sem_in.at[slot])
                for s in range(SUBLANES)]

    def y_copies(b, slot, groups):
        return [pltpu.make_async_copy(ybuf.at[slot, pl.ds(0, groups), :, s, :],
                                      y_hbm.at[pl.ds(bgrp_ref[b], groups), s], sem_out.at[slot])
                for s in range(SUBLANES)]

    def start_all(copies):
        for cp in copies:
            cp.start()

    def wait_all(copies):
        for cp in copies:
            cp.wait()

    @pl.when((e == 0) & (n_used > 0))
    def _():
        for_size(0, lambda groups: start_all(x_copies(0, 0, groups)))

    ns = D_FF // FF_SLAB
    for s in range(ns):
        lo, hi = s * FF_SLAB, (s + 1) * FF_SLAB
        wug_bf[:, 2 * lo:2 * lo + FF_SLAB] = wug_ref[0, :, lo:hi].astype(BF16)
        wug_bf[:, 2 * lo + FF_SLAB:2 * hi] = wug_ref[0, :, D_FF + lo:D_FF + hi].astype(BF16)
    wd_bf[...] = wd_ref[0].astype(BF16)
    b_ug = bug_ref[0]
    b_d = bd_ref[0]

    def mlp(slot, groups):
        rows = groups * SUBLANES
        x = _unpack_bf16_pairs(_load_tiles(xbuf.at[slot, pl.ds(0, groups)]))
        for s in range(ns):
            lo, hi = s * FF_SLAB, (s + 1) * FF_SLAB
            hu = jnp.dot(x, wug_bf[:, 2 * lo:2 * hi], preferred_element_type=F32)
            g = jnp.minimum(hu[:, :FF_SLAB] + b_ug[:, lo:hi], SWIGLU_LIMIT)
            u = jnp.clip(hu[:, FF_SLAB:] + b_ug[:, D_FF + lo:D_FF + hi], -SWIGLU_LIMIT, SWIGLU_LIMIT)
            act_ref[0:rows, lo:hi] = (g * _sigmoid(SWIGLU_ALPHA * g) * (u + 1.0)).astype(BF16)
        _store_tiles(ybuf.at[slot, pl.ds(0, groups)],
                     jnp.dot(act_ref[0:rows, :], wd_bf[...], preferred_element_type=F32) + b_d)

    def block(b, carry):
        slot = lax.rem(b, 2)
        for_size(b, lambda groups: wait_all(x_copies(b, slot, groups)))

        @pl.when(b + 1 < n_used)
        def _():
            for_size(b + 1, lambda groups: start_all(x_copies(b + 1, 1 - slot, groups)))

        @pl.when(b >= 2)
        def _():
            for_size(b - 2, lambda groups: wait_all(y_copies(b - 2, slot, groups)))

        def run(groups):
            mlp(slot, groups)
            start_all(y_copies(b, slot, groups))
        for_size(b, run)
        return carry

    lax.fori_loop(b_first, b_last, block, 0)

    @pl.when(e == N_EXPERTS - 1)
    def _():
        for back in (2, 1):
            @pl.when(n_used >= back)
            def _():
                last = n_used - back
                for_size(last, lambda groups: wait_all(y_copies(last, lax.rem(last, 2), groups)))


def _experts(xb, block_end, block_group, block_short, w_up_gate, b_up_gate, w_down, b_down):
    groups = MOE_ROWS // SUBLANES

    def w_map(e, bend, bgrp, bshort):
        return (e, 0, 0)

    grid_spec = pltpu.PrefetchScalarGridSpec(
        num_scalar_prefetch=3,
        grid=(N_EXPERTS,),
        in_specs=[
            pl.BlockSpec(memory_space=pl.ANY),
            pl.BlockSpec((1, D_MODEL, 2 * D_FF), w_map),
            pl.BlockSpec((1, 1, 2 * D_FF), w_map),
            pl.BlockSpec((1, D_FF, D_MODEL), w_map),
            pl.BlockSpec((1, 1, D_MODEL), w_map),
        ],
        out_specs=pl.BlockSpec(memory_space=pl.ANY),
        scratch_shapes=[
            pltpu.VMEM((D_MODEL, 2 * D_FF), BF16), pltpu.VMEM((D_FF, D_MODEL), BF16),
            pltpu.VMEM((MOE_ROWS, D_FF), BF16),
            pltpu.VMEM((2, groups) + PACKED_VIEW, jnp.uint32), pltpu.VMEM((2, groups) + TILE_VIEW, F32),
            pltpu.SemaphoreType.DMA((2,)), pltpu.SemaphoreType.DMA((2,)),
        ],
    )
    return pl.pallas_call(
        _expert_body,
        grid_spec=grid_spec,
        out_shape=jax.ShapeDtypeStruct((xb.shape[0],) + TOKEN_TILES, F32),
        compiler_params=pltpu.CompilerParams(
            dimension_semantics=("arbitrary",), vmem_limit_bytes=VMEM_LIMIT),
        name="moe_experts",
    )(block_end, block_group, block_short, xb, w_up_gate, b_up_gate.reshape(N_EXPERTS, 1, -1),
      w_down, b_down.reshape(N_EXPERTS, 1, -1))


COMBINE_TOKENS = 256


def _combine_body(dest_ref, yb_hbm, h_ref, gate_ref, gain_ref, o_ref, buf_ref, sem):
    i = pl.program_id(0)
    n_steps = pl.num_programs(0)
    nt = COMBINE_TOKENS

    def issue(step, slot):
        base = step * nt

        def body(g, c):
            for s in range(SUBLANES):
                for k in range(TOP_K):
                    pltpu.make_async_copy(
                        yb_hbm.at[dest_ref[k * (dest_ref.shape[0] // TOP_K) + base + g * SUBLANES + s]],
                        buf_ref.at[slot, k, g, :, s, :], sem.at[slot]).start(priority=k % 2)
            return c
        lax.fori_loop(0, nt // SUBLANES, body, 0, unroll=ISSUE_UNROLL)

    slot = lax.rem(i, 2)

    @pl.when(i == 0)
    def _():
        issue(0, 0)

    @pl.when(i + 1 < n_steps)
    def _():
        issue(i + 1, 1 - slot)

    pltpu.make_async_copy(buf_ref.at[slot], buf_ref.at[slot], sem.at[slot]).wait()

    gates = gate_ref[...]
    acc = h_ref[...]
    for k in range(TOP_K):
        acc = acc + gates[:, k:k + 1] * _load_tiles(buf_ref.at[slot, k])
    ms = jnp.mean(acc * acc, axis=-1, keepdims=True)
    o_ref[...] = acc * lax.rsqrt(ms + NORM_EPS) * gain_ref[...]


def _combine(yb, dest_flat, h_res, gates_tk, final_gain):
    n_tok = h_res.shape[0]
    nt = COMBINE_TOKENS
    grid_spec = pltpu.PrefetchScalarGridSpec(
        num_scalar_prefetch=1,
        grid=(n_tok // nt,),
        in_specs=[
            pl.BlockSpec(memory_space=pl.ANY),
            pl.BlockSpec((nt, D_MODEL), lambda i, d: (i, 0)),
            pl.BlockSpec((nt, TOP_K), lambda i, d: (i, 0)),
            pl.BlockSpec((1, D_MODEL), lambda i, d: (0, 0)),
        ],
        out_specs=pl.BlockSpec((nt, D_MODEL), lambda i, d: (i, 0)),
        scratch_shapes=[pltpu.VMEM((2, TOP_K, nt // SUBLANES) + TILE_VIEW, F32),
                        pltpu.SemaphoreType.DMA((2,))],
    )
    return pl.pallas_call(
        _combine_body,
        grid_spec=grid_spec,
        out_shape=jax.ShapeDtypeStruct((n_tok, D_MODEL), F32),
        compiler_params=pltpu.CompilerParams(
            dimension_semantics=("arbitrary",), vmem_limit_bytes=VMEM_LIMIT),
        name="moe_combine",
    )(dest_flat, yb, h_res, gates_tk, final_gain.reshape(1, D_MODEL))


def kernel(x, norm_mix_gain, w_in, hg_lb_fwd, hg_lb_bwd, hg_norm_gain, attn_sink, w_out,
           norm_ffn_gain, w_router, b_router, w_up_gate, b_up_gate, w_down, b_down,
           final_norm_gain):
    bsz, seqlen, d = x.shape
    n_tok = bsz * seqlen
    x2d = x.reshape(n_tok, d)

    lb_fwd = jax.nn.softmax(hg_lb_fwd.astype(F32), axis=0)[0]
    lb_bwd = jax.nn.softmax(hg_lb_bwd.astype(F32), axis=0)[0]

    proj = _inproj(x2d, norm_mix_gain[0], w_in[0].astype(BF16))
    o_f, o_b = _hgrn2(proj, lb_fwd, lb_bwd, bsz, seqlen)
    h_res, gate_t, dest_t, counts, xb = _merge_route(
        o_f, o_b, proj, attn_sink[0], x2d, seqlen, hg_norm_gain[0], w_out[0],
        norm_ffn_gain[0], w_router[0].T.astype(F32), b_router[0].astype(F32))

    n_rows = xb.shape[0]
    cap = n_rows // N_EXPERTS
    counts = counts[:, 0]
    padded = (counts + MOE_PAD - 1) // MOE_PAD * MOE_PAD
    full = padded // MOE_ROWS
    blocks = full + (padded % MOE_ROWS) // MOE_PAD
    block_end = jnp.cumsum(blocks).astype(jnp.int32)
    n_blocks_max = n_tok * TOP_K // MOE_PAD + N_EXPERTS
    b = jnp.arange(n_blocks_max, dtype=jnp.int32)
    owner = (block_end[None, :] <= b[:, None]).astype(jnp.int32)
    expert_of = jnp.minimum(jnp.sum(owner, axis=-1), N_EXPERTS - 1)
    within = b - jnp.sum(owner * blocks[None, :], axis=-1)
    is_owner = expert_of[:, None] == jnp.arange(N_EXPERTS, dtype=jnp.int32)[None, :]
    block_short = (within == jnp.sum(jnp.where(is_owner, full[None, :], 0), axis=-1)).astype(jnp.int32)
    block_group = ((expert_of * cap + within * MOE_ROWS) // SUBLANES).astype(jnp.int32)
    dest_flat = dest_t.reshape(-1)

    yb = _experts(xb.reshape((n_rows // SUBLANES,) + PACKED_TOKEN_TILES), block_end, block_group, block_short,
                  w_up_gate[0], b_up_gate[0], w_down[0], b_down[0])
    out = _combine(yb.reshape(n_rows, LANE_TILES, LANES), dest_flat, h_res, gate_t.T, final_norm_gain)
    return out.reshape(bsz, seqlen, d)
```

```python
import functools

import jax
import jax.numpy as jnp
from jax import lax
from jax.experimental import pallas as pl
from jax.experimental.pallas import tpu as pltpu

F32 = jnp.float32
BF16 = jnp.bfloat16

D_MODEL = 1024
HG_HEADS = 8
HG_DK = 128
HG_CHUNK = 128
HG_GROUP = 16
AT_Q_HEADS = 16
AT_KV_HEADS = 4
AT_GROUP = AT_Q_HEADS // AT_KV_HEADS
AT_HEAD_DIM = 64
WINDOW = 128
AT_BLOCK = WINDOW
AT_KVWIDTH = AT_KV_HEADS * AT_HEAD_DIM
N_EXPERTS = 32
TOP_K = 4
D_FF = 1024
SWIGLU_LIMIT = 7.0
SWIGLU_ALPHA = 1.702
NORM_EPS = 1e-5

COL_HQ, COL_FF, COL_FB, COL_HI, COL_HG = 0, 1024, 2048, 3072, 4096
COL_AQ, COL_AK, COL_AV, COL_GA, COL_GB = 5120, 6144, 6400, 6656, 7680

MOE_ROWS = 1024
MOE_PAD = 512
FF_SLAB = 256
MERGE_SUB = 128
VMEM_LIMIT = 56 * 1024 * 1024

NT_DIMS = (((1,), (1,)), ((), ()))

SUBLANES = 8
LANES = 128
LANE_TILES = D_MODEL // LANES
TILE_VIEW = (LANE_TILES, SUBLANES, LANES)
TOKEN_TILES = (SUBLANES, LANE_TILES, LANES)


def _store_tiles(ref, val):
    groups = val.shape[0] // SUBLANES
    for t in range(val.shape[1] // LANES):
        ref[:, t] = val[:, t * LANES:(t + 1) * LANES].reshape(groups, SUBLANES, LANES)


def _load_tiles(ref):
    groups = ref.shape[0]
    return jnp.concatenate(
        [ref[:, t].reshape(groups * SUBLANES, LANES) for t in range(ref.shape[1])], axis=1)


PACKED_TILES = LANE_TILES // 2
PACKED_VIEW = (PACKED_TILES, SUBLANES, LANES)
PACKED_TOKEN_TILES = (SUBLANES, PACKED_TILES, LANES)


def _pack_bf16_pairs(x):
    half = x.shape[1] // 2
    lo = pltpu.bitcast(x[:, :half].astype(BF16).astype(F32), jnp.uint32)
    hi = pltpu.bitcast(x[:, half:].astype(BF16).astype(F32), jnp.uint32)
    return hi | lax.shift_right_logical(lo, jnp.uint32(16))


def _unpack_bf16_pairs(w):
    lo = pltpu.bitcast(lax.shift_left(w, jnp.uint32(16)), F32)
    hi = pltpu.bitcast(w & jnp.uint32(0xFFFF0000), F32)
    return jnp.concatenate([lo, hi], axis=1).astype(BF16)


def _sigmoid(x):
    return 0.5 * jnp.tanh(0.5 * x) + 0.5


def _inproj_body(x_ref, g_ref, w_ref, o_ref, xn_ref):
    @pl.when(pl.program_id(1) == 0)
    def _():
        x = x_ref[...]
        ms = jnp.mean(x * x, axis=-1, keepdims=True)
        xn_ref[...] = (x * lax.rsqrt(ms + NORM_EPS) * g_ref[...]).astype(BF16)

    o_ref[...] = jnp.dot(xn_ref[...], w_ref[...], preferred_element_type=F32).astype(o_ref.dtype)


def _inproj(x2d, gain, w, tm=1024, tn=4352):
    n_tok = x2d.shape[0]
    n_out = w.shape[1]
    return pl.pallas_call(
        _inproj_body,
        grid=(n_tok // tm, n_out // tn),
        in_specs=[
            pl.BlockSpec((tm, D_MODEL), lambda i, j: (i, 0)),
            pl.BlockSpec((1, D_MODEL), lambda i, j: (0, 0)),
            pl.BlockSpec((D_MODEL, tn), lambda i, j: (0, j)),
        ],
        out_specs=pl.BlockSpec((tm, tn), lambda i, j: (i, j)),
        out_shape=jax.ShapeDtypeStruct((n_tok, n_out), BF16),
        scratch_shapes=[pltpu.VMEM((tm, D_MODEL), BF16)],
        compiler_params=pltpu.CompilerParams(
            dimension_semantics=("parallel", "arbitrary"), vmem_limit_bytes=VMEM_LIMIT),
        name="inproj",
    )(x2d, gain.reshape(1, D_MODEL), w)


class _Dir:
    def __init__(self, q_ref, f_ref, v_ref, o_ref, st_ref, lb, tri, mask, mid_row, last_row):
        self.q_ref, self.f_ref, self.v_ref, self.o_ref, self.st_ref = q_ref, f_ref, v_ref, o_ref, st_ref
        self.c0, self.c1 = 0.5 * (1.0 + lb), 0.5 * (1.0 - lb)
        self.tri, self.mask, self.mid_row, self.last_row = tri, mask, mid_row, last_row


def _gla_group(chains):
    c = HG_CHUNK
    dk = HG_DK
    pre = []
    for d, r0 in chains:
        h = 0.5 * d.q_ref[pl.ds(r0, c), :].astype(F32)
        q = h * jnp.tanh(h) + h
        ct = d.c1 * jnp.tanh(0.5 * d.f_ref[pl.ds(r0, c), :].astype(F32))
        logf = jnp.log2(d.c0 + ct)
        pre.append((q, (1.0 - d.c0) - ct, logf.astype(BF16)))
    cums = []
    for (d, _), (_, _, hl) in zip(chains, pre):
        cums.append(jnp.dot(d.tri, hl, preferred_element_type=F32))
    mids = []
    for (d, r0), (q, k, _), cum in zip(chains, pre, cums):
        mid = cum[d.mid_row:d.mid_row + 1, :]
        last = cum[d.last_row:d.last_row + 1, :]
        a = cum - mid
        grow = jnp.exp2(a)
        qa = (q * grow).astype(BF16)
        ka = (k / grow).astype(BF16)
        to_start = jnp.exp2(mid)
        to_end = jnp.exp2(last - mid)
        dec = jnp.exp2(last)
        scores = lax.dot_general(qa, ka, NT_DIMS, preferred_element_type=F32)
        mids.append((qa, ka, to_start, to_end, dec, scores))
    outs = []
    for (d, r0), (qa, ka, to_start, to_end, dec, scores) in zip(chains, mids):
        v = d.v_ref[pl.ds(r0, c), :]
        sm = jnp.where(d.mask, scores, 0.0).astype(BF16)
        o_intra = jnp.dot(sm, v, preferred_element_type=F32)
        upd = jnp.dot(jnp.transpose(v), ka, preferred_element_type=F32) * to_end
        outs.append((o_intra, upd))
    for (d, r0), (qa, _, to_start, _, dec, _), (o_intra, upd) in zip(chains, mids, outs):
        st = d.st_ref[...]
        d.o_ref[pl.ds(r0, c), :] = (o_intra + lax.dot_general(
            qa, (st * to_start).astype(BF16), NT_DIMS,
            preferred_element_type=F32)).astype(d.o_ref.dtype)
        d.st_ref[...] = st * dec + upd


def _hgrn2_body(qf_ref, ff_ref, vf_ref, qb_ref, fb_ref, vb_ref, lbf_ref, lbb_ref,
                of_ref, ob_ref, sf_ref, sb_ref):
    @pl.when(pl.program_id(2) == 0)
    def _():
        sf_ref[...] = jnp.zeros_like(sf_ref)
        sb_ref[...] = jnp.zeros_like(sb_ref)

    c = HG_CHUNK
    n_chunks = qf_ref.shape[0] // c
    g = min(HG_GROUP, n_chunks)
    row = lax.broadcasted_iota(jnp.int32, (c, c), 0)
    col = lax.broadcasted_iota(jnp.int32, (c, c), 1)
    lower = col <= row
    upper = col >= row
    fwd = _Dir(qf_ref, ff_ref, vf_ref, of_ref, sf_ref, lbf_ref[...],
               jnp.where(lower, 1.0, 0.0).astype(BF16), lower, c // 2 - 1, c - 1)
    bwd = _Dir(qb_ref, fb_ref, vb_ref, ob_ref, sb_ref, lbb_ref[...],
               jnp.where(upper, 1.0, 0.0).astype(BF16), upper, c // 2, 0)

    def step(n, carry):
        chains = []
        for u in range(g):
            chains.append((fwd, pl.multiple_of((n * g + u) * c, c)))
        for u in range(g):
            chains.append((bwd, pl.multiple_of((n_chunks - 1 - n * g - u) * c, c)))
        _gla_group(chains)
        return carry

    lax.fori_loop(0, n_chunks // g, step, 0)


def _hgrn2(proj, lb_fwd, lb_bwd, bsz, seqlen, rows=8192):
    n_tok = bsz * seqlen
    rows = min(rows, seqlen)
    nb = seqlen // rows
    dk = HG_DK

    def fwd_map(col0):
        return lambda b, h, j: (b * nb + j, col0 // dk + h)

    def bwd_map(col0):
        return lambda b, h, j: (b * nb + (nb - 1 - j), col0 // dk + h)

    blk = (rows, dk)
    lb_spec = pl.BlockSpec((1, dk), lambda b, h, j: (0, h))
    return pl.pallas_call(
        _hgrn2_body,
        grid=(bsz, HG_HEADS, nb),
        in_specs=[
            pl.BlockSpec(blk, fwd_map(COL_HQ)), pl.BlockSpec(blk, fwd_map(COL_FF)),
            pl.BlockSpec(blk, fwd_map(COL_HI)),
            pl.BlockSpec(blk, bwd_map(COL_HQ)), pl.BlockSpec(blk, bwd_map(COL_FB)),
            pl.BlockSpec(blk, bwd_map(COL_HI)),
            lb_spec, lb_spec,
        ],
        out_specs=[pl.BlockSpec(blk, fwd_map(0)), pl.BlockSpec(blk, bwd_map(0))],
        out_shape=[jax.ShapeDtypeStruct((n_tok, HG_HEADS * dk), BF16)] * 2,
        scratch_shapes=[pltpu.VMEM((dk, dk), F32), pltpu.VMEM((dk, dk), F32)],
        compiler_params=pltpu.CompilerParams(
            dimension_semantics=("parallel", "parallel", "arbitrary"),
            vmem_limit_bytes=VMEM_LIMIT),
        name="hgrn2",
    )(proj, proj, proj, proj, proj, proj, lb_fwd.reshape(1, -1), lb_bwd.reshape(1, -1))


def _window_attention(sink_ref, q_ref, k_refs, v_refs, starts_sequence, ends_sequence):
    blk = AT_BLOCK
    span = 3 * blk
    dh = AT_HEAD_DIM
    lanes = 2 * dh
    n_q = q_ref.shape[0] // blk
    lo_half = lax.broadcasted_iota(jnp.int32, (blk, lanes), 1) < dh
    row = lax.broadcasted_iota(jnp.int32, (blk, span), 0)
    col = lax.broadcasted_iota(jnp.int32, (blk, span), 1)
    absrel = jnp.abs(row + blk - col)
    base_inner = jnp.where(absrel <= WINDOW, -absrel.astype(F32), -jnp.inf)

    def padded(t, g):
        t = t[:, (g // 2) * lanes:(g // 2 + 1) * lanes].astype(F32)
        if g % 2 == 0:
            lo = jnp.where(lo_half, t, 0.0)
            hi = pltpu.roll(lo, dh, axis=1)
        else:
            hi = jnp.where(lo_half, 0.0, t)
            lo = pltpu.roll(hi, dh, axis=1)
        return lo.astype(BF16), hi.astype(BF16)

    def key_blocks(refs):
        prev_ref, own_ref, next_ref = refs
        return ([prev_ref[...]] + [own_ref[b * blk:(b + 1) * blk, :] for b in range(n_q)]
                + [next_ref[...]])

    k_pads = [[padded(t, g) for t in key_blocks(k_refs)] for g in range(AT_KV_HEADS)]
    v_pads = [[padded(t, g) for t in key_blocks(v_refs)] for g in range(AT_KV_HEADS)]

    def banded(pads, first):
        use = pads[first:first + 3]
        return jnp.concatenate([p[0] for p in use] + [p[1] for p in use], axis=0)

    results = []
    for qb in range(n_q):
        base = base_inner
        if qb == 0:
            base = jnp.where(jnp.logical_and(starts_sequence, col < blk), -jnp.inf, base)
        if qb == n_q - 1:
            base = jnp.where(jnp.logical_and(ends_sequence, col >= 2 * blk), -jnp.inf, base)
        rows = slice(qb * blk, (qb + 1) * blk)
        scores = []
        for g in range(AT_KV_HEADS):
            q2 = q_ref[rows, g * 2 * lanes:(g + 1) * 2 * lanes] * (dh ** -0.5)
            lhs = jnp.concatenate([q2[:, :lanes], q2[:, lanes:]], axis=0).astype(BF16)
            scores.append(lax.dot_general(lhs, banded(k_pads[g], qb), NT_DIMS,
                                          preferred_element_type=F32))
        probs, denoms = [], []
        for g in range(AT_KV_HEADS):
            p_rows, d_rows = [], []
            for r in range(2):
                p_cols, d_cols = [], []
                for c in range(2):
                    h = g * AT_GROUP + 2 * r + c
                    slope = 2.0 ** (-8.0 * (h + 1) / AT_Q_HEADS)
                    sink = sink_ref[h]
                    s = scores[g][r * blk:(r + 1) * blk, c * span:(c + 1) * span] + slope * base
                    m = jnp.maximum(jnp.max(s, axis=-1, keepdims=True), sink)
                    p = jnp.exp(s - m)
                    d_cols.append(jnp.sum(p, axis=-1, keepdims=True) + jnp.exp(sink - m))
                    p_cols.append(p.astype(BF16))
                p_rows.append(jnp.concatenate(p_cols, axis=1))
                d_rows.append(jnp.where(lo_half, d_cols[0], d_cols[1]))
            probs.append(jnp.concatenate(p_rows, axis=0))
            denoms.append(d_rows)
        outs = []
        for g in range(AT_KV_HEADS):
            o = jnp.dot(probs[g], banded(v_pads[g], qb), preferred_element_type=F32)
            for r in range(2):
                outs.append(o[r * blk:(r + 1) * blk, :] / denoms[g][r])
        results.append(jnp.concatenate(outs, axis=-1))
    return results


def _merge_body(n_steps, steps_per_seq, sink_ref, of_ref, ob_ref, hg_ref, ga0_ref, ga1_ref, gb0_ref,
                gb1_ref, q_ref, kp_ref, ko_ref, kn_ref, vp_ref, vo_ref, vn_ref, x_ref,
                gn_ref, wo_ref, g2_ref, wr_ref, br_ref,
                h_ref, gate_ref, dest_ref, cnt_ref, xb_hbm,
                carry_ref, ring, dest_vmem, dest_smem, cnt_vmem, cnt_smem, zero_ref,
                sem_rows, sem_dest, sem_misc):
    i = pl.program_id(0)
    slot = lax.rem(i, 2)
    tm = x_ref.shape[0]
    cap = xb_hbm.shape[0] // N_EXPERTS

    def rows_wait(sl):
        pltpu.make_async_copy(xb_hbm.at[pl.ds(0, tm * TOP_K)], xb_hbm.at[pl.ds(0, tm * TOP_K)],
                              sem_rows.at[sl]).wait()

    def dest_copy(sl):
        return pltpu.make_async_copy(dest_vmem.at[sl], dest_smem.at[sl], sem_dest.at[sl])

    def issue_rows(sl):
        dest_copy(sl).wait()

        def body(g, c):
            for s in range(SUBLANES):
                for k in range(TOP_K):
                    pltpu.make_async_copy(
                        ring.at[sl, g, :, s, :], xb_hbm.at[dest_smem[sl, k, g * SUBLANES + s]],
                        sem_rows.at[sl]).start(priority=k % 2)
            return c
        lax.fori_loop(0, tm // SUBLANES, body, 0, unroll=ISSUE_UNROLL)

    @pl.when(i == 0)
    def _():
        carry_ref[...] = jnp.zeros_like(carry_ref)

    @pl.when(i >= 2)
    def _():
        rows_wait(slot)

    sub = MERGE_SUB
    subs = [slice(s0, s0 + sub) for s0 in range(0, x_ref.shape[0], sub)]
    gn = gn_ref[...]

    step_in_seq = lax.rem(i, steps_per_seq)
    att_blocks = _window_attention(
        sink_ref, q_ref, (kp_ref, ko_ref, kn_ref), (vp_ref, vo_ref, vn_ref),
        step_in_seq == 0, step_in_seq == steps_per_seq - 1)
    per_sub = sub // AT_BLOCK
    y_att = [jnp.concatenate(att_blocks[n * per_sub:(n + 1) * per_sub], axis=0)
             for n in range(len(subs))]

    merged = []
    for rs, ya in zip(subs, y_att):
        o = of_ref[rs, :].astype(F32) + ob_ref[rs, :].astype(F32)
        parts = []
        for h in range(HG_HEADS):
            oh = o[:, h * HG_DK:(h + 1) * HG_DK]
            ms = jnp.mean(oh * oh, axis=-1, keepdims=True)
            parts.append(oh * lax.rsqrt(ms + NORM_EPS) * gn)
        hg = hg_ref[rs, :].astype(F32)
        y_rec = jnp.concatenate(parts, axis=-1) * (hg * _sigmoid(hg))
        gate_a = jnp.concatenate([ga0_ref[rs, :], ga1_ref[rs, :]], axis=-1).astype(F32)
        gate_b = jnp.concatenate([gb0_ref[rs, :], gb1_ref[rs, :]], axis=-1).astype(F32)
        merged.append((_sigmoid(gate_a) * y_rec
                       + _sigmoid(gate_b) * ya).astype(BF16))

    w_out = wo_ref[...].astype(BF16)
    xns = []
    for rs, mg in zip(subs, merged):
        h_res = x_ref[rs, :] + jnp.dot(mg, w_out, preferred_element_type=F32)
        h_ref[rs, :] = h_res
        ms = jnp.mean(h_res * h_res, axis=-1, keepdims=True)
        xn = h_res * lax.rsqrt(ms + NORM_EPS) * g2_ref[...]
        _store_tiles(ring.at[slot, rs.start // SUBLANES:rs.stop // SUBLANES], _pack_bf16_pairs(xn))
        xns.append(xn)

    logits = [lax.dot_general(wr_ref[...], xn, NT_DIMS, preferred_element_type=F32,
                              precision=lax.Precision.HIGHEST) + br_ref[...] for xn in xns]
    e_iota = lax.broadcasted_iota(jnp.int32, (N_EXPERTS, sub), 0).astype(F32)
    picked = []
    for rs, vals in zip(subs, logits):
        tops, idxs = [], []
        for _ in range(TOP_K):
            m = jnp.max(vals, axis=0, keepdims=True)
            sel = jnp.min(jnp.where(vals == m, e_iota, float(N_EXPERTS)), axis=0, keepdims=True)
            tops.append(m)
            idxs.append(sel)
            vals = jnp.where(e_iota == sel, -jnp.inf, vals)
        exps = [jnp.exp(t - tops[0]) for t in tops]
        den = exps[0] + exps[1] + exps[2] + exps[3]
        gate_ref[:, rs] = jnp.concatenate([e / den for e in exps], axis=0)
        picked.append((idxs, jnp.where(vals == -jnp.inf, 1.0, 0.0)))

    s_iota = lax.broadcasted_iota(jnp.int32, (sub, sub), 0)
    t_iota = lax.broadcasted_iota(jnp.int32, (sub, sub), 1)
    strict = jnp.where(s_iota < t_iota, 1.0, 0.0).astype(BF16)
    prefixes = [jnp.dot(chosen.astype(BF16), strict, preferred_element_type=F32)
                for _, chosen in picked]
    carry = carry_ref[...]
    for rs, (idxs, chosen), prefix in zip(subs, picked, prefixes):
        prefix = prefix + carry
        dests = [sel * float(cap) + jnp.sum(jnp.where(e_iota == sel, prefix, 0.0), axis=0, keepdims=True)
                 for sel in idxs]
        dest = jnp.concatenate(dests, axis=0).astype(jnp.int32)
        dest_ref[:, rs] = dest
        dest_vmem[slot, :, rs] = dest
        carry = carry + jnp.sum(chosen, axis=1, keepdims=True)
    carry_ref[...] = carry
    counts = jnp.broadcast_to(carry, cnt_ref.shape).astype(jnp.int32)
    cnt_ref[...] = counts

    dest_copy(slot).start()

    @pl.when(i >= 1)
    def _():
        issue_rows(1 - slot)

    @pl.when(i == n_steps - 1)
    def _():
        issue_rows(slot)
        if n_steps >= 2:
            rows_wait(1 - slot)
        rows_wait(slot)
        cnt_vmem[...] = counts
        cp = pltpu.make_async_copy(cnt_vmem, cnt_smem, sem_misc.at[0])
        cp.start()
        cp.wait()
        zero_ref[...] = jnp.zeros_like(zero_ref)

        def tail(e):
            return pltpu.make_async_copy(
                zero_ref, xb_hbm.at[pl.ds(e * cap + cnt_smem[e, 0], MOE_PAD)], sem_misc.at[0])
        for e in range(N_EXPERTS):
            tail(e).start()
        for e in range(N_EXPERTS):
            tail(e).wait()


def _merge_route(o_f, o_b, proj, sink, x2d, seqlen, gn_gain, w_out, g2, w_router_t, b_router, tm=512):
    n_tok = x2d.shape[0]
    nblk = n_tok // tm
    row_blk = (tm, D_MODEL)
    half = D_MODEL // 2
    half_blk = (tm, half)

    def rows(c):
        return lambda i: (i, c)

    per_step = tm // AT_BLOCK
    n_kv_blocks = n_tok // AT_BLOCK
    kv_blk = (AT_BLOCK, AT_KVWIDTH)
    kv_own = (tm, AT_KVWIDTH)

    def kv_prev(c):
        return lambda i: (jnp.maximum(i * per_step - 1, 0), c)

    def kv_next(c):
        return lambda i: (jnp.minimum((i + 1) * per_step, n_kv_blocks - 1), c)

    kcol = COL_AK // AT_KVWIDTH
    vcol = COL_AV // AT_KVWIDTH
    full = lambda i: (0, 0)
    tok_out = pl.BlockSpec((TOP_K, tm), lambda i: (0, i))
    cap = n_tok + MOE_PAD
    return pl.pallas_call(
        functools.partial(_merge_body, nblk, seqlen // tm),
        grid=(nblk,),
        in_specs=[
            pl.BlockSpec(memory_space=pltpu.SMEM),
            pl.BlockSpec(row_blk, rows(0)), pl.BlockSpec(row_blk, rows(0)),
            pl.BlockSpec(row_blk, rows(COL_HG // D_MODEL)),
            pl.BlockSpec(half_blk, rows(COL_GA // half)),
            pl.BlockSpec(half_blk, rows(COL_GA // half + 1)),
            pl.BlockSpec(half_blk, rows(COL_GB // half)),
            pl.BlockSpec(half_blk, rows(COL_GB // half + 1)),
            pl.BlockSpec(row_blk, rows(COL_AQ // D_MODEL)),
            pl.BlockSpec(kv_blk, kv_prev(kcol)), pl.BlockSpec(kv_own, rows(kcol)),
            pl.BlockSpec(kv_blk, kv_next(kcol)),
            pl.BlockSpec(kv_blk, kv_prev(vcol)), pl.BlockSpec(kv_own, rows(vcol)),
            pl.BlockSpec(kv_blk, kv_next(vcol)),
            pl.BlockSpec(row_blk, rows(0)),
            pl.BlockSpec((1, HG_DK), full),
            pl.BlockSpec((D_MODEL, D_MODEL), full),
            pl.BlockSpec((1, D_MODEL), full),
            pl.BlockSpec((N_EXPERTS, D_MODEL), full),
            pl.BlockSpec((N_EXPERTS, 1), full),
        ],
        out_specs=[
            pl.BlockSpec(row_blk, rows(0)),
            tok_out, tok_out,
            pl.BlockSpec((N_EXPERTS, LANES), full),
            pl.BlockSpec(memory_space=pl.ANY),
        ],
        out_shape=[
            jax.ShapeDtypeStruct((n_tok, D_MODEL), F32),
            jax.ShapeDtypeStruct((TOP_K, n_tok), F32),
            jax.ShapeDtypeStruct((TOP_K, n_tok), jnp.int32),
            jax.ShapeDtypeStruct((N_EXPERTS, LANES), jnp.int32),
            jax.ShapeDtypeStruct((N_EXPERTS * cap, PACKED_TILES, LANES), jnp.uint32),
        ],
        scratch_shapes=[
            pltpu.VMEM((N_EXPERTS, 1), F32),
            pltpu.VMEM((2, tm // SUBLANES) + PACKED_VIEW, jnp.uint32),
            pltpu.VMEM((2, TOP_K, tm), jnp.int32), pltpu.SMEM((2, TOP_K, tm), jnp.int32),
            pltpu.VMEM((N_EXPERTS, LANES), jnp.int32), pltpu.SMEM((N_EXPERTS, LANES), jnp.int32),
            pltpu.VMEM((MOE_PAD, PACKED_TILES, LANES), jnp.uint32),
            pltpu.SemaphoreType.DMA((2,)), pltpu.SemaphoreType.DMA((2,)), pltpu.SemaphoreType.DMA((1,)),
        ],
        compiler_params=pltpu.CompilerParams(
            dimension_semantics=("arbitrary",), vmem_limit_bytes=VMEM_LIMIT),
        name="merge_route",
    )(sink.astype(F32), o_f, o_b, proj, proj, proj, proj, proj, proj, proj, proj, proj, proj, proj, proj,
      x2d, gn_gain.reshape(1, HG_DK), w_out,
      g2.reshape(1, D_MODEL), w_router_t, b_router.reshape(N_EXPERTS, 1))


ISSUE_UNROLL = 2


def _expert_body(bend_ref, bgrp_ref, bshort_ref, x_hbm, wug_ref, bug_ref, wd_ref, bd_ref, y_hbm,
                 wug_bf, wd_bf, act_ref, xbuf, ybuf, sem_in, sem_out):
    e = pl.program_id(0)
    n_used = bend_ref[N_EXPERTS - 1]
    b_first = jnp.where(e == 0, 0, bend_ref[jnp.maximum(e - 1, 0)])
    b_last = bend_ref[e]

    def for_size(b, fn):
        short = bshort_ref[b] == 1

        @pl.when(short)
        def _():
            fn(MOE_PAD // SUBLANES)

        @pl.when(jnp.logical_not(short))
        def _():
            fn(MOE_ROWS // SUBLANES)

    def x_copies(b, slot, groups):
        return [pltpu.make_async_copy(x_hbm.at[pl.ds(bgrp_ref[b], groups), s],
                                      xbuf.at[slot, pl.ds(0, groups), :, s, :], sem_in.at[slot])
                for s in range(SUBLANES)]

    def y_copies(b, slot, groups):
        return [pltpu.make_async_copy(ybuf.at[slot, pl.ds(0, groups), :, s, :],
                                      y_hbm.at[pl.ds(bgrp_ref[b], groups), s], sem_out.at[slot])
                for s in range(SUBLANES)]

    def start_all(copies):
        for cp in copies:
            cp.start()

    def wait_all(copies):
        for cp in copies:
            cp.wait()

    @pl.when((e == 0) & (n_used > 0))
    def _():
        for_size(0, lambda groups: start_all(x_copies(0, 0, groups)))

    ns = D_FF // FF_SLAB
    for s in range(ns):
        lo, hi = s * FF_SLAB, (s + 1) * FF_SLAB
        wug_bf[:, 2 * lo:2 * lo + FF_SLAB] = wug_ref[0, :, lo:hi].astype(BF16)
        wug_bf[:, 2 * lo + FF_SLAB:2 * hi] = wug_ref[0, :, D_FF + lo:D_FF + hi].astype(BF16)
    wd_bf[...] = wd_ref[0].astype(BF16)
    b_ug = bug_ref[0]
    b_d = bd_ref[0]

    def mlp(slot, groups):
        rows = groups * SUBLANES
        x = _unpack_bf16_pairs(_load_tiles(xbuf.at[slot, pl.ds(0, groups)]))
        for s in range(ns):
            lo, hi = s * FF_SLAB, (s + 1) * FF_SLAB
            hu = jnp.dot(x, wug_bf[:, 2 * lo:2 * hi], preferred_element_type=F32)
            g = jnp.minimum(hu[:, :FF_SLAB] + b_ug[:, lo:hi], SWIGLU_LIMIT)
            u = jnp.clip(hu[:, FF_SLAB:] + b_ug[:, D_FF + lo:D_FF + hi], -SWIGLU_LIMIT, SWIGLU_LIMIT)
            act_ref[0:rows, lo:hi] = (g * _sigmoid(SWIGLU_ALPHA * g) * (u + 1.0)).astype(BF16)
        _store_tiles(ybuf.at[slot, pl.ds(0, groups)],
                     jnp.dot(act_ref[0:rows, :], wd_bf[...], preferred_element_type=F32) + b_d)

    def block(b, carry):
        slot = lax.rem(b, 2)
        for_size(b, lambda groups: wait_all(x_copies(b, slot, groups)))

        @pl.when(b + 1 < n_used)
        def _():
            for_size(b + 1, lambda groups: start_all(x_copies(b + 1, 1 - slot, groups)))

        @pl.when(b >= 2)
        def _():
            for_size(b - 2, lambda groups: wait_all(y_copies(b - 2, slot, groups)))

        def run(groups):
            mlp(slot, groups)
            start_all(y_copies(b, slot, groups))
        for_size(b, run)
        return carry

    lax.fori_loop(b_first, b_last, block, 0)

    @pl.when(e == N_EXPERTS - 1)
    def _():
        for back in (2, 1):
            @pl.when(n_used >= back)
            def _():
                last = n_used - back
                for_size(last, lambda groups: wait_all(y_copies(last, lax.rem(last, 2), groups)))


def _experts(xb, block_end, block_group, block_short, w_up_gate, b_up_gate, w_down, b_down):
    groups = MOE_ROWS // SUBLANES

    def w_map(e, bend, bgrp, bshort):
        return (e, 0, 0)

    grid_spec = pltpu.PrefetchScalarGridSpec(
        num_scalar_prefetch=3,
        grid=(N_EXPERTS,),
        in_specs=[
            pl.BlockSpec(memory_space=pl.ANY),
            pl.BlockSpec((1, D_MODEL, 2 * D_FF), w_map),
            pl.BlockSpec((1, 1, 2 * D_FF), w_map),
            pl.BlockSpec((1, D_FF, D_MODEL), w_map),
            pl.BlockSpec((1, 1, D_MODEL), w_map),
        ],
        out_specs=pl.BlockSpec(memory_space=pl.ANY),
        scratch_shapes=[
            pltpu.VMEM((D_MODEL, 2 * D_FF), BF16), pltpu.VMEM((D_FF, D_MODEL), BF16),
            pltpu.VMEM((MOE_ROWS, D_FF), BF16),
            pltpu.VMEM((2, groups) + PACKED_VIEW, jnp.uint32), pltpu.VMEM((2, groups) + TILE_VIEW, F32),
            pltpu.SemaphoreType.DMA((2,)), pltpu.SemaphoreType.DMA((2,)),
        ],
    )
    return pl.pallas_call(
        _expert_body,
        grid_spec=grid_spec,
        out_shape=jax.ShapeDtypeStruct((xb.shape[0],) + TOKEN_TILES, F32),
        compiler_params=pltpu.CompilerParams(
            dimension_semantics=("arbitrary",), vmem_limit_bytes=VMEM_LIMIT),
        name="moe_experts",
    )(block_end, block_group, block_short, xb, w_up_gate, b_up_gate.reshape(N_EXPERTS, 1, -1),
      w_down, b_down.reshape(N_EXPERTS, 1, -1))


COMBINE_TOKENS = 256


def _combine_body(dest_ref, yb_hbm, h_ref, gate_ref, gain_ref, o_ref, buf_ref, sem):
    i = pl.program_id(0)
    n_steps = pl.num_programs(0)
    nt = COMBINE_TOKENS

    def issue(step, slot):
        base = step * nt

        def body(g, c):
            for s in range(SUBLANES):
                for k in range(TOP_K):
                    pltpu.make_async_copy(
                        yb_hbm.at[dest_ref[k * (dest_ref.shape[0] // TOP_K) + base + g * SUBLANES + s]],
                        buf_ref.at[slot, k, g, :, s, :], sem.at[slot]).start(priority=k % 2)
            return c
        lax.fori_loop(0, nt // SUBLANES, body, 0, unroll=ISSUE_UNROLL)

    slot = lax.rem(i, 2)

    @pl.when(i == 0)
    def _():
        issue(0, 0)

    @pl.when(i + 1 < n_steps)
    def _():
        issue(i + 1, 1 - slot)

    pltpu.make_async_copy(buf_ref.at[slot], buf_ref.at[slot], sem.at[slot]).wait()

    gates = gate_ref[...]
    acc = h_ref[...]
    for k in range(TOP_K):
        acc = acc + gates[:, k:k + 1] * _load_tiles(buf_ref.at[slot, k])
    ms = jnp.mean(acc * acc, axis=-1, keepdims=True)
    o_ref[...] = acc * lax.rsqrt(ms + NORM_EPS) * gain_ref[...]


def _combine(yb, dest_flat, h_res, gates_tk, final_gain):
    n_tok = h_res.shape[0]
    nt = COMBINE_TOKENS
    grid_spec = pltpu.PrefetchScalarGridSpec(
        num_scalar_prefetch=1,
        grid=(n_tok // nt,),
        in_specs=[
            pl.BlockSpec(memory_space=pl.ANY),
            pl.BlockSpec((nt, D_MODEL), lambda i, d: (i, 0)),
            pl.BlockSpec((nt, TOP_K), lambda i, d: (i, 0)),
            pl.BlockSpec((1, D_MODEL), lambda i, d: (0, 0)),
        ],
        out_specs=pl.BlockSpec((nt, D_MODEL), lambda i, d: (i, 0)),
        scratch_shapes=[pltpu.VMEM((2, TOP_K, nt // SUBLANES) + TILE_VIEW, F32),
                        pltpu.SemaphoreType.DMA((2,))],
    )
    return pl.pallas_call(
        _combine_body,
        grid_spec=grid_spec,
        out_shape=jax.ShapeDtypeStruct((n_tok, D_MODEL), F32),
        compiler_params=pltpu.CompilerParams(
            dimension_semantics=("arbitrary",), vmem_limit_bytes=VMEM_LIMIT),
        name="moe_combine",
    )(dest_flat, yb, h_res, gates_tk, final_gain.reshape(1, D_MODEL))


def kernel(x, norm_mix_gain, w_in, hg_lb_fwd, hg_lb_bwd, hg_norm_gain, attn_sink, w_out,
           norm_ffn_gain, w_router, b_router, w_up_gate, b_up_gate, w_down, b_down,
           final_norm_gain):
    bsz, seqlen, d = x.shape
    n_tok = bsz * seqlen
    x2d = x.reshape(n_tok, d)

    lb_fwd = jax.nn.softmax(hg_lb_fwd.astype(F32), axis=0)[0]
    lb_bwd = jax.nn.softmax(hg_lb_bwd.astype(F32), axis=0)[0]

    proj = _inproj(x2d, norm_mix_gain[0], w_in[0].astype(BF16))
    o_f, o_b = _hgrn2(proj, lb_fwd, lb_bwd, bsz, seqlen)
    h_res, gate_t, dest_t, counts, xb = _merge_route(
        o_f, o_b, proj, attn_sink[0], x2d, seqlen, hg_norm_gain[0], w_out[0],
        norm_ffn_gain[0], w_router[0].T.astype(F32), b_router[0].astype(F32))

    n_rows = xb.shape[0]
    cap = n_rows // N_EXPERTS
    counts = counts[:, 0]
    padded = (counts + MOE_PAD - 1) // MOE_PAD * MOE_PAD
    full = padded // MOE_ROWS
    blocks = full + (padded % MOE_ROWS) // MOE_PAD
    block_end = jnp.cumsum(blocks).astype(jnp.int32)
    n_blocks_max = n_tok * TOP_K // MOE_PAD + N_EXPERTS
    b = jnp.arange(n_blocks_max, dtype=jnp.int32)
    owner = (block_end[None, :] <= b[:, None]).astype(jnp.int32)
    expert_of = jnp.minimum(jnp.sum(owner, axis=-1), N_EXPERTS - 1)
    within = b - jnp.sum(owner * blocks[None, :], axis=-1)
    is_owner = expert_of[:, None] == jnp.arange(N_EXPERTS, dtype=jnp.int32)[None, :]
    block_short = (within == jnp.sum(jnp.where(is_owner, full[None, :], 0), axis=-1)).astype(jnp.int32)
    block_group = ((expert_of * cap + within * MOE_ROWS) // SUBLANES).astype(jnp.int32)
    dest_flat = dest_t.reshape(-1)

    yb = _experts(xb.reshape((n_rows // SUBLANES,) + PACKED_TOKEN_TILES), block_end, block_group, block_short,
                  w_up_gate[0], b_up_gate[0], w_down[0], b_down[0])
    out = _combine(yb.reshape(n_rows, LANE_TILES, LANES), dest_flat, h_res, gate_t.T, final_norm_gain)
    return out.reshape(bsz, seqlen, d)
```
